```python
import math
import jax, jax.numpy as jnp
from jax import lax
import numpy as np

D_MODEL = 2048
BATCH = 4
SEQ = 2048
DEPTH = 4
DEC_BATCH = 8
DEC_SEQ = 8
PAST_LEN = 16384
PAGE_SIZE = 128

N_EVEN = (DEPTH + 1) // 2
N_ODD = DEPTH // 2
EPS = 1e-6
FOX_HEADS = 8
FOX_HD = 128
FOX_W = FOX_HEADS * FOX_HD
Q_BLOCK = 128
SSD_D_INNER = D_MODEL
SSD_HD = 64
SSD_HEADS = SSD_D_INNER // SSD_HD
SSD_GROUPS = 4
SSD_STATE = 128
SSD_CONV = 4
SSD_CHUNK = 128
SSD_CONV_DIM = SSD_D_INNER + 2 * SSD_GROUPS * SSD_STATE
SPLIT_EVEN = (FOX_W, 2 * FOX_W, 3 * FOX_W, 3 * FOX_W + FOX_HEADS,
              3 * FOX_W + FOX_HEADS + SSD_D_INNER,
              3 * FOX_W + FOX_HEADS + SSD_D_INNER + SSD_CONV_DIM)
EVEN_IN = 3 * FOX_W + FOX_HEADS + SSD_D_INNER + SSD_CONV_DIM + SSD_HEADS
EVEN_MIX = FOX_W + SSD_D_INNER
S5_W = D_MODEL
S5_GROUP = 16
S5_GROUPS = S5_W // S5_GROUP
S5_STATE = 64
MEM_TOKENS = 256
MEM_HEADS = 4
MEM_HD = 128
MEM_W = MEM_HEADS * MEM_HD
FFN_HIDDEN = -(-8 * D_MODEL // (3 * 256)) * 256

kernel_name = 'fox_ssd_s5_hybrid_decode_step'


def rms_norm(x, g):
    xf = x.astype(jnp.float32)
    y = xf * lax.rsqrt(jnp.mean(xf * xf, axis=-1, keepdims=True) + EPS)
    return (y * g.astype(jnp.float32)).astype(x.dtype)


def swiglu_ffn(x, w_up, w_down):
    g, u = jnp.split(x @ w_up, 2, axis=-1)
    return (jax.nn.silu(g) * u) @ w_down


def memory_kv(mem, g_mem, w_mkv, k_norm_g):
    b, m, _ = mem.shape
    k, v = jnp.split(rms_norm(mem, g_mem) @ w_mkv, 2, axis=-1)
    k = rms_norm(k.reshape(b, m, MEM_HEADS, MEM_HD), k_norm_g)
    return k, v.reshape(b, m, MEM_HEADS, MEM_HD)


def memory_cross_attention(xn, w_mq, q_norm_g, mem_k, mem_v, w_mo):
    b, l, _ = xn.shape
    q = rms_norm((xn @ w_mq).reshape(b, l, MEM_HEADS, MEM_HD), q_norm_g)
    s = jnp.einsum('bqhd,bkhd->bhqk', q, mem_k).astype(jnp.float32) * MEM_HD ** -0.5
    p = jax.nn.softmax(s, axis=-1).astype(mem_v.dtype)
    o = jnp.einsum('bhqk,bkhd->bqhd', p, mem_v).reshape(b, l, MEM_W)
    return o @ w_mo


def fox_attention_prompt(q, k, v, log_f):
    b, l, h, d = q.shape
    n_blk = l // Q_BLOCK
    c = jnp.cumsum(log_f, axis=1).transpose(0, 2, 1)
    q_blocks = q.reshape(b, n_blk, Q_BLOCK, h, d).transpose(1, 0, 2, 3, 4)
    c_blocks = c.reshape(b, h, n_blk, Q_BLOCK).transpose(2, 0, 1, 3)
    k_pos = jnp.arange(l)

    def block(args):
        i, q_i, c_i = args
        q_pos = i * Q_BLOCK + jnp.arange(Q_BLOCK)
        s = jnp.einsum('bqhd,bkhd->bhqk', q_i, k).astype(jnp.float32) * d ** -0.5
        s = s + (c_i[..., :, None] - c[..., None, :])
        s = jnp.where(k_pos[None, :] <= q_pos[:, None], s, -jnp.inf)
        p = jax.nn.softmax(s, axis=-1).astype(v.dtype)
        return jnp.einsum('bhqk,bkhd->bqhd', p, v)

    o = lax.map(block, (jnp.arange(n_blk), q_blocks, c_blocks))
    return o.transpose(1, 0, 2, 3, 4).reshape(b, l, h, d)


def fox_attention_sample(q, k, v, log_f, k_past, v_past, log_f_past):
    b, t, h, d = q.shape
    n_past = k_past.shape[1]
    scale = d ** -0.5
    c_new = jnp.cumsum(log_f, axis=1).transpose(0, 2, 1)
    lp = log_f_past.astype(jnp.float32)
    tail = (lax.cumsum(lp, axis=1, reverse=True) - lp).transpose(0, 2, 1)
    s_past = jnp.einsum('bthd,bshd->bhts', q, k_past).astype(jnp.float32) * scale
    s_past = s_past + c_new[..., :, None] + tail[..., None, :]
    s_new = jnp.einsum('bthd,bshd->bhts', q, k).astype(jnp.float32) * scale
    s_new = s_new + (c_new[..., :, None] - c_new[..., None, :])
    s_new = jnp.where(jnp.tril(jnp.ones((t, t), dtype=bool)), s_new, -jnp.inf)
    p = jax.nn.softmax(jnp.concatenate([s_past, s_new], axis=-1), axis=-1).astype(v.dtype)
    return (jnp.einsum('bhts,bshd->bthd', p[..., :n_past], v_past)
            + jnp.einsum('bhts,bshd->bthd', p[..., n_past:], v))


def causal_depthwise_conv(u, buf, w, bias):
    l = u.shape[1]
    full = jnp.concatenate([buf.astype(u.dtype), u], axis=1)
    y = bias + sum(full[:, k:k + l] * w[k] for k in range(SSD_CONV))
    return y, full[:, l:]


def ssd_chunked(x, dt, A, Bm, Cm, h0):
    f32 = jnp.float32
    b, l, H, P = x.shape
    G, N = SSD_GROUPS, SSD_STATE
    R = H // G
    q = SSD_CHUNK if l % SSD_CHUNK == 0 else l
    c = l // q
    a = (dt * A).reshape(b, c, q, G, R)
    xdt = (x.astype(f32) * dt[..., None]).reshape(b, c, q, G, R, P)
    Bc = Bm.astype(f32).reshape(b, c, q, G, N)
    Cc = Cm.astype(f32).reshape(b, c, q, G, N)
    a_cum = jnp.cumsum(a, axis=2)
    causal = jnp.tril(jnp.ones((q, q), dtype=bool))[None, None, :, :, None, None]
    seg = a_cum[:, :, :, None] - a_cum[:, :, None, :]
    decay_in = jnp.exp(jnp.where(causal, seg, -jnp.inf))
    cb = jnp.einsum('bcqgn,bcsgn->bcqsg', Cc, Bc)
    y_diag = jnp.einsum('bcqsg,bcqsgr,bcsgrp->bcqgrp', cb, decay_in, xdt)
    decay_out = jnp.exp(a_cum[:, :, -1:] - a_cum)
    states = jnp.einsum('bcsgn,bcsgr,bcsgrp->bcgrpn', Bc, decay_out, xdt)
    chunk_decay = jnp.exp(a_cum[:, :, -1])

    def step(h, inp):
        st, dec = inp
        return h * dec[..., None, None] + st, h

    h_last, h_start = lax.scan(step, h0.astype(f32).reshape(b, G, R, P, N),
                               (jnp.moveaxis(states, 1, 0), jnp.moveaxis(chunk_decay, 1, 0)))
    h_start = jnp.moveaxis(h_start, 0, 1)
    y_off = jnp.einsum('bcqgn,bcgrpn,bcqgr->bcqgrp', Cc, h_start, jnp.exp(a_cum))
    return (y_diag + y_off).reshape(b, l, H, P), h_last.reshape(b, H, P, N)


def _complex_affine_combine(e1, e2):
    a1r, a1i, b1r, b1i = e1
    a2r, a2i, b2r, b2i = e2
    return (a2r * a1r - a2i * a1i, a2r * a1i + a2i * a1r,
            a2r * b1r - a2i * b1i + b2r, a2r * b1i + a2i * b1r + b2i)


def s5_scan(u, A_re, A_im, B_re, B_im, C_re, C_im, D_skip, log_dt, x0_re, x0_im):
    f32 = jnp.float32
    b, l, w = u.shape
    uf = u.astype(f32).reshape(b, l, S5_GROUPS, S5_GROUP)
    lam_re = jnp.minimum(A_re.astype(f32), -1e-4)
    lam_im = A_im.astype(f32)
    dt = jnp.exp(log_dt.astype(f32))[:, None]
    mag = jnp.exp(lam_re * dt)
    ang = lam_im * dt
    lb_re, lb_im = mag * jnp.cos(ang), mag * jnp.sin(ang)
    nr, ni = lb_re - 1.0, lb_im
    den = lam_re * lam_re + lam_im * lam_im
    coef_re = (nr * lam_re + ni * lam_im) / den
    coef_im = (ni * lam_re - nr * lam_im) / den
    Br, Bi = B_re.astype(f32), B_im.astype(f32)
    bb_re = coef_re[..., None] * Br - coef_im[..., None] * Bi
    bb_im = coef_re[..., None] * Bi + coef_im[..., None] * Br
    bu_re = jnp.einsum('blgk,gnk->blgn', uf, bb_re)
    bu_im = jnp.einsum('blgk,gnk->blgn', uf, bb_im)
    a_re = jnp.broadcast_to(lb_re, (1, l) + lb_re.shape)
    a_im = jnp.broadcast_to(lb_im, (1, l) + lb_im.shape)
    ac_re, ac_im, s_re, s_im = lax.associative_scan(
        _complex_affine_combine, (a_re, a_im, bu_re, bu_im), axis=1)
    x0r = x0_re.astype(f32)[:, None]
    x0i = x0_im.astype(f32)[:, None]
    s_re, s_im = (s_re + ac_re * x0r - ac_im * x0i, s_im + ac_re * x0i + ac_im * x0r)
    y = (jnp.einsum('blgn,gkn->blgk', s_re, C_re.astype(f32))
         - jnp.einsum('blgn,gkn->blgk', s_im, C_im.astype(f32)))
    y = y.reshape(b, l, w) + D_skip.astype(f32) * u.astype(f32)
    return y.astype(u.dtype), s_re[:, -1], s_im[:, -1]


def even_layer_mixer(xn, W, j, conv_buf, ssm0, fox_past):
    b, l, _ = xn.shape
    proj = xn @ W['w_in_even'][j]
    q, k, v, f_logit, z, xbc, dt_raw = jnp.split(proj, SPLIT_EVEN, axis=-1)
    q = rms_norm(q.reshape(b, l, FOX_HEADS, FOX_HD), W['fox_q_norm'][j])
    k = rms_norm(k.reshape(b, l, FOX_HEADS, FOX_HD), W['fox_k_norm'][j])
    v = v.reshape(b, l, FOX_HEADS, FOX_HD)
    log_f = jax.nn.log_sigmoid(f_logit.astype(jnp.float32) + W['fox_b_forget'][j].astype(jnp.float32))
    if fox_past is None:
        o_fox = fox_attention_prompt(q, k, v, log_f)
    else:
        o_fox = fox_attention_sample(q, k, v, log_f, *fox_past)
    xbc, new_buf = causal_depthwise_conv(xbc, conv_buf, W['ssd_conv_w'][j], W['ssd_conv_b'][j])
    xbc = jax.nn.silu(xbc)
    xs, Bm, Cm = jnp.split(xbc, [SSD_D_INNER, SSD_D_INNER + SSD_GROUPS * SSD_STATE], axis=-1)
    xs = xs.reshape(b, l, SSD_HEADS, SSD_HD)
    dt = jax.nn.softplus(dt_raw.astype(jnp.float32) + W['ssd_dt_bias'][j].astype(jnp.float32))
    A = -jnp.exp(W['ssd_A_log'][j].astype(jnp.float32))
    y, h = ssd_chunked(xs, dt, A, Bm.reshape(b, l, SSD_GROUPS, SSD_STATE),
                       Cm.reshape(b, l, SSD_GROUPS, SSD_STATE), ssm0)
    y = (y.astype(xs.dtype) + W['ssd_D'][j][:, None] * xs).reshape(b, l, SSD_D_INNER)
    y = rms_norm(y * jax.nn.silu(z), W['ssd_norm'][j])
    out = jnp.concatenate([o_fox.reshape(b, l, FOX_W), y], axis=-1) @ W['w_out_even'][j]
    return out, k, v, log_f, h, new_buf


def odd_layer_mixer(xn, W, j, re0, im0):
    u = xn @ W['w_in_odd'][j]
    y, s_re, s_im = s5_scan(u, W['s5_A_re'][j], W['s5_A_im'][j], W['s5_B_re'][j], W['s5_B_im'][j],
                            W['s5_C_re'][j], W['s5_C_im'][j], W['s5_D'][j], W['s5_log_dt'][j], re0, im0)
    g = jax.nn.gelu(y)
    a, gate = jnp.split(g @ W['s5_w_glu'][j], 2, axis=-1)
    return a * jax.nn.sigmoid(gate), s_re, s_im


def gather_pages(pool, j, page_table):
    g = pool[j, page_table]
    return g.reshape((g.shape[0], g.shape[1] * g.shape[2]) + g.shape[3:])


def trunk(x, W, mem_k, mem_v, conv0, ssm0, s5_re0, s5_im0, fox_cache):
    fk, fv, fl, hs, bufs, srs, sis = [], [], [], [], [], [], []
    for i in range(DEPTH):
        j = i // 2
        xn = rms_norm(x, W['norm_mix'][i])
        if i % 2 == 0:
            past = None
            if fox_cache is not None:
                k_pool, v_pool, lf_pool, page_table = fox_cache
                past = (gather_pages(k_pool, j, page_table), gather_pages(v_pool, j, page_table),
                        gather_pages(lf_pool, j, page_table))
            out, k, v, lf, h, buf = even_layer_mixer(xn, W, j, conv0[j], ssm0[j], past)
            fk.append(k); fv.append(v); fl.append(lf); hs.append(h); bufs.append(buf)
        else:
            out, sr, si = odd_layer_mixer(xn, W, j, s5_re0[j], s5_im0[j])
            srs.append(sr); sis.append(si)
        x = x + out
        x = x + memory_cross_attention(rms_norm(x, W['norm_cross'][i]), W['w_mq'][i], W['mem_q_norm'][i],
                                       mem_k[i], mem_v[i], W['w_mo'][i])
        x = x + swiglu_ffn(rms_norm(x, W['norm_ffn'][i]), W['w_ffn_up'][i], W['w_ffn_down'][i])
    return (x, jnp.stack(fk), jnp.stack(fv), jnp.stack(fl), jnp.stack(hs), jnp.stack(bufs),
            jnp.stack(srs), jnp.stack(sis))


def setup_inputs(seed: int = 0) -> dict:
    key = jax.random.key(seed)
    keys = jax.random.split(key, 64)
    counter = [0]

    def nk():
        counter[0] += 1
        return keys[counter[0] - 1]

    def nrm(shape, scale=1.0):
        return jax.random.normal(nk(), shape, jnp.float32) * scale

    def gain(shape):
        return 1.0 + nrm(shape, 0.02)

    n_pages = PAST_LEN // PAGE_SIZE
    n_used = DEC_BATCH * n_pages
    n_pool = n_used + (n_used + 3) // 4
    page_table = jax.random.permutation(nk(), n_pool)[:n_used].reshape(DEC_BATCH, n_pages).astype(jnp.int32)
    ssd_dt0 = jnp.exp(jax.random.uniform(nk(), (N_EVEN, SSD_HEADS), jnp.float32,
                                         math.log(1e-3), math.log(1e-1)))
    return {
        'x_prompt': nrm((BATCH, SEQ, D_MODEL)),
        'x_sample': nrm((DEC_BATCH, DEC_SEQ, D_MODEL)),
        'mem_prompt': nrm((BATCH, MEM_TOKENS, D_MODEL)),
        'cache_fox_k': nrm((N_EVEN, n_pool, PAGE_SIZE, FOX_HEADS, FOX_HD)),
        'cache_fox_v': nrm((N_EVEN, n_pool, PAGE_SIZE, FOX_HEADS, FOX_HD)),
        'cache_fox_logf': jax.nn.log_sigmoid(4.0 + nrm((N_EVEN, n_pool, PAGE_SIZE, FOX_HEADS))),
        'cache_mem_k': nrm((DEPTH, DEC_BATCH, MEM_TOKENS, MEM_HEADS, MEM_HD)),
        'cache_mem_v': nrm((DEPTH, DEC_BATCH, MEM_TOKENS, MEM_HEADS, MEM_HD)),
        'state_ssd': nrm((N_EVEN, DEC_BATCH, SSD_HEADS, SSD_HD, SSD_STATE), 0.1),
        'state_conv': nrm((N_EVEN, DEC_BATCH, SSD_CONV - 1, SSD_CONV_DIM)),
        'state_s5_re': nrm((N_ODD, DEC_BATCH, S5_GROUPS, S5_STATE), 0.1),
        'state_s5_im': nrm((N_ODD, DEC_BATCH, S5_GROUPS, S5_STATE), 0.1),
        'page_table': page_table,
        'norm_mix': gain((DEPTH, D_MODEL)),
        'norm_cross': gain((DEPTH, D_MODEL)),
        'norm_mem': gain((DEPTH, D_MODEL)),
        'norm_ffn': gain((DEPTH, D_MODEL)),
        'w_in_even': nrm((N_EVEN, D_MODEL, EVEN_IN), D_MODEL ** -0.5),
        'fox_b_forget': 4.0 + nrm((N_EVEN, FOX_HEADS), 0.5),
        'fox_q_norm': gain((N_EVEN, FOX_HD)),
        'fox_k_norm': gain((N_EVEN, FOX_HD)),
        'ssd_conv_w': nrm((N_EVEN, SSD_CONV, SSD_CONV_DIM), 0.5),
        'ssd_conv_b': nrm((N_EVEN, SSD_CONV_DIM), 0.02),
        'ssd_dt_bias': ssd_dt0 + jnp.log(-jnp.expm1(-ssd_dt0)),
        'ssd_A_log': jnp.log(jax.random.uniform(nk(), (N_EVEN, SSD_HEADS), jnp.float32, 1.0, 16.0)),
        'ssd_D': 1.0 + nrm((N_EVEN, SSD_HEADS), 0.1),
        'ssd_norm': gain((N_EVEN, SSD_D_INNER)),
        'w_out_even': nrm((N_EVEN, EVEN_MIX, D_MODEL), EVEN_MIX ** -0.5),
        'w_in_odd': nrm((N_ODD, D_MODEL, S5_W), D_MODEL ** -0.5),
        's5_A_re': -0.5 + nrm((N_ODD, S5_GROUPS, S5_STATE), 0.01),
        's5_A_im': jnp.pi * jnp.arange(S5_STATE, dtype=jnp.float32) + nrm((N_ODD, S5_GROUPS, S5_STATE), 0.01),
        's5_B_re': nrm((N_ODD, S5_GROUPS, S5_STATE, S5_GROUP), (2 * S5_GROUP) ** -0.5),
        's5_B_im': nrm((N_ODD, S5_GROUPS, S5_STATE, S5_GROUP), (2 * S5_GROUP) ** -0.5),
        's5_C_re': nrm((N_ODD, S5_GROUPS, S5_GROUP, S5_STATE), S5_STATE ** -0.5),
        's5_C_im': nrm((N_ODD, S5_GROUPS, S5_GROUP, S5_STATE), S5_STATE ** -0.5),
        's5_D': 1.0 + nrm((N_ODD, S5_W), 0.1),
        's5_log_dt': jax.random.uniform(nk(), (N_ODD, S5_GROUPS), jnp.float32, math.log(1e-3), math.log(1e-1)),
        's5_w_glu': nrm((N_ODD, S5_W, 2 * D_MODEL), S5_W ** -0.5),
        'w_mq': nrm((DEPTH, D_MODEL, MEM_W), D_MODEL ** -0.5),
        'w_mkv': nrm((DEPTH, D_MODEL, 2 * MEM_W), D_MODEL ** -0.5),
        'mem_q_norm': gain((DEPTH, MEM_HD)),
        'mem_k_norm': gain((DEPTH, MEM_HD)),
        'w_mo': nrm((DEPTH, MEM_W, D_MODEL), MEM_W ** -0.5),
        'w_ffn_up': nrm((DEPTH, D_MODEL, 2 * FFN_HIDDEN), D_MODEL ** -0.5),
        'w_ffn_down': nrm((DEPTH, FFN_HIDDEN, D_MODEL), FFN_HIDDEN ** -0.5),
    }


def reference(x_prompt, x_sample, mem_prompt, cache_fox_k, cache_fox_v, cache_fox_logf, cache_mem_k, cache_mem_v,
              state_ssd, state_conv, state_s5_re, state_s5_im, page_table,
              norm_mix, norm_cross, norm_mem, norm_ffn,
              w_in_even, fox_b_forget, fox_q_norm, fox_k_norm, ssd_conv_w, ssd_conv_b, ssd_dt_bias, ssd_A_log,
              ssd_D, ssd_norm, w_out_even,
              w_in_odd, s5_A_re, s5_A_im, s5_B_re, s5_B_im, s5_C_re, s5_C_im, s5_D, s5_log_dt, s5_w_glu,
              w_mq, w_mkv, mem_q_norm, mem_k_norm, w_mo, w_ffn_up, w_ffn_down):
    W = {
        'norm_mix': norm_mix, 'norm_cross': norm_cross, 'norm_ffn': norm_ffn,
        'w_in_even': w_in_even, 'fox_b_forget': fox_b_forget, 'fox_q_norm': fox_q_norm, 'fox_k_norm': fox_k_norm,
        'ssd_conv_w': ssd_conv_w, 'ssd_conv_b': ssd_conv_b, 'ssd_dt_bias': ssd_dt_bias, 'ssd_A_log': ssd_A_log,
        'ssd_D': ssd_D, 'ssd_norm': ssd_norm, 'w_out_even': w_out_even,
        'w_in_odd': w_in_odd, 's5_A_re': s5_A_re, 's5_A_im': s5_A_im, 's5_B_re': s5_B_re, 's5_B_im': s5_B_im,
        's5_C_re': s5_C_re, 's5_C_im': s5_C_im, 's5_D': s5_D, 's5_log_dt': s5_log_dt, 's5_w_glu': s5_w_glu,
        'w_mq': w_mq, 'mem_q_norm': mem_q_norm, 'w_mo': w_mo, 'w_ffn_up': w_ffn_up, 'w_ffn_down': w_ffn_down,
    }
    b = x_prompt.shape[0]
    mk, mv = [], []
    for i in range(DEPTH):
        k_i, v_i = memory_kv(mem_prompt, norm_mem[i], w_mkv[i], mem_k_norm[i])
        mk.append(k_i); mv.append(v_i)
    mem_k_p = jnp.stack(mk)
    mem_v_p = jnp.stack(mv)
    (y_prompt, fox_k_p, fox_v_p, fox_logf_p, ssd_p, conv_p, s5_re_p, s5_im_p) = trunk(
        x_prompt, W, mem_k_p, mem_v_p,
        jnp.zeros((N_EVEN, b, SSD_CONV - 1, SSD_CONV_DIM), x_prompt.dtype),
        jnp.zeros((N_EVEN, b, SSD_HEADS, SSD_HD, SSD_STATE), jnp.float32),
        jnp.zeros((N_ODD, b, S5_GROUPS, S5_STATE), jnp.float32),
        jnp.zeros((N_ODD, b, S5_GROUPS, S5_STATE), jnp.float32),
        None)
    (y_sample, fox_k_s, fox_v_s, fox_logf_s, ssd_s, conv_s, s5_re_s, s5_im_s) = trunk(
        x_sample, W, cache_mem_k, cache_mem_v, state_conv, state_ssd, state_s5_re, state_s5_im,
        (cache_fox_k, cache_fox_v, cache_fox_logf, page_table))
    return (y_prompt, y_sample,
            fox_k_p, fox_v_p, fox_logf_p, mem_k_p, mem_v_p, ssd_p, conv_p, s5_re_p, s5_im_p,
            fox_k_s, fox_v_s, fox_logf_s, ssd_s, conv_s, s5_re_s, s5_im_s)
```

```python
import functools
import math

import jax
import jax.numpy as jnp
from jax import lax
from jax.experimental import pallas as pl
from jax.experimental.pallas import tpu as pltpu

F32 = jnp.float32
BF16 = jnp.bfloat16
EPS = 1e-6
LANES = 128
SUBLANES = 8
VMEM_LIMIT_BYTES = 56 * 1024 * 1024

FOX_HEADS = 8
HEAD_DIM = 128
FOX_W = FOX_HEADS * HEAD_DIM
SSD_HEADS = 32
SSD_HD = 64
SSD_GROUPS = 4
SSD_STATE = 128
SSD_CHUNK = 128
SSD_INNER = SSD_HEADS * SSD_HD
SSD_CONV = 4
CONV_DIM = SSD_INNER + 2 * SSD_GROUPS * SSD_STATE
DT_LANE0 = FOX_HEADS
S5_GROUP = 16
S5_STATE = 64
S5_GB = 16
MEM_HEADS = 4
HIGHEST = lax.Precision.HIGHEST


def _cparams(sem):
    return pltpu.CompilerParams(dimension_semantics=sem, vmem_limit_bytes=VMEM_LIMIT_BYTES)


def _gelu_tanh(x):
    return 0.5 * x * (1.0 + jnp.tanh(math.sqrt(2.0 / math.pi) * (x + 0.044715 * x * x * x)))


def _softplus(x):
    return jnp.maximum(x, 0.0) + jnp.log1p(jnp.exp(-jnp.abs(x)))


def _silu(x):
    return x * jax.nn.sigmoid(x)


def _rms(x, gain):
    return x * lax.rsqrt(jnp.mean(x * x, axis=-1, keepdims=True) + EPS) * gain


def _dot(a, b):
    return jnp.dot(a, b, preferred_element_type=F32)


def _dot_nt(a, b):
    return lax.dot_general(a, b, (((1,), (1,)), ((), ())), preferred_element_type=F32)


def _dot_exact(a, b):
    return jnp.dot(a, b, preferred_element_type=F32, precision=HIGHEST)


def _dense_kernel(*refs, pro, epi, n_out):
    it = iter(refs)
    x_ref = next(it)
    gain_ref = next(it) if pro == "norm" else None
    w_ref = next(it)
    w2_ref = next(it) if epi in ("swiglu", "glu_res") else None
    res_ref = next(it) if epi in ("residual", "glu_res") else None
    hg_ref = next(it) if epi == "headnorm" else None
    out_refs = [next(it) for _ in range(n_out)]
    xs_ref = next(it) if pro != "none" else None

    if pro != "none":
        @pl.when(pl.program_id(1) == 0)
        def _():
            xf = x_ref[...].astype(F32)
            if pro == "norm":
                xf = _rms(xf, gain_ref[...])
            elif pro == "gelu":
                xf = _gelu_tanh(xf)
            xs_ref[...] = xf.astype(BF16)
        lhs = xs_ref[...]
    else:
        lhs = x_ref[...]

    acc = _dot(lhs, w_ref[...].astype(BF16))
    if epi == "swiglu":
        acc = _silu(acc) * _dot(lhs, w2_ref[...].astype(BF16))
    elif epi == "glu_res":
        acc = res_ref[...] + acc * jax.nn.sigmoid(_dot(lhs, w2_ref[...].astype(BF16)))
    elif epi == "residual":
        acc = res_ref[...] + acc

    if epi == "headnorm":
        for c in range(acc.shape[1] // HEAD_DIM):
            sl = slice(c * HEAD_DIM, (c + 1) * HEAD_DIM)
            blk = _rms(acc[:, sl], hg_ref[...])
            for o in out_refs:
                o[:, sl] = blk.astype(o.dtype)
    else:
        for o in out_refs:
            o[...] = acc.astype(o.dtype)


def _dense(x, w, *, n_cols, tn, col0=0, col0_2=None, pro="none", gain=None, epi="plain",
           res=None, head_gain=None, out_dtypes=(F32,), tm_max=1024):
    m, k = x.shape
    tm = min(tm_max, m)
    assert m % tm == 0 and n_cols % tn == 0 and col0 % tn == 0
    b0 = col0 // tn
    in_specs = [pl.BlockSpec((tm, k), lambda i, j: (i, 0))]
    args = [x]
    if pro == "norm":
        in_specs.append(pl.BlockSpec((1, k), lambda i, j: (0, 0)))
        args.append(gain.reshape(1, k).astype(F32))
    in_specs.append(pl.BlockSpec((k, tn), lambda i, j: (0, b0 + j)))
    args.append(w)
    if epi in ("swiglu", "glu_res"):
        assert col0_2 % tn == 0
        b2 = col0_2 // tn
        in_specs.append(pl.BlockSpec((k, tn), lambda i, j: (0, b2 + j)))
        args.append(w)
    if epi in ("residual", "glu_res"):
        in_specs.append(pl.BlockSpec((tm, tn), lambda i, j: (i, j)))
        args.append(res)
    if epi == "headnorm":
        in_specs.append(pl.BlockSpec((1, HEAD_DIM), lambda i, j: (0, 0)))
        args.append(head_gain.reshape(1, HEAD_DIM).astype(F32))
    out_shape = [jax.ShapeDtypeStruct((m, n_cols), dt) for dt in out_dtypes]
    out_specs = [pl.BlockSpec((tm, tn), lambda i, j: (i, j)) for _ in out_dtypes]
    scratch = [pltpu.VMEM((tm, k), BF16)] if pro != "none" else []
    outs = pl.pallas_call(
        functools.partial(_dense_kernel, pro=pro, epi=epi, n_out=len(out_dtypes)),
        grid=(m // tm, n_cols // tn),
        in_specs=in_specs, out_specs=out_specs, out_shape=out_shape,
        scratch_shapes=scratch,
        compiler_params=_cparams(("parallel", "arbitrary")),
    )(*args)
    return outs


def _outproj_kernel(o_ref, y_ref, g_ref, w_ref, res_ref, out_ref, yn_ref):
    k1 = o_ref.shape[1]

    @pl.when(pl.program_id(1) == 0)
    def _():
        yn_ref[...] = _rms(y_ref[...], g_ref[...]).astype(BF16)
    acc = _dot(o_ref[...], w_ref[:k1, :].astype(BF16)) + _dot(yn_ref[...], w_ref[k1:, :].astype(BF16))
    out_ref[...] = res_ref[...] + acc


def _outproj_even(o_fox, yg, gain, w_out, res, tn=256, tm_max=1024):
    m, d = res.shape
    tm = min(tm_max, m)
    k1, k2 = o_fox.shape[1], yg.shape[1]
    return pl.pallas_call(
        _outproj_kernel,
        grid=(m // tm, d // tn),
        in_specs=[
            pl.BlockSpec((tm, k1), lambda i, j: (i, 0)),
            pl.BlockSpec((tm, k2), lambda i, j: (i, 0)),
            pl.BlockSpec((1, k2), lambda i, j: (0, 0)),
            pl.BlockSpec((k1 + k2, tn), lambda i, j: (0, j)),
            pl.BlockSpec((tm, tn), lambda i, j: (i, j)),
        ],
        out_specs=pl.BlockSpec((tm, tn), lambda i, j: (i, j)),
        out_shape=jax.ShapeDtypeStruct((m, d), F32),
        scratch_shapes=[pltpu.VMEM((tm, k2), BF16)],
        compiler_params=_cparams(("parallel", "arbitrary")),
    )(o_fox, yg, gain.reshape(1, k2).astype(F32), w_out, res)


def _lane_cumsum(x):
    lane = lax.broadcasted_iota(jnp.int32, x.shape, 1)
    k = 1
    while k < LANES:
        x = x + jnp.where(lane >= k, pltpu.roll(x, k, axis=1), 0.0)
        k *= 2
    return x


def _forget_kernel(raw_ref, b_ref, lf_ref, negc_ref):
    lp = raw_ref.shape[1]
    lf = -_softplus(-(raw_ref[0] + b_ref[...]))
    lf_ref[0] = lf[:, :FOX_HEADS]
    carry = jnp.zeros((FOX_HEADS, 1), F32)
    for c in range(lp // LANES):
        blk = lf[c * LANES:(c + 1) * LANES, :].T[:FOX_HEADS, :]
        cs = _lane_cumsum(blk) + carry
        negc_ref[0, :, c * LANES:(c + 1) * LANES] = -cs
        carry = cs[:, LANES - 1:LANES]


def _forget(raw, b_vec):
    b, lp, _ = raw.shape
    return pl.pallas_call(
        _forget_kernel,
        grid=(b,),
        in_specs=[pl.BlockSpec((1, lp, LANES), lambda i: (i, 0, 0)),
                  pl.BlockSpec((1, LANES), lambda i: (0, 0))],
        out_specs=[pl.BlockSpec((1, lp, FOX_HEADS), lambda i: (i, 0, 0)),
                   pl.BlockSpec((1, FOX_HEADS, lp), lambda i: (i, 0, 0))],
        out_shape=[jax.ShapeDtypeStruct((b, lp, FOX_HEADS), F32),
                   jax.ShapeDtypeStruct((b, FOX_HEADS, lp), F32)],
        compiler_params=_cparams(("parallel",)),
    )(raw, b_vec)


def _fox_prompt_kernel(q_ref, k_ref, v_ref, nb_ref, o_ref, m_sc, l_sc, acc_sc, *, tq, tk):
    h = pl.program_id(1)
    qi = pl.program_id(2)
    kj = pl.program_id(3)

    @pl.when(kj == 0)
    def _():
        m_sc[...] = jnp.full(m_sc.shape, -jnp.inf, F32)
        l_sc[...] = jnp.zeros(l_sc.shape, F32)
        acc_sc[...] = jnp.zeros(acc_sc.shape, F32)

    @pl.when(kj * tk <= qi * tq + (tq - 1))
    def _():
        s = _dot_nt(q_ref[0], k_ref[0]) * (HEAD_DIM ** -0.5)
        s = s + nb_ref[0, pl.ds(h, 1), :]
        row = qi * tq + lax.broadcasted_iota(jnp.int32, s.shape, 0)
        col = kj * tk + lax.broadcasted_iota(jnp.int32, s.shape, 1)
        s = jnp.where(col <= row, s, -jnp.inf)
        m_old = m_sc[...]
        m_new = jnp.maximum(m_old, jnp.max(s, axis=-1, keepdims=True))
        alpha = jnp.exp(m_old - m_new)
        p = jnp.exp(s - m_new)
        l_sc[...] = alpha * l_sc[...] + jnp.sum(p, axis=-1, keepdims=True)
        acc_sc[...] = alpha * acc_sc[...] + _dot(p.astype(BF16), v_ref[0])
        m_sc[...] = m_new

    @pl.when(kj == pl.num_programs(3) - 1)
    def _():
        o_ref[0] = (acc_sc[...] / l_sc[...]).astype(o_ref.dtype)


def _fox_prompt(q, k, v, negc, tq=512, tk=512):
    b, l, _ = q.shape
    tq, tk = min(tq, l), min(tk, l)
    nq, nk = l // tq, l // tk

    def kv_map(bi, h, qi, kj):
        return (bi, jnp.minimum(kj, (qi * tq + tq - 1) // tk), h)

    def nb_map(bi, h, qi, kj):
        return (bi, 0, jnp.minimum(kj, (qi * tq + tq - 1) // tk))

    return pl.pallas_call(
        functools.partial(_fox_prompt_kernel, tq=tq, tk=tk),
        grid=(b, FOX_HEADS, nq, nk),
        in_specs=[pl.BlockSpec((1, tq, HEAD_DIM), lambda bi, h, qi, kj: (bi, qi, h)),
                  pl.BlockSpec((1, tk, HEAD_DIM), kv_map),
                  pl.BlockSpec((1, tk, HEAD_DIM), kv_map),
                  pl.BlockSpec((1, FOX_HEADS, tk), nb_map)],
        out_specs=pl.BlockSpec((1, tq, HEAD_DIM), lambda bi, h, qi, kj: (bi, qi, h)),
        out_shape=jax.ShapeDtypeStruct((b, l, FOX_W), BF16),
        scratch_shapes=[pltpu.VMEM((tq, 1), F32), pltpu.VMEM((tq, 1), F32),
                        pltpu.VMEM((tq, HEAD_DIM), F32)],
        compiler_params=_cparams(("parallel", "parallel", "parallel", "arbitrary")),
    )(q, k, v, negc)


def _tail_kernel(pt_ref, lf_ref, tail_ref, carry_sc):
    p = pl.program_id(1)

    @pl.when(p == 0)
    def _():
        carry_sc[...] = jnp.zeros(carry_sc.shape, F32)

    x = lf_ref[0, 0]
    incl = _lane_cumsum(x)
    total = incl[:, LANES - 1:LANES]
    tail_ref[0] = carry_sc[...] + (total - incl)
    carry_sc[...] = carry_sc[...] + total


def _past_tail(lf_pool_t, page_table, layer):
    b, n_pages = page_table.shape
    page = lf_pool_t.shape[-1]
    grid_spec = pltpu.PrefetchScalarGridSpec(
        num_scalar_prefetch=1,
        grid=(b, n_pages),
        in_specs=[pl.BlockSpec((1, 1, FOX_HEADS, page),
                               lambda bi, p, pt: (layer, pt[bi, n_pages - 1 - p], 0, 0))],
        out_specs=pl.BlockSpec((1, FOX_HEADS, page), lambda bi, p, pt: (bi, 0, n_pages - 1 - p)),
        scratch_shapes=[pltpu.VMEM((FOX_HEADS, 1), F32)],
    )
    return pl.pallas_call(
        _tail_kernel, grid_spec=grid_spec,
        out_shape=jax.ShapeDtypeStruct((b, FOX_HEADS, n_pages * page), F32),
        compiler_params=_cparams(("parallel", "arbitrary")),
    )(page_table, lf_pool_t)


def _fox_decode_kernel(pt_ref, q_ref, kp_ref, vp_ref, bp_ref, kn_ref, vn_ref, bn_ref, o_ref,
                       m_sc, l_sc, acc_sc, *, t_new):
    p = pl.program_id(1)
    n_past = pl.num_programs(1) - 1

    @pl.when(p == 0)
    def _():
        m_sc[...] = jnp.full(m_sc.shape, -jnp.inf, F32)
        l_sc[...] = jnp.zeros(l_sc.shape, F32)
        acc_sc[...] = jnp.zeros(acc_sc.shape, F32)

    def step(k, v, bias, new_block):
        for h in range(FOX_HEADS):
            sl = slice(h * HEAD_DIM, (h + 1) * HEAD_DIM)
            s = _dot_nt(q_ref[0, :, sl], k[:, sl].astype(BF16)) * (HEAD_DIM ** -0.5)
            s = s + bias[h:h + 1, :]
            if new_block:
                row = lax.broadcasted_iota(jnp.int32, s.shape, 0)
                col = lax.broadcasted_iota(jnp.int32, s.shape, 1)
                s = jnp.where((col <= row) & (col < t_new), s, -jnp.inf)
            m_old = m_sc[h]
            m_new = jnp.maximum(m_old, jnp.max(s, axis=-1, keepdims=True))
            alpha = jnp.exp(m_old - m_new)
            pr = jnp.exp(s - m_new)
            l_sc[h] = alpha * l_sc[h] + jnp.sum(pr, axis=-1, keepdims=True)
            acc_sc[h] = alpha * acc_sc[h] + _dot(pr.astype(BF16), v[:, sl].astype(BF16))
            m_sc[h] = m_new

    @pl.when(p < n_past)
    def _():
        step(kp_ref[0, 0], vp_ref[0, 0], bp_ref[0], False)

    @pl.when(p == n_past)
    def _():
        step(kn_ref[0], vn_ref[0], bn_ref[0], True)
        for h in range(FOX_HEADS):
            o_ref[0, :, h * HEAD_DIM:(h + 1) * HEAD_DIM] = (acc_sc[h] / l_sc[h]).astype(o_ref.dtype)


def _fox_decode(q, k_new, v_new, negc_new, k_pool, v_pool, tail, page_table, layer, t_new):
    b, t, _ = q.shape
    n_pages = page_table.shape[1]
    page = k_pool.shape[2]

    def pool_map(bi, p, pt):
        return (layer, pt[bi, jnp.minimum(p, n_pages - 1)], 0, 0)

    grid_spec = pltpu.PrefetchScalarGridSpec(
        num_scalar_prefetch=1,
        grid=(b, n_pages + 1),
        in_specs=[pl.BlockSpec((1, t, FOX_W), lambda bi, p, pt: (bi, 0, 0)),
                  pl.BlockSpec((1, 1, page, FOX_W), pool_map),
                  pl.BlockSpec((1, 1, page, FOX_W), pool_map),
                  pl.BlockSpec((1, FOX_HEADS, page), lambda bi, p, pt: (bi, 0, jnp.minimum(p, n_pages - 1))),
                  pl.BlockSpec((1, page, FOX_W), lambda bi, p, pt: (bi, 0, 0)),
                  pl.BlockSpec((1, page, FOX_W), lambda bi, p, pt: (bi, 0, 0)),
                  pl.BlockSpec((1, FOX_HEADS, page), lambda bi, p, pt: (bi, 0, 0))],
        out_specs=pl.BlockSpec((1, t, FOX_W), lambda bi, p, pt: (bi, 0, 0)),
        scratch_shapes=[pltpu.VMEM((FOX_HEADS, t, 1), F32), pltpu.VMEM((FOX_HEADS, t, 1), F32),
                        pltpu.VMEM((FOX_HEADS, t, HEAD_DIM), F32)],
    )
    return pl.pallas_call(
        functools.partial(_fox_decode_kernel, t_new=t_new), grid_spec=grid_spec,
        out_shape=jax.ShapeDtypeStruct((b, t, FOX_W), BF16),
        compiler_params=_cparams(("parallel", "arbitrary")),
    )(page_table, q, k_pool, v_pool, tail, k_new, v_new, negc_new)


def _conv_kernel(x_ref, buf_ref, w_ref, b_ref, y_ref, nb_ref, full_sc):
    l = x_ref.shape[1]
    pad = SUBLANES
    full_sc[pad - (SSD_CONV - 1):pad, :] = buf_ref[0]
    full_sc[pad:pad + l, :] = x_ref[0]
    acc = b_ref[...] + full_sc[pad - 3:pad - 3 + l, :] * w_ref[0:1, :]
    for kk in range(1, SSD_CONV):
        acc = acc + full_sc[pad - 3 + kk:pad - 3 + kk + l, :] * w_ref[kk:kk + 1, :]
    y_ref[0] = _silu(acc)
    nb_ref[0] = full_sc[pad + l - (SSD_CONV - 1):pad + l, :]


def _conv(zx, col0, buf, w, bias, tc=512):
    b, l, _ = zx.shape
    c = w.shape[1]
    cb0 = col0 // tc
    return pl.pallas_call(
        _conv_kernel,
        grid=(b, c // tc),
        in_specs=[pl.BlockSpec((1, l, tc), lambda i, j: (i, 0, cb0 + j)),
                  pl.BlockSpec((1, SSD_CONV - 1, tc), lambda i, j: (i, 0, j)),
                  pl.BlockSpec((SSD_CONV, tc), lambda i, j: (0, j)),
                  pl.BlockSpec((1, tc), lambda i, j: (0, j))],
        out_specs=[pl.BlockSpec((1, l, tc), lambda i, j: (i, 0, j)),
                   pl.BlockSpec((1, SSD_CONV - 1, tc), lambda i, j: (i, 0, j))],
        out_shape=[jax.ShapeDtypeStruct((b, l, c), F32),
                   jax.ShapeDtypeStruct((b, SSD_CONV - 1, c), F32)],
        scratch_shapes=[pltpu.VMEM((l + SUBLANES, tc), F32)],
        compiler_params=_cparams(("parallel", "parallel")),
    )(zx, buf, w, bias.reshape(1, c))


def _ssd_kernel(xc_ref, tail_ref, z_ref, dtb_ref, alog_ref, dexp_ref, e_ref, h0_ref,
                y_ref, hout_ref, h_sc, *, valid_len):
    c = pl.program_id(1)
    q = SSD_CHUNK
    gw = SSD_INNER // SSD_GROUPS
    hpg = SSD_HEADS // SSD_GROUPS

    @pl.when(c == 0)
    def _():
        h_sc[...] = h0_ref[0]

    lane = lax.broadcasted_iota(jnp.int32, (q, LANES), 1)
    row = lax.broadcasted_iota(jnp.int32, (q, LANES), 0)
    col_i = lax.broadcasted_iota(jnp.int32, (q, q), 1)
    row_i = lax.broadcasted_iota(jnp.int32, (q, q), 0)
    causal = col_i <= row_i
    tri = causal.astype(F32)

    dt = _softplus(tail_ref[0] + dtb_ref[...])
    live = (lane >= DT_LANE0) & (lane < DT_LANE0 + SSD_HEADS) & (c * q + row < valid_len)
    dt = jnp.where(live, dt, 0.0)
    a = dt * (-jnp.exp(alog_ref[...]))
    a_cum = _dot_exact(tri, a)
    a_cum_t = a_cum.T
    a_last = a_cum[q - 1:q, :]
    ea = jnp.exp(a_cum)
    w1 = dt * jnp.exp(a_last - a_cum)

    for g in range(SSD_GROUPS):
        e = e_ref[g]
        dt_e = _dot_exact(dt, e)
        w1_e = _dot_exact(w1, e)
        ea_e = _dot_exact(ea, e)
        xs = xc_ref[0, :, g * gw:(g + 1) * gw]
        bb = xc_ref[0, :, SSD_INNER + g * SSD_STATE:SSD_INNER + (g + 1) * SSD_STATE].astype(BF16)
        cc = xc_ref[0, :, SSD_INNER + (SSD_GROUPS + g) * SSD_STATE:
                    SSD_INNER + (SSD_GROUPS + g + 1) * SSD_STATE].astype(BF16)
        cb = _dot_nt(cc, bb)
        xdt = xs * dt_e
        hg = h_sc[g * gw:(g + 1) * gw, :]
        y = _dot_nt(cc, hg.astype(BF16)) * ea_e
        st = _dot((xs * w1_e).T.astype(BF16), bb)
        for hp in range(hpg // 2):
            pair = xdt[:, hp * LANES:(hp + 1) * LANES]
            lane_p = lax.broadcasted_iota(jnp.int32, pair.shape, 1)
            yp = None
            for sub in range(2):
                h = 2 * hp + sub
                ln = DT_LANE0 + g * hpg + h
                seg = a_cum[:, ln:ln + 1] - a_cum_t[ln:ln + 1, :]
                dec = jnp.exp(jnp.where(causal, seg, -jnp.inf))
                mat = (cb * dec).astype(BF16)
                in_head = (lane_p >= sub * SSD_HD) & (lane_p < (sub + 1) * SSD_HD)
                rhs = jnp.where(in_head, pair, 0.0).astype(BF16)
                part = _dot(mat, rhs)
                yp = part if yp is None else yp + part
                r0 = g * gw + h * SSD_HD
                h_sc[r0:r0 + SSD_HD, :] = (hg[h * SSD_HD:(h + 1) * SSD_HD, :] * jnp.exp(a_last[:, ln:ln + 1])
                                           + st[h * SSD_HD:(h + 1) * SSD_HD, :])
            cs = slice(g * gw + hp * LANES, g * gw + (hp + 1) * LANES)
            yt = yp + y[:, hp * LANES:(hp + 1) * LANES] + dexp_ref[:, cs] * xs[:, hp * LANES:(hp + 1) * LANES]
            y_ref[0, :, cs] = yt * _silu(z_ref[0, :, cs])

    @pl.when(c == pl.num_programs(1) - 1)
    def _():
        hout_ref[0] = h_sc[...]


def _ssd(xc, tail, zx, dtb_vec, alog_vec, d_exp, e_mat, h0, valid_len):
    b, lp, _ = xc.shape
    nc = lp // SSD_CHUNK
    return pl.pallas_call(
        functools.partial(_ssd_kernel, valid_len=valid_len),
        grid=(b, nc),
        in_specs=[pl.BlockSpec((1, SSD_CHUNK, CONV_DIM), lambda i, c: (i, c, 0)),
                  pl.BlockSpec((1, SSD_CHUNK, LANES), lambda i, c: (i, c, 0)),
                  pl.BlockSpec((1, SSD_CHUNK, SSD_INNER), lambda i, c: (i, c, 0)),
                  pl.BlockSpec((1, LANES), lambda i, c: (0, 0)),
                  pl.BlockSpec((1, LANES), lambda i, c: (0, 0)),
                  pl.BlockSpec((1, SSD_INNER), lambda i, c: (0, 0)),
                  pl.BlockSpec((SSD_GROUPS, LANES, SSD_INNER // SSD_GROUPS), lambda i, c: (0, 0, 0)),
                  pl.BlockSpec((1, SSD_INNER, SSD_STATE), lambda i, c: (i, 0, 0))],
        out_specs=[pl.BlockSpec((1, SSD_CHUNK, SSD_INNER), lambda i, c: (i, c, 0)),
                   pl.BlockSpec((1, SSD_INNER, SSD_STATE), lambda i, c: (i, 0, 0))],
        out_shape=[jax.ShapeDtypeStruct((b, lp, SSD_INNER), F32),
                   jax.ShapeDtypeStruct((b, SSD_INNER, SSD_STATE), F32)],
        scratch_shapes=[pltpu.VMEM((SSD_INNER, SSD_STATE), F32)],
        compiler_params=_cparams(("parallel", "arbitrary")),
    )(xc, tail, zx, dtb_vec, alog_vec, d_exp, e_mat, h0)


def _s5_prep_kernel(are_ref, aim_ref, ldt_ref, bre_ref, bim_ref, pwr_ref, pwi_ref, bbr_ref, bbi_ref):
    lam_re = jnp.minimum(are_ref[...], -1e-4)
    lam_im = aim_ref[...]
    dt = jnp.exp(ldt_ref[...])
    mag = jnp.exp(lam_re * dt)
    ang = lam_im * dt
    lb_re, lb_im = mag * jnp.cos(ang), mag * jnp.sin(ang)
    nr, ni = lb_re - 1.0, lb_im
    den = lam_re * lam_re + lam_im * lam_im
    coef_re = (nr * lam_re + ni * lam_im) / den
    coef_im = (ni * lam_re - nr * lam_im) / den
    for k in range(S5_GROUP):
        br, bi = bre_ref[k], bim_ref[k]
        bbr_ref[k] = coef_re * br - coef_im * bi
        bbi_ref[k] = coef_re * bi + coef_im * br
    pr, pi = lb_re, lb_im
    pwr_ref[0], pwi_ref[0] = pr, pi
    for r in range(1, SUBLANES):
        pr, pi = pr * lb_re - pi * lb_im, pr * lb_im + pi * lb_re
        pwr_ref[r], pwi_ref[r] = pr, pi


def _s5_prep(a_re, a_im, log_dt, b_re_t, b_im_t):
    g, n = a_re.shape
    return pl.pallas_call(
        _s5_prep_kernel,
        out_shape=[jax.ShapeDtypeStruct((SUBLANES, g, n), F32), jax.ShapeDtypeStruct((SUBLANES, g, n), F32),
                   jax.ShapeDtypeStruct((S5_GROUP, g, n), F32), jax.ShapeDtypeStruct((S5_GROUP, g, n), F32)],
    )(a_re, a_im, log_dt.reshape(g, 1), b_re_t, b_im_t)


def _s5_kernel(u_ref, bre_ref, bim_ref, cre_ref, cim_ref, pwr_ref, pwi_ref, d_ref, s0r_ref, s0i_ref,
               g_ref, sr_ref, si_ref, xr_sc, xi_sc, *, tt):
    l = u_ref.shape[1]
    w = xr_sc.shape[1]
    pwr, pwi = pwr_ref[0], pwi_ref[0]
    rows = lax.broadcasted_iota(jnp.int32, (SUBLANES, w), 0)
    steps = []
    for kk in (1, 2, 4):
        keep = rows >= kk
        steps.append((kk,
                      jnp.where(keep, jnp.broadcast_to(pwr[kk - 1:kk, :], (SUBLANES, w)), 0.0),
                      jnp.where(keep, jnp.broadcast_to(pwi[kk - 1:kk, :], (SUBLANES, w)), 0.0)))

    def block(i, carry):
        cr, ci = carry
        r0 = pl.multiple_of(i * SUBLANES, SUBLANES)
        xr = xr_sc[pl.ds(r0, SUBLANES), :]
        xi = xi_sc[pl.ds(r0, SUBLANES), :]
        for kk, tr, ti in steps:
            sr = pltpu.roll(xr, kk, axis=0)
            si = pltpu.roll(xi, kk, axis=0)
            xr, xi = xr + (tr * sr - ti * si), xi + (tr * si + ti * sr)
        xr, xi = xr + (pwr * cr - pwi * ci), xi + (pwr * ci + pwi * cr)
        xr_sc[pl.ds(r0, SUBLANES), :] = xr
        xi_sc[pl.ds(r0, SUBLANES), :] = xi
        return xr[SUBLANES - 1:SUBLANES, :], xi[SUBLANES - 1:SUBLANES, :]

    def tile(t, carry):
        t0 = pl.multiple_of(t * tt, tt)
        u = u_ref[0, pl.ds(t0, tt), :]
        ub = u.astype(BF16)
        xr_sc[...] = _dot(ub, bre_ref[0])
        xi_sc[...] = _dot(ub, bim_ref[0])
        carry = lax.fori_loop(0, tt // SUBLANES, block, carry)
        y = _dot(xr_sc[...].astype(BF16), cre_ref[0]) - _dot(xi_sc[...].astype(BF16), cim_ref[0])
        y = y + d_ref[0] * u
        g_ref[0, pl.ds(t0, tt), :] = _gelu_tanh(y).astype(g_ref.dtype)
        return carry

    cr, ci = lax.fori_loop(0, l // tt, tile, (s0r_ref[0, 0], s0i_ref[0, 0]))
    sr_ref[0, 0] = cr
    si_ref[0, 0] = ci


def _s5(u, bbd_re, bbd_im, cbd_re, cbd_im, pw_re, pw_im, d_vec, s0_re, s0_im, tt=256):
    b, l, d = u.shape
    ngb = bbd_re.shape[0]
    cw = d // ngb
    sw = bbd_re.shape[2]
    tt = min(tt, l)
    state_spec = pl.BlockSpec((1, 1, 1, sw), lambda i, j: (i, j, 0, 0))
    return pl.pallas_call(
        functools.partial(_s5_kernel, tt=tt),
        grid=(b, ngb),
        in_specs=[pl.BlockSpec((1, l, cw), lambda i, j: (i, 0, j)),
                  pl.BlockSpec((1, cw, sw), lambda i, j: (j, 0, 0)),
                  pl.BlockSpec((1, cw, sw), lambda i, j: (j, 0, 0)),
                  pl.BlockSpec((1, sw, cw), lambda i, j: (j, 0, 0)),
                  pl.BlockSpec((1, sw, cw), lambda i, j: (j, 0, 0)),
                  pl.BlockSpec((1, SUBLANES, sw), lambda i, j: (j, 0, 0)),
                  pl.BlockSpec((1, SUBLANES, sw), lambda i, j: (j, 0, 0)),
                  pl.BlockSpec((1, 1, cw), lambda i, j: (j, 0, 0)),
                  state_spec, state_spec],
        out_specs=[pl.BlockSpec((1, l, cw), lambda i, j: (i, 0, j)), state_spec, state_spec],
        out_shape=[jax.ShapeDtypeStruct((b, l, d), BF16),
                   jax.ShapeDtypeStruct((b, ngb, 1, sw), F32),
                   jax.ShapeDtypeStruct((b, ngb, 1, sw), F32)],
        scratch_shapes=[pltpu.VMEM((tt, sw), F32), pltpu.VMEM((tt, sw), F32)],
        compiler_params=_cparams(("parallel", "parallel")),
    )(u, bbd_re, bbd_im, cbd_re, cbd_im, pw_re, pw_im, d_vec, s0_re, s0_im)


def _cross_kernel(x_ref, g_ref, wq_ref, qg_ref, k_ref, v_ref, wo_ref, o_ref, att_sc):
    x = x_ref[0]
    xn = _rms(x, g_ref[...]).astype(BF16)
    q = _dot(xn, wq_ref[...].astype(BF16))
    for h in range(MEM_HEADS):
        sl = slice(h * HEAD_DIM, (h + 1) * HEAD_DIM)
        qh = _rms(q[:, sl], qg_ref[...]).astype(BF16)
        s = _dot_nt(qh, k_ref[0, :, sl].astype(BF16)) * (HEAD_DIM ** -0.5)
        p = jnp.exp(s - jnp.max(s, axis=-1, keepdims=True))
        p = p / jnp.sum(p, axis=-1, keepdims=True)
        att_sc[:, sl] = _dot(p.astype(BF16), v_ref[0, :, sl].astype(BF16)).astype(BF16)
    o_ref[0] = x + _dot(att_sc[...], wo_ref[...].astype(BF16))


def _cross_attn(x, gain, w_q, q_gain, mem_k, mem_v, w_o, tq=512):
    b, l, d = x.shape
    mt, mw = mem_k.shape[1], mem_k.shape[2]
    tq = min(tq, l)
    return pl.pallas_call(
        _cross_kernel,
        grid=(b, l // tq),
        in_specs=[pl.BlockSpec((1, tq, d), lambda i, j: (i, j, 0)),
                  pl.BlockSpec((1, d), lambda i, j: (0, 0)),
                  pl.BlockSpec((d, mw), lambda i, j: (0, 0)),
                  pl.BlockSpec((1, HEAD_DIM), lambda i, j: (0, 0)),
                  pl.BlockSpec((1, mt, mw), lambda i, j: (i, 0, 0)),
                  pl.BlockSpec((1, mt, mw), lambda i, j: (i, 0, 0)),
                  pl.BlockSpec((mw, d), lambda i, j: (0, 0))],
        out_specs=pl.BlockSpec((1, tq, d), lambda i, j: (i, j, 0)),
        out_shape=jax.ShapeDtypeStruct((b, l, d), F32),
        scratch_shapes=[pltpu.VMEM((tq, mw), BF16)],
        compiler_params=_cparams(("parallel", "parallel")),
    )(x, gain.reshape(1, d), w_q, q_gain.reshape(1, HEAD_DIM), mem_k, mem_v, w_o)


def _pack_even_weight(w):
    d = w.shape[0]
    f0 = 3 * FOX_W
    z0 = f0 + FOX_HEADS
    dt0 = z0 + SSD_INNER + CONV_DIM
    tail = jnp.concatenate([w[:, f0:z0], w[:, dt0:dt0 + SSD_HEADS],
                            jnp.zeros((d, LANES - FOX_HEADS - SSD_HEADS), w.dtype)], axis=1)
    return jnp.concatenate([w[:, :f0], w[:, z0:dt0], tail], axis=1)


def _lane_vec(vals, lane0):
    v = jnp.zeros((1, LANES), F32)
    return v.at[0, lane0:lane0 + vals.shape[0]].set(vals.astype(F32))


def _head_expand():
    rows = jnp.arange(LANES)[None, :, None]
    cols = jnp.arange(SSD_INNER // SSD_GROUPS)[None, None, :]
    g = jnp.arange(SSD_GROUPS)[:, None, None]
    return (rows == DT_LANE0 + g * (SSD_HEADS // SSD_GROUPS) + cols // SSD_HD).astype(F32)


def _block_diag(m, out_rows_first):
    eye = jnp.eye(S5_GB, dtype=m.dtype)
    bd = jnp.einsum("bgxy,gh->bgxhy", m, eye)
    nb, g, a, _, bb = bd.shape
    return bd.reshape(nb, g * a, g * bb)


def _pad_rows(x, rows):
    return jnp.pad(x, ((0, 0), (0, rows - x.shape[1]), (0, 0)))


def _trunk(x, W, mem_k, mem_v, conv0, ssm0, s5_re0, s5_im0, fox_cache):
    b, l, d = x.shape
    t = b * l
    depth = W["norm_mix"].shape[0]
    lp = max(l, SSD_CHUNK)
    e_mat = _head_expand()
    fk, fv, fl, hs, bufs, srs, sis = [], [], [], [], [], [], []
    x2 = x.reshape(t, d)
    for i in range(depth):
        j = i // 2
        if i % 2 == 0:
            wp = W["w_in_even_packed"][j]
            nm = W["norm_mix"][i]
            (q_bf,) = _dense(x2, wp, n_cols=FOX_W, tn=512, col0=0, pro="norm", gain=nm, epi="headnorm",
                             head_gain=W["fox_q_norm"][j], out_dtypes=(BF16,))
            k_f, k_bf = _dense(x2, wp, n_cols=FOX_W, tn=512, col0=FOX_W, pro="norm", gain=nm, epi="headnorm",
                               head_gain=W["fox_k_norm"][j], out_dtypes=(F32, BF16))
            v_f, v_bf = _dense(x2, wp, n_cols=FOX_W, tn=512, col0=2 * FOX_W, pro="norm", gain=nm,
                               out_dtypes=(F32, BF16))
            (zx,) = _dense(x2, wp, n_cols=SSD_INNER + CONV_DIM, tn=512, col0=3 * FOX_W, pro="norm", gain=nm)
            (tail,) = _dense(x2, wp, n_cols=LANES, tn=LANES, col0=3 * FOX_W + SSD_INNER + CONV_DIM,
                             pro="norm", gain=nm)
            zx3 = zx.reshape(b, l, -1)
            tail3 = tail.reshape(b, l, LANES)
            tail_p = _pad_rows(tail3, lp) if lp != l else tail3
            lf, negc = _forget(tail_p, _lane_vec(W["fox_b_forget"][j], 0))
            lf = lf[:, :l]
            q3, k3, v3 = (a.reshape(b, l, FOX_W) for a in (q_bf, k_bf, v_bf))
            if fox_cache is None:
                o_fox = _fox_prompt(q3, k3, v3, negc)
            else:
                k_pool, v_pool, lf_pool_t, page_table = fox_cache
                page = k_pool.shape[2]
                tail_past = _past_tail(lf_pool_t, page_table, j)
                o_fox = _fox_decode(q3, _pad_rows(k_f.reshape(b, l, FOX_W), page),
                                    _pad_rows(v_f.reshape(b, l, FOX_W), page), negc[:, :, :page],
                                    k_pool, v_pool, tail_past, page_table, j, l)
            xc, new_buf = _conv(zx3, SSD_INNER, conv0[j], W["ssd_conv_w"][j], W["ssd_conv_b"][j])
            if lp != l:
                xc_p, z_p = _pad_rows(xc, lp), _pad_rows(zx3[:, :, :SSD_INNER], lp)
            else:
                xc_p, z_p = xc, zx3
            yg, h_last = _ssd(xc_p, tail_p, z_p, _lane_vec(W["ssd_dt_bias"][j], DT_LANE0),
                              _lane_vec(W["ssd_A_log"][j], DT_LANE0),
                              jnp.repeat(W["ssd_D"][j].astype(F32), SSD_HD).reshape(1, SSD_INNER),
                              e_mat, ssm0[j].reshape(b, SSD_INNER, SSD_STATE), l)
            yg = yg[:, :l].reshape(t, SSD_INNER)
            x2 = _outproj_even(o_fox.reshape(t, FOX_W), yg, W["ssd_norm"][j], W["w_out_even"][j], x2)
            fk.append(k_f.reshape(b, l, FOX_HEADS, HEAD_DIM))
            fv.append(v_f.reshape(b, l, FOX_HEADS, HEAD_DIM))
            fl.append(lf)
            hs.append(h_last.reshape(b, SSD_HEADS, SSD_HD, SSD_STATE))
            bufs.append(new_buf)
        else:
            (u,) = _dense(x2, W["w_in_odd"][j], n_cols=d, tn=512, pro="norm", gain=W["norm_mix"][i])
            p = W["s5_packed"][j]
            ngb = p["bbd_re"].shape[0]
            g_bf, s_re, s_im = _s5(u.reshape(b, l, d), p["bbd_re"], p["bbd_im"], p["cbd_re"], p["cbd_im"],
                                   p["pw_re"], p["pw_im"], p["d_vec"],
                                   s5_re0[j].reshape(b, ngb, 1, -1), s5_im0[j].reshape(b, ngb, 1, -1))
            wg = W["s5_w_glu"][j]
            (x2,) = _dense(g_bf.reshape(t, d), wg, n_cols=d, tn=512, col0=0, col0_2=d, epi="glu_res", res=x2)
            srs.append(s_re.reshape(b, -1, S5_STATE))
            sis.append(s_im.reshape(b, -1, S5_STATE))
        x3 = _cross_attn(x2.reshape(b, l, d), W["norm_cross"][i], W["w_mq"][i], W["mem_q_norm"][i],
                         mem_k[i], mem_v[i], W["w_mo"][i])
        x2 = x3.reshape(t, d)
        hid = W["w_ffn_down"][i].shape[0]
        (hmid,) = _dense(x2, W["w_ffn_up"][i], n_cols=hid, tn=512, col0=0, col0_2=hid, pro="norm",
                         gain=W["norm_ffn"][i], epi="swiglu", out_dtypes=(BF16,))
        (x2,) = _dense(hmid, W["w_ffn_down"][i], n_cols=d, tn=256, epi="residual", res=x2)
    return (x2.reshape(b, l, d), jnp.stack(fk), jnp.stack(fv), jnp.stack(fl), jnp.stack(hs), jnp.stack(bufs),
            jnp.stack(srs), jnp.stack(sis))


def _pack_s5(a_re, a_im, b_re, b_im, c_re, c_im, d_skip, log_dt):
    g, n, k = b_re.shape
    ngb = g // S5_GB
    pw_re, pw_im, bb_re, bb_im = _s5_prep(a_re.astype(F32), a_im.astype(F32), log_dt.astype(F32),
                                          jnp.transpose(b_re, (2, 0, 1)).astype(F32),
                                          jnp.transpose(b_im, (2, 0, 1)).astype(F32))

    def bmat(bb):
        m = jnp.transpose(bb, (1, 0, 2)).reshape(ngb, S5_GB, k, n)
        return _block_diag(m, True).astype(BF16)

    def cmat(cm):
        m = jnp.transpose(cm.astype(F32), (0, 2, 1)).reshape(ngb, S5_GB, n, k)
        return _block_diag(m, False).astype(BF16)

    def pw(p):
        return jnp.transpose(p.reshape(SUBLANES, ngb, S5_GB * n), (1, 0, 2))

    return dict(bbd_re=bmat(bb_re), bbd_im=bmat(bb_im), cbd_re=cmat(c_re), cbd_im=cmat(c_im),
                pw_re=pw(pw_re), pw_im=pw(pw_im), d_vec=d_skip.astype(F32).reshape(ngb, 1, S5_GB * k))


def kernel(x_prompt, x_sample, mem_prompt, cache_fox_k, cache_fox_v, cache_fox_logf, cache_mem_k, cache_mem_v,
           state_ssd, state_conv, state_s5_re, state_s5_im, page_table,
           norm_mix, norm_cross, norm_mem, norm_ffn,
           w_in_even, fox_b_forget, fox_q_norm, fox_k_norm, ssd_conv_w, ssd_conv_b, ssd_dt_bias, ssd_A_log,
           ssd_D, ssd_norm, w_out_even,
           w_in_odd, s5_A_re, s5_A_im, s5_B_re, s5_B_im, s5_C_re, s5_C_im, s5_D, s5_log_dt, s5_w_glu,
           w_mq, w_mkv, mem_q_norm, mem_k_norm, w_mo, w_ffn_up, w_ffn_down):
    depth = norm_mix.shape[0]
    n_even, n_odd = w_in_even.shape[0], w_in_odd.shape[0]
    b, l, d = x_prompt.shape
    W = {
        "norm_mix": norm_mix, "norm_cross": norm_cross, "norm_ffn": norm_ffn,
        "w_in_even_packed": [_pack_even_weight(w_in_even[j]) for j in range(n_even)],
        "fox_b_forget": fox_b_forget, "fox_q_norm": fox_q_norm, "fox_k_norm": fox_k_norm,
        "ssd_conv_w": ssd_conv_w, "ssd_conv_b": ssd_conv_b, "ssd_dt_bias": ssd_dt_bias, "ssd_A_log": ssd_A_log,
        "ssd_D": ssd_D, "ssd_norm": ssd_norm, "w_out_even": w_out_even,
        "w_in_odd": w_in_odd, "s5_w_glu": s5_w_glu,
        "s5_packed": [_pack_s5(s5_A_re[j], s5_A_im[j], s5_B_re[j], s5_B_im[j], s5_C_re[j], s5_C_im[j],
                               s5_D[j], s5_log_dt[j]) for j in range(n_odd)],
        "w_mq": w_mq, "mem_q_norm": mem_q_norm, "w_mo": w_mo, "w_ffn_up": w_ffn_up, "w_ffn_down": w_ffn_down,
    }
    mt = mem_prompt.shape[1]
    mw = w_mkv.shape[2] // 2
    mem2 = mem_prompt.reshape(b * mt, d)
    mk, mv = [], []
    for i in range(depth):
        (k_i,) = _dense(mem2, w_mkv[i], n_cols=mw, tn=mw, col0=0, pro="norm", gain=norm_mem[i],
                        epi="headnorm", head_gain=mem_k_norm[i])
        (v_i,) = _dense(mem2, w_mkv[i], n_cols=mw, tn=mw, col0=mw, pro="norm", gain=norm_mem[i])
        mk.append(k_i.reshape(b, mt, mw))
        mv.append(v_i.reshape(b, mt, mw))
    mem_k_p = jnp.stack(mk)
    mem_v_p = jnp.stack(mv)
    n_grp = s5_A_re.shape[1]
    (y_prompt, fox_k_p, fox_v_p, fox_logf_p, ssd_p, conv_p, s5_re_p, s5_im_p) = _trunk(
        x_prompt, W, mem_k_p, mem_v_p,
        jnp.zeros((n_even, b, SSD_CONV - 1, CONV_DIM), F32),
        jnp.zeros((n_even, b, SSD_HEADS, SSD_HD, SSD_STATE), F32),
        jnp.zeros((n_odd, b, n_grp, S5_STATE), F32),
        jnp.zeros((n_odd, b, n_grp, S5_STATE), F32),
        None)
    db = x_sample.shape[0]
    n_pool, page = cache_fox_k.shape[1], cache_fox_k.shape[2]
    fox_cache = (cache_fox_k.reshape(n_even, n_pool, page, FOX_W),
                 cache_fox_v.reshape(n_even, n_pool, page, FOX_W),
                 jnp.transpose(cache_fox_logf.astype(F32), (0, 1, 3, 2)), page_table)
    (y_sample, fox_k_s, fox_v_s, fox_logf_s, ssd_s, conv_s, s5_re_s, s5_im_s) = _trunk(
        x_sample, W, cache_mem_k.reshape(depth, db, mt, mw), cache_mem_v.reshape(depth, db, mt, mw),
        state_conv, state_ssd, state_s5_re, state_s5_im, fox_cache)
    hd = mw // MEM_HEADS
    return (y_prompt, y_sample,
            fox_k_p, fox_v_p, fox_logf_p,
            mem_k_p.reshape(depth, b, mt, MEM_HEADS, hd), mem_v_p.reshape(depth, b, mt, MEM_HEADS, hd),
            ssd_p, conv_p, s5_re_p, s5_im_p,
            fox_k_s, fox_v_s, fox_logf_s, ssd_s, conv_s, s5_re_s, s5_im_s)
```

```python
import functools
import math

import jax
import jax.numpy as jnp
from jax import lax
from jax.experimental import pallas as pl
from jax.experimental.pallas import tpu as pltpu

F32 = jnp.float32
BF16 = jnp.bfloat16
EPS = 1e-6
LANES = 128
SUBLANES = 8
VMEM_LIMIT_BYTES = 56 * 1024 * 1024

FOX_HEADS = 8
HEAD_DIM = 128
FOX_W = FOX_HEADS * HEAD_DIM
SSD_HEADS = 32
SSD_HD = 64
SSD_GROUPS = 4
SSD_STATE = 128
SSD_CHUNK = 128
SSD_INNER = SSD_HEADS * SSD_HD
SSD_CONV = 4
CONV_DIM = SSD_INNER + 2 * SSD_GROUPS * SSD_STATE
DT_LANE0 = FOX_HEADS
S5_GROUP = 16
S5_STATE = 64
S5_GB = 16
MEM_HEADS = 4
HIGHEST = lax.Precision.HIGHEST


def _cparams(sem):
    return pltpu.CompilerParams(dimension_semantics=sem, vmem_limit_bytes=VMEM_LIMIT_BYTES)


def _gelu_tanh(x):
    return 0.5 * x * (1.0 + jnp.tanh(math.sqrt(2.0 / math.pi) * (x + 0.044715 * x * x * x)))


def _softplus(x):
    return jnp.maximum(x, 0.0) + jnp.log1p(jnp.exp(-jnp.abs(x)))


def _silu(x):
    return x * jax.nn.sigmoid(x)


def _rms(x, gain):
    return x * lax.rsqrt(jnp.mean(x * x, axis=-1, keepdims=True) + EPS) * gain


def _dot(a, b):
    return jnp.dot(a, b, preferred_element_type=F32)


def _dot_nt(a, b):
    return lax.dot_general(a, b, (((1,), (1,)), ((), ())), preferred_element_type=F32)


def _dot_exact(a, b):
    return jnp.dot(a, b, preferred_element_type=F32, precision=HIGHEST)


def _dense_kernel(*refs, pro, epi, n_out):
    it = iter(refs)
    x_ref = next(it)
    gain_ref = next(it) if pro == "norm" else None
    w_ref = next(it)
    w2_ref = next(it) if epi in ("swiglu", "glu_res") else None
    res_ref = next(it) if epi in ("residual", "glu_res") else None
    hg_ref = next(it) if epi == "headnorm" else None
    out_refs = [next(it) for _ in range(n_out)]
    xs_ref = next(it) if pro != "none" else None

    if pro != "none":
        @pl.when(pl.program_id(1) == 0)
        def _():
            xf = x_ref[...].astype(F32)
            if pro == "norm":
                xf = _rms(xf, gain_ref[...])
            elif pro == "gelu":
                xf = _gelu_tanh(xf)
            xs_ref[...] = xf.astype(BF16)
        lhs = xs_ref[...]
    else:
        lhs = x_ref[...]

    acc = _dot(lhs, w_ref[...].astype(BF16))
    if epi == "swiglu":
        acc = _silu(acc) * _dot(lhs, w2_ref[...].astype(BF16))
    elif epi == "glu_res":
        acc = res_ref[...] + acc * jax.nn.sigmoid(_dot(lhs, w2_ref[...].astype(BF16)))
    elif epi == "residual":
        acc = res_ref[...] + acc

    if epi == "headnorm":
        for c in range(acc.shape[1] // HEAD_DIM):
            sl = slice(c * HEAD_DIM, (c + 1) * HEAD_DIM)
            blk = _rms(acc[:, sl], hg_ref[...])
            for o in out_refs:
                o[:, sl] = blk.astype(o.dtype)
    else:
        for o in out_refs:
            o[...] = acc.astype(o.dtype)


def _wspec(w, layer, k, tn, blk0):
    if w.ndim == 2:
        return pl.BlockSpec((k, tn), lambda i, j: (0, blk0 + j))
    return pl.BlockSpec((None, k, tn), lambda i, j: (layer, 0, blk0 + j))


def _dense(x, w, *, n_cols, tn, name, layer=0, col0=0, col0_2=None, pro="none", gain=None, epi="plain",
           res=None, head_gain=None, out_dtypes=(F32,), tm_max=1024):
    m, k = x.shape
    tm = min(tm_max, m)
    assert m % tm == 0 and n_cols % tn == 0 and col0 % tn == 0
    b0 = col0 // tn
    in_specs = [pl.BlockSpec((tm, k), lambda i, j: (i, 0))]
    args = [x]
    if pro == "norm":
        in_specs.append(pl.BlockSpec((1, k), lambda i, j: (0, 0)))
        args.append(gain.reshape(1, k).astype(F32))
    in_specs.append(_wspec(w, layer, k, tn, b0))
    args.append(w)
    if epi in ("swiglu", "glu_res"):
        assert col0_2 % tn == 0
        in_specs.append(_wspec(w, layer, k, tn, col0_2 // tn))
        args.append(w)
    if epi in ("residual", "glu_res"):
        in_specs.append(pl.BlockSpec((tm, tn), lambda i, j: (i, j)))
        args.append(res)
    if epi == "headnorm":
        in_specs.append(pl.BlockSpec((1, HEAD_DIM), lambda i, j: (0, 0)))
        args.append(head_gain.reshape(1, HEAD_DIM).astype(F32))
    out_shape = [jax.ShapeDtypeStruct((m, n_cols), dt) for dt in out_dtypes]
    out_specs = [pl.BlockSpec((tm, tn), lambda i, j: (i, j)) for _ in out_dtypes]
    scratch = [pltpu.VMEM((tm, k), BF16)] if pro != "none" else []
    outs = pl.pallas_call(
        functools.partial(_dense_kernel, pro=pro, epi=epi, n_out=len(out_dtypes)),
        grid=(m // tm, n_cols // tn),
        in_specs=in_specs, out_specs=out_specs, out_shape=out_shape,
        scratch_shapes=scratch,
        compiler_params=_cparams(("parallel", "arbitrary")),
        name=name,
    )(*args)
    return outs


def _inproj_even_kernel(x_ref, g_ref, wqkv_ref, wzx_ref, wf_ref, wdt_ref, qg_ref, kg_ref,
                        q_ref, kf_ref, kb_ref, vf_ref, vb_ref, zx_ref, tail_ref, xs_ref, *, nq, nzx):
    j = pl.program_id(1)

    @pl.when(j == 0)
    def _():
        xs_ref[...] = _rms(x_ref[...], g_ref[...]).astype(BF16)

    def headnorm(acc, gain, outs):
        for c in range(acc.shape[1] // HEAD_DIM):
            sl = slice(c * HEAD_DIM, (c + 1) * HEAD_DIM)
            blk = _rms(acc[:, sl], gain)
            for o in outs:
                o[:, sl] = blk.astype(o.dtype)

    @pl.when(j < nq)
    def _():
        headnorm(_dot(xs_ref[...], wqkv_ref[...].astype(BF16)), qg_ref[...], (q_ref,))

    @pl.when((j >= nq) & (j < 2 * nq))
    def _():
        headnorm(_dot(xs_ref[...], wqkv_ref[...].astype(BF16)), kg_ref[...], (kf_ref, kb_ref))

    @pl.when((j >= 2 * nq) & (j < 3 * nq))
    def _():
        acc = _dot(xs_ref[...], wqkv_ref[...].astype(BF16))
        vf_ref[...] = acc
        vb_ref[...] = acc.astype(BF16)

    @pl.when((j >= 3 * nq) & (j < 3 * nq + nzx))
    def _():
        zx_ref[...] = _dot(xs_ref[...], wzx_ref[...].astype(BF16))

    @pl.when(j == 3 * nq + nzx)
    def _():
        lane = lax.broadcasted_iota(jnp.int32, wf_ref.shape, 1)
        wt = jnp.where(lane < DT_LANE0, wf_ref[...],
                       jnp.where(lane < DT_LANE0 + SSD_HEADS, wdt_ref[...], 0.0))
        tail_ref[...] = _dot(xs_ref[...], wt.astype(BF16))


def _inproj_even(x, gain, w_all, w_zx, layer, q_gain, k_gain, tn=512, tm_max=1024):
    m, k = x.shape
    tm = min(tm_max, m)
    nq = FOX_W // tn
    nzx = w_zx.shape[1] // tn
    f_blk = 3 * FOX_W // LANES
    dt_blk = (3 * FOX_W + FOX_HEADS + SSD_INNER + CONV_DIM) // LANES
    assert (3 * FOX_W + FOX_HEADS + SSD_INNER + CONV_DIM) % LANES == DT_LANE0

    def clip(j, lo, n):
        return jnp.clip(j - lo, 0, n - 1)

    tile = lambda lo, n: pl.BlockSpec((tm, tn), lambda i, j: (i, clip(j, lo, n)))
    outs = pl.pallas_call(
        functools.partial(_inproj_even_kernel, nq=nq, nzx=nzx),
        grid=(m // tm, 3 * nq + nzx + 1),
        in_specs=[pl.BlockSpec((tm, k), lambda i, j: (i, 0), pipeline_mode=pl.Buffered(1)),
                  pl.BlockSpec((1, k), lambda i, j: (0, 0)),
                  pl.BlockSpec((None, k, tn), lambda i, j: (layer, 0, clip(j, 0, 3 * nq))),
                  pl.BlockSpec((k, tn), lambda i, j: (0, clip(j, 3 * nq, nzx))),
                  pl.BlockSpec((None, k, LANES), lambda i, j: (layer, 0, f_blk), pipeline_mode=pl.Buffered(1)),
                  pl.BlockSpec((None, k, LANES), lambda i, j: (layer, 0, dt_blk), pipeline_mode=pl.Buffered(1)),
                  pl.BlockSpec((1, HEAD_DIM), lambda i, j: (0, 0)),
                  pl.BlockSpec((1, HEAD_DIM), lambda i, j: (0, 0))],
        out_specs=[tile(0, nq), tile(nq, nq), tile(nq, nq), tile(2 * nq, nq), tile(2 * nq, nq),
                   tile(3 * nq, nzx), pl.BlockSpec((tm, LANES), lambda i, j: (i, 0))],
        out_shape=[jax.ShapeDtypeStruct((m, FOX_W), BF16),
                   jax.ShapeDtypeStruct((m, FOX_W), F32), jax.ShapeDtypeStruct((m, FOX_W), BF16),
                   jax.ShapeDtypeStruct((m, FOX_W), F32), jax.ShapeDtypeStruct((m, FOX_W), BF16),
                   jax.ShapeDtypeStruct((m, nzx * tn), F32), jax.ShapeDtypeStruct((m, LANES), F32)],
        scratch_shapes=[pltpu.VMEM((tm, k), BF16)],
        compiler_params=_cparams(("parallel", "arbitrary")),
        name="inproj_even",
    )(x, gain.reshape(1, k), w_all, w_zx, w_all, w_all, q_gain.reshape(1, HEAD_DIM), k_gain.reshape(1, HEAD_DIM))
    return outs


def _outproj_kernel(o_ref, y_ref, g_ref, w_ref, res_ref, out_ref, yn_ref):
    k1 = o_ref.shape[1]

    @pl.when(pl.program_id(1) == 0)
    def _():
        yn_ref[...] = _rms(y_ref[...], g_ref[...]).astype(BF16)
    acc = _dot(o_ref[...], w_ref[:k1, :].astype(BF16)) + _dot(yn_ref[...], w_ref[k1:, :].astype(BF16))
    out_ref[...] = res_ref[...] + acc


def _outproj_even(o_fox, yg, gain, w_out, layer, res, tn=256, tm_max=1024):
    m, d = res.shape
    tm = min(tm_max, m)
    k1, k2 = o_fox.shape[1], yg.shape[1]
    return pl.pallas_call(
        _outproj_kernel,
        grid=(m // tm, d // tn),
        in_specs=[
            pl.BlockSpec((tm, k1), lambda i, j: (i, 0)),
            pl.BlockSpec((tm, k2), lambda i, j: (i, 0)),
            pl.BlockSpec((1, k2), lambda i, j: (0, 0)),
            pl.BlockSpec((None, k1 + k2, tn), lambda i, j: (layer, 0, j)),
            pl.BlockSpec((tm, tn), lambda i, j: (i, j)),
        ],
        out_specs=pl.BlockSpec((tm, tn), lambda i, j: (i, j)),
        out_shape=jax.ShapeDtypeStruct((m, d), F32),
        scratch_shapes=[pltpu.VMEM((tm, k2), BF16)],
        compiler_params=_cparams(("parallel", "arbitrary")),
        name="outproj_even",
    )(o_fox, yg, gain.reshape(1, k2).astype(F32), w_out, res)


def _lane_cumsum(x):
    lane = lax.broadcasted_iota(jnp.int32, x.shape, 1)
    k = 1
    while k < LANES:
        x = x + jnp.where(lane >= k, pltpu.roll(x, k, axis=1), 0.0)
        k *= 2
    return x


def _forget_kernel(raw_ref, b_ref, lf_ref, negc_ref, ctm_ref):
    lp = raw_ref.shape[1]
    lf = -_softplus(-(raw_ref[0] + b_ref[...]))
    lf_ref[0] = lf[:, :FOX_HEADS]
    row_i = lax.broadcasted_iota(jnp.int32, (LANES, LANES), 0)
    col_i = lax.broadcasted_iota(jnp.int32, (LANES, LANES), 1)
    tri = (col_i <= row_i).astype(F32)
    ctm_ref[0] = _dot_exact(tri, lf[:LANES, :])[:, :FOX_HEADS]
    carry = jnp.zeros((FOX_HEADS, 1), F32)
    for c in range(lp // LANES):
        blk = lf[c * LANES:(c + 1) * LANES, :].T[:FOX_HEADS, :]
        cs = _lane_cumsum(blk) + carry
        negc_ref[0, :, c * LANES:(c + 1) * LANES] = -cs
        carry = cs[:, LANES - 1:LANES]


def _forget(raw, b_vec):
    b, lp, _ = raw.shape
    return pl.pallas_call(
        _forget_kernel,
        grid=(b,),
        in_specs=[pl.BlockSpec((1, lp, LANES), lambda i: (i, 0, 0)),
                  pl.BlockSpec((1, LANES), lambda i: (0, 0))],
        out_specs=[pl.BlockSpec((1, lp, FOX_HEADS), lambda i: (i, 0, 0)),
                   pl.BlockSpec((1, FOX_HEADS, lp), lambda i: (i, 0, 0)),
                   pl.BlockSpec((1, LANES, FOX_HEADS), lambda i: (i, 0, 0))],
        out_shape=[jax.ShapeDtypeStruct((b, lp, FOX_HEADS), F32),
                   jax.ShapeDtypeStruct((b, FOX_HEADS, lp), F32),
                   jax.ShapeDtypeStruct((b, LANES, FOX_HEADS), F32)],
        compiler_params=_cparams(("parallel",)),
        name="forget_gates",
    )(raw, b_vec)


def _fox_prompt_kernel(q_ref, k_ref, v_ref, nb_ref, o_ref, m_sc, l_sc, acc_sc, *, tq, tk):
    h = pl.program_id(1)
    qi = pl.program_id(2)
    kj = pl.program_id(3)

    @pl.when(kj == 0)
    def _():
        m_sc[...] = jnp.full(m_sc.shape, -jnp.inf, F32)
        l_sc[...] = jnp.zeros(l_sc.shape, F32)
        acc_sc[...] = jnp.zeros(acc_sc.shape, F32)

    @pl.when(kj * tk <= qi * tq + (tq - 1))
    def _():
        s = _dot_nt(q_ref[0], k_ref[0]) * (HEAD_DIM ** -0.5)
        s = s + nb_ref[0, pl.ds(h, 1), :]
        row = qi * tq + lax.broadcasted_iota(jnp.int32, s.shape, 0)
        col = kj * tk + lax.broadcasted_iota(jnp.int32, s.shape, 1)
        s = jnp.where(col <= row, s, -jnp.inf)
        m_old = m_sc[...]
        m_new = jnp.maximum(m_old, jnp.max(s, axis=-1, keepdims=True))
        alpha = jnp.exp(m_old - m_new)
        p = jnp.exp(s - m_new)
        l_sc[...] = alpha * l_sc[...] + jnp.sum(p, axis=-1, keepdims=True)
        acc_sc[...] = alpha * acc_sc[...] + _dot(p.astype(BF16), v_ref[0])
        m_sc[...] = m_new

    @pl.when(kj == pl.num_programs(3) - 1)
    def _():
        o_ref[0] = (acc_sc[...] / l_sc[...]).astype(o_ref.dtype)


def _fox_prompt(q, k, v, negc, tq=512, tk=512):
    b, l, _ = q.shape
    tq, tk = min(tq, l), min(tk, l)
    nq, nk = l // tq, l // tk

    def kv_map(bi, h, qi, kj):
        return (bi, jnp.minimum(kj, (qi * tq + tq - 1) // tk), h)

    def nb_map(bi, h, qi, kj):
        return (bi, 0, jnp.minimum(kj, (qi * tq + tq - 1) // tk))

    return pl.pallas_call(
        functools.partial(_fox_prompt_kernel, tq=tq, tk=tk),
        grid=(b, FOX_HEADS, nq, nk),
        in_specs=[pl.BlockSpec((1, tq, HEAD_DIM), lambda bi, h, qi, kj: (bi, qi, h)),
                  pl.BlockSpec((1, tk, HEAD_DIM), kv_map),
                  pl.BlockSpec((1, tk, HEAD_DIM), kv_map),
                  pl.BlockSpec((1, FOX_HEADS, tk), nb_map)],
        out_specs=pl.BlockSpec((1, tq, HEAD_DIM), lambda bi, h, qi, kj: (bi, qi, h)),
        out_shape=jax.ShapeDtypeStruct((b, l, FOX_W), BF16),
        scratch_shapes=[pltpu.VMEM((tq, 1), F32), pltpu.VMEM((tq, 1), F32),
                        pltpu.VMEM((tq, HEAD_DIM), F32)],
        compiler_params=_cparams(("parallel", "parallel", "parallel", "arbitrary")),
        name="fox_prompt",
    )(q, k, v, negc)


def _pool_cumsum_kernel(x_ref, o_ref):
    n = x_ref.shape[0]
    row_i = lax.broadcasted_iota(jnp.int32, (n, n), 0)
    col_i = lax.broadcasted_iota(jnp.int32, (n, n), 1)
    o_ref[...] = _dot_exact((col_i <= row_i).astype(F32), x_ref[...])


def _pool_cumsum(x, tc=2048):
    nl, page, cols = x.shape
    tc = tc if cols % tc == 0 else cols
    return pl.pallas_call(
        _pool_cumsum_kernel,
        grid=(nl, cols // tc),
        in_specs=[pl.BlockSpec((None, page, tc), lambda i, j: (i, 0, j))],
        out_specs=pl.BlockSpec((None, page, tc), lambda i, j: (i, 0, j)),
        out_shape=jax.ShapeDtypeStruct((nl, page, cols), F32),
        compiler_params=_cparams(("parallel", "parallel")),
        name="pool_logf_cumsum",
    )(x)


def _fox_decode_kernel(pt_ref, q_ref, *refs, gp, t_new):
    kv_refs = refs[:4 * gp]
    kn_ref, vn_ref, cn_ref, o_ref, m_sc, l_sc, acc_sc, carry_sc = refs[4 * gp:]
    p = pl.program_id(1)
    n_past = pl.num_programs(1) - 1
    rows = q_ref.shape[1]
    scale = HEAD_DIM ** -0.5

    @pl.when(p == 0)
    def _():
        m_sc[...] = jnp.full(m_sc.shape, -jnp.inf, F32)
        l_sc[...] = jnp.zeros(l_sc.shape, F32)
        acc_sc[...] = jnp.zeros(acc_sc.shape, F32)
        carry_sc[...] = jnp.zeros(carry_sc.shape, F32)

    def update(s, v, state):
        m_old, l_old, acc = state
        m_new = jnp.maximum(m_old, jnp.max(s, axis=-1, keepdims=True))
        alpha = jnp.exp(m_old - m_new)
        pr = jnp.exp(s - m_new)
        l_new = alpha * l_old + jnp.sum(pr, axis=-1, keepdims=True)
        acc = alpha * acc + _dot(pr.astype(BF16), v.astype(BF16))
        return m_new, l_new, acc

    @pl.when(p < n_past)
    def _():
        q = q_ref[0]
        width = kv_refs[0].shape[0]
        row = lax.broadcasted_iota(jnp.int32, (rows, width), 0)
        col = lax.broadcasted_iota(jnp.int32, (rows, width), 1)
        own_head = (col % FOX_HEADS) == (row // t_new)
        state = (m_sc[...], l_sc[...], acc_sc[...])
        carry = carry_sc[...]
        for g in range(gp):
            k_ref, v_ref, loc_ref, tot_ref = kv_refs[4 * g:4 * g + 4]
            s = _dot_nt(q, k_ref[...].astype(BF16)) * scale - (carry + loc_ref[...])
            state = update(jnp.where(own_head, s, -jnp.inf), v_ref[...], state)
            carry = carry + tot_ref[...]
        m_sc[...], l_sc[...], acc_sc[...] = state
        carry_sc[...] = carry

    @pl.when(p == n_past)
    def _():
        width = kn_ref.shape[1]
        row = lax.broadcasted_iota(jnp.int32, (rows, width), 0)
        col = lax.broadcasted_iota(jnp.int32, (rows, width), 1)
        keep = ((col % FOX_HEADS) == (row // t_new)) & ((col // FOX_HEADS) <= (row % t_new))
        s = _dot_nt(q_ref[0], kn_ref[0].astype(BF16)) * scale - (carry_sc[:, :width] + cn_ref[0])
        m, l, acc = update(jnp.where(keep, s, -jnp.inf), vn_ref[0], (m_sc[...], l_sc[...], acc_sc[...]))
        o_ref[0] = (acc / l).astype(o_ref.dtype)


def _fox_decode(q, k_new, v_new, c_new, k_pool, v_pool, loc, tot, page_table, layer, t_new, gp=8):
    b, rows, _ = q.shape
    n_pages = page_table.shape[1]
    gp = max(g for g in range(1, gp + 1) if n_pages % g == 0)
    n_steps = n_pages // gp
    width = k_pool.shape[2]

    def page_map(g):
        return lambda bi, p, pt: (layer, pt[bi, jnp.minimum(p, n_steps - 1) * gp + g], 0, 0)

    in_specs = [pl.BlockSpec((1, rows, HEAD_DIM), lambda bi, p, pt: (bi, 0, 0))]
    args = [q]
    for g in range(gp):
        in_specs += [pl.BlockSpec((None, None, width, HEAD_DIM), page_map(g)),
                     pl.BlockSpec((None, None, width, HEAD_DIM), page_map(g)),
                     pl.BlockSpec((None, None, 1, width), page_map(g)),
                     pl.BlockSpec((None, None, 1, width), page_map(g))]
        args += [k_pool, v_pool, loc, tot]
    new_spec = pl.BlockSpec((1, k_new.shape[1], HEAD_DIM), lambda bi, p, pt: (bi, 0, 0))
    in_specs += [new_spec, new_spec, pl.BlockSpec((1, 1, c_new.shape[2]), lambda bi, p, pt: (bi, 0, 0))]
    args += [k_new, v_new, c_new]
    grid_spec = pltpu.PrefetchScalarGridSpec(
        num_scalar_prefetch=1,
        grid=(b, n_steps + 1),
        in_specs=in_specs,
        out_specs=pl.BlockSpec((1, rows, HEAD_DIM), lambda bi, p, pt: (bi, 0, 0)),
        scratch_shapes=[pltpu.VMEM((rows, 1), F32), pltpu.VMEM((rows, 1), F32),
                        pltpu.VMEM((rows, HEAD_DIM), F32), pltpu.VMEM((1, width), F32)],
    )
    return pl.pallas_call(
        functools.partial(_fox_decode_kernel, gp=gp, t_new=t_new), grid_spec=grid_spec,
        out_shape=jax.ShapeDtypeStruct((b, rows, HEAD_DIM), BF16),
        compiler_params=_cparams(("parallel", "arbitrary")),
        name="fox_decode",
    )(page_table, *args)


def _conv_kernel(x_ref, buf_ref, w_ref, b_ref, y_ref, nb_ref, full_sc):
    l = x_ref.shape[1]
    pad = SUBLANES
    full_sc[pad - (SSD_CONV - 1):pad, :] = buf_ref[0]
    full_sc[pad:pad + l, :] = x_ref[0]
    acc = b_ref[...] + full_sc[pad - 3:pad - 3 + l, :] * w_ref[0:1, :]
    for kk in range(1, SSD_CONV):
        acc = acc + full_sc[pad - 3 + kk:pad - 3 + kk + l, :] * w_ref[kk:kk + 1, :]
    y_ref[0] = _silu(acc)
    nb_ref[0] = full_sc[pad + l - (SSD_CONV - 1):pad + l, :]


def _conv(zx, col0, buf, w, bias, tc=512):
    b, l, _ = zx.shape
    c = w.shape[1]
    cb0 = col0 // tc
    return pl.pallas_call(
        _conv_kernel,
        grid=(b, c // tc),
        in_specs=[pl.BlockSpec((1, l, tc), lambda i, j: (i, 0, cb0 + j)),
                  pl.BlockSpec((1, SSD_CONV - 1, tc), lambda i, j: (i, 0, j)),
                  pl.BlockSpec((SSD_CONV, tc), lambda i, j: (0, j)),
                  pl.BlockSpec((1, tc), lambda i, j: (0, j))],
        out_specs=[pl.BlockSpec((1, l, tc), lambda i, j: (i, 0, j)),
                   pl.BlockSpec((1, SSD_CONV - 1, tc), lambda i, j: (i, 0, j))],
        out_shape=[jax.ShapeDtypeStruct((b, l, c), F32),
                   jax.ShapeDtypeStruct((b, SSD_CONV - 1, c), F32)],
        scratch_shapes=[pltpu.VMEM((l + SUBLANES, tc), F32)],
        compiler_params=_cparams(("parallel", "parallel")),
        name="ssd_conv",
    )(zx, buf, w, bias.reshape(1, c))


def _ssd_kernel(xc_ref, tail_ref, z_ref, dtb_ref, alog_ref, dexp_ref, e_ref, h0_ref,
                y_ref, hout_ref, h_sc, *, valid_len):
    c = pl.program_id(1)
    q = SSD_CHUNK
    gw = SSD_INNER // SSD_GROUPS
    hpg = SSD_HEADS // SSD_GROUPS

    @pl.when(c == 0)
    def _():
        h_sc[...] = h0_ref[0]

    lane = lax.broadcasted_iota(jnp.int32, (q, LANES), 1)
    row = lax.broadcasted_iota(jnp.int32, (q, LANES), 0)
    col_i = lax.broadcasted_iota(jnp.int32, (q, q), 1)
    row_i = lax.broadcasted_iota(jnp.int32, (q, q), 0)
    causal = col_i <= row_i
    tri = causal.astype(F32)

    dt = _softplus(tail_ref[0] + dtb_ref[...])
    live = (lane >= DT_LANE0) & (lane < DT_LANE0 + SSD_HEADS) & (c * q + row < valid_len)
    dt = jnp.where(live, dt, 0.0)
    a = dt * (-jnp.exp(alog_ref[...]))
    a_cum = _dot_exact(tri, a)
    a_cum_t = a_cum.T
    a_last = a_cum[q - 1:q, :]
    ea = jnp.exp(a_cum)
    w1 = dt * jnp.exp(a_last - a_cum)

    for g in range(SSD_GROUPS):
        e = e_ref[g]
        dt_e = _dot_exact(dt, e)
        w1_e = _dot_exact(w1, e)
        ea_e = _dot_exact(ea, e)
        xs = xc_ref[0, :, g * gw:(g + 1) * gw]
        bb = xc_ref[0, :, SSD_INNER + g * SSD_STATE:SSD_INNER + (g + 1) * SSD_STATE].astype(BF16)
        cc = xc_ref[0, :, SSD_INNER + (SSD_GROUPS + g) * SSD_STATE:
                    SSD_INNER + (SSD_GROUPS + g + 1) * SSD_STATE].astype(BF16)
        cb = _dot_nt(cc, bb)
        xdt = xs * dt_e
        hg = h_sc[g * gw:(g + 1) * gw, :]
        y = _dot_nt(cc, hg.astype(BF16)) * ea_e
        st = _dot((xs * w1_e).T.astype(BF16), bb)
        for hp in range(hpg // 2):
            pair = xdt[:, hp * LANES:(hp + 1) * LANES]
            lane_p = lax.broadcasted_iota(jnp.int32, pair.shape, 1)
            yp = None
            for sub in range(2):
                h = 2 * hp + sub
                ln = DT_LANE0 + g * hpg + h
                seg = a_cum[:, ln:ln + 1] - a_cum_t[ln:ln + 1, :]
                dec = jnp.exp(jnp.where(causal, seg, -jnp.inf))
                mat = (cb * dec).astype(BF16)
                in_head = (lane_p >= sub * SSD_HD) & (lane_p < (sub + 1) * SSD_HD)
                rhs = jnp.where(in_head, pair, 0.0).astype(BF16)
                part = _dot(mat, rhs)
                yp = part if yp is None else yp + part
                r0 = g * gw + h * SSD_HD
                h_sc[r0:r0 + SSD_HD, :] = (hg[h * SSD_HD:(h + 1) * SSD_HD, :] * jnp.exp(a_last[:, ln:ln + 1])
                                           + st[h * SSD_HD:(h + 1) * SSD_HD, :])
            cs = slice(g * gw + hp * LANES, g * gw + (hp + 1) * LANES)
            yt = yp + y[:, hp * LANES:(hp + 1) * LANES] + dexp_ref[:, cs] * xs[:, hp * LANES:(hp + 1) * LANES]
            y_ref[0, :, cs] = yt * _silu(z_ref[0, :, cs])

    @pl.when(c == pl.num_programs(1) - 1)
    def _():
        hout_ref[0] = h_sc[...]


def _ssd(xc, tail, zx, dtb_vec, alog_vec, d_exp, e_mat, h0, valid_len):
    b, lp, _ = xc.shape
    nc = lp // SSD_CHUNK
    return pl.pallas_call(
        functools.partial(_ssd_kernel, valid_len=valid_len),
        grid=(b, nc),
        in_specs=[pl.BlockSpec((1, SSD_CHUNK, CONV_DIM), lambda i, c: (i, c, 0)),
                  pl.BlockSpec((1, SSD_CHUNK, LANES), lambda i, c: (i, c, 0)),
                  pl.BlockSpec((1, SSD_CHUNK, SSD_INNER), lambda i, c: (i, c, 0)),
                  pl.BlockSpec((1, LANES), lambda i, c: (0, 0)),
                  pl.BlockSpec((1, LANES), lambda i, c: (0, 0)),
                  pl.BlockSpec((1, SSD_INNER), lambda i, c: (0, 0)),
                  pl.BlockSpec((SSD_GROUPS, LANES, SSD_INNER // SSD_GROUPS), lambda i, c: (0, 0, 0)),
                  pl.BlockSpec((1, SSD_INNER, SSD_STATE), lambda i, c: (i, 0, 0))],
        out_specs=[pl.BlockSpec((1, SSD_CHUNK, SSD_INNER), lambda i, c: (i, c, 0)),
                   pl.BlockSpec((1, SSD_INNER, SSD_STATE), lambda i, c: (i, 0, 0))],
        out_shape=[jax.ShapeDtypeStruct((b, lp, SSD_INNER), F32),
                   jax.ShapeDtypeStruct((b, SSD_INNER, SSD_STATE), F32)],
        scratch_shapes=[pltpu.VMEM((SSD_INNER, SSD_STATE), F32)],
        compiler_params=_cparams(("parallel", "arbitrary")),
        name="ssd_chunked",
    )(xc, tail, zx, dtb_vec, alog_vec, d_exp, e_mat, h0)


def _s5_prep_kernel(are_ref, aim_ref, ldt_ref, bre_ref, bim_ref, pwr_ref, pwi_ref, bbr_ref, bbi_ref):
    lam_re = jnp.minimum(are_ref[...], -1e-4)
    lam_im = aim_ref[...]
    dt = jnp.exp(ldt_ref[...])
    mag = jnp.exp(lam_re * dt)
    ang = lam_im * dt
    lb_re, lb_im = mag * jnp.cos(ang), mag * jnp.sin(ang)
    nr, ni = lb_re - 1.0, lb_im
    den = lam_re * lam_re + lam_im * lam_im
    coef_re = (nr * lam_re + ni * lam_im) / den
    coef_im = (ni * lam_re - nr * lam_im) / den
    for k in range(S5_GROUP):
        br, bi = bre_ref[k], bim_ref[k]
        bbr_ref[k] = coef_re * br - coef_im * bi
        bbi_ref[k] = coef_re * bi + coef_im * br
    pr, pi = lb_re, lb_im
    pwr_ref[0], pwi_ref[0] = pr, pi
    for r in range(1, SUBLANES):
        pr, pi = pr * lb_re - pi * lb_im, pr * lb_im + pi * lb_re
        pwr_ref[r], pwi_ref[r] = pr, pi


def _s5_prep(a_re, a_im, log_dt, b_re_t, b_im_t):
    g, n = a_re.shape
    return pl.pallas_call(
        _s5_prep_kernel,
        out_shape=[jax.ShapeDtypeStruct((SUBLANES, g, n), F32), jax.ShapeDtypeStruct((SUBLANES, g, n), F32),
                   jax.ShapeDtypeStruct((S5_GROUP, g, n), F32), jax.ShapeDtypeStruct((S5_GROUP, g, n), F32)],
        name="s5_discretise",
    )(a_re, a_im, log_dt.reshape(g, 1), b_re_t, b_im_t)


def _s5_kernel(u_ref, bre_ref, bim_ref, cre_ref, cim_ref, pwr_ref, pwi_ref, d_ref, s0r_ref, s0i_ref,
               g_ref, sr_ref, si_ref, xr_sc, xi_sc, *, tt):
    l = u_ref.shape[1]
    w = xr_sc.shape[1]
    pwr, pwi = pwr_ref[0], pwi_ref[0]
    rows = lax.broadcasted_iota(jnp.int32, (SUBLANES, w), 0)
    steps = []
    for kk in (1, 2, 4):
        keep = rows >= kk
        steps.append((kk,
                      jnp.where(keep, jnp.broadcast_to(pwr[kk - 1:kk, :], (SUBLANES, w)), 0.0),
                      jnp.where(keep, jnp.broadcast_to(pwi[kk - 1:kk, :], (SUBLANES, w)), 0.0)))

    def block(i, carry):
        cr, ci = carry
        r0 = pl.multiple_of(i * SUBLANES, SUBLANES)
        xr = xr_sc[pl.ds(r0, SUBLANES), :]
        xi = xi_sc[pl.ds(r0, SUBLANES), :]
        for kk, tr, ti in steps:
            sr = pltpu.roll(xr, kk, axis=0)
            si = pltpu.roll(xi, kk, axis=0)
            xr, xi = xr + (tr * sr - ti * si), xi + (tr * si + ti * sr)
        xr, xi = xr + (pwr * cr - pwi * ci), xi + (pwr * ci + pwi * cr)
        xr_sc[pl.ds(r0, SUBLANES), :] = xr
        xi_sc[pl.ds(r0, SUBLANES), :] = xi
        return xr[SUBLANES - 1:SUBLANES, :], xi[SUBLANES - 1:SUBLANES, :]

    def tile(t, carry):
        t0 = pl.multiple_of(t * tt, tt)
        u = u_ref[0, pl.ds(t0, tt), :]
        ub = u.astype(BF16)
        xr_sc[...] = _dot(ub, bre_ref[0])
        xi_sc[...] = _dot(ub, bim_ref[0])
        carry = lax.fori_loop(0, tt // SUBLANES, block, carry)
        y = _dot(xr_sc[...].astype(BF16), cre_ref[0]) - _dot(xi_sc[...].astype(BF16), cim_ref[0])
        y = y + d_ref[0] * u
        g_ref[0, pl.ds(t0, tt), :] = _gelu_tanh(y).astype(g_ref.dtype)
        return carry

    cr, ci = lax.fori_loop(0, l // tt, tile, (s0r_ref[0, 0], s0i_ref[0, 0]))
    sr_ref[0, 0] = cr
    si_ref[0, 0] = ci


def _s5(u, bbd_re, bbd_im, cbd_re, cbd_im, pw_re, pw_im, d_vec, s0_re, s0_im, tt=256):
    b, l, d = u.shape
    ngb = bbd_re.shape[0]
    cw = d // ngb
    sw = bbd_re.shape[2]
    tt = min(tt, l)
    state_spec = pl.BlockSpec((1, 1, 1, sw), lambda i, j: (i, j, 0, 0))
    return pl.pallas_call(
        functools.partial(_s5_kernel, tt=tt),
        grid=(b, ngb),
        in_specs=[pl.BlockSpec((1, l, cw), lambda i, j: (i, 0, j)),
                  pl.BlockSpec((1, cw, sw), lambda i, j: (j, 0, 0)),
                  pl.BlockSpec((1, cw, sw), lambda i, j: (j, 0, 0)),
                  pl.BlockSpec((1, sw, cw), lambda i, j: (j, 0, 0)),
                  pl.BlockSpec((1, sw, cw), lambda i, j: (j, 0, 0)),
                  pl.BlockSpec((1, SUBLANES, sw), lambda i, j: (j, 0, 0)),
                  pl.BlockSpec((1, SUBLANES, sw), lambda i, j: (j, 0, 0)),
                  pl.BlockSpec((1, 1, cw), lambda i, j: (j, 0, 0)),
                  state_spec, state_spec],
        out_specs=[pl.BlockSpec((1, l, cw), lambda i, j: (i, 0, j)), state_spec, state_spec],
        out_shape=[jax.ShapeDtypeStruct((b, l, d), BF16),
                   jax.ShapeDtypeStruct((b, ngb, 1, sw), F32),
                   jax.ShapeDtypeStruct((b, ngb, 1, sw), F32)],
        scratch_shapes=[pltpu.VMEM((tt, sw), F32), pltpu.VMEM((tt, sw), F32)],
        compiler_params=_cparams(("parallel", "parallel")),
        name="s5_scan",
    )(u, bbd_re, bbd_im, cbd_re, cbd_im, pw_re, pw_im, d_vec, s0_re, s0_im)


def _cross_kernel(x_ref, g_ref, wq_ref, qg_ref, k_ref, v_ref, wo_ref, o_ref, att_sc):
    x = x_ref[0]
    xn = _rms(x, g_ref[...]).astype(BF16)
    q = _dot(xn, wq_ref[...].astype(BF16))
    for h in range(MEM_HEADS):
        sl = slice(h * HEAD_DIM, (h + 1) * HEAD_DIM)
        qh = _rms(q[:, sl], qg_ref[...]).astype(BF16)
        s = _dot_nt(qh, k_ref[:, sl].astype(BF16)) * (HEAD_DIM ** -0.5)
        p = jnp.exp(s - jnp.max(s, axis=-1, keepdims=True))
        p = p / jnp.sum(p, axis=-1, keepdims=True)
        att_sc[:, sl] = _dot(p.astype(BF16), v_ref[:, sl].astype(BF16)).astype(BF16)
    o_ref[0] = x + _dot(att_sc[...], wo_ref[...].astype(BF16))


def _cross_attn(x, gain, w_q, q_gain, mem_k, mem_v, w_o, layer, tq=512):
    b, l, d = x.shape
    mt, mw = mem_k.shape[2], mem_k.shape[3]
    tq = min(tq, l)
    return pl.pallas_call(
        _cross_kernel,
        grid=(b, l // tq),
        in_specs=[pl.BlockSpec((1, tq, d), lambda i, j: (i, j, 0)),
                  pl.BlockSpec((1, d), lambda i, j: (0, 0)),
                  pl.BlockSpec((None, d, mw), lambda i, j: (layer, 0, 0)),
                  pl.BlockSpec((1, HEAD_DIM), lambda i, j: (0, 0)),
                  pl.BlockSpec((None, None, mt, mw), lambda i, j: (layer, i, 0, 0)),
                  pl.BlockSpec((None, None, mt, mw), lambda i, j: (layer, i, 0, 0)),
                  pl.BlockSpec((None, mw, d), lambda i, j: (layer, 0, 0))],
        out_specs=pl.BlockSpec((1, tq, d), lambda i, j: (i, j, 0)),
        out_shape=jax.ShapeDtypeStruct((b, l, d), F32),
        scratch_shapes=[pltpu.VMEM((tq, mw), BF16)],
        compiler_params=_cparams(("parallel", "parallel")),
        name="cross_attn",
    )(x, gain.reshape(1, d), w_q, q_gain.reshape(1, HEAD_DIM), mem_k, mem_v, w_o)


def _lane_vec(vals, lane0):
    v = jnp.zeros((1, LANES), F32)
    return v.at[0, lane0:lane0 + vals.shape[0]].set(vals.astype(F32))


def _head_expand():
    rows = jnp.arange(LANES)[None, :, None]
    cols = jnp.arange(SSD_INNER // SSD_GROUPS)[None, None, :]
    g = jnp.arange(SSD_GROUPS)[:, None, None]
    return (rows == DT_LANE0 + g * (SSD_HEADS // SSD_GROUPS) + cols // SSD_HD).astype(F32)


def _block_diag(m):
    eye = jnp.eye(S5_GB, dtype=m.dtype)
    bd = jnp.einsum("bgxy,gh->bgxhy", m, eye)
    nb, g, a, _, bb = bd.shape
    return bd.reshape(nb, g * a, g * bb)


def _pad_rows(x, rows):
    return jnp.pad(x, ((0, 0), (0, rows - x.shape[1]), (0, 0)))


def _trunk(x, W, mem_k, mem_v, conv0, ssm0, s5_re0, s5_im0, fox_cache):
    b, l, d = x.shape
    t = b * l
    depth = W["norm_mix"].shape[0]
    lp = max(l, SSD_CHUNK)
    e_mat = _head_expand()
    fk, fv, fl, hs, bufs, srs, sis = [], [], [], [], [], [], []
    x2 = x.reshape(t, d)
    for i in range(depth):
        j = i // 2
        if i % 2 == 0:
            q_bf, k_f, k_bf, v_f, v_bf, zx, tail = _inproj_even(
                x2, W["norm_mix"][i], W["w_in_even"], W["w_zx"][j], j, W["fox_q_norm"][j], W["fox_k_norm"][j])
            zx3 = zx.reshape(b, l, -1)
            tail3 = tail.reshape(b, l, LANES)
            tail_p = _pad_rows(tail3, lp) if lp != l else tail3
            lf, negc, c_tm = _forget(tail_p, _lane_vec(W["fox_b_forget"][j], 0))
            lf = lf[:, :l]
            if fox_cache is None:
                o_fox = _fox_prompt(q_bf.reshape(b, l, FOX_W), k_bf.reshape(b, l, FOX_W),
                                    v_bf.reshape(b, l, FOX_W), negc).reshape(t, FOX_W)
            else:
                k_pool, v_pool, loc, tot, page_table = fox_cache
                rows = FOX_HEADS * l
                q_ht = jnp.transpose(q_bf.reshape(b, l, FOX_HEADS, HEAD_DIM), (0, 2, 1, 3)).reshape(b, rows, HEAD_DIM)
                o_ht = _fox_decode(q_ht, _pad_rows(k_f.reshape(b, rows, HEAD_DIM), LANES),
                                   _pad_rows(v_f.reshape(b, rows, HEAD_DIM), LANES),
                                   _pad_rows(c_tm[:, :l].reshape(b, rows, 1), LANES).reshape(b, 1, LANES),
                                   k_pool, v_pool, loc, tot, page_table, j, l)
                o_fox = jnp.transpose(o_ht.reshape(b, FOX_HEADS, l, HEAD_DIM), (0, 2, 1, 3)).reshape(t, FOX_W)
            xc, new_buf = _conv(zx3, SSD_INNER, conv0[j], W["ssd_conv_w"][j], W["ssd_conv_b"][j])
            if lp != l:
                xc_p, z_p = _pad_rows(xc, lp), _pad_rows(zx3[:, :, :SSD_INNER], lp)
            else:
                xc_p, z_p = xc, zx3
            yg, h_last = _ssd(xc_p, tail_p, z_p, _lane_vec(W["ssd_dt_bias"][j], DT_LANE0),
                              _lane_vec(W["ssd_A_log"][j], DT_LANE0),
                              jnp.repeat(W["ssd_D"][j].astype(F32), SSD_HD).reshape(1, SSD_INNER),
                              e_mat, ssm0[j].reshape(b, SSD_INNER, SSD_STATE), l)
            yg = yg[:, :l].reshape(t, SSD_INNER)
            x2 = _outproj_even(o_fox, yg, W["ssd_norm"][j], W["w_out_even"], j, x2)
            fk.append(k_f.reshape(b, l, FOX_HEADS, HEAD_DIM))
            fv.append(v_f.reshape(b, l, FOX_HEADS, HEAD_DIM))
            fl.append(lf)
            hs.append(h_last.reshape(b, SSD_HEADS, SSD_HD, SSD_STATE))
            bufs.append(new_buf)
        else:
            (u,) = _dense(x2, W["w_in_odd"], layer=j, n_cols=d, tn=512, pro="norm", gain=W["norm_mix"][i],
                          name="inproj_odd")
            p = W["s5_packed"][j]
            ngb = p["bbd_re"].shape[0]
            g_bf, s_re, s_im = _s5(u.reshape(b, l, d), p["bbd_re"], p["bbd_im"], p["cbd_re"], p["cbd_im"],
                                   p["pw_re"], p["pw_im"], p["d_vec"],
                                   s5_re0[j].reshape(b, ngb, 1, -1), s5_im0[j].reshape(b, ngb, 1, -1))
            (x2,) = _dense(g_bf.reshape(t, d), W["s5_w_glu"], layer=j, n_cols=d, tn=512, col0=0, col0_2=d,
                           epi="glu_res", res=x2, name="s5_glu_out")
            srs.append(s_re.reshape(b, -1, S5_STATE))
            sis.append(s_im.reshape(b, -1, S5_STATE))
        x3 = _cross_attn(x2.reshape(b, l, d), W["norm_cross"][i], W["w_mq"], W["mem_q_norm"][i],
                         mem_k, mem_v, W["w_mo"], i)
        x2 = x3.reshape(t, d)
        hid = W["w_ffn_down"].shape[1]
        (hmid,) = _dense(x2, W["w_ffn_up"], layer=i, n_cols=hid, tn=512, col0=0, col0_2=hid, pro="norm",
                         gain=W["norm_ffn"][i], epi="swiglu", out_dtypes=(BF16,), name="ffn_up")
        (x2,) = _dense(hmid, W["w_ffn_down"], layer=i, n_cols=d, tn=256, epi="residual", res=x2, name="ffn_down")
    return (x2.reshape(b, l, d), jnp.stack(fk), jnp.stack(fv), jnp.stack(fl), jnp.stack(hs), jnp.stack(bufs),
            jnp.stack(srs), jnp.stack(sis))


def _pack_s5(a_re, a_im, b_re, b_im, c_re, c_im, d_skip, log_dt):
    g, n, k = b_re.shape
    ngb = g // S5_GB
    pw_re, pw_im, bb_re, bb_im = _s5_prep(a_re.astype(F32), a_im.astype(F32), log_dt.astype(F32),
                                          jnp.transpose(b_re, (2, 0, 1)).astype(F32),
                                          jnp.transpose(b_im, (2, 0, 1)).astype(F32))

    def bmat(bb):
        return _block_diag(jnp.transpose(bb, (1, 0, 2)).reshape(ngb, S5_GB, k, n)).astype(BF16)

    def cmat(cm):
        return _block_diag(jnp.transpose(cm.astype(F32), (0, 2, 1)).reshape(ngb, S5_GB, n, k)).astype(BF16)

    def pw(p):
        return jnp.transpose(p.reshape(SUBLANES, ngb, S5_GB * n), (1, 0, 2))

    return dict(bbd_re=bmat(bb_re), bbd_im=bmat(bb_im), cbd_re=cmat(c_re), cbd_im=cmat(c_im),
                pw_re=pw(pw_re), pw_im=pw(pw_im), d_vec=d_skip.astype(F32).reshape(ngb, 1, S5_GB * k))


def kernel(x_prompt, x_sample, mem_prompt, cache_fox_k, cache_fox_v, cache_fox_logf, cache_mem_k, cache_mem_v,
           state_ssd, state_conv, state_s5_re, state_s5_im, page_table,
           norm_mix, norm_cross, norm_mem, norm_ffn,
           w_in_even, fox_b_forget, fox_q_norm, fox_k_norm, ssd_conv_w, ssd_conv_b, ssd_dt_bias, ssd_A_log,
           ssd_D, ssd_norm, w_out_even,
           w_in_odd, s5_A_re, s5_A_im, s5_B_re, s5_B_im, s5_C_re, s5_C_im, s5_D, s5_log_dt, s5_w_glu,
           w_mq, w_mkv, mem_q_norm, mem_k_norm, w_mo, w_ffn_up, w_ffn_down):
    depth = norm_mix.shape[0]
    n_even, n_odd = w_in_even.shape[0], w_in_odd.shape[0]
    b, l, d = x_prompt.shape
    z0 = 3 * FOX_W + FOX_HEADS
    W = {
        "norm_mix": norm_mix, "norm_cross": norm_cross, "norm_ffn": norm_ffn,
        "w_in_even": w_in_even,
        "w_zx": [w_in_even[j, :, z0:z0 + SSD_INNER + CONV_DIM] for j in range(n_even)],
        "fox_b_forget": fox_b_forget, "fox_q_norm": fox_q_norm, "fox_k_norm": fox_k_norm,
        "ssd_conv_w": ssd_conv_w, "ssd_conv_b": ssd_conv_b, "ssd_dt_bias": ssd_dt_bias, "ssd_A_log": ssd_A_log,
        "ssd_D": ssd_D, "ssd_norm": ssd_norm, "w_out_even": w_out_even,
        "w_in_odd": w_in_odd, "s5_w_glu": s5_w_glu,
        "s5_packed": [_pack_s5(s5_A_re[j], s5_A_im[j], s5_B_re[j], s5_B_im[j], s5_C_re[j], s5_C_im[j],
                               s5_D[j], s5_log_dt[j]) for j in range(n_odd)],
        "w_mq": w_mq, "mem_q_norm": mem_q_norm, "w_mo": w_mo, "w_ffn_up": w_ffn_up, "w_ffn_down": w_ffn_down,
    }
    mt = mem_prompt.shape[1]
    mw = w_mkv.shape[2] // 2
    mem2 = mem_prompt.reshape(b * mt, d)
    mk, mv = [], []
    for i in range(depth):
        (k_i,) = _dense(mem2, w_mkv, layer=i, n_cols=mw, tn=mw, col0=0, pro="norm", gain=norm_mem[i],
                        epi="headnorm", head_gain=mem_k_norm[i], name="mem_k_proj")
        (v_i,) = _dense(mem2, w_mkv, layer=i, n_cols=mw, tn=mw, col0=mw, pro="norm", gain=norm_mem[i],
                        name="mem_v_proj")
        mk.append(k_i.reshape(b, mt, mw))
        mv.append(v_i.reshape(b, mt, mw))
    mem_k_p = jnp.stack(mk)
    mem_v_p = jnp.stack(mv)
    n_grp = s5_A_re.shape[1]
    (y_prompt, fox_k_p, fox_v_p, fox_logf_p, ssd_p, conv_p, s5_re_p, s5_im_p) = _trunk(
        x_prompt, W, mem_k_p, mem_v_p,
        jnp.zeros((n_even, b, SSD_CONV - 1, CONV_DIM), F32),
        jnp.zeros((n_even, b, SSD_HEADS, SSD_HD, SSD_STATE), F32),
        jnp.zeros((n_odd, b, n_grp, S5_STATE), F32),
        jnp.zeros((n_odd, b, n_grp, S5_STATE), F32),
        None)
    db = x_sample.shape[0]
    n_pool, page = cache_fox_k.shape[1], cache_fox_k.shape[2]
    assert page == LANES
    width = page * FOX_HEADS
    lf_t = jnp.transpose(cache_fox_logf.astype(F32), (0, 2, 1, 3)).reshape(n_even, page, n_pool * FOX_HEADS)
    incl = _pool_cumsum(lf_t).reshape(n_even, page, n_pool, FOX_HEADS)
    loc = jnp.transpose(incl, (0, 2, 1, 3)).reshape(n_even, n_pool, 1, width)
    tot = jnp.tile(incl[:, page - 1], (1, 1, page)).reshape(n_even, n_pool, 1, width)
    fox_cache = (cache_fox_k.reshape(n_even, n_pool, width, HEAD_DIM),
                 cache_fox_v.reshape(n_even, n_pool, width, HEAD_DIM), loc, tot, page_table)
    (y_sample, fox_k_s, fox_v_s, fox_logf_s, ssd_s, conv_s, s5_re_s, s5_im_s) = _trunk(
        x_sample, W, cache_mem_k.reshape(depth, db, mt, mw), cache_mem_v.reshape(depth, db, mt, mw),
        state_conv, state_ssd, state_s5_re, state_s5_im, fox_cache)
    hd = mw // MEM_HEADS
    return (y_prompt, y_sample,
            fox_k_p, fox_v_p, fox_logf_p,
            mem_k_p.reshape(depth, b, mt, MEM_HEADS, hd), mem_v_p.reshape(depth, b, mt, MEM_HEADS, hd),
            ssd_p, conv_p, s5_re_p, s5_im_p,
            fox_k_s, fox_v_s, fox_logf_s, ssd_s, conv_s, s5_re_s, s5_im_s)
```

```python
import functools
import math

import jax
import jax.numpy as jnp
from jax import lax
from jax.experimental import pallas as pl
from jax.experimental.pallas import tpu as pltpu

F32 = jnp.float32
BF16 = jnp.bfloat16
EPS = 1e-6
LANES = 128
SUBLANES = 8
VMEM_LIMIT_BYTES = 56 * 1024 * 1024

FOX_HEADS = 8
HEAD_DIM = 128
FOX_W = FOX_HEADS * HEAD_DIM
SSD_HEADS = 32
SSD_HD = 64
SSD_GROUPS = 4
SSD_STATE = 128
SSD_CHUNK = 128
SSD_INNER = SSD_HEADS * SSD_HD
SSD_CONV = 4
CONV_DIM = SSD_INNER + 2 * SSD_GROUPS * SSD_STATE
DT_LANE0 = FOX_HEADS
S5_GROUP = 16
S5_STATE = 64
S5_GB = 16
MEM_HEADS = 4
HIGHEST = lax.Precision.HIGHEST


def _cparams(sem):
    return pltpu.CompilerParams(dimension_semantics=sem, vmem_limit_bytes=VMEM_LIMIT_BYTES)


def _gelu_tanh(x):
    return 0.5 * x * (1.0 + jnp.tanh(math.sqrt(2.0 / math.pi) * (x + 0.044715 * x * x * x)))


def _softplus(x):
    return jnp.maximum(x, 0.0) + jnp.log1p(jnp.exp(-jnp.abs(x)))


def _silu(x):
    return x * jax.nn.sigmoid(x)


def _rms(x, gain):
    return x * lax.rsqrt(jnp.mean(x * x, axis=-1, keepdims=True) + EPS) * gain


def _dot(a, b):
    return jnp.dot(a, b, preferred_element_type=F32)


def _dot_nt(a, b):
    return lax.dot_general(a, b, (((1,), (1,)), ((), ())), preferred_element_type=F32)


def _dot_exact(a, b):
    return jnp.dot(a, b, preferred_element_type=F32, precision=HIGHEST)


def _dense_kernel(*refs, pro, epi, n_out):
    it = iter(refs)
    x_ref = next(it)
    gain_ref = next(it) if pro == "norm" else None
    w_ref = next(it)
    w2_ref = next(it) if epi in ("swiglu", "glu_res") else None
    res_ref = next(it) if epi in ("residual", "glu_res") else None
    hg_ref = next(it) if epi == "headnorm" else None
    out_refs = [next(it) for _ in range(n_out)]
    xs_ref = next(it) if pro != "none" else None

    if pro != "none":
        @pl.when(pl.program_id(1) == 0)
        def _():
            xf = x_ref[...].astype(F32)
            if pro == "norm":
                xf = _rms(xf, gain_ref[...])
            elif pro == "gelu":
                xf = _gelu_tanh(xf)
            xs_ref[...] = xf.astype(BF16)
        lhs = xs_ref[...]
    else:
        lhs = x_ref[...]

    acc = _dot(lhs, w_ref[...].astype(BF16))
    if epi == "swiglu":
        acc = _silu(acc) * _dot(lhs, w2_ref[...].astype(BF16))
    elif epi == "glu_res":
        acc = res_ref[...] + acc * jax.nn.sigmoid(_dot(lhs, w2_ref[...].astype(BF16)))
    elif epi == "residual":
        acc = res_ref[...] + acc

    if epi == "headnorm":
        for c in range(acc.shape[1] // HEAD_DIM):
            sl = slice(c * HEAD_DIM, (c + 1) * HEAD_DIM)
            blk = _rms(acc[:, sl], hg_ref[...])
            for o in out_refs:
                o[:, sl] = blk.astype(o.dtype)
    else:
        for o in out_refs:
            o[...] = acc.astype(o.dtype)


def _wspec(w, layer, k, tn, blk0):
    if w.ndim == 2:
        return pl.BlockSpec((k, tn), lambda i, j: (0, blk0 + j))
    return pl.BlockSpec((None, k, tn), lambda i, j: (layer, 0, blk0 + j))


def _dense(x, w, *, n_cols, tn, name, layer=0, col0=0, col0_2=None, pro="none", gain=None, epi="plain",
           res=None, head_gain=None, out_dtypes=(F32,), tm_max=1024):
    m, k = x.shape
    tm = min(tm_max, m)
    assert m % tm == 0 and n_cols % tn == 0 and col0 % tn == 0
    b0 = col0 // tn
    in_specs = [pl.BlockSpec((tm, k), lambda i, j: (i, 0))]
    args = [x]
    if pro == "norm":
        in_specs.append(pl.BlockSpec((1, k), lambda i, j: (0, 0)))
        args.append(gain.reshape(1, k).astype(F32))
    in_specs.append(_wspec(w, layer, k, tn, b0))
    args.append(w)
    if epi in ("swiglu", "glu_res"):
        assert col0_2 % tn == 0
        in_specs.append(_wspec(w, layer, k, tn, col0_2 // tn))
        args.append(w)
    if epi in ("residual", "glu_res"):
        in_specs.append(pl.BlockSpec((tm, tn), lambda i, j: (i, j)))
        args.append(res)
    if epi == "headnorm":
        in_specs.append(pl.BlockSpec((1, HEAD_DIM), lambda i, j: (0, 0)))
        args.append(head_gain.reshape(1, HEAD_DIM).astype(F32))
    out_shape = [jax.ShapeDtypeStruct((m, n_cols), dt) for dt in out_dtypes]
    out_specs = [pl.BlockSpec((tm, tn), lambda i, j: (i, j)) for _ in out_dtypes]
    scratch = [pltpu.VMEM((tm, k), BF16)] if pro != "none" else []
    outs = pl.pallas_call(
        functools.partial(_dense_kernel, pro=pro, epi=epi, n_out=len(out_dtypes)),
        grid=(m // tm, n_cols // tn),
        in_specs=in_specs, out_specs=out_specs, out_shape=out_shape,
        scratch_shapes=scratch,
        compiler_params=_cparams(("parallel", "arbitrary")),
        name=name,
    )(*args)
    return outs


def _inproj_even_kernel(x_ref, g_ref, wqkv_ref, wzx_ref, wf_ref, wdt_ref, qg_ref, kg_ref,
                        q_ref, kf_ref, kb_ref, vf_ref, vb_ref, zx_ref, tail_ref, xs_ref, *, nq, nzx):
    j = pl.program_id(1)

    @pl.when(j == 0)
    def _():
        xs_ref[...] = _rms(x_ref[...], g_ref[...]).astype(BF16)

    def headnorm(acc, gain, outs):
        for c in range(acc.shape[1] // HEAD_DIM):
            sl = slice(c * HEAD_DIM, (c + 1) * HEAD_DIM)
            blk = _rms(acc[:, sl], gain)
            for o in outs:
                o[:, sl] = blk.astype(o.dtype)

    @pl.when(j < nq)
    def _():
        headnorm(_dot(xs_ref[...], wqkv_ref[...].astype(BF16)), qg_ref[...], (q_ref,))

    @pl.when((j >= nq) & (j < 2 * nq))
    def _():
        headnorm(_dot(xs_ref[...], wqkv_ref[...].astype(BF16)), kg_ref[...], (kf_ref, kb_ref))

    @pl.when((j >= 2 * nq) & (j < 3 * nq))
    def _():
        acc = _dot(xs_ref[...], wqkv_ref[...].astype(BF16))
        vf_ref[...] = acc
        vb_ref[...] = acc.astype(BF16)

    @pl.when((j >= 3 * nq) & (j < 3 * nq + nzx))
    def _():
        zx_ref[...] = _dot(xs_ref[...], wzx_ref[...].astype(BF16))

    @pl.when(j == 3 * nq + nzx)
    def _():
        lane = lax.broadcasted_iota(jnp.int32, wf_ref.shape, 1)
        wt = jnp.where(lane < DT_LANE0, wf_ref[...],
                       jnp.where(lane < DT_LANE0 + SSD_HEADS, wdt_ref[...], 0.0))
        tail_ref[...] = _dot(xs_ref[...], wt.astype(BF16))


def _inproj_even(x, gain, w_all, w_zx, layer, q_gain, k_gain, tn=512, tm_max=1024):
    m, k = x.shape
    tm = min(tm_max, m)
    nq = FOX_W // tn
    nzx = w_zx.shape[1] // tn
    f_blk = 3 * FOX_W // LANES
    dt_blk = (3 * FOX_W + FOX_HEADS + SSD_INNER + CONV_DIM) // LANES
    assert (3 * FOX_W + FOX_HEADS + SSD_INNER + CONV_DIM) % LANES == DT_LANE0

    def clip(j, lo, n):
        return jnp.clip(j - lo, 0, n - 1)

    tile = lambda lo, n: pl.BlockSpec((tm, tn), lambda i, j: (i, clip(j, lo, n)))
    outs = pl.pallas_call(
        functools.partial(_inproj_even_kernel, nq=nq, nzx=nzx),
        grid=(m // tm, 3 * nq + nzx + 1),
        in_specs=[pl.BlockSpec((tm, k), lambda i, j: (i, 0), pipeline_mode=pl.Buffered(1)),
                  pl.BlockSpec((1, k), lambda i, j: (0, 0)),
                  pl.BlockSpec((None, k, tn), lambda i, j: (layer, 0, clip(j, 0, 3 * nq))),
                  pl.BlockSpec((k, tn), lambda i, j: (0, clip(j, 3 * nq, nzx))),
                  pl.BlockSpec((None, k, LANES), lambda i, j: (layer, 0, f_blk), pipeline_mode=pl.Buffered(1)),
                  pl.BlockSpec((None, k, LANES), lambda i, j: (layer, 0, dt_blk), pipeline_mode=pl.Buffered(1)),
                  pl.BlockSpec((1, HEAD_DIM), lambda i, j: (0, 0)),
                  pl.BlockSpec((1, HEAD_DIM), lambda i, j: (0, 0))],
        out_specs=[tile(0, nq), tile(nq, nq), tile(nq, nq), tile(2 * nq, nq), tile(2 * nq, nq),
                   tile(3 * nq, nzx), pl.BlockSpec((tm, LANES), lambda i, j: (i, 0))],
        out_shape=[jax.ShapeDtypeStruct((m, FOX_W), BF16),
                   jax.ShapeDtypeStruct((m, FOX_W), F32), jax.ShapeDtypeStruct((m, FOX_W), BF16),
                   jax.ShapeDtypeStruct((m, FOX_W), F32), jax.ShapeDtypeStruct((m, FOX_W), BF16),
                   jax.ShapeDtypeStruct((m, nzx * tn), F32), jax.ShapeDtypeStruct((m, LANES), F32)],
        scratch_shapes=[pltpu.VMEM((tm, k), BF16)],
        compiler_params=_cparams(("parallel", "arbitrary")),
        name="inproj_even",
    )(x, gain.reshape(1, k), w_all, w_zx, w_all, w_all, q_gain.reshape(1, HEAD_DIM), k_gain.reshape(1, HEAD_DIM))
    return outs


def _outproj_kernel(o_ref, y_ref, g_ref, w_ref, res_ref, out_ref, yn_ref):
    k1 = o_ref.shape[1]

    @pl.when(pl.program_id(1) == 0)
    def _():
        yn_ref[...] = _rms(y_ref[...], g_ref[...]).astype(BF16)
    acc = _dot(o_ref[...], w_ref[:k1, :].astype(BF16)) + _dot(yn_ref[...], w_ref[k1:, :].astype(BF16))
    out_ref[...] = res_ref[...] + acc


def _outproj_even(o_fox, yg, gain, w_out, layer, res, tn=256, tm_max=1024):
    m, d = res.shape
    tm = min(tm_max, m)
    k1, k2 = o_fox.shape[1], yg.shape[1]
    return pl.pallas_call(
        _outproj_kernel,
        grid=(m // tm, d // tn),
        in_specs=[
            pl.BlockSpec((tm, k1), lambda i, j: (i, 0)),
            pl.BlockSpec((tm, k2), lambda i, j: (i, 0)),
            pl.BlockSpec((1, k2), lambda i, j: (0, 0)),
            pl.BlockSpec((None, k1 + k2, tn), lambda i, j: (layer, 0, j)),
            pl.BlockSpec((tm, tn), lambda i, j: (i, j)),
        ],
        out_specs=pl.BlockSpec((tm, tn), lambda i, j: (i, j)),
        out_shape=jax.ShapeDtypeStruct((m, d), F32),
        scratch_shapes=[pltpu.VMEM((tm, k2), BF16)],
        compiler_params=_cparams(("parallel", "arbitrary")),
        name="outproj_even",
    )(o_fox, yg, gain.reshape(1, k2).astype(F32), w_out, res)


def _lane_cumsum(x):
    lane = lax.broadcasted_iota(jnp.int32, x.shape, 1)
    k = 1
    while k < LANES:
        x = x + jnp.where(lane >= k, pltpu.roll(x, k, axis=1), 0.0)
        k *= 2
    return x


def _forget_kernel(raw_ref, b_ref, lf_ref, negc_ref, ctm_ref):
    lp = raw_ref.shape[1]
    lf = -_softplus(-(raw_ref[0] + b_ref[...]))
    lf_ref[0] = lf[:, :FOX_HEADS]
    row_i = lax.broadcasted_iota(jnp.int32, (LANES, LANES), 0)
    col_i = lax.broadcasted_iota(jnp.int32, (LANES, LANES), 1)
    tri = (col_i <= row_i).astype(F32)
    ctm_ref[0] = _dot_exact(tri, lf[:LANES, :])[:, :FOX_HEADS]
    carry = jnp.zeros((FOX_HEADS, 1), F32)
    for c in range(lp // LANES):
        blk = lf[c * LANES:(c + 1) * LANES, :].T[:FOX_HEADS, :]
        cs = _lane_cumsum(blk) + carry
        negc_ref[0, :, c * LANES:(c + 1) * LANES] = -cs
        carry = cs[:, LANES - 1:LANES]


def _forget(raw, b_vec):
    b, lp, _ = raw.shape
    return pl.pallas_call(
        _forget_kernel,
        grid=(b,),
        in_specs=[pl.BlockSpec((1, lp, LANES), lambda i: (i, 0, 0)),
                  pl.BlockSpec((1, LANES), lambda i: (0, 0))],
        out_specs=[pl.BlockSpec((1, lp, FOX_HEADS), lambda i: (i, 0, 0)),
                   pl.BlockSpec((1, FOX_HEADS, lp), lambda i: (i, 0, 0)),
                   pl.BlockSpec((1, LANES, FOX_HEADS), lambda i: (i, 0, 0))],
        out_shape=[jax.ShapeDtypeStruct((b, lp, FOX_HEADS), F32),
                   jax.ShapeDtypeStruct((b, FOX_HEADS, lp), F32),
                   jax.ShapeDtypeStruct((b, LANES, FOX_HEADS), F32)],
        compiler_params=_cparams(("parallel",)),
        name="forget_gates",
    )(raw, b_vec)


def _fox_prompt_kernel(q_ref, k_ref, v_ref, nb_ref, o_ref, m_sc, acc_sc, mask_sc, *, tq, tk):
    qi = pl.program_id(1)
    kj = pl.program_id(2)

    @pl.when(kj == 0)
    def _():
        m_sc[...] = jnp.full(m_sc.shape, -jnp.inf, F32)
        acc_sc[...] = jnp.zeros(acc_sc.shape, F32)

    @pl.when(kj * tk <= qi * tq + (tq - 1))
    def _():
        row = qi * tq + lax.broadcasted_iota(jnp.int32, (tq, tk), 0)
        col = kj * tk + lax.broadcasted_iota(jnp.int32, (tq, tk), 1)
        mask_sc[...] = jnp.where(col <= row, 0.0, -jnp.inf)
        ones = jnp.ones((tk, HEAD_DIM), BF16)
        for h in range(FOX_HEADS):
            sl = slice(h * HEAD_DIM, (h + 1) * HEAD_DIM)
            s = _dot_nt(q_ref[0, :, sl], k_ref[0, :, sl]) * (HEAD_DIM ** -0.5)
            s = s + nb_ref[0, h:h + 1, :] + mask_sc[...]
            m_old = m_sc[h]
            m_new = jnp.maximum(m_old, jnp.max(s, axis=-1, keepdims=True))
            p = jnp.exp(s - m_new).astype(BF16)
            pv = _dot(p, jnp.concatenate([v_ref[0, :, sl], ones], axis=1))
            acc_sc[h] = jnp.exp(m_old - m_new) * acc_sc[h] + pv
            m_sc[h] = m_new

    @pl.when(kj == pl.num_programs(2) - 1)
    def _():
        for h in range(FOX_HEADS):
            acc = acc_sc[h]
            o_ref[0, :, h * HEAD_DIM:(h + 1) * HEAD_DIM] = (acc[:, :HEAD_DIM] / acc[:, HEAD_DIM:]).astype(o_ref.dtype)


def _fox_prompt(q, k, v, negc, tq=512, tk=512):
    b, l, _ = q.shape
    tq, tk = min(tq, l), min(tk, l)
    nq, nk = l // tq, l // tk

    def last_needed(qi, kj):
        return jnp.minimum(kj, (qi * tq + tq - 1) // tk)

    return pl.pallas_call(
        functools.partial(_fox_prompt_kernel, tq=tq, tk=tk),
        grid=(b, nq, nk),
        in_specs=[pl.BlockSpec((1, tq, FOX_W), lambda bi, qi, kj: (bi, qi, 0)),
                  pl.BlockSpec((1, tk, FOX_W), lambda bi, qi, kj: (bi, last_needed(qi, kj), 0)),
                  pl.BlockSpec((1, tk, FOX_W), lambda bi, qi, kj: (bi, last_needed(qi, kj), 0)),
                  pl.BlockSpec((1, FOX_HEADS, tk), lambda bi, qi, kj: (bi, 0, last_needed(qi, kj)))],
        out_specs=pl.BlockSpec((1, tq, FOX_W), lambda bi, qi, kj: (bi, qi, 0)),
        out_shape=jax.ShapeDtypeStruct((b, l, FOX_W), BF16),
        scratch_shapes=[pltpu.VMEM((FOX_HEADS, tq, 1), F32),
                        pltpu.VMEM((FOX_HEADS, tq, 2 * HEAD_DIM), F32),
                        pltpu.VMEM((tq, tk), F32)],
        compiler_params=_cparams(("parallel", "parallel", "arbitrary")),
        name="fox_prompt",
    )(q, k, v, negc)


def _pool_cumsum_kernel(x_ref, o_ref):
    n = x_ref.shape[0]
    row_i = lax.broadcasted_iota(jnp.int32, (n, n), 0)
    col_i = lax.broadcasted_iota(jnp.int32, (n, n), 1)
    o_ref[...] = _dot_exact((col_i <= row_i).astype(F32), x_ref[...])


def _pool_cumsum(x, tc=2048):
    nl, page, cols = x.shape
    tc = tc if cols % tc == 0 else cols
    return pl.pallas_call(
        _pool_cumsum_kernel,
        grid=(nl, cols // tc),
        in_specs=[pl.BlockSpec((None, page, tc), lambda i, j: (i, 0, j))],
        out_specs=pl.BlockSpec((None, page, tc), lambda i, j: (i, 0, j)),
        out_shape=jax.ShapeDtypeStruct((nl, page, cols), F32),
        compiler_params=_cparams(("parallel", "parallel")),
        name="pool_logf_cumsum",
    )(x)


def _fox_decode_kernel(pt_ref, q_ref, *refs, gp, t_new):
    kv_refs = refs[:4 * gp]
    kn_ref, vn_ref, cn_ref, o_ref, m_sc, l_sc, acc_sc, carry_sc = refs[4 * gp:]
    p = pl.program_id(1)
    n_past = pl.num_programs(1) - 1
    rows = q_ref.shape[1]
    scale = HEAD_DIM ** -0.5

    @pl.when(p == 0)
    def _():
        m_sc[...] = jnp.full(m_sc.shape, -jnp.inf, F32)
        l_sc[...] = jnp.zeros(l_sc.shape, F32)
        acc_sc[...] = jnp.zeros(acc_sc.shape, F32)
        carry_sc[...] = jnp.zeros(carry_sc.shape, F32)

    def update(s, v, state):
        m_old, l_old, acc = state
        m_new = jnp.maximum(m_old, jnp.max(s, axis=-1, keepdims=True))
        alpha = jnp.exp(m_old - m_new)
        pr = jnp.exp(s - m_new)
        l_new = alpha * l_old + jnp.sum(pr, axis=-1, keepdims=True)
        acc = alpha * acc + _dot(pr.astype(BF16), v.astype(BF16))
        return m_new, l_new, acc

    @pl.when(p < n_past)
    def _():
        q = q_ref[0]
        width = kv_refs[0].shape[0]
        row = lax.broadcasted_iota(jnp.int32, (rows, width), 0)
        col = lax.broadcasted_iota(jnp.int32, (rows, width), 1)
        own_head = (col % FOX_HEADS) == (row // t_new)
        carry = carry_sc[...]
        scores = []
        for g in range(gp):
            k_ref, _, loc_ref, tot_ref = kv_refs[4 * g:4 * g + 4]
            s = _dot_nt(q, k_ref[...].astype(BF16)) * scale - (carry + loc_ref[...])
            scores.append(jnp.where(own_head, s, -jnp.inf))
            carry = carry + tot_ref[...]
        carry_sc[...] = carry
        m_old = m_sc[...]
        m_new = m_old
        for s in scores:
            m_new = jnp.maximum(m_new, jnp.max(s, axis=-1, keepdims=True))
        alpha = jnp.exp(m_old - m_new)
        l_new = alpha * l_sc[...]
        acc = alpha * acc_sc[...]
        for g, s in enumerate(scores):
            pr = jnp.exp(s - m_new)
            l_new = l_new + jnp.sum(pr, axis=-1, keepdims=True)
            acc = acc + _dot(pr.astype(BF16), kv_refs[4 * g + 1][...].astype(BF16))
        m_sc[...], l_sc[...], acc_sc[...] = m_new, l_new, acc

    @pl.when(p == n_past)
    def _():
        width = kn_ref.shape[1]
        row = lax.broadcasted_iota(jnp.int32, (rows, width), 0)
        col = lax.broadcasted_iota(jnp.int32, (rows, width), 1)
        keep = ((col % FOX_HEADS) == (row // t_new)) & ((col // FOX_HEADS) <= (row % t_new))
        s = _dot_nt(q_ref[0], kn_ref[0].astype(BF16)) * scale - (carry_sc[:, :width] + cn_ref[0])
        m, l, acc = update(jnp.where(keep, s, -jnp.inf), vn_ref[0], (m_sc[...], l_sc[...], acc_sc[...]))
        o_ref[0] = (acc / l).astype(o_ref.dtype)


def _fox_decode(q, k_new, v_new, c_new, k_pool, v_pool, loc, tot, page_table, layer, t_new, gp=8):
    b, rows, _ = q.shape
    n_pages = page_table.shape[1]
    gp = max(g for g in range(1, gp + 1) if n_pages % g == 0)
    n_steps = n_pages // gp
    width = k_pool.shape[2]

    def page_map(g):
        return lambda bi, p, pt: (layer, pt[bi, jnp.minimum(p, n_steps - 1) * gp + g], 0, 0)

    in_specs = [pl.BlockSpec((1, rows, HEAD_DIM), lambda bi, p, pt: (bi, 0, 0))]
    args = [q]
    for g in range(gp):
        in_specs += [pl.BlockSpec((None, None, width, HEAD_DIM), page_map(g)),
                     pl.BlockSpec((None, None, width, HEAD_DIM), page_map(g)),
                     pl.BlockSpec((None, None, 1, width), page_map(g)),
                     pl.BlockSpec((None, None, 1, width), page_map(g))]
        args += [k_pool, v_pool, loc, tot]
    new_spec = pl.BlockSpec((1, k_new.shape[1], HEAD_DIM), lambda bi, p, pt: (bi, 0, 0))
    in_specs += [new_spec, new_spec, pl.BlockSpec((1, 1, c_new.shape[2]), lambda bi, p, pt: (bi, 0, 0))]
    args += [k_new, v_new, c_new]
    grid_spec = pltpu.PrefetchScalarGridSpec(
        num_scalar_prefetch=1,
        grid=(b, n_steps + 1),
        in_specs=in_specs,
        out_specs=pl.BlockSpec((1, rows, HEAD_DIM), lambda bi, p, pt: (bi, 0, 0)),
        scratch_shapes=[pltpu.VMEM((rows, 1), F32), pltpu.VMEM((rows, 1), F32),
                        pltpu.VMEM((rows, HEAD_DIM), F32), pltpu.VMEM((1, width), F32)],
    )
    return pl.pallas_call(
        functools.partial(_fox_decode_kernel, gp=gp, t_new=t_new), grid_spec=grid_spec,
        out_shape=jax.ShapeDtypeStruct((b, rows, HEAD_DIM), BF16),
        compiler_params=_cparams(("parallel", "arbitrary")),
        name="fox_decode",
    )(page_table, *args)


def _conv_kernel(x_ref, buf_ref, w_ref, b_ref, y_ref, nb_ref, full_sc):
    l = x_ref.shape[1]
    pad = SUBLANES
    full_sc[pad - (SSD_CONV - 1):pad, :] = buf_ref[0]
    full_sc[pad:pad + l, :] = x_ref[0]
    acc = b_ref[...] + full_sc[pad - 3:pad - 3 + l, :] * w_ref[0:1, :]
    for kk in range(1, SSD_CONV):
        acc = acc + full_sc[pad - 3 + kk:pad - 3 + kk + l, :] * w_ref[kk:kk + 1, :]
    y_ref[0] = _silu(acc)
    nb_ref[0] = full_sc[pad + l - (SSD_CONV - 1):pad + l, :]


def _conv(zx, col0, buf, w, bias, tc=512):
    b, l, _ = zx.shape
    c = w.shape[1]
    cb0 = col0 // tc
    return pl.pallas_call(
        _conv_kernel,
        grid=(b, c // tc),
        in_specs=[pl.BlockSpec((1, l, tc), lambda i, j: (i, 0, cb0 + j)),
                  pl.BlockSpec((1, SSD_CONV - 1, tc), lambda i, j: (i, 0, j)),
                  pl.BlockSpec((SSD_CONV, tc), lambda i, j: (0, j)),
                  pl.BlockSpec((1, tc), lambda i, j: (0, j))],
        out_specs=[pl.BlockSpec((1, l, tc), lambda i, j: (i, 0, j)),
                   pl.BlockSpec((1, SSD_CONV - 1, tc), lambda i, j: (i, 0, j))],
        out_shape=[jax.ShapeDtypeStruct((b, l, c), F32),
                   jax.ShapeDtypeStruct((b, SSD_CONV - 1, c), F32)],
        scratch_shapes=[pltpu.VMEM((l + SUBLANES, tc), F32)],
        compiler_params=_cparams(("parallel", "parallel")),
        name="ssd_conv",
    )(zx, buf, w, bias.reshape(1, c))


def _ssd_kernel(xc_ref, tail_ref, z_ref, dtb_ref, alog_ref, dexp_ref, e_ref, h0_ref,
                y_ref, hout_ref, h_sc, *, valid_len):
    c = pl.program_id(1)
    q = SSD_CHUNK
    gw = SSD_INNER // SSD_GROUPS
    hpg = SSD_HEADS // SSD_GROUPS

    @pl.when(c == 0)
    def _():
        h_sc[...] = h0_ref[0]

    lane = lax.broadcasted_iota(jnp.int32, (q, LANES), 1)
    row = lax.broadcasted_iota(jnp.int32, (q, LANES), 0)
    col_i = lax.broadcasted_iota(jnp.int32, (q, q), 1)
    row_i = lax.broadcasted_iota(jnp.int32, (q, q), 0)
    causal = col_i <= row_i
    tri = causal.astype(F32)

    dt = _softplus(tail_ref[0] + dtb_ref[...])
    live = (lane >= DT_LANE0) & (lane < DT_LANE0 + SSD_HEADS) & (c * q + row < valid_len)
    dt = jnp.where(live, dt, 0.0)
    a = dt * (-jnp.exp(alog_ref[...]))
    a_cum = _dot_exact(tri, a)
    a_cum_t = a_cum.T
    a_last = a_cum[q - 1:q, :]
    ea = jnp.exp(a_cum)
    w1 = dt * jnp.exp(a_last - a_cum)

    for g in range(SSD_GROUPS):
        e = e_ref[g]
        dt_e = _dot_exact(dt, e)
        w1_e = _dot_exact(w1, e)
        ea_e = _dot_exact(ea, e)
        xs = xc_ref[0, :, g * gw:(g + 1) * gw]
        bb = xc_ref[0, :, SSD_INNER + g * SSD_STATE:SSD_INNER + (g + 1) * SSD_STATE].astype(BF16)
        cc = xc_ref[0, :, SSD_INNER + (SSD_GROUPS + g) * SSD_STATE:
                    SSD_INNER + (SSD_GROUPS + g + 1) * SSD_STATE].astype(BF16)
        cb = _dot_nt(cc, bb)
        xdt = xs * dt_e
        hg = h_sc[g * gw:(g + 1) * gw, :]
        y = _dot_nt(cc, hg.astype(BF16)) * ea_e
        st = _dot((xs * w1_e).T.astype(BF16), bb)
        for hp in range(hpg // 2):
            pair = xdt[:, hp * LANES:(hp + 1) * LANES]
            lane_p = lax.broadcasted_iota(jnp.int32, pair.shape, 1)
            yp = None
            for sub in range(2):
                h = 2 * hp + sub
                ln = DT_LANE0 + g * hpg + h
                seg = a_cum[:, ln:ln + 1] - a_cum_t[ln:ln + 1, :]
                dec = jnp.exp(jnp.where(causal, seg, -jnp.inf))
                mat = (cb * dec).astype(BF16)
                in_head = (lane_p >= sub * SSD_HD) & (lane_p < (sub + 1) * SSD_HD)
                rhs = jnp.where(in_head, pair, 0.0).astype(BF16)
                part = _dot(mat, rhs)
                yp = part if yp is None else yp + part
                r0 = g * gw + h * SSD_HD
                h_sc[r0:r0 + SSD_HD, :] = (hg[h * SSD_HD:(h + 1) * SSD_HD, :] * jnp.exp(a_last[:, ln:ln + 1])
                                           + st[h * SSD_HD:(h + 1) * SSD_HD, :])
            cs = slice(g * gw + hp * LANES, g * gw + (hp + 1) * LANES)
            yt = yp + y[:, hp * LANES:(hp + 1) * LANES] + dexp_ref[:, cs] * xs[:, hp * LANES:(hp + 1) * LANES]
            y_ref[0, :, cs] = yt * _silu(z_ref[0, :, cs])

    @pl.when(c == pl.num_programs(1) - 1)
    def _():
        hout_ref[0] = h_sc[...]


def _ssd(xc, tail, zx, dtb_vec, alog_vec, d_exp, e_mat, h0, valid_len):
    b, lp, _ = xc.shape
    nc = lp // SSD_CHUNK
    return pl.pallas_call(
        functools.partial(_ssd_kernel, valid_len=valid_len),
        grid=(b, nc),
        in_specs=[pl.BlockSpec((1, SSD_CHUNK, CONV_DIM), lambda i, c: (i, c, 0)),
                  pl.BlockSpec((1, SSD_CHUNK, LANES), lambda i, c: (i, c, 0)),
                  pl.BlockSpec((1, SSD_CHUNK, SSD_INNER), lambda i, c: (i, c, 0)),
                  pl.BlockSpec((1, LANES), lambda i, c: (0, 0)),
                  pl.BlockSpec((1, LANES), lambda i, c: (0, 0)),
                  pl.BlockSpec((1, SSD_INNER), lambda i, c: (0, 0)),
                  pl.BlockSpec((SSD_GROUPS, LANES, SSD_INNER // SSD_GROUPS), lambda i, c: (0, 0, 0)),
                  pl.BlockSpec((1, SSD_INNER, SSD_STATE), lambda i, c: (i, 0, 0))],
        out_specs=[pl.BlockSpec((1, SSD_CHUNK, SSD_INNER), lambda i, c: (i, c, 0)),
                   pl.BlockSpec((1, SSD_INNER, SSD_STATE), lambda i, c: (i, 0, 0))],
        out_shape=[jax.ShapeDtypeStruct((b, lp, SSD_INNER), F32),
                   jax.ShapeDtypeStruct((b, SSD_INNER, SSD_STATE), F32)],
        scratch_shapes=[pltpu.VMEM((SSD_INNER, SSD_STATE), F32)],
        compiler_params=_cparams(("parallel", "arbitrary")),
        name="ssd_chunked",
    )(xc, tail, zx, dtb_vec, alog_vec, d_exp, e_mat, h0)


S5_TILE = 256
S5_STEPS = S5_TILE // SUBLANES


def _cmul(ar, ai, br, bi):
    return ar * br - ai * bi, ar * bi + ai * br


def _s5_prep_kernel(are_ref, aim_ref, ldt_ref, bre_ref, bim_ref, psr_ref, psi_ref, pcr_ref, pci_ref,
                    bbr_ref, bbi_ref):
    lam_re = jnp.minimum(are_ref[...], -1e-4)
    lam_im = aim_ref[...]
    dt = jnp.exp(ldt_ref[...])
    mag = jnp.exp(lam_re * dt)
    ang = lam_im * dt
    lb_re, lb_im = mag * jnp.cos(ang), mag * jnp.sin(ang)
    nr, ni = lb_re - 1.0, lb_im
    den = lam_re * lam_re + lam_im * lam_im
    coef_re = (nr * lam_re + ni * lam_im) / den
    coef_im = (ni * lam_re - nr * lam_im) / den
    for k in range(S5_GROUP):
        br, bi = bre_ref[k], bim_ref[k]
        bbr_ref[k] = coef_re * br - coef_im * bi
        bbi_ref[k] = coef_re * bi + coef_im * br
    pr, pi = lb_re, lb_im
    for r in range(S5_STEPS):
        psr_ref[r], psi_ref[r] = pr, pi
        if r + 1 < S5_STEPS:
            pr, pi = _cmul(pr, pi, lb_re, lb_im)
    qr, qi = pr, pi
    for c in range(SUBLANES):
        pcr_ref[c], pci_ref[c] = qr, qi
        if c + 1 < SUBLANES:
            qr, qi = _cmul(qr, qi, pr, pi)


def _s5_prep(a_re, a_im, log_dt, b_re_t, b_im_t):
    g, n = a_re.shape
    sd = jax.ShapeDtypeStruct
    return pl.pallas_call(
        _s5_prep_kernel,
        out_shape=[sd((S5_STEPS, g, n), F32), sd((S5_STEPS, g, n), F32),
                   sd((SUBLANES, g, n), F32), sd((SUBLANES, g, n), F32),
                   sd((S5_GROUP, g, n), F32), sd((S5_GROUP, g, n), F32)],
        name="s5_discretise",
    )(a_re, a_im, log_dt.reshape(g, 1), b_re_t, b_im_t)


def _s5_kernel(u_ref, perm_ref, permt_ref, bre_ref, bim_ref, cre_ref, cim_ref, psr_ref, psi_ref,
               pcr_ref, pci_ref, d_ref, s0r_ref, s0i_ref, g_ref, sr_ref, si_ref, xr_sc, xi_sc, *, tt):
    l = u_ref.shape[1]
    w = xr_sc.shape[1]
    ts = tt // SUBLANES
    rows = lax.broadcasted_iota(jnp.int32, (SUBLANES, w), 0)
    bc = lambda v: jnp.broadcast_to(v, (SUBLANES, w))
    lam_r, lam_i = bc(psr_ref[0, 0:1, :]), bc(psi_ref[0, 0:1, :])
    pcr, pci = pcr_ref[0], pci_ref[0]

    def blk(i):
        return pl.ds(pl.multiple_of(i * SUBLANES, SUBLANES), SUBLANES)

    def pass1(i, st):
        sr, si = _cmul(lam_r, lam_i, st[0], st[1])
        sr, si = sr + xr_sc[blk(i), :], si + xi_sc[blk(i), :]
        xr_sc[blk(i), :] = sr
        xi_sc[blk(i), :] = si
        return sr, si

    def tile(t, carry):
        cr, ci = carry
        t0 = pl.multiple_of(t * tt, tt)
        u = u_ref[0, pl.ds(t0, tt), :]
        up = u.astype(BF16)
        if ts > 1:
            up = _dot(perm_ref[...], up).astype(BF16)
        xr_sc[...] = _dot(up, bre_ref[0])
        xi_sc[...] = _dot(up, bim_ref[0])
        zero = jnp.zeros((SUBLANES, w), F32)
        tr, ti = lax.fori_loop(0, ts, pass1, (zero, zero), unroll=min(4, ts))
        for kk in (1, 2, 4):
            keep = rows >= kk
            qr = jnp.where(keep, bc(pcr[kk - 1:kk, :]), 0.0)
            qi = jnp.where(keep, bc(pci[kk - 1:kk, :]), 0.0)
            dr, di = _cmul(qr, qi, pltpu.roll(tr, kk, axis=0), pltpu.roll(ti, kk, axis=0))
            tr, ti = tr + dr, ti + di
        dr, di = _cmul(pcr, pci, bc(cr), bc(ci))
        tr, ti = tr + dr, ti + di
        er = jnp.where(rows == 0, bc(cr), pltpu.roll(tr, 1, axis=0))
        ei = jnp.where(rows == 0, bc(ci), pltpu.roll(ti, 1, axis=0))

        def pass2(i, _):
            pr, pi = bc(psr_ref[0, pl.ds(i, 1), :]), bc(psi_ref[0, pl.ds(i, 1), :])
            dr, di = _cmul(pr, pi, er, ei)
            xr_sc[blk(i), :] = xr_sc[blk(i), :] + dr
            xi_sc[blk(i), :] = xi_sc[blk(i), :] + di
            return 0

        lax.fori_loop(0, ts, pass2, 0, unroll=min(4, ts))
        yp = _dot(xr_sc[...].astype(BF16), cre_ref[0]) - _dot(xi_sc[...].astype(BF16), cim_ref[0])
        y = yp
        if ts > 1:
            hi = yp.astype(BF16)
            r1 = yp - hi.astype(F32)
            mid = r1.astype(BF16)
            lo = (r1 - mid.astype(F32)).astype(BF16)
            pt = permt_ref[...]
            y = _dot(pt, hi) + _dot(pt, mid) + _dot(pt, lo)
        y = y + d_ref[0] * u
        g_ref[0, pl.ds(t0, tt), :] = _gelu_tanh(y).astype(g_ref.dtype)
        return tr[SUBLANES - 1:SUBLANES, :], ti[SUBLANES - 1:SUBLANES, :]

    cr, ci = lax.fori_loop(0, l // tt, tile, (s0r_ref[0, 0], s0i_ref[0, 0]))
    sr_ref[0, 0] = cr
    si_ref[0, 0] = ci


def _s5(u, p, s0_re, s0_im):
    b, l, d = u.shape
    ngb = p["bbd_re"].shape[0]
    cw = d // ngb
    sw = p["bbd_re"].shape[2]
    tt = min(S5_TILE, l)
    ts = tt // SUBLANES
    assert l % tt == 0 and ts in (1, S5_STEPS)
    steps_re, steps_im = p["ps_re"][:, :ts], p["ps_im"][:, :ts]
    chunk_re, chunk_im = (p["pc_re"], p["pc_im"]) if ts == S5_STEPS else (p["ps_re"][:, :SUBLANES],
                                                                          p["ps_im"][:, :SUBLANES])
    r = jnp.arange(tt)
    perm = (r[None, :] == ((r % SUBLANES) * ts + r // SUBLANES)[:, None]).astype(BF16)
    state_spec = pl.BlockSpec((1, 1, 1, sw), lambda i, j: (i, j, 0, 0))
    whole = lambda a: pl.BlockSpec(a.shape, lambda i, j: (0,) * a.ndim)
    per_gb = lambda a: pl.BlockSpec((1,) + a.shape[1:], lambda i, j: (j, 0, 0))
    args = [perm, perm.T, p["bbd_re"], p["bbd_im"], p["cbd_re"], p["cbd_im"], steps_re, steps_im,
            chunk_re, chunk_im, p["d_vec"]]
    return pl.pallas_call(
        functools.partial(_s5_kernel, tt=tt),
        grid=(b, ngb),
        in_specs=[pl.BlockSpec((1, l, cw), lambda i, j: (i, 0, j)), whole(perm), whole(perm)]
                 + [per_gb(a) for a in args[2:]] + [state_spec, state_spec],
        out_specs=[pl.BlockSpec((1, l, cw), lambda i, j: (i, 0, j)), state_spec, state_spec],
        out_shape=[jax.ShapeDtypeStruct((b, l, d), BF16),
                   jax.ShapeDtypeStruct((b, ngb, 1, sw), F32),
                   jax.ShapeDtypeStruct((b, ngb, 1, sw), F32)],
        scratch_shapes=[pltpu.VMEM((tt, sw), F32), pltpu.VMEM((tt, sw), F32)],
        compiler_params=_cparams(("parallel", "parallel")),
        name="s5_scan",
    )(u, *args, s0_re, s0_im)


def _cross_kernel(x_ref, g_ref, wq_ref, qg_ref, k_ref, v_ref, wo_ref, o_ref, att_sc):
    x = x_ref[0]
    xn = _rms(x, g_ref[...]).astype(BF16)
    q = _dot(xn, wq_ref[...].astype(BF16))
    for h in range(MEM_HEADS):
        sl = slice(h * HEAD_DIM, (h + 1) * HEAD_DIM)
        qh = _rms(q[:, sl], qg_ref[...]).astype(BF16)
        s = _dot_nt(qh, k_ref[:, sl].astype(BF16)) * (HEAD_DIM ** -0.5)
        p = jnp.exp(s - jnp.max(s, axis=-1, keepdims=True))
        p = p / jnp.sum(p, axis=-1, keepdims=True)
        att_sc[:, sl] = _dot(p.astype(BF16), v_ref[:, sl].astype(BF16)).astype(BF16)
    o_ref[0] = x + _dot(att_sc[...], wo_ref[...].astype(BF16))


def _cross_attn(x, gain, w_q, q_gain, mem_k, mem_v, w_o, layer, tq=512):
    b, l, d = x.shape
    mt, mw = mem_k.shape[2], mem_k.shape[3]
    tq = min(tq, l)
    return pl.pallas_call(
        _cross_kernel,
        grid=(b, l // tq),
        in_specs=[pl.BlockSpec((1, tq, d), lambda i, j: (i, j, 0)),
                  pl.BlockSpec((1, d), lambda i, j: (0, 0)),
                  pl.BlockSpec((None, d, mw), lambda i, j: (layer, 0, 0)),
                  pl.BlockSpec((1, HEAD_DIM), lambda i, j: (0, 0)),
                  pl.BlockSpec((None, None, mt, mw), lambda i, j: (layer, i, 0, 0)),
                  pl.BlockSpec((None, None, mt, mw), lambda i, j: (layer, i, 0, 0)),
                  pl.BlockSpec((None, mw, d), lambda i, j: (layer, 0, 0))],
        out_specs=pl.BlockSpec((1, tq, d), lambda i, j: (i, j, 0)),
        out_shape=jax.ShapeDtypeStruct((b, l, d), F32),
        scratch_shapes=[pltpu.VMEM((tq, mw), BF16)],
        compiler_params=_cparams(("parallel", "parallel")),
        name="cross_attn",
    )(x, gain.reshape(1, d), w_q, q_gain.reshape(1, HEAD_DIM), mem_k, mem_v, w_o)


def _lane_vec(vals, lane0):
    v = jnp.zeros((1, LANES), F32)
    return v.at[0, lane0:lane0 + vals.shape[0]].set(vals.astype(F32))


def _head_expand():
    rows = jnp.arange(LANES)[None, :, None]
    cols = jnp.arange(SSD_INNER // SSD_GROUPS)[None, None, :]
    g = jnp.arange(SSD_GROUPS)[:, None, None]
    return (rows == DT_LANE0 + g * (SSD_HEADS // SSD_GROUPS) + cols // SSD_HD).astype(F32)


def _block_diag(m):
    eye = jnp.eye(S5_GB, dtype=m.dtype)
    bd = jnp.einsum("bgxy,gh->bgxhy", m, eye)
    nb, g, a, _, bb = bd.shape
    return bd.reshape(nb, g * a, g * bb)


def _pad_rows(x, rows):
    return jnp.pad(x, ((0, 0), (0, rows - x.shape[1]), (0, 0)))


def _trunk(x, W, mem_k, mem_v, conv0, ssm0, s5_re0, s5_im0, fox_cache):
    b, l, d = x.shape
    t = b * l
    depth = W["norm_mix"].shape[0]
    lp = max(l, SSD_CHUNK)
    e_mat = _head_expand()
    fk, fv, fl, hs, bufs, srs, sis = [], [], [], [], [], [], []
    x2 = x.reshape(t, d)
    for i in range(depth):
        j = i // 2
        if i % 2 == 0:
            q_bf, k_f, k_bf, v_f, v_bf, zx, tail = _inproj_even(
                x2, W["norm_mix"][i], W["w_in_even"], W["w_zx"][j], j, W["fox_q_norm"][j], W["fox_k_norm"][j])
            zx3 = zx.reshape(b, l, -1)
            tail3 = tail.reshape(b, l, LANES)
            tail_p = _pad_rows(tail3, lp) if lp != l else tail3
            lf, negc, c_tm = _forget(tail_p, _lane_vec(W["fox_b_forget"][j], 0))
            lf = lf[:, :l]
            if fox_cache is None:
                o_fox = _fox_prompt(q_bf.reshape(b, l, FOX_W), k_bf.reshape(b, l, FOX_W),
                                    v_bf.reshape(b, l, FOX_W), negc).reshape(t, FOX_W)
            else:
                k_pool, v_pool, loc, tot, page_table = fox_cache
                rows = FOX_HEADS * l
                q_ht = jnp.transpose(q_bf.reshape(b, l, FOX_HEADS, HEAD_DIM), (0, 2, 1, 3)).reshape(b, rows, HEAD_DIM)
                o_ht = _fox_decode(q_ht, _pad_rows(k_f.reshape(b, rows, HEAD_DIM), LANES),
                                   _pad_rows(v_f.reshape(b, rows, HEAD_DIM), LANES),
                                   _pad_rows(c_tm[:, :l].reshape(b, rows, 1), LANES).reshape(b, 1, LANES),
                                   k_pool, v_pool, loc, tot, page_table, j, l)
                o_fox = jnp.transpose(o_ht.reshape(b, FOX_HEADS, l, HEAD_DIM), (0, 2, 1, 3)).reshape(t, FOX_W)
            xc, new_buf = _conv(zx3, SSD_INNER, conv0[j], W["ssd_conv_w"][j], W["ssd_conv_b"][j])
            if lp != l:
                xc_p, z_p = _pad_rows(xc, lp), _pad_rows(zx3[:, :, :SSD_INNER], lp)
            else:
                xc_p, z_p = xc, zx3
            yg, h_last = _ssd(xc_p, tail_p, z_p, _lane_vec(W["ssd_dt_bias"][j], DT_LANE0),
                              _lane_vec(W["ssd_A_log"][j], DT_LANE0),
                              jnp.repeat(W["ssd_D"][j].astype(F32), SSD_HD).reshape(1, SSD_INNER),
                              e_mat, ssm0[j].reshape(b, SSD_INNER, SSD_STATE), l)
            yg = yg[:, :l].reshape(t, SSD_INNER)
            x2 = _outproj_even(o_fox, yg, W["ssd_norm"][j], W["w_out_even"], j, x2)
            fk.append(k_f.reshape(b, l, FOX_HEADS, HEAD_DIM))
            fv.append(v_f.reshape(b, l, FOX_HEADS, HEAD_DIM))
            fl.append(lf)
            hs.append(h_last.reshape(b, SSD_HEADS, SSD_HD, SSD_STATE))
            bufs.append(new_buf)
        else:
            (u,) = _dense(x2, W["w_in_odd"], layer=j, n_cols=d, tn=512, pro="norm", gain=W["norm_mix"][i],
                          name="inproj_odd")
            p = W["s5_packed"][j]
            ngb = p["bbd_re"].shape[0]
            g_bf, s_re, s_im = _s5(u.reshape(b, l, d), p,
                                   s5_re0[j].reshape(b, ngb, 1, -1), s5_im0[j].reshape(b, ngb, 1, -1))
            (x2,) = _dense(g_bf.reshape(t, d), W["s5_w_glu"], layer=j, n_cols=d, tn=512, col0=0, col0_2=d,
                           epi="glu_res", res=x2, name="s5_glu_out")
            srs.append(s_re.reshape(b, -1, S5_STATE))
            sis.append(s_im.reshape(b, -1, S5_STATE))
        x3 = _cross_attn(x2.reshape(b, l, d), W["norm_cross"][i], W["w_mq"], W["mem_q_norm"][i],
                         mem_k, mem_v, W["w_mo"], i)
        x2 = x3.reshape(t, d)
        hid = W["w_ffn_down"].shape[1]
        (hmid,) = _dense(x2, W["w_ffn_up"], layer=i, n_cols=hid, tn=512, col0=0, col0_2=hid, pro="norm",
                         gain=W["norm_ffn"][i], epi="swiglu", out_dtypes=(BF16,), name="ffn_up")
        (x2,) = _dense(hmid, W["w_ffn_down"], layer=i, n_cols=d, tn=256, epi="residual", res=x2, name="ffn_down")
    return (x2.reshape(b, l, d), jnp.stack(fk), jnp.stack(fv), jnp.stack(fl), jnp.stack(hs), jnp.stack(bufs),
            jnp.stack(srs), jnp.stack(sis))


def _pack_s5(a_re, a_im, b_re, b_im, c_re, c_im, d_skip, log_dt):
    g, n, k = b_re.shape
    ngb = g // S5_GB
    ps_re, ps_im, pc_re, pc_im, bb_re, bb_im = _s5_prep(
        a_re.astype(F32), a_im.astype(F32), log_dt.astype(F32),
        jnp.transpose(b_re, (2, 0, 1)).astype(F32), jnp.transpose(b_im, (2, 0, 1)).astype(F32))

    def bmat(bb):
        return _block_diag(jnp.transpose(bb, (1, 0, 2)).reshape(ngb, S5_GB, k, n)).astype(BF16)

    def cmat(cm):
        return _block_diag(jnp.transpose(cm.astype(F32), (0, 2, 1)).reshape(ngb, S5_GB, n, k)).astype(BF16)

    def rows(pw):
        return jnp.transpose(pw.reshape(pw.shape[0], ngb, S5_GB * n), (1, 0, 2))

    return dict(bbd_re=bmat(bb_re), bbd_im=bmat(bb_im), cbd_re=cmat(c_re), cbd_im=cmat(c_im),
                ps_re=rows(ps_re), ps_im=rows(ps_im), pc_re=rows(pc_re), pc_im=rows(pc_im),
                d_vec=d_skip.astype(F32).reshape(ngb, 1, S5_GB * k))


def kernel(x_prompt, x_sample, mem_prompt, cache_fox_k, cache_fox_v, cache_fox_logf, cache_mem_k, cache_mem_v,
           state_ssd, state_conv, state_s5_re, state_s5_im, page_table,
           norm_mix, norm_cross, norm_mem, norm_ffn,
           w_in_even, fox_b_forget, fox_q_norm, fox_k_norm, ssd_conv_w, ssd_conv_b, ssd_dt_bias, ssd_A_log,
           ssd_D, ssd_norm, w_out_even,
           w_in_odd, s5_A_re, s5_A_im, s5_B_re, s5_B_im, s5_C_re, s5_C_im, s5_D, s5_log_dt, s5_w_glu,
           w_mq, w_mkv, mem_q_norm, mem_k_norm, w_mo, w_ffn_up, w_ffn_down):
    depth = norm_mix.shape[0]
    n_even, n_odd = w_in_even.shape[0], w_in_odd.shape[0]
    b, l, d = x_prompt.shape
    z0 = 3 * FOX_W + FOX_HEADS
    W = {
        "norm_mix": norm_mix, "norm_cross": norm_cross, "norm_ffn": norm_ffn,
        "w_in_even": w_in_even,
        "w_zx": [w_in_even[j, :, z0:z0 + SSD_INNER + CONV_DIM] for j in range(n_even)],
        "fox_b_forget": fox_b_forget, "fox_q_norm": fox_q_norm, "fox_k_norm": fox_k_norm,
        "ssd_conv_w": ssd_conv_w, "ssd_conv_b": ssd_conv_b, "ssd_dt_bias": ssd_dt_bias, "ssd_A_log": ssd_A_log,
        "ssd_D": ssd_D, "ssd_norm": ssd_norm, "w_out_even": w_out_even,
        "w_in_odd": w_in_odd, "s5_w_glu": s5_w_glu,
        "s5_packed": [_pack_s5(s5_A_re[j], s5_A_im[j], s5_B_re[j], s5_B_im[j], s5_C_re[j], s5_C_im[j],
                               s5_D[j], s5_log_dt[j]) for j in range(n_odd)],
        "w_mq": w_mq, "mem_q_norm": mem_q_norm, "w_mo": w_mo, "w_ffn_up": w_ffn_up, "w_ffn_down": w_ffn_down,
    }
    mt = mem_prompt.shape[1]
    mw = w_mkv.shape[2] // 2
    mem2 = mem_prompt.reshape(b * mt, d)
    mk, mv = [], []
    for i in range(depth):
        (k_i,) = _dense(mem2, w_mkv, layer=i, n_cols=mw, tn=mw, col0=0, pro="norm", gain=norm_mem[i],
                        epi="headnorm", head_gain=mem_k_norm[i], name="mem_k_proj")
        (v_i,) = _dense(mem2, w_mkv, layer=i, n_cols=mw, tn=mw, col0=mw, pro="norm", gain=norm_mem[i],
                        name="mem_v_proj")
        mk.append(k_i.reshape(b, mt, mw))
        mv.append(v_i.reshape(b, mt, mw))
    mem_k_p = jnp.stack(mk)
    mem_v_p = jnp.stack(mv)
    n_grp = s5_A_re.shape[1]
    (y_prompt, fox_k_p, fox_v_p, fox_logf_p, ssd_p, conv_p, s5_re_p, s5_im_p) = _trunk(
        x_prompt, W, mem_k_p, mem_v_p,
        jnp.zeros((n_even, b, SSD_CONV - 1, CONV_DIM), F32),
        jnp.zeros((n_even, b, SSD_HEADS, SSD_HD, SSD_STATE), F32),
        jnp.zeros((n_odd, b, n_grp, S5_STATE), F32),
        jnp.zeros((n_odd, b, n_grp, S5_STATE), F32),
        None)
    db = x_sample.shape[0]
    n_pool, page = cache_fox_k.shape[1], cache_fox_k.shape[2]
    assert page == LANES
    width = page * FOX_HEADS
    lf_t = jnp.transpose(cache_fox_logf.astype(F32), (0, 2, 1, 3)).reshape(n_even, page, n_pool * FOX_HEADS)
    incl = _pool_cumsum(lf_t).reshape(n_even, page, n_pool, FOX_HEADS)
    loc = jnp.transpose(incl, (0, 2, 1, 3)).reshape(n_even, n_pool, 1, width)
    tot = jnp.tile(incl[:, page - 1], (1, 1, page)).reshape(n_even, n_pool, 1, width)
    fox_cache = (cache_fox_k.reshape(n_even, n_pool, width, HEAD_DIM),
                 cache_fox_v.reshape(n_even, n_pool, width, HEAD_DIM), loc, tot, page_table)
    (y_sample, fox_k_s, fox_v_s, fox_logf_s, ssd_s, conv_s, s5_re_s, s5_im_s) = _trunk(
        x_sample, W, cache_mem_k.reshape(depth, db, mt, mw), cache_mem_v.reshape(depth, db, mt, mw),
        state_conv, state_ssd, state_s5_re, state_s5_im, fox_cache)
    hd = mw // MEM_HEADS
    return (y_prompt, y_sample,
            fox_k_p, fox_v_p, fox_logf_p,
            mem_k_p.reshape(depth, b, mt, MEM_HEADS, hd), mem_v_p.reshape(depth, b, mt, MEM_HEADS, hd),
            ssd_p, conv_p, s5_re_p, s5_im_p,
            fox_k_s, fox_v_s, fox_logf_s, ssd_s, conv_s, s5_re_s, s5_im_s)
```

```python
import functools
import math

import jax
import jax.numpy as jnp
from jax import lax
from jax.experimental import pallas as pl
from jax.experimental.pallas import tpu as pltpu

F32 = jnp.float32
BF16 = jnp.bfloat16
EPS = 1e-6
LANES = 128
SUBLANES = 8
VMEM_LIMIT_BYTES = 56 * 1024 * 1024

FOX_HEADS = 8
HEAD_DIM = 128
FOX_W = FOX_HEADS * HEAD_DIM
SSD_HEADS = 32
SSD_HD = 64
SSD_GROUPS = 4
SSD_STATE = 128
SSD_CHUNK = 128
SSD_INNER = SSD_HEADS * SSD_HD
SSD_CONV = 4
CONV_DIM = SSD_INNER + 2 * SSD_GROUPS * SSD_STATE
DT_LANE0 = FOX_HEADS
S5_GROUP = 16
S5_STATE = 64
S5_GB = 16
MEM_HEADS = 4
HIGHEST = lax.Precision.HIGHEST


def _cparams(sem):
    return pltpu.CompilerParams(dimension_semantics=sem, vmem_limit_bytes=VMEM_LIMIT_BYTES)


def _gelu_tanh(x):
    return 0.5 * x * (1.0 + jnp.tanh(math.sqrt(2.0 / math.pi) * (x + 0.044715 * x * x * x)))


def _softplus(x):
    return jnp.maximum(x, 0.0) + jnp.log1p(jnp.exp(-jnp.abs(x)))


def _silu(x):
    return x * jax.nn.sigmoid(x)


def _rms(x, gain):
    return x * lax.rsqrt(jnp.mean(x * x, axis=-1, keepdims=True) + EPS) * gain


def _dot(a, b):
    return jnp.dot(a, b, preferred_element_type=F32)


def _dot_nt(a, b):
    return lax.dot_general(a, b, (((1,), (1,)), ((), ())), preferred_element_type=F32)


def _dot_exact(a, b):
    return jnp.dot(a, b, preferred_element_type=F32, precision=HIGHEST)


def _dense_kernel(*refs, pro, epi, n_out):
    it = iter(refs)
    x_ref = next(it)
    gain_ref = next(it) if pro == "norm" else None
    w_ref = next(it)
    w2_ref = next(it) if epi in ("swiglu", "glu_res") else None
    res_ref = next(it) if epi in ("residual", "glu_res") else None
    hg_ref = next(it) if epi == "headnorm" else None
    out_refs = [next(it) for _ in range(n_out)]
    xs_ref = next(it) if pro != "none" else None

    if pro != "none":
        @pl.when(pl.program_id(1) == 0)
        def _():
            xf = x_ref[...].astype(F32)
            if pro == "norm":
                xf = _rms(xf, gain_ref[...])
            elif pro == "gelu":
                xf = _gelu_tanh(xf)
            xs_ref[...] = xf.astype(BF16)
        lhs = xs_ref[...]
    else:
        lhs = x_ref[...]

    acc = _dot(lhs, w_ref[...].astype(BF16))
    if epi == "swiglu":
        acc = _silu(acc) * _dot(lhs, w2_ref[...].astype(BF16))
    elif epi == "glu_res":
        acc = res_ref[...] + acc * jax.nn.sigmoid(_dot(lhs, w2_ref[...].astype(BF16)))
    elif epi == "residual":
        acc = res_ref[...] + acc

    if epi == "headnorm":
        for c in range(acc.shape[1] // HEAD_DIM):
            sl = slice(c * HEAD_DIM, (c + 1) * HEAD_DIM)
            blk = _rms(acc[:, sl], hg_ref[...])
            for o in out_refs:
                o[:, sl] = blk.astype(o.dtype)
    else:
        for o in out_refs:
            o[...] = acc.astype(o.dtype)


def _wspec(w, layer, k, tn, blk0):
    if w.ndim == 2:
        return pl.BlockSpec((k, tn), lambda i, j: (0, blk0 + j))
    return pl.BlockSpec((None, k, tn), lambda i, j: (layer, 0, blk0 + j))


def _dense(x, w, *, n_cols, tn, name, layer=0, col0=0, col0_2=None, pro="none", gain=None, epi="plain",
           res=None, head_gain=None, out_dtypes=(F32,), tm_max=1024):
    m, k = x.shape
    tm = min(tm_max, m)
    assert m % tm == 0 and n_cols % tn == 0 and col0 % tn == 0
    b0 = col0 // tn
    in_specs = [pl.BlockSpec((tm, k), lambda i, j: (i, 0))]
    args = [x]
    if pro == "norm":
        in_specs.append(pl.BlockSpec((1, k), lambda i, j: (0, 0)))
        args.append(gain.reshape(1, k).astype(F32))
    in_specs.append(_wspec(w, layer, k, tn, b0))
    args.append(w)
    if epi in ("swiglu", "glu_res"):
        assert col0_2 % tn == 0
        in_specs.append(_wspec(w, layer, k, tn, col0_2 // tn))
        args.append(w)
    if epi in ("residual", "glu_res"):
        in_specs.append(pl.BlockSpec((tm, tn), lambda i, j: (i, j)))
        args.append(res)
    if epi == "headnorm":
        in_specs.append(pl.BlockSpec((1, HEAD_DIM), lambda i, j: (0, 0)))
        args.append(head_gain.reshape(1, HEAD_DIM).astype(F32))
    out_shape = [jax.ShapeDtypeStruct((m, n_cols), dt) for dt in out_dtypes]
    out_specs = [pl.BlockSpec((tm, tn), lambda i, j: (i, j)) for _ in out_dtypes]
    scratch = [pltpu.VMEM((tm, k), BF16)] if pro != "none" else []
    outs = pl.pallas_call(
        functools.partial(_dense_kernel, pro=pro, epi=epi, n_out=len(out_dtypes)),
        grid=(m // tm, n_cols // tn),
        in_specs=in_specs, out_specs=out_specs, out_shape=out_shape,
        scratch_shapes=scratch,
        compiler_params=_cparams(("parallel", "arbitrary")),
        name=name,
    )(*args)
    return outs


def _inproj_even_kernel(x_ref, g_ref, wqkv_ref, wzx_ref, wt_ref, qg_ref, kg_ref,
                        q_ref, kf_ref, kb_ref, vf_ref, vb_ref, zx_ref, tail_ref, xs_ref, *, nq, nzx):
    j = pl.program_id(1)

    @pl.when(j == 0)
    def _():
        xs_ref[...] = _rms(x_ref[...], g_ref[...]).astype(BF16)

    def headnorm(acc, gain, outs):
        for c in range(acc.shape[1] // HEAD_DIM):
            sl = slice(c * HEAD_DIM, (c + 1) * HEAD_DIM)
            blk = _rms(acc[:, sl], gain)
            for o in outs:
                o[:, sl] = blk.astype(o.dtype)

    @pl.when(j < nq)
    def _():
        headnorm(_dot(xs_ref[...], wqkv_ref[...].astype(BF16)), qg_ref[...], (q_ref,))

    @pl.when((j >= nq) & (j < 2 * nq))
    def _():
        headnorm(_dot(xs_ref[...], wqkv_ref[...].astype(BF16)), kg_ref[...], (kf_ref, kb_ref))

    @pl.when((j >= 2 * nq) & (j < 3 * nq))
    def _():
        acc = _dot(xs_ref[...], wqkv_ref[...].astype(BF16))
        vf_ref[...] = acc
        vb_ref[...] = acc.astype(BF16)

    @pl.when((j >= 3 * nq) & (j < 3 * nq + nzx))
    def _():
        zx_ref[...] = _dot(xs_ref[...], wzx_ref[...].astype(BF16))

    @pl.when(j == 3 * nq + nzx)
    def _():
        tail_ref[...] = _dot(xs_ref[...], wt_ref[...].astype(BF16))


def _inproj_even(x, gain, w_all, w_zx, w_tail, layer, q_gain, k_gain, tn=512, tm_max=1024):
    m, k = x.shape
    tm = min(tm_max, m)
    nq = FOX_W // tn
    nzx = w_zx.shape[1] // tn

    def clip(j, lo, n):
        return jnp.clip(j - lo, 0, n - 1)

    tile = lambda lo, n: pl.BlockSpec((tm, tn), lambda i, j: (i, clip(j, lo, n)))
    outs = pl.pallas_call(
        functools.partial(_inproj_even_kernel, nq=nq, nzx=nzx),
        grid=(m // tm, 3 * nq + nzx + 1),
        in_specs=[pl.BlockSpec((tm, k), lambda i, j: (i, 0), pipeline_mode=pl.Buffered(1)),
                  pl.BlockSpec((1, k), lambda i, j: (0, 0)),
                  pl.BlockSpec((None, k, tn), lambda i, j: (layer, 0, clip(j, 0, 3 * nq))),
                  pl.BlockSpec((k, tn), lambda i, j: (0, clip(j, 3 * nq, nzx))),
                  pl.BlockSpec((None, k, LANES), lambda i, j: (layer, 0, 0), pipeline_mode=pl.Buffered(1)),
                  pl.BlockSpec((1, HEAD_DIM), lambda i, j: (0, 0)),
                  pl.BlockSpec((1, HEAD_DIM), lambda i, j: (0, 0))],
        out_specs=[tile(0, nq), tile(nq, nq), tile(nq, nq), tile(2 * nq, nq), tile(2 * nq, nq),
                   tile(3 * nq, nzx), pl.BlockSpec((tm, LANES), lambda i, j: (i, 0))],
        out_shape=[jax.ShapeDtypeStruct((m, FOX_W), BF16),
                   jax.ShapeDtypeStruct((m, FOX_W), F32), jax.ShapeDtypeStruct((m, FOX_W), BF16),
                   jax.ShapeDtypeStruct((m, FOX_W), F32), jax.ShapeDtypeStruct((m, FOX_W), BF16),
                   jax.ShapeDtypeStruct((m, nzx * tn), F32), jax.ShapeDtypeStruct((m, LANES), F32)],
        scratch_shapes=[pltpu.VMEM((tm, k), BF16)],
        compiler_params=_cparams(("parallel", "arbitrary")),
        name="inproj_even",
    )(x, gain.reshape(1, k), w_all, w_zx, w_tail, q_gain.reshape(1, HEAD_DIM), k_gain.reshape(1, HEAD_DIM))
    return outs


def _outproj_kernel(o_ref, y_ref, g_ref, w_ref, res_ref, out_ref, yn_ref):
    k1 = o_ref.shape[1]

    @pl.when(pl.program_id(1) == 0)
    def _():
        yn_ref[...] = _rms(y_ref[...], g_ref[...]).astype(BF16)
    acc = _dot(o_ref[...], w_ref[:k1, :].astype(BF16)) + _dot(yn_ref[...], w_ref[k1:, :].astype(BF16))
    out_ref[...] = res_ref[...] + acc


def _outproj_even(o_fox, yg, gain, w_out, layer, res, tn=256, tm_max=1024):
    m, d = res.shape
    tm = min(tm_max, m)
    k1, k2 = o_fox.shape[1], yg.shape[1]
    return pl.pallas_call(
        _outproj_kernel,
        grid=(m // tm, d // tn),
        in_specs=[
            pl.BlockSpec((tm, k1), lambda i, j: (i, 0)),
            pl.BlockSpec((tm, k2), lambda i, j: (i, 0)),
            pl.BlockSpec((1, k2), lambda i, j: (0, 0)),
            pl.BlockSpec((None, k1 + k2, tn), lambda i, j: (layer, 0, j)),
            pl.BlockSpec((tm, tn), lambda i, j: (i, j)),
        ],
        out_specs=pl.BlockSpec((tm, tn), lambda i, j: (i, j)),
        out_shape=jax.ShapeDtypeStruct((m, d), F32),
        scratch_shapes=[pltpu.VMEM((tm, k2), BF16)],
        compiler_params=_cparams(("parallel", "arbitrary")),
        name="outproj_even",
    )(o_fox, yg, gain.reshape(1, k2).astype(F32), w_out, res)


def _lane_cumsum(x):
    lane = lax.broadcasted_iota(jnp.int32, x.shape, 1)
    k = 1
    while k < LANES:
        x = x + jnp.where(lane >= k, pltpu.roll(x, k, axis=1), 0.0)
        k *= 2
    return x


def _forget_kernel(raw_ref, b_ref, lf_ref, negc_ref, ctm_ref):
    lp = raw_ref.shape[1]
    lf = -_softplus(-(raw_ref[0] + b_ref[...]))
    lf_ref[0] = lf[:, :FOX_HEADS]
    row_i = lax.broadcasted_iota(jnp.int32, (LANES, LANES), 0)
    col_i = lax.broadcasted_iota(jnp.int32, (LANES, LANES), 1)
    tri = (col_i <= row_i).astype(F32)
    ctm_ref[0] = _dot_exact(tri, lf[:LANES, :])[:, :FOX_HEADS]
    carry = jnp.zeros((FOX_HEADS, 1), F32)
    for c in range(lp // LANES):
        blk = lf[c * LANES:(c + 1) * LANES, :].T[:FOX_HEADS, :]
        cs = _lane_cumsum(blk) + carry
        negc_ref[0, :, c * LANES:(c + 1) * LANES] = -cs
        carry = cs[:, LANES - 1:LANES]


def _forget(raw, b_vec):
    b, lp, _ = raw.shape
    return pl.pallas_call(
        _forget_kernel,
        grid=(b,),
        in_specs=[pl.BlockSpec((1, lp, LANES), lambda i: (i, 0, 0)),
                  pl.BlockSpec((1, LANES), lambda i: (0, 0))],
        out_specs=[pl.BlockSpec((1, lp, FOX_HEADS), lambda i: (i, 0, 0)),
                   pl.BlockSpec((1, FOX_HEADS, lp), lambda i: (i, 0, 0)),
                   pl.BlockSpec((1, LANES, FOX_HEADS), lambda i: (i, 0, 0))],
        out_shape=[jax.ShapeDtypeStruct((b, lp, FOX_HEADS), F32),
                   jax.ShapeDtypeStruct((b, FOX_HEADS, lp), F32),
                   jax.ShapeDtypeStruct((b, LANES, FOX_HEADS), F32)],
        compiler_params=_cparams(("parallel",)),
        name="forget_gates",
    )(raw, b_vec)


def _fox_prompt_kernel(q_ref, k_ref, v_ref, nb_ref, o_ref, m_sc, acc_sc, mask_sc, *, tq, tk):
    qi = pl.program_id(1)
    kj = pl.program_id(2)

    @pl.when(kj == 0)
    def _():
        m_sc[...] = jnp.full(m_sc.shape, -jnp.inf, F32)
        acc_sc[...] = jnp.zeros(acc_sc.shape, F32)

    @pl.when(kj * tk <= qi * tq + (tq - 1))
    def _():
        row = qi * tq + lax.broadcasted_iota(jnp.int32, (tq, tk), 0)
        col = kj * tk + lax.broadcasted_iota(jnp.int32, (tq, tk), 1)
        mask_sc[...] = jnp.where(col <= row, 0.0, -jnp.inf)
        ones = jnp.ones((tk, HEAD_DIM), BF16)
        for h in range(FOX_HEADS):
            sl = slice(h * HEAD_DIM, (h + 1) * HEAD_DIM)
            s = _dot_nt(q_ref[0, :, sl], k_ref[0, :, sl]) * (HEAD_DIM ** -0.5)
            s = s + nb_ref[0, h:h + 1, :] + mask_sc[...]
            m_old = m_sc[h]
            m_new = jnp.maximum(m_old, jnp.max(s, axis=-1, keepdims=True))
            p = jnp.exp(s - m_new).astype(BF16)
            pv = _dot(p, jnp.concatenate([v_ref[0, :, sl], ones], axis=1))
            acc_sc[h] = jnp.exp(m_old - m_new) * acc_sc[h] + pv
            m_sc[h] = m_new

    @pl.when(kj == pl.num_programs(2) - 1)
    def _():
        for h in range(FOX_HEADS):
            acc = acc_sc[h]
            o_ref[0, :, h * HEAD_DIM:(h + 1) * HEAD_DIM] = (acc[:, :HEAD_DIM] / acc[:, HEAD_DIM:]).astype(o_ref.dtype)


def _fox_prompt(q, k, v, negc, tq=512, tk=512):
    b, l, _ = q.shape
    tq, tk = min(tq, l), min(tk, l)
    nq, nk = l // tq, l // tk

    def last_needed(qi, kj):
        return jnp.minimum(kj, (qi * tq + tq - 1) // tk)

    return pl.pallas_call(
        functools.partial(_fox_prompt_kernel, tq=tq, tk=tk),
        grid=(b, nq, nk),
        in_specs=[pl.BlockSpec((1, tq, FOX_W), lambda bi, qi, kj: (bi, qi, 0)),
                  pl.BlockSpec((1, tk, FOX_W), lambda bi, qi, kj: (bi, last_needed(qi, kj), 0)),
                  pl.BlockSpec((1, tk, FOX_W), lambda bi, qi, kj: (bi, last_needed(qi, kj), 0)),
                  pl.BlockSpec((1, FOX_HEADS, tk), lambda bi, qi, kj: (bi, 0, last_needed(qi, kj)))],
        out_specs=pl.BlockSpec((1, tq, FOX_W), lambda bi, qi, kj: (bi, qi, 0)),
        out_shape=jax.ShapeDtypeStruct((b, l, FOX_W), BF16),
        scratch_shapes=[pltpu.VMEM((FOX_HEADS, tq, 1), F32),
                        pltpu.VMEM((FOX_HEADS, tq, 2 * HEAD_DIM), F32),
                        pltpu.VMEM((tq, tk), F32)],
        compiler_params=_cparams(("parallel", "parallel", "arbitrary")),
        name="fox_prompt",
    )(q, k, v, negc)


def _pool_cumsum_kernel(x_ref, o_ref):
    n = x_ref.shape[0]
    row_i = lax.broadcasted_iota(jnp.int32, (n, n), 0)
    col_i = lax.broadcasted_iota(jnp.int32, (n, n), 1)
    o_ref[...] = _dot_exact((col_i <= row_i).astype(F32), x_ref[...])


def _pool_cumsum(x, tc=2048):
    nl, page, cols = x.shape
    tc = tc if cols % tc == 0 else cols
    return pl.pallas_call(
        _pool_cumsum_kernel,
        grid=(nl, cols // tc),
        in_specs=[pl.BlockSpec((None, page, tc), lambda i, j: (i, 0, j))],
        out_specs=pl.BlockSpec((None, page, tc), lambda i, j: (i, 0, j)),
        out_shape=jax.ShapeDtypeStruct((nl, page, cols), F32),
        compiler_params=_cparams(("parallel", "parallel")),
        name="pool_logf_cumsum",
    )(x)


def _fox_decode_kernel(pt_ref, q_ref, *refs, gp, t_new):
    kv_refs = refs[:4 * gp]
    kn_ref, vn_ref, cn_ref, o_ref, m_sc, l_sc, acc_sc, carry_sc = refs[4 * gp:]
    p = pl.program_id(1)
    n_past = pl.num_programs(1) - 1
    rows = q_ref.shape[1]
    scale = HEAD_DIM ** -0.5

    @pl.when(p == 0)
    def _():
        m_sc[...] = jnp.full(m_sc.shape, -jnp.inf, F32)
        l_sc[...] = jnp.zeros(l_sc.shape, F32)
        acc_sc[...] = jnp.zeros(acc_sc.shape, F32)
        carry_sc[...] = jnp.zeros(carry_sc.shape, F32)

    def update(s, v, state):
        m_old, l_old, acc = state
        m_new = jnp.maximum(m_old, jnp.max(s, axis=-1, keepdims=True))
        alpha = jnp.exp(m_old - m_new)
        pr = jnp.exp(s - m_new)
        l_new = alpha * l_old + jnp.sum(pr, axis=-1, keepdims=True)
        acc = alpha * acc + _dot(pr.astype(BF16), v.astype(BF16))
        return m_new, l_new, acc

    @pl.when(p < n_past)
    def _():
        q = q_ref[0]
        width = kv_refs[0].shape[0]
        row = lax.broadcasted_iota(jnp.int32, (rows, width), 0)
        col = lax.broadcasted_iota(jnp.int32, (rows, width), 1)
        own_head = (col % FOX_HEADS) == (row // t_new)
        carry = carry_sc[...]
        scores = []
        for g in range(gp):
            k_ref, _, loc_ref, tot_ref = kv_refs[4 * g:4 * g + 4]
            s = _dot_nt(q, k_ref[...].astype(BF16)) * scale - (carry + loc_ref[...])
            scores.append(jnp.where(own_head, s, -jnp.inf))
            carry = carry + tot_ref[...]
        carry_sc[...] = carry
        m_old = m_sc[...]
        m_new = m_old
        for s in scores:
            m_new = jnp.maximum(m_new, jnp.max(s, axis=-1, keepdims=True))
        alpha = jnp.exp(m_old - m_new)
        l_new = alpha * l_sc[...]
        acc = alpha * acc_sc[...]
        for g, s in enumerate(scores):
            pr = jnp.exp(s - m_new)
            l_new = l_new + jnp.sum(pr, axis=-1, keepdims=True)
            acc = acc + _dot(pr.astype(BF16), kv_refs[4 * g + 1][...].astype(BF16))
        m_sc[...], l_sc[...], acc_sc[...] = m_new, l_new, acc

    @pl.when(p == n_past)
    def _():
        width = kn_ref.shape[1]
        row = lax.broadcasted_iota(jnp.int32, (rows, width), 0)
        col = lax.broadcasted_iota(jnp.int32, (rows, width), 1)
        keep = ((col % FOX_HEADS) == (row // t_new)) & ((col // FOX_HEADS) <= (row % t_new))
        s = _dot_nt(q_ref[0], kn_ref[0].astype(BF16)) * scale - (carry_sc[:, :width] + cn_ref[0])
        m, l, acc = update(jnp.where(keep, s, -jnp.inf), vn_ref[0], (m_sc[...], l_sc[...], acc_sc[...]))
        o_ref[0] = (acc / l).astype(o_ref.dtype)


def _fox_decode(q, k_new, v_new, c_new, k_pool, v_pool, loc, tot, page_table, layer, t_new, gp=8):
    b, rows, _ = q.shape
    n_pages = page_table.shape[1]
    gp = max(g for g in range(1, gp + 1) if n_pages % g == 0)
    n_steps = n_pages // gp
    width = k_pool.shape[2]

    def page_map(g):
        return lambda bi, p, pt: (layer, pt[bi, jnp.minimum(p, n_steps - 1) * gp + g], 0, 0)

    in_specs = [pl.BlockSpec((1, rows, HEAD_DIM), lambda bi, p, pt: (bi, 0, 0))]
    args = [q]
    for g in range(gp):
        in_specs += [pl.BlockSpec((None, None, width, HEAD_DIM), page_map(g)),
                     pl.BlockSpec((None, None, width, HEAD_DIM), page_map(g)),
                     pl.BlockSpec((None, None, 1, width), page_map(g)),
                     pl.BlockSpec((None, None, 1, width), page_map(g))]
        args += [k_pool, v_pool, loc, tot]
    new_spec = pl.BlockSpec((1, k_new.shape[1], HEAD_DIM), lambda bi, p, pt: (bi, 0, 0))
    in_specs += [new_spec, new_spec, pl.BlockSpec((1, 1, c_new.shape[2]), lambda bi, p, pt: (bi, 0, 0))]
    args += [k_new, v_new, c_new]
    grid_spec = pltpu.PrefetchScalarGridSpec(
        num_scalar_prefetch=1,
        grid=(b, n_steps + 1),
        in_specs=in_specs,
        out_specs=pl.BlockSpec((1, rows, HEAD_DIM), lambda bi, p, pt: (bi, 0, 0)),
        scratch_shapes=[pltpu.VMEM((rows, 1), F32), pltpu.VMEM((rows, 1), F32),
                        pltpu.VMEM((rows, HEAD_DIM), F32), pltpu.VMEM((1, width), F32)],
    )
    return pl.pallas_call(
        functools.partial(_fox_decode_kernel, gp=gp, t_new=t_new), grid_spec=grid_spec,
        out_shape=jax.ShapeDtypeStruct((b, rows, HEAD_DIM), BF16),
        compiler_params=_cparams(("parallel", "arbitrary")),
        name="fox_decode",
    )(page_table, *args)


def _conv_kernel(x_ref, buf_ref, w_ref, b_ref, y_ref, nb_ref, full_sc):
    l = x_ref.shape[1]
    pad = SUBLANES
    full_sc[pad - (SSD_CONV - 1):pad, :] = buf_ref[0]
    full_sc[pad:pad + l, :] = x_ref[0]
    acc = b_ref[...] + full_sc[pad - 3:pad - 3 + l, :] * w_ref[0:1, :]
    for kk in range(1, SSD_CONV):
        acc = acc + full_sc[pad - 3 + kk:pad - 3 + kk + l, :] * w_ref[kk:kk + 1, :]
    y_ref[0] = _silu(acc)
    nb_ref[0] = full_sc[pad + l - (SSD_CONV - 1):pad + l, :]


def _conv(zx, col0, buf, w, bias, tc=512):
    b, l, _ = zx.shape
    c = w.shape[1]
    cb0 = col0 // tc
    return pl.pallas_call(
        _conv_kernel,
        grid=(b, c // tc),
        in_specs=[pl.BlockSpec((1, l, tc), lambda i, j: (i, 0, cb0 + j)),
                  pl.BlockSpec((1, SSD_CONV - 1, tc), lambda i, j: (i, 0, j)),
                  pl.BlockSpec((SSD_CONV, tc), lambda i, j: (0, j)),
                  pl.BlockSpec((1, tc), lambda i, j: (0, j))],
        out_specs=[pl.BlockSpec((1, l, tc), lambda i, j: (i, 0, j)),
                   pl.BlockSpec((1, SSD_CONV - 1, tc), lambda i, j: (i, 0, j))],
        out_shape=[jax.ShapeDtypeStruct((b, l, c), F32),
                   jax.ShapeDtypeStruct((b, SSD_CONV - 1, c), F32)],
        scratch_shapes=[pltpu.VMEM((l + SUBLANES, tc), F32)],
        compiler_params=_cparams(("parallel", "parallel")),
        name="ssd_conv",
    )(zx, buf, w, bias.reshape(1, c))


def _ssd_kernel(xc_ref, tail_ref, z_ref, dtb_ref, alog_ref, dexp_ref, e_ref, h0_ref,
                y_ref, hout_ref, h_sc, *, valid_len):
    c = pl.program_id(1)
    q = SSD_CHUNK
    gw = SSD_INNER // SSD_GROUPS
    hpg = SSD_HEADS // SSD_GROUPS

    @pl.when(c == 0)
    def _():
        h_sc[...] = h0_ref[0]

    lane = lax.broadcasted_iota(jnp.int32, (q, LANES), 1)
    row = lax.broadcasted_iota(jnp.int32, (q, LANES), 0)
    col_i = lax.broadcasted_iota(jnp.int32, (q, q), 1)
    row_i = lax.broadcasted_iota(jnp.int32, (q, q), 0)
    causal = col_i <= row_i
    tri = causal.astype(F32)

    dt = _softplus(tail_ref[0] + dtb_ref[...])
    live = (lane >= DT_LANE0) & (lane < DT_LANE0 + SSD_HEADS) & (c * q + row < valid_len)
    dt = jnp.where(live, dt, 0.0)
    a = dt * (-jnp.exp(alog_ref[...]))
    a_cum = _dot_exact(tri, a)
    a_cum_t = a_cum.T
    dt_t = dt.T
    a_last = a_cum[q - 1:q, :]
    fac = jnp.concatenate([dt * jnp.exp(a_last - a_cum), jnp.exp(a_cum)], axis=0)
    fac_hi = fac.astype(BF16)
    fac_lo = (fac - fac_hi.astype(F32)).astype(BF16)
    fac2 = jnp.concatenate([fac_hi, fac_lo], axis=0)

    for g in range(SSD_GROUPS):
        ex = _dot(fac2, e_ref[g])
        ex = ex[:2 * q] + ex[2 * q:]
        w1_e, ea_e = ex[:q], ex[q:]
        xs = xc_ref[0, :, g * gw:(g + 1) * gw]
        bb = xc_ref[0, :, SSD_INNER + g * SSD_STATE:SSD_INNER + (g + 1) * SSD_STATE].astype(BF16)
        cc = xc_ref[0, :, SSD_INNER + (SSD_GROUPS + g) * SSD_STATE:
                    SSD_INNER + (SSD_GROUPS + g + 1) * SSD_STATE].astype(BF16)
        cb = _dot_nt(cc, bb)
        hg = h_sc[g * gw:(g + 1) * gw, :]
        y = _dot_nt(cc, hg.astype(BF16)) * ea_e
        st = _dot((xs * w1_e).T.astype(BF16), bb)
        for hp in range(hpg // 2):
            pair = xs[:, hp * LANES:(hp + 1) * LANES]
            lane_p = lax.broadcasted_iota(jnp.int32, pair.shape, 1)
            yp = None
            for sub in range(2):
                h = 2 * hp + sub
                ln = DT_LANE0 + g * hpg + h
                seg = a_cum[:, ln:ln + 1] - a_cum_t[ln:ln + 1, :]
                dec = jnp.exp(jnp.where(causal, seg, -jnp.inf))
                mat = (cb * dec * dt_t[ln:ln + 1, :]).astype(BF16)
                in_head = (lane_p >= sub * SSD_HD) & (lane_p < (sub + 1) * SSD_HD)
                rhs = jnp.where(in_head, pair, 0.0).astype(BF16)
                part = _dot(mat, rhs)
                yp = part if yp is None else yp + part
                r0 = g * gw + h * SSD_HD
                h_sc[r0:r0 + SSD_HD, :] = (hg[h * SSD_HD:(h + 1) * SSD_HD, :] * jnp.exp(a_last[:, ln:ln + 1])
                                           + st[h * SSD_HD:(h + 1) * SSD_HD, :])
            cs = slice(g * gw + hp * LANES, g * gw + (hp + 1) * LANES)
            yt = yp + y[:, hp * LANES:(hp + 1) * LANES] + dexp_ref[:, cs] * xs[:, hp * LANES:(hp + 1) * LANES]
            y_ref[0, :, cs] = yt * _silu(z_ref[0, :, cs])

    @pl.when(c == pl.num_programs(1) - 1)
    def _():
        hout_ref[0] = h_sc[...]


def _ssd(xc, tail, zx, dtb_vec, alog_vec, d_exp, e_mat, h0, valid_len):
    b, lp, _ = xc.shape
    nc = lp // SSD_CHUNK
    return pl.pallas_call(
        functools.partial(_ssd_kernel, valid_len=valid_len),
        grid=(b, nc),
        in_specs=[pl.BlockSpec((1, SSD_CHUNK, CONV_DIM), lambda i, c: (i, c, 0)),
                  pl.BlockSpec((1, SSD_CHUNK, LANES), lambda i, c: (i, c, 0)),
                  pl.BlockSpec((1, SSD_CHUNK, SSD_INNER), lambda i, c: (i, c, 0)),
                  pl.BlockSpec((1, LANES), lambda i, c: (0, 0)),
                  pl.BlockSpec((1, LANES), lambda i, c: (0, 0)),
                  pl.BlockSpec((1, SSD_INNER), lambda i, c: (0, 0)),
                  pl.BlockSpec((SSD_GROUPS, LANES, SSD_INNER // SSD_GROUPS), lambda i, c: (0, 0, 0)),
                  pl.BlockSpec((1, SSD_INNER, SSD_STATE), lambda i, c: (i, 0, 0))],
        out_specs=[pl.BlockSpec((1, SSD_CHUNK, SSD_INNER), lambda i, c: (i, c, 0)),
                   pl.BlockSpec((1, SSD_INNER, SSD_STATE), lambda i, c: (i, 0, 0))],
        out_shape=[jax.ShapeDtypeStruct((b, lp, SSD_INNER), F32),
                   jax.ShapeDtypeStruct((b, SSD_INNER, SSD_STATE), F32)],
        scratch_shapes=[pltpu.VMEM((SSD_INNER, SSD_STATE), F32)],
        compiler_params=_cparams(("parallel", "arbitrary")),
        name="ssd_chunked",
    )(xc, tail, zx, dtb_vec, alog_vec, d_exp, e_mat, h0)


S5_TILE = 256
S5_STEPS = S5_TILE // SUBLANES


def _cmul(ar, ai, br, bi):
    return ar * br - ai * bi, ar * bi + ai * br


def _s5_prep_kernel(are_ref, aim_ref, ldt_ref, bre_ref, bim_ref, psr_ref, psi_ref, pcr_ref, pci_ref,
                    bbr_ref, bbi_ref):
    lam_re = jnp.minimum(are_ref[...], -1e-4)
    lam_im = aim_ref[...]
    dt = jnp.exp(ldt_ref[...])
    mag = jnp.exp(lam_re * dt)
    ang = lam_im * dt
    lb_re, lb_im = mag * jnp.cos(ang), mag * jnp.sin(ang)
    nr, ni = lb_re - 1.0, lb_im
    den = lam_re * lam_re + lam_im * lam_im
    coef_re = (nr * lam_re + ni * lam_im) / den
    coef_im = (ni * lam_re - nr * lam_im) / den
    for k in range(S5_GROUP):
        br, bi = bre_ref[k], bim_ref[k]
        bbr_ref[k] = coef_re * br - coef_im * bi
        bbi_ref[k] = coef_re * bi + coef_im * br
    pr, pi = lb_re, lb_im
    for r in range(S5_STEPS):
        psr_ref[r], psi_ref[r] = pr, pi
        if r + 1 < S5_STEPS:
            pr, pi = _cmul(pr, pi, lb_re, lb_im)
    qr, qi = pr, pi
    for c in range(SUBLANES):
        pcr_ref[c], pci_ref[c] = qr, qi
        if c + 1 < SUBLANES:
            qr, qi = _cmul(qr, qi, pr, pi)


def _s5_prep(a_re, a_im, log_dt, b_re_t, b_im_t):
    g, n = a_re.shape
    sd = jax.ShapeDtypeStruct
    return pl.pallas_call(
        _s5_prep_kernel,
        out_shape=[sd((S5_STEPS, g, n), F32), sd((S5_STEPS, g, n), F32),
                   sd((SUBLANES, g, n), F32), sd((SUBLANES, g, n), F32),
                   sd((S5_GROUP, g, n), F32), sd((S5_GROUP, g, n), F32)],
        name="s5_discretise",
    )(a_re, a_im, log_dt.reshape(g, 1), b_re_t, b_im_t)


def _s5_kernel(u_ref, perm_ref, permt_ref, bre_ref, bim_ref, cre_ref, cim_ref, psr_ref, psi_ref,
               pcr_ref, pci_ref, d_ref, s0r_ref, s0i_ref, g_ref, sr_ref, si_ref, xr_sc, xi_sc, *, tt, nh):
    l = u_ref.shape[1]
    w = xr_sc.shape[2]
    cw = u_ref.shape[2] // nh
    ts = tt // SUBLANES
    rows = lax.broadcasted_iota(jnp.int32, (SUBLANES, w), 0)
    bc = lambda v: jnp.broadcast_to(v, (SUBLANES, w))

    def blk(i):
        return slice(i * SUBLANES, (i + 1) * SUBLANES)

    def tile(t, carry):
        t0 = pl.multiple_of(t * tt, tt)
        us = [u_ref[0, pl.ds(t0, tt), h * cw:(h + 1) * cw] for h in range(nh)]
        for h in range(nh):
            up = us[h].astype(BF16)
            if ts > 1:
                up = _dot(perm_ref[...], up).astype(BF16)
            xr_sc[h] = _dot(up, bre_ref[h])
            xi_sc[h] = _dot(up, bim_ref[h])
        ends = []
        for h in range(nh):
            lam_r, lam_i = bc(psr_ref[h, 0:1, :]), bc(psi_ref[h, 0:1, :])

            def pass1(i, st, h=h, lam_r=lam_r, lam_i=lam_i):
                sr, si = _cmul(lam_r, lam_i, st[0], st[1])
                sr, si = sr + xr_sc[h, blk(i), :], si + xi_sc[h, blk(i), :]
                xr_sc[h, blk(i), :] = sr
                xi_sc[h, blk(i), :] = si
                return sr, si

            st = (jnp.zeros((SUBLANES, w), F32), jnp.zeros((SUBLANES, w), F32))
            for i in range(ts):
                st = pass1(i, st)
            ends.append(st)
        new_carry = []
        entries = []
        for h in range(nh):
            tr, ti = ends[h]
            cr, ci = carry[2 * h], carry[2 * h + 1]
            pcr, pci = pcr_ref[h], pci_ref[h]
            for kk in (1, 2, 4):
                keep = rows >= kk
                qr = jnp.where(keep, bc(pcr[kk - 1:kk, :]), 0.0)
                qi = jnp.where(keep, bc(pci[kk - 1:kk, :]), 0.0)
                dr, di = _cmul(qr, qi, pltpu.roll(tr, kk, axis=0), pltpu.roll(ti, kk, axis=0))
                tr, ti = tr + dr, ti + di
            dr, di = _cmul(pcr, pci, bc(cr), bc(ci))
            tr, ti = tr + dr, ti + di
            entries.append((jnp.where(rows == 0, bc(cr), pltpu.roll(tr, 1, axis=0)),
                            jnp.where(rows == 0, bc(ci), pltpu.roll(ti, 1, axis=0))))
            new_carry += [tr[SUBLANES - 1:SUBLANES, :], ti[SUBLANES - 1:SUBLANES, :]]
        for h in range(nh):
            er, ei = entries[h]

            def pass2(i, _, h=h, er=er, ei=ei):
                pr, pi = bc(psr_ref[h, i:i + 1, :]), bc(psi_ref[h, i:i + 1, :])
                dr, di = _cmul(pr, pi, er, ei)
                xr_sc[h, blk(i), :] = xr_sc[h, blk(i), :] + dr
                xi_sc[h, blk(i), :] = xi_sc[h, blk(i), :] + di
                return 0

            for i in range(ts):
                pass2(i, 0)
        for h in range(nh):
            y = _dot(xr_sc[h].astype(BF16), cre_ref[h]) - _dot(xi_sc[h].astype(BF16), cim_ref[h])
            if ts > 1:
                hi = y.astype(BF16)
                r1 = y - hi.astype(F32)
                mid = r1.astype(BF16)
                lo = (r1 - mid.astype(F32)).astype(BF16)
                pt = permt_ref[...]
                y = _dot(pt, hi) + _dot(pt, mid) + _dot(pt, lo)
            y = y + d_ref[h] * us[h]
            g_ref[0, pl.ds(t0, tt), h * cw:(h + 1) * cw] = _gelu_tanh(y).astype(g_ref.dtype)
        return tuple(new_carry)

    init = []
    for h in range(nh):
        init += [s0r_ref[0, h], s0i_ref[0, h]]
    fin = lax.fori_loop(0, l // tt, tile, tuple(init))
    for h in range(nh):
        sr_ref[0, h] = fin[2 * h]
        si_ref[0, h] = fin[2 * h + 1]


def _s5(u, p, s0_re, s0_im, nh=2):
    b, l, d = u.shape
    ngb = p["bbd_re"].shape[0]
    cw = d // ngb
    sw = p["bbd_re"].shape[2]
    tt = min(S5_TILE, l)
    ts = tt // SUBLANES
    assert l % tt == 0 and ts in (1, S5_STEPS) and ngb % nh == 0
    steps_re, steps_im = p["ps_re"][:, :ts], p["ps_im"][:, :ts]
    chunk_re, chunk_im = (p["pc_re"], p["pc_im"]) if ts == S5_STEPS else (p["ps_re"][:, :SUBLANES],
                                                                          p["ps_im"][:, :SUBLANES])
    r = jnp.arange(tt)
    perm = (r[None, :] == ((r % SUBLANES) * ts + r // SUBLANES)[:, None]).astype(BF16)
    state_spec = pl.BlockSpec((1, nh, 1, sw), lambda i, j: (i, j, 0, 0))
    whole = lambda a: pl.BlockSpec(a.shape, lambda i, j: (0,) * a.ndim)
    per_gb = lambda a: pl.BlockSpec((nh,) + a.shape[1:], lambda i, j: (j, 0, 0))
    args = [perm, perm.T, p["bbd_re"], p["bbd_im"], p["cbd_re"], p["cbd_im"], steps_re, steps_im,
            chunk_re, chunk_im, p["d_vec"]]
    return pl.pallas_call(
        functools.partial(_s5_kernel, tt=tt, nh=nh),
        grid=(b, ngb // nh),
        in_specs=[pl.BlockSpec((1, l, nh * cw), lambda i, j: (i, 0, j)), whole(perm), whole(perm)]
                 + [per_gb(a) for a in args[2:]] + [state_spec, state_spec],
        out_specs=[pl.BlockSpec((1, l, nh * cw), lambda i, j: (i, 0, j)), state_spec, state_spec],
        out_shape=[jax.ShapeDtypeStruct((b, l, d), BF16),
                   jax.ShapeDtypeStruct((b, ngb, 1, sw), F32),
                   jax.ShapeDtypeStruct((b, ngb, 1, sw), F32)],
        scratch_shapes=[pltpu.VMEM((nh, tt, sw), F32), pltpu.VMEM((nh, tt, sw), F32)],
        compiler_params=_cparams(("parallel", "parallel")),
        name="s5_scan",
    )(u, *args, s0_re, s0_im)


def _cross_kernel(x_ref, g_ref, wq_ref, qg_ref, k_ref, v_ref, wo_ref, o_ref, att_sc):
    x = x_ref[0]
    xn = _rms(x, g_ref[...]).astype(BF16)
    q = _dot(xn, wq_ref[...].astype(BF16))
    for h in range(MEM_HEADS):
        sl = slice(h * HEAD_DIM, (h + 1) * HEAD_DIM)
        qh = _rms(q[:, sl], qg_ref[...]).astype(BF16)
        s = _dot_nt(qh, k_ref[:, sl].astype(BF16)) * (HEAD_DIM ** -0.5)
        p = jnp.exp(s - jnp.max(s, axis=-1, keepdims=True))
        p = p / jnp.sum(p, axis=-1, keepdims=True)
        att_sc[:, sl] = _dot(p.astype(BF16), v_ref[:, sl].astype(BF16)).astype(BF16)
    o_ref[0] = x + _dot(att_sc[...], wo_ref[...].astype(BF16))


def _cross_attn(x, gain, w_q, q_gain, mem_k, mem_v, w_o, layer, tq=512):
    b, l, d = x.shape
    mt, mw = mem_k.shape[2], mem_k.shape[3]
    tq = min(tq, l)
    return pl.pallas_call(
        _cross_kernel,
        grid=(b, l // tq),
        in_specs=[pl.BlockSpec((1, tq, d), lambda i, j: (i, j, 0)),
                  pl.BlockSpec((1, d), lambda i, j: (0, 0)),
                  pl.BlockSpec((None, d, mw), lambda i, j: (layer, 0, 0)),
                  pl.BlockSpec((1, HEAD_DIM), lambda i, j: (0, 0)),
                  pl.BlockSpec((None, None, mt, mw), lambda i, j: (layer, i, 0, 0)),
                  pl.BlockSpec((None, None, mt, mw), lambda i, j: (layer, i, 0, 0)),
                  pl.BlockSpec((None, mw, d), lambda i, j: (layer, 0, 0))],
        out_specs=pl.BlockSpec((1, tq, d), lambda i, j: (i, j, 0)),
        out_shape=jax.ShapeDtypeStruct((b, l, d), F32),
        scratch_shapes=[pltpu.VMEM((tq, mw), BF16)],
        compiler_params=_cparams(("parallel", "parallel")),
        name="cross_attn",
    )(x, gain.reshape(1, d), w_q, q_gain.reshape(1, HEAD_DIM), mem_k, mem_v, w_o)


def _lane_vec(vals, lane0):
    v = jnp.zeros((1, LANES), F32)
    return v.at[0, lane0:lane0 + vals.shape[0]].set(vals.astype(F32))


def _head_expand():
    rows = jnp.arange(LANES)[None, :, None]
    cols = jnp.arange(SSD_INNER // SSD_GROUPS)[None, None, :]
    g = jnp.arange(SSD_GROUPS)[:, None, None]
    return (rows == DT_LANE0 + g * (SSD_HEADS // SSD_GROUPS) + cols // SSD_HD).astype(BF16)


def _block_diag(m, reps):
    nb, rows, c = m.shape
    t = jnp.tile(m, (1, 1, reps))
    rb = jnp.arange(rows)[:, None] // (rows // reps)
    cb = jnp.arange(reps * c)[None, :] // c
    return jnp.where((rb == cb)[None], t, 0.0).astype(BF16)


def _pad_rows(x, rows):
    return jnp.pad(x, ((0, 0), (0, rows - x.shape[1]), (0, 0)))


def _trunk(x, W, mem_k, mem_v, conv0, ssm0, s5_re0, s5_im0, fox_cache):
    b, l, d = x.shape
    t = b * l
    depth = W["norm_mix"].shape[0]
    lp = max(l, SSD_CHUNK)
    e_mat = _head_expand()
    fk, fv, fl, hs, bufs, srs, sis = [], [], [], [], [], [], []
    x2 = x.reshape(t, d)
    for i in range(depth):
        j = i // 2
        if i % 2 == 0:
            q_bf, k_f, k_bf, v_f, v_bf, zx, tail = _inproj_even(
                x2, W["norm_mix"][i], W["w_in_even"], W["w_zx"][j], W["w_tail"], j,
                W["fox_q_norm"][j], W["fox_k_norm"][j])
            zx3 = zx.reshape(b, l, -1)
            tail3 = tail.reshape(b, l, LANES)
            tail_p = _pad_rows(tail3, lp) if lp != l else tail3
            lf, negc, c_tm = _forget(tail_p, _lane_vec(W["fox_b_forget"][j], 0))
            lf = lf[:, :l]
            if fox_cache is None:
                o_fox = _fox_prompt(q_bf.reshape(b, l, FOX_W), k_bf.reshape(b, l, FOX_W),
                                    v_bf.reshape(b, l, FOX_W), negc).reshape(t, FOX_W)
            else:
                k_pool, v_pool, loc, tot, page_table = fox_cache
                rows = FOX_HEADS * l
                q_ht = jnp.transpose(q_bf.reshape(b, l, FOX_HEADS, HEAD_DIM), (0, 2, 1, 3)).reshape(b, rows, HEAD_DIM)
                o_ht = _fox_decode(q_ht, _pad_rows(k_f.reshape(b, rows, HEAD_DIM), LANES),
                                   _pad_rows(v_f.reshape(b, rows, HEAD_DIM), LANES),
                                   _pad_rows(c_tm[:, :l].reshape(b, rows, 1), LANES).reshape(b, 1, LANES),
                                   k_pool, v_pool, loc, tot, page_table, j, l)
                o_fox = jnp.transpose(o_ht.reshape(b, FOX_HEADS, l, HEAD_DIM), (0, 2, 1, 3)).reshape(t, FOX_W)
            xc, new_buf = _conv(zx3, SSD_INNER, conv0[j], W["ssd_conv_w"][j], W["ssd_conv_b"][j])
            if lp != l:
                xc_p, z_p = _pad_rows(xc, lp), _pad_rows(zx3[:, :, :SSD_INNER], lp)
            else:
                xc_p, z_p = xc, zx3
            yg, h_last = _ssd(xc_p, tail_p, z_p, _lane_vec(W["ssd_dt_bias"][j], DT_LANE0),
                              _lane_vec(W["ssd_A_log"][j], DT_LANE0),
                              jnp.repeat(W["ssd_D"][j].astype(F32), SSD_HD).reshape(1, SSD_INNER),
                              e_mat, ssm0[j].reshape(b, SSD_INNER, SSD_STATE), l)
            yg = yg[:, :l].reshape(t, SSD_INNER)
            x2 = _outproj_even(o_fox, yg, W["ssd_norm"][j], W["w_out_even"], j, x2)
            fk.append(k_f.reshape(b, l, FOX_HEADS, HEAD_DIM))
            fv.append(v_f.reshape(b, l, FOX_HEADS, HEAD_DIM))
            fl.append(lf)
            hs.append(h_last.reshape(b, SSD_HEADS, SSD_HD, SSD_STATE))
            bufs.append(new_buf)
        else:
            (u,) = _dense(x2, W["w_in_odd"], layer=j, n_cols=d, tn=512, pro="norm", gain=W["norm_mix"][i],
                          name="inproj_odd")
            p = W["s5_packed"][j]
            ngb = p["bbd_re"].shape[0]
            g_bf, s_re, s_im = _s5(u.reshape(b, l, d), p,
                                   s5_re0[j].reshape(b, ngb, 1, -1), s5_im0[j].reshape(b, ngb, 1, -1))
            (x2,) = _dense(g_bf.reshape(t, d), W["s5_w_glu"], layer=j, n_cols=d, tn=512, col0=0, col0_2=d,
                           epi="glu_res", res=x2, name="s5_glu_out")
            srs.append(s_re.reshape(b, -1, S5_STATE))
            sis.append(s_im.reshape(b, -1, S5_STATE))
        x3 = _cross_attn(x2.reshape(b, l, d), W["norm_cross"][i], W["w_mq"], W["mem_q_norm"][i],
                         mem_k, mem_v, W["w_mo"], i)
        x2 = x3.reshape(t, d)
        hid = W["w_ffn_down"].shape[1]
        (hmid,) = _dense(x2, W["w_ffn_up"], layer=i, n_cols=hid, tn=512, col0=0, col0_2=hid, pro="norm",
                         gain=W["norm_ffn"][i], epi="swiglu", out_dtypes=(BF16,), name="ffn_up")
        (x2,) = _dense(hmid, W["w_ffn_down"], layer=i, n_cols=d, tn=256, epi="residual", res=x2, name="ffn_down")
    return (x2.reshape(b, l, d), jnp.stack(fk), jnp.stack(fv), jnp.stack(fl), jnp.stack(hs), jnp.stack(bufs),
            jnp.stack(srs), jnp.stack(sis))


def _pack_s5(a_re, a_im, b_re, b_im, c_re, c_im, d_skip, log_dt):
    g, n, k = b_re.shape
    ngb = g // S5_GB
    ps_re, ps_im, pc_re, pc_im, bb_re, bb_im = _s5_prep(
        a_re.astype(F32), a_im.astype(F32), log_dt.astype(F32),
        jnp.transpose(b_re, (2, 0, 1)).astype(F32), jnp.transpose(b_im, (2, 0, 1)).astype(F32))

    def bmat(bb):
        return _block_diag(jnp.transpose(bb, (1, 0, 2)).reshape(ngb, S5_GB * k, n), S5_GB)

    def cmat(cm):
        return _block_diag(jnp.transpose(cm.astype(F32), (0, 2, 1)).reshape(ngb, S5_GB * n, k), S5_GB)

    def rows(pw):
        return jnp.transpose(pw.reshape(pw.shape[0], ngb, S5_GB * n), (1, 0, 2))

    return dict(bbd_re=bmat(bb_re), bbd_im=bmat(bb_im), cbd_re=cmat(c_re), cbd_im=cmat(c_im),
                ps_re=rows(ps_re), ps_im=rows(ps_im), pc_re=rows(pc_re), pc_im=rows(pc_im),
                d_vec=d_skip.astype(F32).reshape(ngb, 1, S5_GB * k))


def kernel(x_prompt, x_sample, mem_prompt, cache_fox_k, cache_fox_v, cache_fox_logf, cache_mem_k, cache_mem_v,
           state_ssd, state_conv, state_s5_re, state_s5_im, page_table,
           norm_mix, norm_cross, norm_mem, norm_ffn,
           w_in_even, fox_b_forget, fox_q_norm, fox_k_norm, ssd_conv_w, ssd_conv_b, ssd_dt_bias, ssd_A_log,
           ssd_D, ssd_norm, w_out_even,
           w_in_odd, s5_A_re, s5_A_im, s5_B_re, s5_B_im, s5_C_re, s5_C_im, s5_D, s5_log_dt, s5_w_glu,
           w_mq, w_mkv, mem_q_norm, mem_k_norm, w_mo, w_ffn_up, w_ffn_down):
    depth = norm_mix.shape[0]
    n_even, n_odd = w_in_even.shape[0], w_in_odd.shape[0]
    b, l, d = x_prompt.shape
    z0 = 3 * FOX_W + FOX_HEADS
    dt0 = z0 + SSD_INNER + CONV_DIM
    W = {
        "norm_mix": norm_mix, "norm_cross": norm_cross, "norm_ffn": norm_ffn,
        "w_in_even": w_in_even[:, :, :3 * FOX_W],
        "w_zx": [w_in_even[j, :, z0:z0 + SSD_INNER + CONV_DIM] for j in range(n_even)],
        "w_tail": jnp.concatenate([w_in_even[:, :, 3 * FOX_W:z0], w_in_even[:, :, dt0:dt0 + SSD_HEADS],
                                   jnp.zeros((n_even, d, LANES - FOX_HEADS - SSD_HEADS), w_in_even.dtype)], axis=2),
        "fox_b_forget": fox_b_forget, "fox_q_norm": fox_q_norm, "fox_k_norm": fox_k_norm,
        "ssd_conv_w": ssd_conv_w, "ssd_conv_b": ssd_conv_b, "ssd_dt_bias": ssd_dt_bias, "ssd_A_log": ssd_A_log,
        "ssd_D": ssd_D, "ssd_norm": ssd_norm, "w_out_even": w_out_even,
        "w_in_odd": w_in_odd, "s5_w_glu": s5_w_glu,
        "s5_packed": [_pack_s5(s5_A_re[j], s5_A_im[j], s5_B_re[j], s5_B_im[j], s5_C_re[j], s5_C_im[j],
                               s5_D[j], s5_log_dt[j]) for j in range(n_odd)],
        "w_mq": w_mq, "mem_q_norm": mem_q_norm, "w_mo": w_mo, "w_ffn_up": w_ffn_up, "w_ffn_down": w_ffn_down,
    }
    mt = mem_prompt.shape[1]
    mw = w_mkv.shape[2] // 2
    mem2 = mem_prompt.reshape(b * mt, d)
    mk, mv = [], []
    for i in range(depth):
        (k_i,) = _dense(mem2, w_mkv, layer=i, n_cols=mw, tn=mw, col0=0, pro="norm", gain=norm_mem[i],
                        epi="headnorm", head_gain=mem_k_norm[i], name="mem_k_proj")
        (v_i,) = _dense(mem2, w_mkv, layer=i, n_cols=mw, tn=mw, col0=mw, pro="norm", gain=norm_mem[i],
                        name="mem_v_proj")
        mk.append(k_i.reshape(b, mt, mw))
        mv.append(v_i.reshape(b, mt, mw))
    mem_k_p = jnp.stack(mk)
    mem_v_p = jnp.stack(mv)
    n_grp = s5_A_re.shape[1]
    (y_prompt, fox_k_p, fox_v_p, fox_logf_p, ssd_p, conv_p, s5_re_p, s5_im_p) = _trunk(
        x_prompt, W, mem_k_p, mem_v_p,
        jnp.zeros((n_even, b, SSD_CONV - 1, CONV_DIM), F32),
        jnp.zeros((n_even, b, SSD_HEADS, SSD_HD, SSD_STATE), F32),
        jnp.zeros((n_odd, b, n_grp, S5_STATE), F32),
        jnp.zeros((n_odd, b, n_grp, S5_STATE), F32),
        None)
    db = x_sample.shape[0]
    n_pool, page = cache_fox_k.shape[1], cache_fox_k.shape[2]
    assert page == LANES
    width = page * FOX_HEADS
    lf_t = jnp.transpose(cache_fox_logf.astype(F32), (0, 2, 1, 3)).reshape(n_even, page, n_pool * FOX_HEADS)
    incl = _pool_cumsum(lf_t).reshape(n_even, page, n_pool, FOX_HEADS)
    loc = jnp.transpose(incl, (0, 2, 1, 3)).reshape(n_even, n_pool, 1, width)
    tot = jnp.tile(incl[:, page - 1], (1, 1, page)).reshape(n_even, n_pool, 1, width)
    fox_cache = (cache_fox_k.reshape(n_even, n_pool, width, HEAD_DIM),
                 cache_fox_v.reshape(n_even, n_pool, width, HEAD_DIM), loc, tot, page_table)
    (y_sample, fox_k_s, fox_v_s, fox_logf_s, ssd_s, conv_s, s5_re_s, s5_im_s) = _trunk(
        x_sample, W, cache_mem_k.reshape(depth, db, mt, mw), cache_mem_v.reshape(depth, db, mt, mw),
        state_conv, state_ssd, state_s5_re, state_s5_im, fox_cache)
    hd = mw // MEM_HEADS
    return (y_prompt, y_sample,
            fox_k_p, fox_v_p, fox_logf_p,
            mem_k_p.reshape(depth, b, mt, MEM_HEADS, hd), mem_v_p.reshape(depth, b, mt, MEM_HEADS, hd),
            ssd_p, conv_p, s5_re_p, s5_im_p,
            fox_k_s, fox_v_s, fox_logf_s, ssd_s, conv_s, s5_re_s, s5_im_s)
```

```python
import functools
import math

import jax
import jax.numpy as jnp
from jax import lax
from jax.experimental import pallas as pl
from jax.experimental.pallas import tpu as pltpu

F32 = jnp.float32
BF16 = jnp.bfloat16
EPS = 1e-6
LANES = 128
SUBLANES = 8
VMEM_LIMIT_BYTES = 56 * 1024 * 1024

FOX_HEADS = 8
HEAD_DIM = 128
FOX_W = FOX_HEADS * HEAD_DIM
SSD_HEADS = 32
SSD_HD = 64
SSD_GROUPS = 4
SSD_STATE = 128
SSD_CHUNK = 128
SSD_INNER = SSD_HEADS * SSD_HD
SSD_CONV = 4
CONV_DIM = SSD_INNER + 2 * SSD_GROUPS * SSD_STATE
DT_LANE0 = FOX_HEADS
S5_GROUP = 16
S5_STATE = 64
S5_GB = 16
MEM_HEADS = 4
HIGHEST = lax.Precision.HIGHEST


def _cparams(sem):
    return pltpu.CompilerParams(dimension_semantics=sem, vmem_limit_bytes=VMEM_LIMIT_BYTES)


def _gelu_tanh(x):
    return 0.5 * x * (1.0 + jnp.tanh(math.sqrt(2.0 / math.pi) * (x + 0.044715 * x * x * x)))


def _softplus(x):
    return jnp.maximum(x, 0.0) + jnp.log1p(jnp.exp(-jnp.abs(x)))


def _silu(x):
    return x * jax.nn.sigmoid(x)


def _rms(x, gain):
    return x * lax.rsqrt(jnp.mean(x * x, axis=-1, keepdims=True) + EPS) * gain


def _dot(a, b):
    return jnp.dot(a, b, preferred_element_type=F32)


def _dot_nt(a, b):
    return lax.dot_general(a, b, (((1,), (1,)), ((), ())), preferred_element_type=F32)


def _dot_exact(a, b):
    return jnp.dot(a, b, preferred_element_type=F32, precision=HIGHEST)


def _dense_kernel(*refs, pro, epi, n_out):
    it = iter(refs)
    x_ref = next(it)
    gain_ref = next(it) if pro == "norm" else None
    w_ref = next(it)
    w2_ref = next(it) if epi in ("swiglu", "glu_res") else None
    res_ref = next(it) if epi in ("residual", "glu_res") else None
    hg_ref = next(it) if epi == "headnorm" else None
    out_refs = [next(it) for _ in range(n_out)]
    xs_ref = next(it) if pro != "none" else None

    if pro != "none":
        @pl.when(pl.program_id(1) == 0)
        def _():
            xf = x_ref[...].astype(F32)
            if pro == "norm":
                xf = _rms(xf, gain_ref[...])
            elif pro == "gelu":
                xf = _gelu_tanh(xf)
            xs_ref[...] = xf.astype(BF16)
        lhs = xs_ref[...]
    else:
        lhs = x_ref[...]

    acc = _dot(lhs, w_ref[...].astype(BF16))
    if epi == "swiglu":
        acc = _silu(acc) * _dot(lhs, w2_ref[...].astype(BF16))
    elif epi == "glu_res":
        acc = res_ref[...] + acc * jax.nn.sigmoid(_dot(lhs, w2_ref[...].astype(BF16)))
    elif epi == "residual":
        acc = res_ref[...] + acc

    if epi == "headnorm":
        for c in range(acc.shape[1] // HEAD_DIM):
            sl = slice(c * HEAD_DIM, (c + 1) * HEAD_DIM)
            blk = _rms(acc[:, sl], hg_ref[...])
            for o in out_refs:
                o[:, sl] = blk.astype(o.dtype)
    else:
        for o in out_refs:
            o[...] = acc.astype(o.dtype)


def _wspec(w, layer, k, tn, blk0):
    if w.ndim == 2:
        return pl.BlockSpec((k, tn), lambda i, j: (0, blk0 + j))
    return pl.BlockSpec((None, k, tn), lambda i, j: (layer, 0, blk0 + j))


def _dense(x, w, *, n_cols, tn, name, layer=0, col0=0, col0_2=None, pro="none", gain=None, epi="plain",
           res=None, head_gain=None, out_dtypes=(F32,), tm_max=1024, x_single=False):
    m, k = x.shape
    tm = min(tm_max, m)
    assert m % tm == 0 and n_cols % tn == 0 and col0 % tn == 0
    b0 = col0 // tn
    xmode = dict(pipeline_mode=pl.Buffered(1)) if x_single else {}
    in_specs = [pl.BlockSpec((tm, k), lambda i, j: (i, 0), **xmode)]
    args = [x]
    if pro == "norm":
        in_specs.append(pl.BlockSpec((1, k), lambda i, j: (0, 0)))
        args.append(gain.reshape(1, k).astype(F32))
    in_specs.append(_wspec(w, layer, k, tn, b0))
    args.append(w)
    if epi in ("swiglu", "glu_res"):
        assert col0_2 % tn == 0
        in_specs.append(_wspec(w, layer, k, tn, col0_2 // tn))
        args.append(w)
    if epi in ("residual", "glu_res"):
        in_specs.append(pl.BlockSpec((tm, tn), lambda i, j: (i, j)))
        args.append(res)
    if epi == "headnorm":
        in_specs.append(pl.BlockSpec((1, HEAD_DIM), lambda i, j: (0, 0)))
        args.append(head_gain.reshape(1, HEAD_DIM).astype(F32))
    out_shape = [jax.ShapeDtypeStruct((m, n_cols), dt) for dt in out_dtypes]
    out_specs = [pl.BlockSpec((tm, tn), lambda i, j: (i, j)) for _ in out_dtypes]
    scratch = [pltpu.VMEM((tm, k), BF16)] if pro != "none" else []
    outs = pl.pallas_call(
        functools.partial(_dense_kernel, pro=pro, epi=epi, n_out=len(out_dtypes)),
        grid=(m // tm, n_cols // tn),
        in_specs=in_specs, out_specs=out_specs, out_shape=out_shape,
        scratch_shapes=scratch,
        compiler_params=_cparams(("parallel", "arbitrary")),
        name=name,
    )(*args)
    return outs


TM_WIDE = 2048


def _inproj_qkv_kernel(x_ref, g_ref, w_ref, qg_ref, kg_ref, q_ref, kf_ref, kb_ref, vf_ref, vb_ref, xs_ref, *, nq):
    j = pl.program_id(1)

    @pl.when(j == 0)
    def _():
        xs_ref[...] = _rms(x_ref[...], g_ref[...]).astype(BF16)

    def headnorm(acc, gain, outs):
        for c in range(acc.shape[1] // HEAD_DIM):
            sl = slice(c * HEAD_DIM, (c + 1) * HEAD_DIM)
            blk = _rms(acc[:, sl], gain)
            for o in outs:
                o[:, sl] = blk.astype(o.dtype)

    @pl.when(j < nq)
    def _():
        headnorm(_dot(xs_ref[...], w_ref[...].astype(BF16)), qg_ref[...], (q_ref,))

    @pl.when((j >= nq) & (j < 2 * nq))
    def _():
        headnorm(_dot(xs_ref[...], w_ref[...].astype(BF16)), kg_ref[...], (kf_ref, kb_ref))

    @pl.when(j >= 2 * nq)
    def _():
        acc = _dot(xs_ref[...], w_ref[...].astype(BF16))
        vf_ref[...] = acc
        vb_ref[...] = acc.astype(BF16)


def _inproj_qkv(x, gain, w_qkv, layer, q_gain, k_gain, tn=256):
    m, k = x.shape
    tm = min(TM_WIDE, m)
    nq = FOX_W // tn
    tile = lambda lo: pl.BlockSpec((tm, tn), lambda i, j: (i, jnp.clip(j - lo, 0, nq - 1)))
    sd = jax.ShapeDtypeStruct
    return pl.pallas_call(
        functools.partial(_inproj_qkv_kernel, nq=nq),
        grid=(m // tm, 3 * nq),
        in_specs=[pl.BlockSpec((tm, k), lambda i, j: (i, 0), pipeline_mode=pl.Buffered(1)),
                  pl.BlockSpec((1, k), lambda i, j: (0, 0)),
                  pl.BlockSpec((None, k, tn), lambda i, j: (layer, 0, j)),
                  pl.BlockSpec((1, HEAD_DIM), lambda i, j: (0, 0)),
                  pl.BlockSpec((1, HEAD_DIM), lambda i, j: (0, 0))],
        out_specs=[tile(0), tile(nq), tile(nq), tile(2 * nq), tile(2 * nq)],
        out_shape=[sd((m, FOX_W), BF16), sd((m, FOX_W), F32), sd((m, FOX_W), BF16),
                   sd((m, FOX_W), F32), sd((m, FOX_W), BF16)],
        scratch_shapes=[pltpu.VMEM((tm, k), BF16)],
        compiler_params=_cparams(("parallel", "arbitrary")),
        name="inproj_qkv",
    )(x, gain.reshape(1, k), w_qkv, q_gain.reshape(1, HEAD_DIM), k_gain.reshape(1, HEAD_DIM))


def _inproj_zx_kernel(x_ref, g_ref, wzx_ref, wt_ref, zx_ref, tail_ref, xs_ref, *, nzx):
    j = pl.program_id(1)

    @pl.when(j == 0)
    def _():
        xs_ref[...] = _rms(x_ref[...], g_ref[...]).astype(BF16)

    @pl.when(j < nzx)
    def _():
        zx_ref[...] = _dot(xs_ref[...], wzx_ref[...].astype(BF16))

    @pl.when(j == nzx)
    def _():
        tail_ref[...] = _dot(xs_ref[...], wt_ref[...].astype(BF16))


def _inproj_zx(x, gain, w_zx, w_tail, layer, tn=512):
    m, k = x.shape
    tm = min(TM_WIDE, m)
    nzx = w_zx.shape[1] // tn
    return pl.pallas_call(
        functools.partial(_inproj_zx_kernel, nzx=nzx),
        grid=(m // tm, nzx + 1),
        in_specs=[pl.BlockSpec((tm, k), lambda i, j: (i, 0), pipeline_mode=pl.Buffered(1)),
                  pl.BlockSpec((1, k), lambda i, j: (0, 0)),
                  pl.BlockSpec((k, tn), lambda i, j: (0, jnp.minimum(j, nzx - 1))),
                  pl.BlockSpec((None, k, LANES), lambda i, j: (layer, 0, 0), pipeline_mode=pl.Buffered(1))],
        out_specs=[pl.BlockSpec((tm, tn), lambda i, j: (i, jnp.minimum(j, nzx - 1))),
                   pl.BlockSpec((tm, LANES), lambda i, j: (i, 0))],
        out_shape=[jax.ShapeDtypeStruct((m, nzx * tn), F32), jax.ShapeDtypeStruct((m, LANES), F32)],
        scratch_shapes=[pltpu.VMEM((tm, k), BF16)],
        compiler_params=_cparams(("parallel", "arbitrary")),
        name="inproj_zx",
    )(x, gain.reshape(1, k), w_zx, w_tail)


def _outproj_kernel(o_ref, y_ref, g_ref, w_ref, res_ref, out_ref, yn_ref):
    k1 = o_ref.shape[1]

    @pl.when(pl.program_id(1) == 0)
    def _():
        yn_ref[...] = _rms(y_ref[...], g_ref[...]).astype(BF16)
    acc = _dot(o_ref[...], w_ref[:k1, :].astype(BF16)) + _dot(yn_ref[...], w_ref[k1:, :].astype(BF16))
    out_ref[...] = res_ref[...] + acc


def _outproj_even(o_fox, yg, gain, w_out, layer, res, tn=256, tm_max=TM_WIDE):
    m, d = res.shape
    tm = min(tm_max, m)
    k1, k2 = o_fox.shape[1], yg.shape[1]
    return pl.pallas_call(
        _outproj_kernel,
        grid=(m // tm, d // tn),
        in_specs=[
            pl.BlockSpec((tm, k1), lambda i, j: (i, 0)),
            pl.BlockSpec((tm, k2), lambda i, j: (i, 0), pipeline_mode=pl.Buffered(1)),
            pl.BlockSpec((1, k2), lambda i, j: (0, 0)),
            pl.BlockSpec((None, k1 + k2, tn), lambda i, j: (layer, 0, j)),
            pl.BlockSpec((tm, tn), lambda i, j: (i, j)),
        ],
        out_specs=pl.BlockSpec((tm, tn), lambda i, j: (i, j)),
        out_shape=jax.ShapeDtypeStruct((m, d), F32),
        scratch_shapes=[pltpu.VMEM((tm, k2), BF16)],
        compiler_params=_cparams(("parallel", "arbitrary")),
        name="outproj_even",
    )(o_fox, yg, gain.reshape(1, k2).astype(F32), w_out, res)


def _lane_cumsum(x):
    lane = lax.broadcasted_iota(jnp.int32, x.shape, 1)
    k = 1
    while k < LANES:
        x = x + jnp.where(lane >= k, pltpu.roll(x, k, axis=1), 0.0)
        k *= 2
    return x


def _forget_kernel(raw_ref, b_ref, lf_ref, negc_ref, ctm_ref):
    lp = raw_ref.shape[1]
    lf = -_softplus(-(raw_ref[0] + b_ref[...]))
    lf_ref[0] = lf[:, :FOX_HEADS]
    row_i = lax.broadcasted_iota(jnp.int32, (LANES, LANES), 0)
    col_i = lax.broadcasted_iota(jnp.int32, (LANES, LANES), 1)
    tri = (col_i <= row_i).astype(F32)
    ctm_ref[0] = _dot_exact(tri, lf[:LANES, :])[:, :FOX_HEADS]
    carry = jnp.zeros((FOX_HEADS, 1), F32)
    for c in range(lp // LANES):
        blk = lf[c * LANES:(c + 1) * LANES, :].T[:FOX_HEADS, :]
        cs = _lane_cumsum(blk) + carry
        negc_ref[0, :, c * LANES:(c + 1) * LANES] = -cs
        carry = cs[:, LANES - 1:LANES]


def _forget(raw, b_vec):
    b, lp, _ = raw.shape
    return pl.pallas_call(
        _forget_kernel,
        grid=(b,),
        in_specs=[pl.BlockSpec((1, lp, LANES), lambda i: (i, 0, 0)),
                  pl.BlockSpec((1, LANES), lambda i: (0, 0))],
        out_specs=[pl.BlockSpec((1, lp, FOX_HEADS), lambda i: (i, 0, 0)),
                   pl.BlockSpec((1, FOX_HEADS, lp), lambda i: (i, 0, 0)),
                   pl.BlockSpec((1, LANES, FOX_HEADS), lambda i: (i, 0, 0))],
        out_shape=[jax.ShapeDtypeStruct((b, lp, FOX_HEADS), F32),
                   jax.ShapeDtypeStruct((b, FOX_HEADS, lp), F32),
                   jax.ShapeDtypeStruct((b, LANES, FOX_HEADS), F32)],
        compiler_params=_cparams(("parallel",)),
        name="forget_gates",
    )(raw, b_vec)


def _fox_prompt_kernel(q_ref, k_ref, v_ref, nb_ref, o_ref, m_sc, acc_sc, mask_sc, *, tq, tk):
    qi = pl.program_id(1)
    kj = pl.program_id(2)

    @pl.when(kj == 0)
    def _():
        m_sc[...] = jnp.full(m_sc.shape, -jnp.inf, F32)
        acc_sc[...] = jnp.zeros(acc_sc.shape, F32)

    @pl.when(kj * tk <= qi * tq + (tq - 1))
    def _():
        row = qi * tq + lax.broadcasted_iota(jnp.int32, (tq, tk), 0)
        col = kj * tk + lax.broadcasted_iota(jnp.int32, (tq, tk), 1)
        mask_sc[...] = jnp.where(col <= row, 0.0, -jnp.inf)
        ones = jnp.ones((tk, HEAD_DIM), BF16)
        for h in range(FOX_HEADS):
            sl = slice(h * HEAD_DIM, (h + 1) * HEAD_DIM)
            s = _dot_nt(q_ref[0, :, sl], k_ref[0, :, sl]) * (HEAD_DIM ** -0.5)
            s = s + nb_ref[0, h:h + 1, :] + mask_sc[...]
            m_old = m_sc[h]
            m_new = jnp.maximum(m_old, jnp.max(s, axis=-1, keepdims=True))
            p = jnp.exp(s - m_new).astype(BF16)
            pv = _dot(p, jnp.concatenate([v_ref[0, :, sl], ones], axis=1))
            acc_sc[h] = jnp.exp(m_old - m_new) * acc_sc[h] + pv
            m_sc[h] = m_new

    @pl.when(kj == pl.num_programs(2) - 1)
    def _():
        for h in range(FOX_HEADS):
            acc = acc_sc[h]
            o_ref[0, :, h * HEAD_DIM:(h + 1) * HEAD_DIM] = (acc[:, :HEAD_DIM] / acc[:, HEAD_DIM:]).astype(o_ref.dtype)


def _fox_prompt(q, k, v, negc, tq=512, tk=512):
    b, l, _ = q.shape
    tq, tk = min(tq, l), min(tk, l)
    nq, nk = l // tq, l // tk

    def last_needed(qi, kj):
        return jnp.minimum(kj, (qi * tq + tq - 1) // tk)

    return pl.pallas_call(
        functools.partial(_fox_prompt_kernel, tq=tq, tk=tk),
        grid=(b, nq, nk),
        in_specs=[pl.BlockSpec((1, tq, FOX_W), lambda bi, qi, kj: (bi, qi, 0)),
                  pl.BlockSpec((1, tk, FOX_W), lambda bi, qi, kj: (bi, last_needed(qi, kj), 0)),
                  pl.BlockSpec((1, tk, FOX_W), lambda bi, qi, kj: (bi, last_needed(qi, kj), 0)),
                  pl.BlockSpec((1, FOX_HEADS, tk), lambda bi, qi, kj: (bi, 0, last_needed(qi, kj)))],
        out_specs=pl.BlockSpec((1, tq, FOX_W), lambda bi, qi, kj: (bi, qi, 0)),
        out_shape=jax.ShapeDtypeStruct((b, l, FOX_W), BF16),
        scratch_shapes=[pltpu.VMEM((FOX_HEADS, tq, 1), F32),
                        pltpu.VMEM((FOX_HEADS, tq, 2 * HEAD_DIM), F32),
                        pltpu.VMEM((tq, tk), F32)],
        compiler_params=_cparams(("parallel", "parallel", "arbitrary")),
        name="fox_prompt",
    )(q, k, v, negc)


def _pool_cumsum_kernel(x_ref, o_ref):
    n = x_ref.shape[0]
    row_i = lax.broadcasted_iota(jnp.int32, (n, n), 0)
    col_i = lax.broadcasted_iota(jnp.int32, (n, n), 1)
    o_ref[...] = _dot_exact((col_i <= row_i).astype(F32), x_ref[...])


def _pool_cumsum(x, tc=2048):
    nl, page, cols = x.shape
    tc = tc if cols % tc == 0 else cols
    return pl.pallas_call(
        _pool_cumsum_kernel,
        grid=(nl, cols // tc),
        in_specs=[pl.BlockSpec((None, page, tc), lambda i, j: (i, 0, j))],
        out_specs=pl.BlockSpec((None, page, tc), lambda i, j: (i, 0, j)),
        out_shape=jax.ShapeDtypeStruct((nl, page, cols), F32),
        compiler_params=_cparams(("parallel", "parallel")),
        name="pool_logf_cumsum",
    )(x)


def _fox_decode_kernel(pt_ref, q_ref, *refs, gp, t_new):
    kv_refs = refs[:4 * gp]
    kn_ref, vn_ref, cn_ref, o_ref, m_sc, l_sc, acc_sc, carry_sc = refs[4 * gp:]
    p = pl.program_id(1)
    n_past = pl.num_programs(1) - 1
    rows = q_ref.shape[1]
    scale = HEAD_DIM ** -0.5

    @pl.when(p == 0)
    def _():
        m_sc[...] = jnp.full(m_sc.shape, -jnp.inf, F32)
        l_sc[...] = jnp.zeros(l_sc.shape, F32)
        acc_sc[...] = jnp.zeros(acc_sc.shape, F32)
        carry_sc[...] = jnp.zeros(carry_sc.shape, F32)

    def update(s, v, state):
        m_old, l_old, acc = state
        m_new = jnp.maximum(m_old, jnp.max(s, axis=-1, keepdims=True))
        alpha = jnp.exp(m_old - m_new)
        pr = jnp.exp(s - m_new)
        l_new = alpha * l_old + jnp.sum(pr, axis=-1, keepdims=True)
        acc = alpha * acc + _dot(pr.astype(BF16), v.astype(BF16))
        return m_new, l_new, acc

    @pl.when(p < n_past)
    def _():
        q = q_ref[0]
        width = kv_refs[0].shape[0]
        row = lax.broadcasted_iota(jnp.int32, (rows, width), 0)
        col = lax.broadcasted_iota(jnp.int32, (rows, width), 1)
        own_head = (col % FOX_HEADS) == (row // t_new)
        carry = carry_sc[...]
        scores = []
        for g in range(gp):
            k_ref, _, loc_ref, tot_ref = kv_refs[4 * g:4 * g + 4]
            s = _dot_nt(q, k_ref[...].astype(BF16)) * scale - (carry + loc_ref[...])
            scores.append(jnp.where(own_head, s, -jnp.inf))
            carry = carry + tot_ref[...]
        carry_sc[...] = carry
        m_old = m_sc[...]
        m_new = m_old
        for s in scores:
            m_new = jnp.maximum(m_new, jnp.max(s, axis=-1, keepdims=True))
        alpha = jnp.exp(m_old - m_new)
        l_new = alpha * l_sc[...]
        acc = alpha * acc_sc[...]
        for g, s in enumerate(scores):
            pr = jnp.exp(s - m_new)
            l_new = l_new + jnp.sum(pr, axis=-1, keepdims=True)
            acc = acc + _dot(pr.astype(BF16), kv_refs[4 * g + 1][...].astype(BF16))
        m_sc[...], l_sc[...], acc_sc[...] = m_new, l_new, acc

    @pl.when(p == n_past)
    def _():
        width = kn_ref.shape[1]
        row = lax.broadcasted_iota(jnp.int32, (rows, width), 0)
        col = lax.broadcasted_iota(jnp.int32, (rows, width), 1)
        keep = ((col % FOX_HEADS) == (row // t_new)) & ((col // FOX_HEADS) <= (row % t_new))
        s = _dot_nt(q_ref[0], kn_ref[0].astype(BF16)) * scale - (carry_sc[:, :width] + cn_ref[0])
        m, l, acc = update(jnp.where(keep, s, -jnp.inf), vn_ref[0], (m_sc[...], l_sc[...], acc_sc[...]))
        o_ref[0] = (acc / l).astype(o_ref.dtype)


def _fox_decode(q, k_new, v_new, c_new, k_pool, v_pool, loc, tot, page_table, layer, t_new, gp=8):
    b, rows, _ = q.shape
    n_pages = page_table.shape[1]
    gp = max(g for g in range(1, gp + 1) if n_pages % g == 0)
    n_steps = n_pages // gp
    width = k_pool.shape[2]

    def page_map(g):
        return lambda bi, p, pt: (layer, pt[bi, jnp.minimum(p, n_steps - 1) * gp + g], 0, 0)

    in_specs = [pl.BlockSpec((1, rows, HEAD_DIM), lambda bi, p, pt: (bi, 0, 0))]
    args = [q]
    for g in range(gp):
        in_specs += [pl.BlockSpec((None, None, width, HEAD_DIM), page_map(g)),
                     pl.BlockSpec((None, None, width, HEAD_DIM), page_map(g)),
                     pl.BlockSpec((None, None, 1, width), page_map(g)),
                     pl.BlockSpec((None, None, 1, width), page_map(g))]
        args += [k_pool, v_pool, loc, tot]
    new_spec = pl.BlockSpec((1, k_new.shape[1], HEAD_DIM), lambda bi, p, pt: (bi, 0, 0))
    in_specs += [new_spec, new_spec, pl.BlockSpec((1, 1, c_new.shape[2]), lambda bi, p, pt: (bi, 0, 0))]
    args += [k_new, v_new, c_new]
    grid_spec = pltpu.PrefetchScalarGridSpec(
        num_scalar_prefetch=1,
        grid=(b, n_steps + 1),
        in_specs=in_specs,
        out_specs=pl.BlockSpec((1, rows, HEAD_DIM), lambda bi, p, pt: (bi, 0, 0)),
        scratch_shapes=[pltpu.VMEM((rows, 1), F32), pltpu.VMEM((rows, 1), F32),
                        pltpu.VMEM((rows, HEAD_DIM), F32), pltpu.VMEM((1, width), F32)],
    )
    return pl.pallas_call(
        functools.partial(_fox_decode_kernel, gp=gp, t_new=t_new), grid_spec=grid_spec,
        out_shape=jax.ShapeDtypeStruct((b, rows, HEAD_DIM), BF16),
        compiler_params=_cparams(("parallel", "arbitrary")),
        name="fox_decode",
    )(page_table, *args)


def _conv_kernel(x_ref, buf_ref, w_ref, b_ref, y_ref, nb_ref, full_sc):
    l = x_ref.shape[1]
    pad = SUBLANES
    full_sc[pad - (SSD_CONV - 1):pad, :] = buf_ref[0]
    full_sc[pad:pad + l, :] = x_ref[0]
    acc = b_ref[...] + full_sc[pad - 3:pad - 3 + l, :] * w_ref[0:1, :]
    for kk in range(1, SSD_CONV):
        acc = acc + full_sc[pad - 3 + kk:pad - 3 + kk + l, :] * w_ref[kk:kk + 1, :]
    y_ref[0] = _silu(acc)
    nb_ref[0] = full_sc[pad + l - (SSD_CONV - 1):pad + l, :]


def _conv(zx, col0, buf, w, bias, tc=512):
    b, l, _ = zx.shape
    c = w.shape[1]
    cb0 = col0 // tc
    return pl.pallas_call(
        _conv_kernel,
        grid=(b, c // tc),
        in_specs=[pl.BlockSpec((1, l, tc), lambda i, j: (i, 0, cb0 + j)),
                  pl.BlockSpec((1, SSD_CONV - 1, tc), lambda i, j: (i, 0, j)),
                  pl.BlockSpec((SSD_CONV, tc), lambda i, j: (0, j)),
                  pl.BlockSpec((1, tc), lambda i, j: (0, j))],
        out_specs=[pl.BlockSpec((1, l, tc), lambda i, j: (i, 0, j)),
                   pl.BlockSpec((1, SSD_CONV - 1, tc), lambda i, j: (i, 0, j))],
        out_shape=[jax.ShapeDtypeStruct((b, l, c), F32),
                   jax.ShapeDtypeStruct((b, SSD_CONV - 1, c), F32)],
        scratch_shapes=[pltpu.VMEM((l + SUBLANES, tc), F32)],
        compiler_params=_cparams(("parallel", "parallel")),
        name="ssd_conv",
    )(zx, buf, w, bias.reshape(1, c))


def _ssd_kernel(xc_ref, tail_ref, z_ref, dtb_ref, alog_ref, dexp_ref, e_ref, h0_ref,
                y_ref, hout_ref, h_sc, *, valid_len):
    c = pl.program_id(1)
    q = SSD_CHUNK
    gw = SSD_INNER // SSD_GROUPS
    hpg = SSD_HEADS // SSD_GROUPS

    @pl.when(c == 0)
    def _():
        h_sc[...] = h0_ref[0]

    lane = lax.broadcasted_iota(jnp.int32, (q, LANES), 1)
    row = lax.broadcasted_iota(jnp.int32, (q, LANES), 0)
    col_i = lax.broadcasted_iota(jnp.int32, (q, q), 1)
    row_i = lax.broadcasted_iota(jnp.int32, (q, q), 0)
    causal = col_i <= row_i
    tri = causal.astype(F32)

    dt = _softplus(tail_ref[0] + dtb_ref[...])
    live = (lane >= DT_LANE0) & (lane < DT_LANE0 + SSD_HEADS) & (c * q + row < valid_len)
    dt = jnp.where(live, dt, 0.0)
    a = dt * (-jnp.exp(alog_ref[...]))
    a_cum = _dot_exact(tri, a)
    a_cum_t = a_cum.T
    dt_t = dt.T
    a_last = a_cum[q - 1:q, :]
    fac = jnp.concatenate([dt * jnp.exp(a_last - a_cum), jnp.exp(a_cum)], axis=0)
    fac_hi = fac.astype(BF16)
    fac_lo = (fac - fac_hi.astype(F32)).astype(BF16)
    fac2 = jnp.concatenate([fac_hi, fac_lo], axis=0)

    for g in range(SSD_GROUPS):
        ex = _dot(fac2, e_ref[g])
        ex = ex[:2 * q] + ex[2 * q:]
        w1_e, ea_e = ex[:q], ex[q:]
        xs = xc_ref[0, :, g * gw:(g + 1) * gw]
        bb = xc_ref[0, :, SSD_INNER + g * SSD_STATE:SSD_INNER + (g + 1) * SSD_STATE].astype(BF16)
        cc = xc_ref[0, :, SSD_INNER + (SSD_GROUPS + g) * SSD_STATE:
                    SSD_INNER + (SSD_GROUPS + g + 1) * SSD_STATE].astype(BF16)
        cb = _dot_nt(cc, bb)
        hg = h_sc[g * gw:(g + 1) * gw, :]
        y = _dot_nt(cc, hg.astype(BF16)) * ea_e
        st = _dot((xs * w1_e).T.astype(BF16), bb)
        for hp in range(hpg // 2):
            pair = xs[:, hp * LANES:(hp + 1) * LANES]
            lane_p = lax.broadcasted_iota(jnp.int32, pair.shape, 1)
            yp = None
            for sub in range(2):
                h = 2 * hp + sub
                ln = DT_LANE0 + g * hpg + h
                seg = a_cum[:, ln:ln + 1] - a_cum_t[ln:ln + 1, :]
                dec = jnp.exp(jnp.where(causal, seg, -jnp.inf))
                mat = (cb * dec * dt_t[ln:ln + 1, :]).astype(BF16)
                in_head = (lane_p >= sub * SSD_HD) & (lane_p < (sub + 1) * SSD_HD)
                rhs = jnp.where(in_head, pair, 0.0).astype(BF16)
                part = _dot(mat, rhs)
                yp = part if yp is None else yp + part
                r0 = g * gw + h * SSD_HD
                h_sc[r0:r0 + SSD_HD, :] = (hg[h * SSD_HD:(h + 1) * SSD_HD, :] * jnp.exp(a_last[:, ln:ln + 1])
                                           + st[h * SSD_HD:(h + 1) * SSD_HD, :])
            cs = slice(g * gw + hp * LANES, g * gw + (hp + 1) * LANES)
            yt = yp + y[:, hp * LANES:(hp + 1) * LANES] + dexp_ref[:, cs] * xs[:, hp * LANES:(hp + 1) * LANES]
            y_ref[0, :, cs] = yt * _silu(z_ref[0, :, cs])

    @pl.when(c == pl.num_programs(1) - 1)
    def _():
        hout_ref[0] = h_sc[...]


def _ssd(xc, tail, zx, dtb_vec, alog_vec, d_exp, e_mat, h0, valid_len):
    b, lp, _ = xc.shape
    nc = lp // SSD_CHUNK
    return pl.pallas_call(
        functools.partial(_ssd_kernel, valid_len=valid_len),
        grid=(b, nc),
        in_specs=[pl.BlockSpec((1, SSD_CHUNK, CONV_DIM), lambda i, c: (i, c, 0)),
                  pl.BlockSpec((1, SSD_CHUNK, LANES), lambda i, c: (i, c, 0)),
                  pl.BlockSpec((1, SSD_CHUNK, SSD_INNER), lambda i, c: (i, c, 0)),
                  pl.BlockSpec((1, LANES), lambda i, c: (0, 0)),
                  pl.BlockSpec((1, LANES), lambda i, c: (0, 0)),
                  pl.BlockSpec((1, SSD_INNER), lambda i, c: (0, 0)),
                  pl.BlockSpec((SSD_GROUPS, LANES, SSD_INNER // SSD_GROUPS), lambda i, c: (0, 0, 0)),
                  pl.BlockSpec((1, SSD_INNER, SSD_STATE), lambda i, c: (i, 0, 0))],
        out_specs=[pl.BlockSpec((1, SSD_CHUNK, SSD_INNER), lambda i, c: (i, c, 0)),
                   pl.BlockSpec((1, SSD_INNER, SSD_STATE), lambda i, c: (i, 0, 0))],
        out_shape=[jax.ShapeDtypeStruct((b, lp, SSD_INNER), F32),
                   jax.ShapeDtypeStruct((b, SSD_INNER, SSD_STATE), F32)],
        scratch_shapes=[pltpu.VMEM((SSD_INNER, SSD_STATE), F32)],
        compiler_params=_cparams(("parallel", "arbitrary")),
        name="ssd_chunked",
    )(xc, tail, zx, dtb_vec, alog_vec, d_exp, e_mat, h0)


S5_TILE = 256
S5_STEPS = S5_TILE // SUBLANES


def _cmul(ar, ai, br, bi):
    return ar * br - ai * bi, ar * bi + ai * br


def _s5_prep_kernel(are_ref, aim_ref, ldt_ref, bre_ref, bim_ref, psr_ref, psi_ref, pcr_ref, pci_ref,
                    bbr_ref, bbi_ref):
    lam_re = jnp.minimum(are_ref[...], -1e-4)
    lam_im = aim_ref[...]
    dt = jnp.exp(ldt_ref[...])
    mag = jnp.exp(lam_re * dt)
    ang = lam_im * dt
    lb_re, lb_im = mag * jnp.cos(ang), mag * jnp.sin(ang)
    nr, ni = lb_re - 1.0, lb_im
    den = lam_re * lam_re + lam_im * lam_im
    coef_re = (nr * lam_re + ni * lam_im) / den
    coef_im = (ni * lam_re - nr * lam_im) / den
    for k in range(S5_GROUP):
        br, bi = bre_ref[k], bim_ref[k]
        bbr_ref[k] = coef_re * br - coef_im * bi
        bbi_ref[k] = coef_re * bi + coef_im * br
    pr, pi = lb_re, lb_im
    for r in range(S5_STEPS):
        psr_ref[r], psi_ref[r] = pr, pi
        if r + 1 < S5_STEPS:
            pr, pi = _cmul(pr, pi, lb_re, lb_im)
    qr, qi = pr, pi
    for c in range(SUBLANES):
        pcr_ref[c], pci_ref[c] = qr, qi
        if c + 1 < SUBLANES:
            qr, qi = _cmul(qr, qi, pr, pi)


def _s5_prep(a_re, a_im, log_dt, b_re_t, b_im_t):
    g, n = a_re.shape
    sd = jax.ShapeDtypeStruct
    return pl.pallas_call(
        _s5_prep_kernel,
        out_shape=[sd((S5_STEPS, g, n), F32), sd((S5_STEPS, g, n), F32),
                   sd((SUBLANES, g, n), F32), sd((SUBLANES, g, n), F32),
                   sd((S5_GROUP, g, n), F32), sd((S5_GROUP, g, n), F32)],
        name="s5_discretise",
    )(a_re, a_im, log_dt.reshape(g, 1), b_re_t, b_im_t)


def _s5_kernel(u_ref, perm_ref, permt_ref, bre_ref, bim_ref, cre_ref, cim_ref, psr_ref, psi_ref,
               pcr_ref, pci_ref, d_ref, s0r_ref, s0i_ref, g_ref, sr_ref, si_ref, xr_sc, xi_sc, *, tt, nh):
    l = u_ref.shape[1]
    w = xr_sc.shape[2]
    cw = u_ref.shape[2] // nh
    ts = tt // SUBLANES
    rows = lax.broadcasted_iota(jnp.int32, (SUBLANES, w), 0)
    bc = lambda v: jnp.broadcast_to(v, (SUBLANES, w))

    def blk(i):
        return slice(i * SUBLANES, (i + 1) * SUBLANES)

    def tile(t, carry):
        t0 = pl.multiple_of(t * tt, tt)
        us = [u_ref[0, pl.ds(t0, tt), h * cw:(h + 1) * cw] for h in range(nh)]
        for h in range(nh):
            up = us[h].astype(BF16)
            if ts > 1:
                up = _dot(perm_ref[...], up).astype(BF16)
            xr_sc[h] = _dot(up, bre_ref[h])
            xi_sc[h] = _dot(up, bim_ref[h])
        ends = []
        for h in range(nh):
            lam_r, lam_i = bc(psr_ref[h, 0:1, :]), bc(psi_ref[h, 0:1, :])

            def pass1(i, st, h=h, lam_r=lam_r, lam_i=lam_i):
                sr, si = _cmul(lam_r, lam_i, st[0], st[1])
                sr, si = sr + xr_sc[h, blk(i), :], si + xi_sc[h, blk(i), :]
                xr_sc[h, blk(i), :] = sr
                xi_sc[h, blk(i), :] = si
                return sr, si

            st = (jnp.zeros((SUBLANES, w), F32), jnp.zeros((SUBLANES, w), F32))
            for i in range(ts):
                st = pass1(i, st)
            ends.append(st)
        new_carry = []
        entries = []
        for h in range(nh):
            tr, ti = ends[h]
            cr, ci = carry[2 * h], carry[2 * h + 1]
            pcr, pci = pcr_ref[h], pci_ref[h]
            for kk in (1, 2, 4):
                keep = rows >= kk
                qr = jnp.where(keep, bc(pcr[kk - 1:kk, :]), 0.0)
                qi = jnp.where(keep, bc(pci[kk - 1:kk, :]), 0.0)
                dr, di = _cmul(qr, qi, pltpu.roll(tr, kk, axis=0), pltpu.roll(ti, kk, axis=0))
                tr, ti = tr + dr, ti + di
            dr, di = _cmul(pcr, pci, bc(cr), bc(ci))
            tr, ti = tr + dr, ti + di
            entries.append((jnp.where(rows == 0, bc(cr), pltpu.roll(tr, 1, axis=0)),
                            jnp.where(rows == 0, bc(ci), pltpu.roll(ti, 1, axis=0))))
            new_carry += [tr[SUBLANES - 1:SUBLANES, :], ti[SUBLANES - 1:SUBLANES, :]]
        for h in range(nh):
            er, ei = entries[h]

            def pass2(i, _, h=h, er=er, ei=ei):
                pr, pi = bc(psr_ref[h, i:i + 1, :]), bc(psi_ref[h, i:i + 1, :])
                dr, di = _cmul(pr, pi, er, ei)
                xr_sc[h, blk(i), :] = xr_sc[h, blk(i), :] + dr
                xi_sc[h, blk(i), :] = xi_sc[h, blk(i), :] + di
                return 0

            for i in range(ts):
                pass2(i, 0)
        for h in range(nh):
            y = _dot(xr_sc[h].astype(BF16), cre_ref[h]) - _dot(xi_sc[h].astype(BF16), cim_ref[h])
            if ts > 1:
                hi = y.astype(BF16)
                r1 = y - hi.astype(F32)
                mid = r1.astype(BF16)
                lo = (r1 - mid.astype(F32)).astype(BF16)
                pt = permt_ref[...]
                y = _dot(pt, hi) + _dot(pt, mid) + _dot(pt, lo)
            y = y + d_ref[h] * us[h]
            g_ref[0, pl.ds(t0, tt), h * cw:(h + 1) * cw] = _gelu_tanh(y).astype(g_ref.dtype)
        return tuple(new_carry)

    init = []
    for h in range(nh):
        init += [s0r_ref[0, h], s0i_ref[0, h]]
    fin = lax.fori_loop(0, l // tt, tile, tuple(init))
    for h in range(nh):
        sr_ref[0, h] = fin[2 * h]
        si_ref[0, h] = fin[2 * h + 1]


def _s5(u, p, s0_re, s0_im, nh=2):
    b, l, d = u.shape
    ngb = p["bbd_re"].shape[0]
    cw = d // ngb
    sw = p["bbd_re"].shape[2]
    tt = min(S5_TILE, l)
    ts = tt // SUBLANES
    assert l % tt == 0 and ts in (1, S5_STEPS) and ngb % nh == 0
    steps_re, steps_im = p["ps_re"][:, :ts], p["ps_im"][:, :ts]
    chunk_re, chunk_im = (p["pc_re"], p["pc_im"]) if ts == S5_STEPS else (p["ps_re"][:, :SUBLANES],
                                                                          p["ps_im"][:, :SUBLANES])
    r = jnp.arange(tt)
    perm = (r[None, :] == ((r % SUBLANES) * ts + r // SUBLANES)[:, None]).astype(BF16)
    state_spec = pl.BlockSpec((1, nh, 1, sw), lambda i, j: (i, j, 0, 0))
    whole = lambda a: pl.BlockSpec(a.shape, lambda i, j: (0,) * a.ndim)
    per_gb = lambda a: pl.BlockSpec((nh,) + a.shape[1:], lambda i, j: (j, 0, 0))
    args = [perm, perm.T, p["bbd_re"], p["bbd_im"], p["cbd_re"], p["cbd_im"], steps_re, steps_im,
            chunk_re, chunk_im, p["d_vec"]]
    return pl.pallas_call(
        functools.partial(_s5_kernel, tt=tt, nh=nh),
        grid=(b, ngb // nh),
        in_specs=[pl.BlockSpec((1, l, nh * cw), lambda i, j: (i, 0, j)), whole(perm), whole(perm)]
                 + [per_gb(a) for a in args[2:]] + [state_spec, state_spec],
        out_specs=[pl.BlockSpec((1, l, nh * cw), lambda i, j: (i, 0, j)), state_spec, state_spec],
        out_shape=[jax.ShapeDtypeStruct((b, l, d), BF16),
                   jax.ShapeDtypeStruct((b, ngb, 1, sw), F32),
                   jax.ShapeDtypeStruct((b, ngb, 1, sw), F32)],
        scratch_shapes=[pltpu.VMEM((nh, tt, sw), F32), pltpu.VMEM((nh, tt, sw), F32)],
        compiler_params=_cparams(("parallel", "parallel")),
        name="s5_scan",
    )(u, *args, s0_re, s0_im)


def _cross_kernel(x_ref, g_ref, wq_ref, qg_ref, k_ref, v_ref, wo_ref, o_ref, att_sc):
    x = x_ref[0]
    xn = _rms(x, g_ref[...]).astype(BF16)
    q = _dot(xn, wq_ref[...].astype(BF16))
    for h in range(MEM_HEADS):
        sl = slice(h * HEAD_DIM, (h + 1) * HEAD_DIM)
        qh = _rms(q[:, sl], qg_ref[...]).astype(BF16)
        s = _dot_nt(qh, k_ref[:, sl].astype(BF16)) * (HEAD_DIM ** -0.5)
        p = jnp.exp(s - jnp.max(s, axis=-1, keepdims=True))
        p = p / jnp.sum(p, axis=-1, keepdims=True)
        att_sc[:, sl] = _dot(p.astype(BF16), v_ref[:, sl].astype(BF16)).astype(BF16)
    o_ref[0] = x + _dot(att_sc[...], wo_ref[...].astype(BF16))


def _cross_attn(x, gain, w_q, q_gain, mem_k, mem_v, w_o, layer, tq=512):
    b, l, d = x.shape
    mt, mw = mem_k.shape[2], mem_k.shape[3]
    tq = min(tq, l)
    return pl.pallas_call(
        _cross_kernel,
        grid=(b, l // tq),
        in_specs=[pl.BlockSpec((1, tq, d), lambda i, j: (i, j, 0)),
                  pl.BlockSpec((1, d), lambda i, j: (0, 0)),
                  pl.BlockSpec((None, d, mw), lambda i, j: (layer, 0, 0)),
                  pl.BlockSpec((1, HEAD_DIM), lambda i, j: (0, 0)),
                  pl.BlockSpec((None, None, mt, mw), lambda i, j: (layer, i, 0, 0)),
                  pl.BlockSpec((None, None, mt, mw), lambda i, j: (layer, i, 0, 0)),
                  pl.BlockSpec((None, mw, d), lambda i, j: (layer, 0, 0))],
        out_specs=pl.BlockSpec((1, tq, d), lambda i, j: (i, j, 0)),
        out_shape=jax.ShapeDtypeStruct((b, l, d), F32),
        scratch_shapes=[pltpu.VMEM((tq, mw), BF16)],
        compiler_params=_cparams(("parallel", "parallel")),
        name="cross_attn",
    )(x, gain.reshape(1, d), w_q, q_gain.reshape(1, HEAD_DIM), mem_k, mem_v, w_o)


def _lane_vec(vals, lane0):
    v = jnp.zeros((1, LANES), F32)
    return v.at[0, lane0:lane0 + vals.shape[0]].set(vals.astype(F32))


def _head_expand():
    rows = jnp.arange(LANES)[None, :, None]
    cols = jnp.arange(SSD_INNER // SSD_GROUPS)[None, None, :]
    g = jnp.arange(SSD_GROUPS)[:, None, None]
    return (rows == DT_LANE0 + g * (SSD_HEADS // SSD_GROUPS) + cols // SSD_HD).astype(BF16)


def _block_diag(m, reps):
    nb, rows, c = m.shape
    t = jnp.tile(m, (1, 1, reps))
    rb = jnp.arange(rows)[:, None] // (rows // reps)
    cb = jnp.arange(reps * c)[None, :] // c
    return jnp.where((rb == cb)[None], t, 0.0).astype(BF16)


def _pad_rows(x, rows):
    return jnp.pad(x, ((0, 0), (0, rows - x.shape[1]), (0, 0)))


def _trunk(x, W, mem_k, mem_v, conv0, ssm0, s5_re0, s5_im0, fox_cache):
    b, l, d = x.shape
    t = b * l
    depth = W["norm_mix"].shape[0]
    lp = max(l, SSD_CHUNK)
    e_mat = _head_expand()
    fk, fv, fl, hs, bufs, srs, sis = [], [], [], [], [], [], []
    x2 = x.reshape(t, d)
    for i in range(depth):
        j = i // 2
        if i % 2 == 0:
            q_bf, k_f, k_bf, v_f, v_bf = _inproj_qkv(x2, W["norm_mix"][i], W["w_in_even"], j,
                                                     W["fox_q_norm"][j], W["fox_k_norm"][j])
            zx, tail = _inproj_zx(x2, W["norm_mix"][i], W["w_zx"][j], W["w_tail"], j)
            zx3 = zx.reshape(b, l, -1)
            tail3 = tail.reshape(b, l, LANES)
            tail_p = _pad_rows(tail3, lp) if lp != l else tail3
            lf, negc, c_tm = _forget(tail_p, _lane_vec(W["fox_b_forget"][j], 0))
            lf = lf[:, :l]
            if fox_cache is None:
                o_fox = _fox_prompt(q_bf.reshape(b, l, FOX_W), k_bf.reshape(b, l, FOX_W),
                                    v_bf.reshape(b, l, FOX_W), negc).reshape(t, FOX_W)
            else:
                k_pool, v_pool, loc, tot, page_table = fox_cache
                rows = FOX_HEADS * l
                q_ht = jnp.transpose(q_bf.reshape(b, l, FOX_HEADS, HEAD_DIM), (0, 2, 1, 3)).reshape(b, rows, HEAD_DIM)
                o_ht = _fox_decode(q_ht, _pad_rows(k_f.reshape(b, rows, HEAD_DIM), LANES),
                                   _pad_rows(v_f.reshape(b, rows, HEAD_DIM), LANES),
                                   _pad_rows(c_tm[:, :l].reshape(b, rows, 1), LANES).reshape(b, 1, LANES),
                                   k_pool, v_pool, loc, tot, page_table, j, l)
                o_fox = jnp.transpose(o_ht.reshape(b, FOX_HEADS, l, HEAD_DIM), (0, 2, 1, 3)).reshape(t, FOX_W)
            xc, new_buf = _conv(zx3, SSD_INNER, conv0[j], W["ssd_conv_w"][j], W["ssd_conv_b"][j])
            if lp != l:
                xc_p, z_p = _pad_rows(xc, lp), _pad_rows(zx3[:, :, :SSD_INNER], lp)
            else:
                xc_p, z_p = xc, zx3
            yg, h_last = _ssd(xc_p, tail_p, z_p, _lane_vec(W["ssd_dt_bias"][j], DT_LANE0),
                              _lane_vec(W["ssd_A_log"][j], DT_LANE0),
                              jnp.repeat(W["ssd_D"][j].astype(F32), SSD_HD).reshape(1, SSD_INNER),
                              e_mat, ssm0[j].reshape(b, SSD_INNER, SSD_STATE), l)
            yg = yg[:, :l].reshape(t, SSD_INNER)
            x2 = _outproj_even(o_fox, yg, W["ssd_norm"][j], W["w_out_even"], j, x2)
            fk.append(k_f.reshape(b, l, FOX_HEADS, HEAD_DIM))
            fv.append(v_f.reshape(b, l, FOX_HEADS, HEAD_DIM))
            fl.append(lf)
            hs.append(h_last.reshape(b, SSD_HEADS, SSD_HD, SSD_STATE))
            bufs.append(new_buf)
        else:
            (u,) = _dense(x2, W["w_in_odd"], layer=j, n_cols=d, tn=512, pro="norm", gain=W["norm_mix"][i],
                          name="inproj_odd", tm_max=TM_WIDE, x_single=True)
            p = W["s5_packed"][j]
            ngb = p["bbd_re"].shape[0]
            g_bf, s_re, s_im = _s5(u.reshape(b, l, d), p,
                                   s5_re0[j].reshape(b, ngb, 1, -1), s5_im0[j].reshape(b, ngb, 1, -1))
            (x2,) = _dense(g_bf.reshape(t, d), W["s5_w_glu"], layer=j, n_cols=d, tn=512, col0=0, col0_2=d,
                           epi="glu_res", res=x2, name="s5_glu_out")
            srs.append(s_re.reshape(b, -1, S5_STATE))
            sis.append(s_im.reshape(b, -1, S5_STATE))
        x3 = _cross_attn(x2.reshape(b, l, d), W["norm_cross"][i], W["w_mq"], W["mem_q_norm"][i],
                         mem_k, mem_v, W["w_mo"], i)
        x2 = x3.reshape(t, d)
        hid = W["w_ffn_down"].shape[1]
        (hmid,) = _dense(x2, W["w_ffn_up"], layer=i, n_cols=hid, tn=512, col0=0, col0_2=hid, pro="norm",
                         gain=W["norm_ffn"][i], epi="swiglu", out_dtypes=(BF16,), name="ffn_up")
        (x2,) = _dense(hmid, W["w_ffn_down"], layer=i, n_cols=d, tn=256, epi="residual", res=x2, name="ffn_down")
    return (x2.reshape(b, l, d), jnp.stack(fk), jnp.stack(fv), jnp.stack(fl), jnp.stack(hs), jnp.stack(bufs),
            jnp.stack(srs), jnp.stack(sis))


def _pack_s5(a_re, a_im, b_re, b_im, c_re, c_im, d_skip, log_dt):
    g, n, k = b_re.shape
    ngb = g // S5_GB
    ps_re, ps_im, pc_re, pc_im, bb_re, bb_im = _s5_prep(
        a_re.astype(F32), a_im.astype(F32), log_dt.astype(F32),
        jnp.transpose(b_re, (2, 0, 1)).astype(F32), jnp.transpose(b_im, (2, 0, 1)).astype(F32))

    def bmat(bb):
        return _block_diag(jnp.transpose(bb, (1, 0, 2)).reshape(ngb, S5_GB * k, n), S5_GB)

    def cmat(cm):
        return _block_diag(jnp.transpose(cm.astype(F32), (0, 2, 1)).reshape(ngb, S5_GB * n, k), S5_GB)

    def rows(pw):
        return jnp.transpose(pw.reshape(pw.shape[0], ngb, S5_GB * n), (1, 0, 2))

    return dict(bbd_re=bmat(bb_re), bbd_im=bmat(bb_im), cbd_re=cmat(c_re), cbd_im=cmat(c_im),
                ps_re=rows(ps_re), ps_im=rows(ps_im), pc_re=rows(pc_re), pc_im=rows(pc_im),
                d_vec=d_skip.astype(F32).reshape(ngb, 1, S5_GB * k))


def kernel(x_prompt, x_sample, mem_prompt, cache_fox_k, cache_fox_v, cache_fox_logf, cache_mem_k, cache_mem_v,
           state_ssd, state_conv, state_s5_re, state_s5_im, page_table,
           norm_mix, norm_cross, norm_mem, norm_ffn,
           w_in_even, fox_b_forget, fox_q_norm, fox_k_norm, ssd_conv_w, ssd_conv_b, ssd_dt_bias, ssd_A_log,
           ssd_D, ssd_norm, w_out_even,
           w_in_odd, s5_A_re, s5_A_im, s5_B_re, s5_B_im, s5_C_re, s5_C_im, s5_D, s5_log_dt, s5_w_glu,
           w_mq, w_mkv, mem_q_norm, mem_k_norm, w_mo, w_ffn_up, w_ffn_down):
    depth = norm_mix.shape[0]
    n_even, n_odd = w_in_even.shape[0], w_in_odd.shape[0]
    b, l, d = x_prompt.shape
    z0 = 3 * FOX_W + FOX_HEADS
    dt0 = z0 + SSD_INNER + CONV_DIM
    W = {
        "norm_mix": norm_mix, "norm_cross": norm_cross, "norm_ffn": norm_ffn,
        "w_in_even": w_in_even[:, :, :3 * FOX_W],
        "w_zx": [w_in_even[j, :, z0:z0 + SSD_INNER + CONV_DIM] for j in range(n_even)],
        "w_tail": jnp.concatenate([w_in_even[:, :, 3 * FOX_W:z0], w_in_even[:, :, dt0:dt0 + SSD_HEADS],
                                   jnp.zeros((n_even, d, LANES - FOX_HEADS - SSD_HEADS), w_in_even.dtype)], axis=2),
        "fox_b_forget": fox_b_forget, "fox_q_norm": fox_q_norm, "fox_k_norm": fox_k_norm,
        "ssd_conv_w": ssd_conv_w, "ssd_conv_b": ssd_conv_b, "ssd_dt_bias": ssd_dt_bias, "ssd_A_log": ssd_A_log,
        "ssd_D": ssd_D, "ssd_norm": ssd_norm, "w_out_even": w_out_even,
        "w_in_odd": w_in_odd, "s5_w_glu": s5_w_glu,
        "s5_packed": [_pack_s5(s5_A_re[j], s5_A_im[j], s5_B_re[j], s5_B_im[j], s5_C_re[j], s5_C_im[j],
                               s5_D[j], s5_log_dt[j]) for j in range(n_odd)],
        "w_mq": w_mq, "mem_q_norm": mem_q_norm, "w_mo": w_mo, "w_ffn_up": w_ffn_up, "w_ffn_down": w_ffn_down,
    }
    mt = mem_prompt.shape[1]
    mw = w_mkv.shape[2] // 2
    mem2 = mem_prompt.reshape(b * mt, d)
    mk, mv = [], []
    for i in range(depth):
        (k_i,) = _dense(mem2, w_mkv, layer=i, n_cols=mw, tn=mw, col0=0, pro="norm", gain=norm_mem[i],
                        epi="headnorm", head_gain=mem_k_norm[i], name="mem_k_proj")
        (v_i,) = _dense(mem2, w_mkv, layer=i, n_cols=mw, tn=mw, col0=mw, pro="norm", gain=norm_mem[i],
                        name="mem_v_proj")
        mk.append(k_i.reshape(b, mt, mw))
        mv.append(v_i.reshape(b, mt, mw))
    mem_k_p = jnp.stack(mk)
    mem_v_p = jnp.stack(mv)
    n_grp = s5_A_re.shape[1]
    (y_prompt, fox_k_p, fox_v_p, fox_logf_p, ssd_p, conv_p, s5_re_p, s5_im_p) = _trunk(
        x_prompt, W, mem_k_p, mem_v_p,
        jnp.zeros((n_even, b, SSD_CONV - 1, CONV_DIM), F32),
        jnp.zeros((n_even, b, SSD_HEADS, SSD_HD, SSD_STATE), F32),
        jnp.zeros((n_odd, b, n_grp, S5_STATE), F32),
        jnp.zeros((n_odd, b, n_grp, S5_STATE), F32),
        None)
    db = x_sample.shape[0]
    n_pool, page = cache_fox_k.shape[1], cache_fox_k.shape[2]
    assert page == LANES
    width = page * FOX_HEADS
    lf_t = jnp.transpose(cache_fox_logf.astype(F32), (0, 2, 1, 3)).reshape(n_even, page, n_pool * FOX_HEADS)
    incl = _pool_cumsum(lf_t).reshape(n_even, page, n_pool, FOX_HEADS)
    loc = jnp.transpose(incl, (0, 2, 1, 3)).reshape(n_even, n_pool, 1, width)
    tot = jnp.tile(incl[:, page - 1], (1, 1, page)).reshape(n_even, n_pool, 1, width)
    fox_cache = (cache_fox_k.reshape(n_even, n_pool, width, HEAD_DIM),
                 cache_fox_v.reshape(n_even, n_pool, width, HEAD_DIM), loc, tot, page_table)
    (y_sample, fox_k_s, fox_v_s, fox_logf_s, ssd_s, conv_s, s5_re_s, s5_im_s) = _trunk(
        x_sample, W, cache_mem_k.reshape(depth, db, mt, mw), cache_mem_v.reshape(depth, db, mt, mw),
        state_conv, state_ssd, state_s5_re, state_s5_im, fox_cache)
    hd = mw // MEM_HEADS
    return (y_prompt, y_sample,
            fox_k_p, fox_v_p, fox_logf_p,
            mem_k_p.reshape(depth, b, mt, MEM_HEADS, hd), mem_v_p.reshape(depth, b, mt, MEM_HEADS, hd),
            ssd_p, conv_p, s5_re_p, s5_im_p,
            fox_k_s, fox_v_s, fox_logf_s, ssd_s, conv_s, s5_re_s, s5_im_s)
```

```python
import functools
import math

import jax
import jax.numpy as jnp
from jax import lax
from jax.experimental import pallas as pl
from jax.experimental.pallas import tpu as pltpu

F32 = jnp.float32
BF16 = jnp.bfloat16
EPS = 1e-6
LANES = 128
SUBLANES = 8
VMEM_LIMIT_BYTES = 56 * 1024 * 1024

FOX_HEADS = 8
HEAD_DIM = 128
FOX_W = FOX_HEADS * HEAD_DIM
SSD_HEADS = 32
SSD_HD = 64
SSD_GROUPS = 4
SSD_STATE = 128
SSD_CHUNK = 128
SSD_INNER = SSD_HEADS * SSD_HD
SSD_CONV = 4
CONV_DIM = SSD_INNER + 2 * SSD_GROUPS * SSD_STATE
DT_LANE0 = FOX_HEADS
S5_GROUP = 16
S5_STATE = 64
S5_GB = 16
MEM_HEADS = 4
HIGHEST = lax.Precision.HIGHEST


def _cparams(sem):
    return pltpu.CompilerParams(dimension_semantics=sem, vmem_limit_bytes=VMEM_LIMIT_BYTES)


def _gelu_tanh(x):
    return 0.5 * x * (1.0 + jnp.tanh(math.sqrt(2.0 / math.pi) * (x + 0.044715 * x * x * x)))


def _softplus(x):
    return jnp.maximum(x, 0.0) + jnp.log1p(jnp.exp(-jnp.abs(x)))


def _silu(x):
    return x * jax.nn.sigmoid(x)


def _rms(x, gain):
    return x * lax.rsqrt(jnp.mean(x * x, axis=-1, keepdims=True) + EPS) * gain


def _dot(a, b):
    return jnp.dot(a, b, preferred_element_type=F32)


def _dot_nt(a, b):
    return lax.dot_general(a, b, (((1,), (1,)), ((), ())), preferred_element_type=F32)


def _dot_exact(a, b):
    return jnp.dot(a, b, preferred_element_type=F32, precision=HIGHEST)


def _dense_kernel(*refs, pro, epi, n_out):
    it = iter(refs)
    x_ref = next(it)
    gain_ref = next(it) if pro == "norm" else None
    w_ref = next(it)
    w2_ref = next(it) if epi in ("swiglu", "glu_res") else None
    res_ref = next(it) if epi in ("residual", "glu_res") else None
    hg_ref = next(it) if epi == "headnorm" else None
    out_refs = [next(it) for _ in range(n_out)]
    xs_ref = next(it) if pro != "none" else None

    if pro != "none":
        @pl.when(pl.program_id(1) == 0)
        def _():
            xf = x_ref[...].astype(F32)
            if pro == "norm":
                xf = _rms(xf, gain_ref[...])
            elif pro == "gelu":
                xf = _gelu_tanh(xf)
            xs_ref[...] = xf.astype(BF16)
        lhs = xs_ref[...]
    else:
        lhs = x_ref[...]

    acc = _dot(lhs, w_ref[...].astype(BF16))
    if epi == "swiglu":
        acc = _silu(acc) * _dot(lhs, w2_ref[...].astype(BF16))
    elif epi == "glu_res":
        acc = res_ref[...] + acc * jax.nn.sigmoid(_dot(lhs, w2_ref[...].astype(BF16)))
    elif epi == "residual":
        acc = res_ref[...] + acc

    if epi == "headnorm":
        for c in range(acc.shape[1] // HEAD_DIM):
            sl = slice(c * HEAD_DIM, (c + 1) * HEAD_DIM)
            blk = _rms(acc[:, sl], hg_ref[...])
            for o in out_refs:
                o[:, sl] = blk.astype(o.dtype)
    else:
        for o in out_refs:
            o[...] = acc.astype(o.dtype)


def _wspec(w, layer, k, tn, blk0):
    if w.ndim == 2:
        return pl.BlockSpec((k, tn), lambda i, j: (0, blk0 + j))
    return pl.BlockSpec((None, k, tn), lambda i, j: (layer, 0, blk0 + j))


def _dense(x, w, *, n_cols, tn, name, layer=0, col0=0, col0_2=None, pro="none", gain=None, epi="plain",
           res=None, head_gain=None, out_dtypes=(F32,), tm_max=1024, x_single=False):
    m, k = x.shape
    tm = min(tm_max, m)
    assert m % tm == 0 and n_cols % tn == 0 and col0 % tn == 0
    b0 = col0 // tn
    xmode = dict(pipeline_mode=pl.Buffered(1)) if x_single else {}
    in_specs = [pl.BlockSpec((tm, k), lambda i, j: (i, 0), **xmode)]
    args = [x]
    if pro == "norm":
        in_specs.append(pl.BlockSpec((1, k), lambda i, j: (0, 0)))
        args.append(gain.reshape(1, k).astype(F32))
    in_specs.append(_wspec(w, layer, k, tn, b0))
    args.append(w)
    if epi in ("swiglu", "glu_res"):
        assert col0_2 % tn == 0
        in_specs.append(_wspec(w, layer, k, tn, col0_2 // tn))
        args.append(w)
    if epi in ("residual", "glu_res"):
        in_specs.append(pl.BlockSpec((tm, tn), lambda i, j: (i, j)))
        args.append(res)
    if epi == "headnorm":
        in_specs.append(pl.BlockSpec((1, HEAD_DIM), lambda i, j: (0, 0)))
        args.append(head_gain.reshape(1, HEAD_DIM).astype(F32))
    out_shape = [jax.ShapeDtypeStruct((m, n_cols), dt) for dt in out_dtypes]
    out_specs = [pl.BlockSpec((tm, tn), lambda i, j: (i, j)) for _ in out_dtypes]
    scratch = [pltpu.VMEM((tm, k), BF16)] if pro != "none" else []
    outs = pl.pallas_call(
        functools.partial(_dense_kernel, pro=pro, epi=epi, n_out=len(out_dtypes)),
        grid=(m // tm, n_cols // tn),
        in_specs=in_specs, out_specs=out_specs, out_shape=out_shape,
        scratch_shapes=scratch,
        compiler_params=_cparams(("parallel", "arbitrary")),
        name=name,
    )(*args)
    return outs


TM_WIDE = 2048


def _inproj_qkv_kernel(x_ref, g_ref, w_ref, qg_ref, kg_ref, *rest, nq):
    q_ref, kf_ref, kb_ref, vf_ref, vb_ref, xs_ref = rest[-6:]
    j = pl.program_id(1)

    @pl.when(j == 0)
    def _():
        xs_ref[...] = _rms(x_ref[...], g_ref[...]).astype(BF16)

    def headnorm(acc, gain, outs):
        for c in range(acc.shape[1] // HEAD_DIM):
            sl = slice(c * HEAD_DIM, (c + 1) * HEAD_DIM)
            blk = _rms(acc[:, sl], gain)
            for o in outs:
                o[:, sl] = blk.astype(o.dtype)

    @pl.when(j < nq)
    def _():
        headnorm(_dot_nt(xs_ref[...], w_ref[...].astype(BF16)), qg_ref[...], (q_ref,))

    @pl.when((j >= nq) & (j < 2 * nq))
    def _():
        headnorm(_dot_nt(xs_ref[...], w_ref[...].astype(BF16)), kg_ref[...], (kf_ref, kb_ref))

    @pl.when(j >= 2 * nq)
    def _():
        acc = _dot_nt(xs_ref[...], w_ref[...].astype(BF16))
        vf_ref[...] = acc
        vb_ref[...] = acc.astype(BF16)


def _inproj_qkv(x, gain, w_t, layer, n_layers, q_gain, k_gain, stacks=None, tn=256):
    m, k = x.shape
    tm = min(TM_WIDE, m)
    nq = FOX_W // tn
    tile = lambda lo: pl.BlockSpec((tm, tn), lambda i, j: (i, jnp.clip(j - lo, 0, nq - 1)))
    stile = lambda lo: pl.BlockSpec((None, tm, tn), lambda i, j: (layer, i, jnp.clip(j - lo, 0, nq - 1)))
    sd = jax.ShapeDtypeStruct
    in_specs = [pl.BlockSpec((tm, k), lambda i, j: (i, 0), pipeline_mode=pl.Buffered(1)),
                pl.BlockSpec((1, k), lambda i, j: (0, 0)),
                pl.BlockSpec((None, tn, k), lambda i, j: (layer, j, 0)),
                pl.BlockSpec((1, HEAD_DIM), lambda i, j: (0, 0)),
                pl.BlockSpec((1, HEAD_DIM), lambda i, j: (0, 0))]
    args = [x, gain.reshape(1, k), w_t, q_gain.reshape(1, HEAD_DIM), k_gain.reshape(1, HEAD_DIM)]
    aliases = {}
    if stacks is not None:
        in_specs += [pl.BlockSpec(memory_space=pl.ANY), pl.BlockSpec(memory_space=pl.ANY)]
        aliases = {len(args): 1, len(args) + 1: 3}
        args += list(stacks)
    return pl.pallas_call(
        functools.partial(_inproj_qkv_kernel, nq=nq),
        grid=(m // tm, 3 * nq),
        in_specs=in_specs,
        out_specs=[tile(0), stile(nq), tile(nq), stile(2 * nq), tile(2 * nq)],
        out_shape=[sd((m, FOX_W), BF16), sd((n_layers, m, FOX_W), F32), sd((m, FOX_W), BF16),
                   sd((n_layers, m, FOX_W), F32), sd((m, FOX_W), BF16)],
        scratch_shapes=[pltpu.VMEM((tm, k), BF16)],
        input_output_aliases=aliases,
        compiler_params=_cparams(("parallel", "arbitrary")),
        name="inproj_qkv",
    )(*args)


def _inproj_zx_kernel(x_ref, g_ref, wzx_ref, wt_ref, zx_ref, tail_ref, xs_ref, *, nzx):
    j = pl.program_id(1)

    @pl.when(j == 0)
    def _():
        xs_ref[...] = _rms(x_ref[...], g_ref[...]).astype(BF16)

    @pl.when(j < nzx)
    def _():
        zx_ref[...] = _dot_nt(xs_ref[...], wzx_ref[...].astype(BF16))

    @pl.when(j == nzx)
    def _():
        tail_ref[...] = _dot_nt(xs_ref[...], wt_ref[...].astype(BF16))


def _inproj_zx(x, gain, w_zx, w_tail, layer, tn=512):
    m, k = x.shape
    tm = min(TM_WIDE, m)
    nzx = w_zx.shape[1] // tn
    return pl.pallas_call(
        functools.partial(_inproj_zx_kernel, nzx=nzx),
        grid=(m // tm, nzx + 1),
        in_specs=[pl.BlockSpec((tm, k), lambda i, j: (i, 0), pipeline_mode=pl.Buffered(1)),
                  pl.BlockSpec((1, k), lambda i, j: (0, 0)),
                  pl.BlockSpec((None, tn, k), lambda i, j: (layer, jnp.minimum(j, nzx - 1), 0)),
                  pl.BlockSpec((None, LANES, k), lambda i, j: (layer, 0, 0), pipeline_mode=pl.Buffered(1))],
        out_specs=[pl.BlockSpec((tm, tn), lambda i, j: (i, jnp.minimum(j, nzx - 1))),
                   pl.BlockSpec((tm, LANES), lambda i, j: (i, 0))],
        out_shape=[jax.ShapeDtypeStruct((m, nzx * tn), F32), jax.ShapeDtypeStruct((m, LANES), F32)],
        scratch_shapes=[pltpu.VMEM((tm, k), BF16)],
        compiler_params=_cparams(("parallel", "arbitrary")),
        name="inproj_zx",
    )(x, gain.reshape(1, k), w_zx, w_tail)


def _outproj_kernel(o_ref, y_ref, g_ref, w_ref, res_ref, out_ref, yn_ref):
    k1 = o_ref.shape[1]

    @pl.when(pl.program_id(1) == 0)
    def _():
        yn_ref[...] = _rms(y_ref[...], g_ref[...]).astype(BF16)
    acc = _dot(o_ref[...], w_ref[:k1, :].astype(BF16)) + _dot(yn_ref[...], w_ref[k1:, :].astype(BF16))
    out_ref[...] = res_ref[...] + acc


def _outproj_even(o_fox, yg, gain, w_out, layer, res, tn=256, tm_max=TM_WIDE):
    m, d = res.shape
    tm = min(tm_max, m)
    k1, k2 = o_fox.shape[1], yg.shape[1]
    return pl.pallas_call(
        _outproj_kernel,
        grid=(m // tm, d // tn),
        in_specs=[
            pl.BlockSpec((tm, k1), lambda i, j: (i, 0)),
            pl.BlockSpec((tm, k2), lambda i, j: (i, 0), pipeline_mode=pl.Buffered(1)),
            pl.BlockSpec((1, k2), lambda i, j: (0, 0)),
            pl.BlockSpec((None, k1 + k2, tn), lambda i, j: (layer, 0, j)),
            pl.BlockSpec((tm, tn), lambda i, j: (i, j)),
        ],
        out_specs=pl.BlockSpec((tm, tn), lambda i, j: (i, j)),
        out_shape=jax.ShapeDtypeStruct((m, d), F32),
        scratch_shapes=[pltpu.VMEM((tm, k2), BF16)],
        compiler_params=_cparams(("parallel", "arbitrary")),
        name="outproj_even",
    )(o_fox, yg, gain.reshape(1, k2).astype(F32), w_out, res)


def _lane_cumsum(x):
    lane = lax.broadcasted_iota(jnp.int32, x.shape, 1)
    k = 1
    while k < LANES:
        x = x + jnp.where(lane >= k, pltpu.roll(x, k, axis=1), 0.0)
        k *= 2
    return x


def _forget_kernel(raw_ref, b_ref, lf_ref, negc_ref, ctm_ref):
    lp = raw_ref.shape[1]
    lf = -_softplus(-(raw_ref[0] + b_ref[...]))
    lf_ref[0] = lf[:, :FOX_HEADS]
    row_i = lax.broadcasted_iota(jnp.int32, (LANES, LANES), 0)
    col_i = lax.broadcasted_iota(jnp.int32, (LANES, LANES), 1)
    tri = (col_i <= row_i).astype(F32)
    ctm_ref[0] = _dot_exact(tri, lf[:LANES, :])[:, :FOX_HEADS]
    carry = jnp.zeros((FOX_HEADS, 1), F32)
    for c in range(lp // LANES):
        blk = lf[c * LANES:(c + 1) * LANES, :].T[:FOX_HEADS, :]
        cs = _lane_cumsum(blk) + carry
        negc_ref[0, :, c * LANES:(c + 1) * LANES] = -cs
        carry = cs[:, LANES - 1:LANES]


def _forget(raw, b_vec):
    b, lp, _ = raw.shape
    return pl.pallas_call(
        _forget_kernel,
        grid=(b,),
        in_specs=[pl.BlockSpec((1, lp, LANES), lambda i: (i, 0, 0)),
                  pl.BlockSpec((1, LANES), lambda i: (0, 0))],
        out_specs=[pl.BlockSpec((1, lp, FOX_HEADS), lambda i: (i, 0, 0)),
                   pl.BlockSpec((1, FOX_HEADS, lp), lambda i: (i, 0, 0)),
                   pl.BlockSpec((1, LANES, FOX_HEADS), lambda i: (i, 0, 0))],
        out_shape=[jax.ShapeDtypeStruct((b, lp, FOX_HEADS), F32),
                   jax.ShapeDtypeStruct((b, FOX_HEADS, lp), F32),
                   jax.ShapeDtypeStruct((b, LANES, FOX_HEADS), F32)],
        compiler_params=_cparams(("parallel",)),
        name="forget_gates",
    )(raw, b_vec)


def _fox_prompt_kernel(q_ref, k_ref, v_ref, nb_ref, o_ref, m_sc, acc_sc, mask_sc, *, tq, tk):
    qi = pl.program_id(1)
    kj = pl.program_id(2)

    @pl.when(kj == 0)
    def _():
        m_sc[...] = jnp.full(m_sc.shape, -jnp.inf, F32)
        acc_sc[...] = jnp.zeros(acc_sc.shape, F32)

    @pl.when(kj * tk <= qi * tq + (tq - 1))
    def _():
        row = qi * tq + lax.broadcasted_iota(jnp.int32, (tq, tk), 0)
        col = kj * tk + lax.broadcasted_iota(jnp.int32, (tq, tk), 1)
        mask_sc[...] = jnp.where(col <= row, 0.0, -jnp.inf)
        ones = jnp.ones((tk, HEAD_DIM), BF16)
        for h in range(FOX_HEADS):
            sl = slice(h * HEAD_DIM, (h + 1) * HEAD_DIM)
            s = _dot_nt(q_ref[0, :, sl], k_ref[0, :, sl]) * (HEAD_DIM ** -0.5)
            s = s + nb_ref[0, h:h + 1, :] + mask_sc[...]
            m_old = m_sc[h]
            m_new = jnp.maximum(m_old, jnp.max(s, axis=-1, keepdims=True))
            p = jnp.exp(s - m_new).astype(BF16)
            pv = _dot(p, jnp.concatenate([v_ref[0, :, sl], ones], axis=1))
            acc_sc[h] = jnp.exp(m_old - m_new) * acc_sc[h] + pv
            m_sc[h] = m_new

    @pl.when(kj == pl.num_programs(2) - 1)
    def _():
        for h in range(FOX_HEADS):
            acc = acc_sc[h]
            o_ref[0, :, h * HEAD_DIM:(h + 1) * HEAD_DIM] = (acc[:, :HEAD_DIM] / acc[:, HEAD_DIM:]).astype(o_ref.dtype)


def _fox_prompt(q, k, v, negc, tq=512, tk=512):
    b, l, _ = q.shape
    tq, tk = min(tq, l), min(tk, l)
    nq, nk = l // tq, l // tk

    def last_needed(qi, kj):
        return jnp.minimum(kj, (qi * tq + tq - 1) // tk)

    return pl.pallas_call(
        functools.partial(_fox_prompt_kernel, tq=tq, tk=tk),
        grid=(b, nq, nk),
        in_specs=[pl.BlockSpec((1, tq, FOX_W), lambda bi, qi, kj: (bi, qi, 0)),
                  pl.BlockSpec((1, tk, FOX_W), lambda bi, qi, kj: (bi, last_needed(qi, kj), 0)),
                  pl.BlockSpec((1, tk, FOX_W), lambda bi, qi, kj: (bi, last_needed(qi, kj), 0)),
                  pl.BlockSpec((1, FOX_HEADS, tk), lambda bi, qi, kj: (bi, 0, last_needed(qi, kj)))],
        out_specs=pl.BlockSpec((1, tq, FOX_W), lambda bi, qi, kj: (bi, qi, 0)),
        out_shape=jax.ShapeDtypeStruct((b, l, FOX_W), BF16),
        scratch_shapes=[pltpu.VMEM((FOX_HEADS, tq, 1), F32),
                        pltpu.VMEM((FOX_HEADS, tq, 2 * HEAD_DIM), F32),
                        pltpu.VMEM((tq, tk), F32)],
        compiler_params=_cparams(("parallel", "parallel", "arbitrary")),
        name="fox_prompt",
    )(q, k, v, negc)


def _pool_cumsum_kernel(x_ref, o_ref):
    n = x_ref.shape[0]
    row_i = lax.broadcasted_iota(jnp.int32, (n, n), 0)
    col_i = lax.broadcasted_iota(jnp.int32, (n, n), 1)
    o_ref[...] = _dot_exact((col_i <= row_i).astype(F32), x_ref[...])


def _pool_cumsum(x, tc=2048):
    nl, page, cols = x.shape
    tc = tc if cols % tc == 0 else cols
    return pl.pallas_call(
        _pool_cumsum_kernel,
        grid=(nl, cols // tc),
        in_specs=[pl.BlockSpec((None, page, tc), lambda i, j: (i, 0, j))],
        out_specs=pl.BlockSpec((None, page, tc), lambda i, j: (i, 0, j)),
        out_shape=jax.ShapeDtypeStruct((nl, page, cols), F32),
        compiler_params=_cparams(("parallel", "parallel")),
        name="pool_logf_cumsum",
    )(x)


def _fox_decode_kernel(pt_ref, q_ref, *refs, gp, t_new):
    kv_refs = refs[:4 * gp]
    kn_ref, vn_ref, cn_ref, o_ref, m_sc, l_sc, acc_sc, carry_sc = refs[4 * gp:]
    p = pl.program_id(1)
    n_past = pl.num_programs(1) - 1
    rows = q_ref.shape[1]
    scale = HEAD_DIM ** -0.5

    @pl.when(p == 0)
    def _():
        m_sc[...] = jnp.full(m_sc.shape, -jnp.inf, F32)
        l_sc[...] = jnp.zeros(l_sc.shape, F32)
        acc_sc[...] = jnp.zeros(acc_sc.shape, F32)
        carry_sc[...] = jnp.zeros(carry_sc.shape, F32)

    def update(s, v, state):
        m_old, l_old, acc = state
        m_new = jnp.maximum(m_old, jnp.max(s, axis=-1, keepdims=True))
        alpha = jnp.exp(m_old - m_new)
        pr = jnp.exp(s - m_new)
        l_new = alpha * l_old + jnp.sum(pr, axis=-1, keepdims=True)
        acc = alpha * acc + _dot(pr.astype(BF16), v.astype(BF16))
        return m_new, l_new, acc

    @pl.when(p < n_past)
    def _():
        q = q_ref[0]
        width = kv_refs[0].shape[0]
        row = lax.broadcasted_iota(jnp.int32, (rows, width), 0)
        col = lax.broadcasted_iota(jnp.int32, (rows, width), 1)
        own_head = (col % FOX_HEADS) == (row // t_new)
        carry = carry_sc[...]
        scores = []
        for g in range(gp):
            k_ref, _, loc_ref, tot_ref = kv_refs[4 * g:4 * g + 4]
            s = _dot_nt(q, k_ref[...].astype(BF16)) * scale - (carry + loc_ref[...])
            scores.append(jnp.where(own_head, s, -jnp.inf))
            carry = carry + tot_ref[...]
        carry_sc[...] = carry
        m_old = m_sc[...]
        m_new = m_old
        for s in scores:
            m_new = jnp.maximum(m_new, jnp.max(s, axis=-1, keepdims=True))
        alpha = jnp.exp(m_old - m_new)
        l_new = alpha * l_sc[...]
        acc = alpha * acc_sc[...]
        for g, s in enumerate(scores):
            pr = jnp.exp(s - m_new)
            l_new = l_new + jnp.sum(pr, axis=-1, keepdims=True)
            acc = acc + _dot(pr.astype(BF16), kv_refs[4 * g + 1][...].astype(BF16))
        m_sc[...], l_sc[...], acc_sc[...] = m_new, l_new, acc

    @pl.when(p == n_past)
    def _():
        width = kn_ref.shape[1]
        row = lax.broadcasted_iota(jnp.int32, (rows, width), 0)
        col = lax.broadcasted_iota(jnp.int32, (rows, width), 1)
        keep = ((col % FOX_HEADS) == (row // t_new)) & ((col // FOX_HEADS) <= (row % t_new))
        s = _dot_nt(q_ref[0], kn_ref[0].astype(BF16)) * scale - (carry_sc[:, :width] + cn_ref[0])
        m, l, acc = update(jnp.where(keep, s, -jnp.inf), vn_ref[0], (m_sc[...], l_sc[...], acc_sc[...]))
        o_ref[0] = (acc / l).astype(o_ref.dtype)


def _fox_decode(q, k_new, v_new, c_new, k_pool, v_pool, loc, tot, page_table, layer, t_new, gp=16):
    b, rows, _ = q.shape
    n_pages = page_table.shape[1]
    gp = max(g for g in range(1, gp + 1) if n_pages % g == 0)
    n_steps = n_pages // gp
    width = k_pool.shape[2]

    def page_map(g):
        return lambda bi, p, pt: (layer, pt[bi, jnp.minimum(p, n_steps - 1) * gp + g], 0, 0)

    in_specs = [pl.BlockSpec((1, rows, HEAD_DIM), lambda bi, p, pt: (bi, 0, 0))]
    args = [q]
    for g in range(gp):
        in_specs += [pl.BlockSpec((None, None, width, HEAD_DIM), page_map(g)),
                     pl.BlockSpec((None, None, width, HEAD_DIM), page_map(g)),
                     pl.BlockSpec((None, None, 1, width), page_map(g)),
                     pl.BlockSpec((None, None, 1, width), page_map(g))]
        args += [k_pool, v_pool, loc, tot]
    new_spec = pl.BlockSpec((1, k_new.shape[1], HEAD_DIM), lambda bi, p, pt: (bi, 0, 0))
    in_specs += [new_spec, new_spec, pl.BlockSpec((1, 1, c_new.shape[2]), lambda bi, p, pt: (bi, 0, 0))]
    args += [k_new, v_new, c_new]
    grid_spec = pltpu.PrefetchScalarGridSpec(
        num_scalar_prefetch=1,
        grid=(b, n_steps + 1),
        in_specs=in_specs,
        out_specs=pl.BlockSpec((1, rows, HEAD_DIM), lambda bi, p, pt: (bi, 0, 0)),
        scratch_shapes=[pltpu.VMEM((rows, 1), F32), pltpu.VMEM((rows, 1), F32),
                        pltpu.VMEM((rows, HEAD_DIM), F32), pltpu.VMEM((1, width), F32)],
    )
    return pl.pallas_call(
        functools.partial(_fox_decode_kernel, gp=gp, t_new=t_new), grid_spec=grid_spec,
        out_shape=jax.ShapeDtypeStruct((b, rows, HEAD_DIM), BF16),
        compiler_params=_cparams(("parallel", "arbitrary")),
        name="fox_decode",
    )(page_table, *args)


def _conv_kernel(x_ref, buf_ref, w_ref, b_ref, y_ref, nb_ref, full_sc):
    l = x_ref.shape[1]
    pad = SUBLANES
    full_sc[pad - (SSD_CONV - 1):pad, :] = buf_ref[0]
    full_sc[pad:pad + l, :] = x_ref[0]
    acc = b_ref[...] + full_sc[pad - 3:pad - 3 + l, :] * w_ref[0:1, :]
    for kk in range(1, SSD_CONV):
        acc = acc + full_sc[pad - 3 + kk:pad - 3 + kk + l, :] * w_ref[kk:kk + 1, :]
    y_ref[0] = _silu(acc)
    nb_ref[0] = full_sc[pad + l - (SSD_CONV - 1):pad + l, :]


def _conv(zx, col0, buf, w, bias, tc=512):
    b, l, _ = zx.shape
    c = w.shape[1]
    cb0 = col0 // tc
    return pl.pallas_call(
        _conv_kernel,
        grid=(b, c // tc),
        in_specs=[pl.BlockSpec((1, l, tc), lambda i, j: (i, 0, cb0 + j)),
                  pl.BlockSpec((1, SSD_CONV - 1, tc), lambda i, j: (i, 0, j)),
                  pl.BlockSpec((SSD_CONV, tc), lambda i, j: (0, j)),
                  pl.BlockSpec((1, tc), lambda i, j: (0, j))],
        out_specs=[pl.BlockSpec((1, l, tc), lambda i, j: (i, 0, j)),
                   pl.BlockSpec((1, SSD_CONV - 1, tc), lambda i, j: (i, 0, j))],
        out_shape=[jax.ShapeDtypeStruct((b, l, c), F32),
                   jax.ShapeDtypeStruct((b, SSD_CONV - 1, c), F32)],
        scratch_shapes=[pltpu.VMEM((l + SUBLANES, tc), F32)],
        compiler_params=_cparams(("parallel", "parallel")),
        name="ssd_conv",
    )(zx, buf, w, bias.reshape(1, c))


def _ssd_kernel(xc_ref, tail_ref, z_ref, dtb_ref, alog_ref, dexp_ref, e_ref, h0_ref,
                y_ref, hout_ref, h_sc, *, valid_len):
    c = pl.program_id(1)
    q = SSD_CHUNK
    gw = SSD_INNER // SSD_GROUPS
    hpg = SSD_HEADS // SSD_GROUPS

    @pl.when(c == 0)
    def _():
        h_sc[...] = h0_ref[0]

    lane = lax.broadcasted_iota(jnp.int32, (q, LANES), 1)
    row = lax.broadcasted_iota(jnp.int32, (q, LANES), 0)
    col_i = lax.broadcasted_iota(jnp.int32, (q, q), 1)
    row_i = lax.broadcasted_iota(jnp.int32, (q, q), 0)
    causal = col_i <= row_i
    tri = causal.astype(F32)

    dt = _softplus(tail_ref[0] + dtb_ref[...])
    live = (lane >= DT_LANE0) & (lane < DT_LANE0 + SSD_HEADS) & (c * q + row < valid_len)
    dt = jnp.where(live, dt, 0.0)
    a = dt * (-jnp.exp(alog_ref[...]))
    a_cum = _dot_exact(tri, a)
    a_cum_t = a_cum.T
    dt_t = dt.T
    a_last = a_cum[q - 1:q, :]
    fac = jnp.concatenate([dt * jnp.exp(a_last - a_cum), jnp.exp(a_cum)], axis=0)
    fac_hi = fac.astype(BF16)
    fac_lo = (fac - fac_hi.astype(F32)).astype(BF16)
    fac2 = jnp.concatenate([fac_hi, fac_lo], axis=0)

    for g in range(SSD_GROUPS):
        ex = _dot(fac2, e_ref[g])
        ex = ex[:2 * q] + ex[2 * q:]
        w1_e, ea_e = ex[:q], ex[q:]
        xs = xc_ref[0, :, g * gw:(g + 1) * gw]
        bb = xc_ref[0, :, SSD_INNER + g * SSD_STATE:SSD_INNER + (g + 1) * SSD_STATE].astype(BF16)
        cc = xc_ref[0, :, SSD_INNER + (SSD_GROUPS + g) * SSD_STATE:
                    SSD_INNER + (SSD_GROUPS + g + 1) * SSD_STATE].astype(BF16)
        cb = _dot_nt(cc, bb)
        hg = h_sc[g * gw:(g + 1) * gw, :]
        y = _dot_nt(cc, hg.astype(BF16)) * ea_e
        st = _dot((xs * w1_e).T.astype(BF16), bb)
        for hp in range(hpg // 2):
            pair = xs[:, hp * LANES:(hp + 1) * LANES]
            lane_p = lax.broadcasted_iota(jnp.int32, pair.shape, 1)
            yp = None
            for sub in range(2):
                h = 2 * hp + sub
                ln = DT_LANE0 + g * hpg + h
                seg = a_cum[:, ln:ln + 1] - a_cum_t[ln:ln + 1, :]
                dec = jnp.exp(jnp.where(causal, seg, -jnp.inf))
                mat = (cb * dec * dt_t[ln:ln + 1, :]).astype(BF16)
                in_head = (lane_p >= sub * SSD_HD) & (lane_p < (sub + 1) * SSD_HD)
                rhs = jnp.where(in_head, pair, 0.0).astype(BF16)
                part = _dot(mat, rhs)
                yp = part if yp is None else yp + part
                r0 = g * gw + h * SSD_HD
                h_sc[r0:r0 + SSD_HD, :] = (hg[h * SSD_HD:(h + 1) * SSD_HD, :] * jnp.exp(a_last[:, ln:ln + 1])
                                           + st[h * SSD_HD:(h + 1) * SSD_HD, :])
            cs = slice(g * gw + hp * LANES, g * gw + (hp + 1) * LANES)
            yt = yp + y[:, hp * LANES:(hp + 1) * LANES] + dexp_ref[:, cs] * xs[:, hp * LANES:(hp + 1) * LANES]
            y_ref[0, :, cs] = yt * _silu(z_ref[0, :, cs])

    @pl.when(c == pl.num_programs(1) - 1)
    def _():
        hout_ref[0] = h_sc[...]


def _ssd(xc, tail, zx, dtb_vec, alog_vec, d_exp, e_mat, h0, valid_len):
    b, lp, _ = xc.shape
    nc = lp // SSD_CHUNK
    return pl.pallas_call(
        functools.partial(_ssd_kernel, valid_len=valid_len),
        grid=(b, nc),
        in_specs=[pl.BlockSpec((1, SSD_CHUNK, CONV_DIM), lambda i, c: (i, c, 0)),
                  pl.BlockSpec((1, SSD_CHUNK, LANES), lambda i, c: (i, c, 0)),
                  pl.BlockSpec((1, SSD_CHUNK, SSD_INNER), lambda i, c: (i, c, 0)),
                  pl.BlockSpec((1, LANES), lambda i, c: (0, 0)),
                  pl.BlockSpec((1, LANES), lambda i, c: (0, 0)),
                  pl.BlockSpec((1, SSD_INNER), lambda i, c: (0, 0)),
                  pl.BlockSpec((SSD_GROUPS, LANES, SSD_INNER // SSD_GROUPS), lambda i, c: (0, 0, 0)),
                  pl.BlockSpec((1, SSD_INNER, SSD_STATE), lambda i, c: (i, 0, 0))],
        out_specs=[pl.BlockSpec((1, SSD_CHUNK, SSD_INNER), lambda i, c: (i, c, 0)),
                   pl.BlockSpec((1, SSD_INNER, SSD_STATE), lambda i, c: (i, 0, 0))],
        out_shape=[jax.ShapeDtypeStruct((b, lp, SSD_INNER), F32),
                   jax.ShapeDtypeStruct((b, SSD_INNER, SSD_STATE), F32)],
        scratch_shapes=[pltpu.VMEM((SSD_INNER, SSD_STATE), F32)],
        compiler_params=_cparams(("parallel", "arbitrary")),
        name="ssd_chunked",
    )(xc, tail, zx, dtb_vec, alog_vec, d_exp, e_mat, h0)


S5_TILE = 256
S5_STEPS = S5_TILE // SUBLANES


def _cmul(ar, ai, br, bi):
    return ar * br - ai * bi, ar * bi + ai * br


def _s5_prep_kernel(are_ref, aim_ref, ldt_ref, bre_ref, bim_ref, psr_ref, psi_ref, pcr_ref, pci_ref,
                    bbr_ref, bbi_ref):
    lam_re = jnp.minimum(are_ref[...], -1e-4)
    lam_im = aim_ref[...]
    dt = jnp.exp(ldt_ref[...])
    mag = jnp.exp(lam_re * dt)
    ang = lam_im * dt
    lb_re, lb_im = mag * jnp.cos(ang), mag * jnp.sin(ang)
    nr, ni = lb_re - 1.0, lb_im
    den = lam_re * lam_re + lam_im * lam_im
    coef_re = (nr * lam_re + ni * lam_im) / den
    coef_im = (ni * lam_re - nr * lam_im) / den
    for k in range(S5_GROUP):
        br, bi = bre_ref[k], bim_ref[k]
        bbr_ref[k] = coef_re * br - coef_im * bi
        bbi_ref[k] = coef_re * bi + coef_im * br
    pr, pi = lb_re, lb_im
    for r in range(S5_STEPS):
        psr_ref[r], psi_ref[r] = pr, pi
        if r + 1 < S5_STEPS:
            pr, pi = _cmul(pr, pi, lb_re, lb_im)
    qr, qi = pr, pi
    for c in range(SUBLANES):
        pcr_ref[c], pci_ref[c] = qr, qi
        if c + 1 < SUBLANES:
            qr, qi = _cmul(qr, qi, pr, pi)


def _s5_prep(a_re, a_im, log_dt, b_re_t, b_im_t):
    g, n = a_re.shape
    sd = jax.ShapeDtypeStruct
    return pl.pallas_call(
        _s5_prep_kernel,
        out_shape=[sd((S5_STEPS, g, n), F32), sd((S5_STEPS, g, n), F32),
                   sd((SUBLANES, g, n), F32), sd((SUBLANES, g, n), F32),
                   sd((S5_GROUP, g, n), F32), sd((S5_GROUP, g, n), F32)],
        name="s5_discretise",
    )(a_re, a_im, log_dt.reshape(g, 1), b_re_t, b_im_t)


def _s5_kernel(u_ref, perm_ref, permt_ref, bre_ref, bim_ref, cre_ref, cim_ref, psr_ref, psi_ref,
               pcr_ref, pci_ref, d_ref, s0r_ref, s0i_ref, g_ref, sr_ref, si_ref, xr_sc, xi_sc, *, tt, nh):
    l = u_ref.shape[1]
    w = xr_sc.shape[2]
    cw = u_ref.shape[2] // nh
    ts = tt // SUBLANES
    rows = lax.broadcasted_iota(jnp.int32, (SUBLANES, w), 0)
    bc = lambda v: jnp.broadcast_to(v, (SUBLANES, w))

    def blk(i):
        return slice(i * SUBLANES, (i + 1) * SUBLANES)

    def tile(t, carry):
        t0 = pl.multiple_of(t * tt, tt)
        us = [u_ref[0, pl.ds(t0, tt), h * cw:(h + 1) * cw] for h in range(nh)]
        for h in range(nh):
            up = us[h].astype(BF16)
            if ts > 1:
                up = _dot(perm_ref[...], up).astype(BF16)
            xr_sc[h] = _dot(up, bre_ref[h])
            xi_sc[h] = _dot(up, bim_ref[h])
        ends = []
        for h in range(nh):
            lam_r, lam_i = bc(psr_ref[h, 0:1, :]), bc(psi_ref[h, 0:1, :])

            def pass1(i, st, h=h, lam_r=lam_r, lam_i=lam_i):
                sr, si = _cmul(lam_r, lam_i, st[0], st[1])
                sr, si = sr + xr_sc[h, blk(i), :], si + xi_sc[h, blk(i), :]
                xr_sc[h, blk(i), :] = sr
                xi_sc[h, blk(i), :] = si
                return sr, si

            st = (jnp.zeros((SUBLANES, w), F32), jnp.zeros((SUBLANES, w), F32))
            for i in range(ts):
                st = pass1(i, st)
            ends.append(st)
        new_carry = []
        entries = []
        for h in range(nh):
            tr, ti = ends[h]
            cr, ci = carry[2 * h], carry[2 * h + 1]
            pcr, pci = pcr_ref[h], pci_ref[h]
            for kk in (1, 2, 4):
                keep = rows >= kk
                qr = jnp.where(keep, bc(pcr[kk - 1:kk, :]), 0.0)
                qi = jnp.where(keep, bc(pci[kk - 1:kk, :]), 0.0)
                dr, di = _cmul(qr, qi, pltpu.roll(tr, kk, axis=0), pltpu.roll(ti, kk, axis=0))
                tr, ti = tr + dr, ti + di
            dr, di = _cmul(pcr, pci, bc(cr), bc(ci))
            tr, ti = tr + dr, ti + di
            entries.append((jnp.where(rows == 0, bc(cr), pltpu.roll(tr, 1, axis=0)),
                            jnp.where(rows == 0, bc(ci), pltpu.roll(ti, 1, axis=0))))
            new_carry += [tr[SUBLANES - 1:SUBLANES, :], ti[SUBLANES - 1:SUBLANES, :]]
        for h in range(nh):
            er, ei = entries[h]

            def pass2(i, _, h=h, er=er, ei=ei):
                pr, pi = bc(psr_ref[h, i:i + 1, :]), bc(psi_ref[h, i:i + 1, :])
                dr, di = _cmul(pr, pi, er, ei)
                xr_sc[h, blk(i), :] = xr_sc[h, blk(i), :] + dr
                xi_sc[h, blk(i), :] = xi_sc[h, blk(i), :] + di
                return 0

            for i in range(ts):
                pass2(i, 0)
        for h in range(nh):
            y = _dot(xr_sc[h].astype(BF16), cre_ref[h]) - _dot(xi_sc[h].astype(BF16), cim_ref[h])
            if ts > 1:
                hi = y.astype(BF16)
                r1 = y - hi.astype(F32)
                mid = r1.astype(BF16)
                lo = (r1 - mid.astype(F32)).astype(BF16)
                pt = permt_ref[...]
                y = _dot(pt, hi) + _dot(pt, mid) + _dot(pt, lo)
            y = y + d_ref[h] * us[h]
            g_ref[0, pl.ds(t0, tt), h * cw:(h + 1) * cw] = _gelu_tanh(y).astype(g_ref.dtype)
        return tuple(new_carry)

    init = []
    for h in range(nh):
        init += [s0r_ref[0, h], s0i_ref[0, h]]
    fin = lax.fori_loop(0, l // tt, tile, tuple(init))
    for h in range(nh):
        sr_ref[0, h] = fin[2 * h]
        si_ref[0, h] = fin[2 * h + 1]


def _s5(u, p, s0_re, s0_im, nh=2):
    b, l, d = u.shape
    ngb = p["bbd_re"].shape[0]
    cw = d // ngb
    sw = p["bbd_re"].shape[2]
    tt = min(S5_TILE, l)
    ts = tt // SUBLANES
    assert l % tt == 0 and ts in (1, S5_STEPS) and ngb % nh == 0
    steps_re, steps_im = p["ps_re"][:, :ts], p["ps_im"][:, :ts]
    chunk_re, chunk_im = (p["pc_re"], p["pc_im"]) if ts == S5_STEPS else (p["ps_re"][:, :SUBLANES],
                                                                          p["ps_im"][:, :SUBLANES])
    r = jnp.arange(tt)
    perm = (r[None, :] == ((r % SUBLANES) * ts + r // SUBLANES)[:, None]).astype(BF16)
    state_spec = pl.BlockSpec((1, nh, 1, sw), lambda j, i: (i, j, 0, 0))
    whole = lambda a: pl.BlockSpec(a.shape, lambda j, i: (0,) * a.ndim)
    per_gb = lambda a: pl.BlockSpec((nh,) + a.shape[1:], lambda j, i: (j, 0, 0))
    args = [perm, perm.T, p["bbd_re"], p["bbd_im"], p["cbd_re"], p["cbd_im"], steps_re, steps_im,
            chunk_re, chunk_im, p["d_vec"]]
    return pl.pallas_call(
        functools.partial(_s5_kernel, tt=tt, nh=nh),
        grid=(ngb // nh, b),
        in_specs=[pl.BlockSpec((1, l, nh * cw), lambda j, i: (i, 0, j)), whole(perm), whole(perm)]
                 + [per_gb(a) for a in args[2:]] + [state_spec, state_spec],
        out_specs=[pl.BlockSpec((1, l, nh * cw), lambda j, i: (i, 0, j)), state_spec, state_spec],
        out_shape=[jax.ShapeDtypeStruct((b, l, d), BF16),
                   jax.ShapeDtypeStruct((b, ngb, 1, sw), F32),
                   jax.ShapeDtypeStruct((b, ngb, 1, sw), F32)],
        scratch_shapes=[pltpu.VMEM((nh, tt, sw), F32), pltpu.VMEM((nh, tt, sw), F32)],
        compiler_params=_cparams(("parallel", "parallel")),
        name="s5_scan",
    )(u, *args, s0_re, s0_im)


def _cross_kernel(x_ref, g_ref, wq_ref, qg_ref, k_ref, v_ref, wo_ref, o_ref, att_sc):
    x = x_ref[0]
    xn = _rms(x, g_ref[...]).astype(BF16)
    q = _dot(xn, wq_ref[...].astype(BF16))
    for h in range(MEM_HEADS):
        sl = slice(h * HEAD_DIM, (h + 1) * HEAD_DIM)
        qh = _rms(q[:, sl], qg_ref[...]).astype(BF16)
        s = _dot_nt(qh, k_ref[:, sl].astype(BF16)) * (HEAD_DIM ** -0.5)
        p = jnp.exp(s - jnp.max(s, axis=-1, keepdims=True))
        p = p / jnp.sum(p, axis=-1, keepdims=True)
        att_sc[:, sl] = _dot(p.astype(BF16), v_ref[:, sl].astype(BF16)).astype(BF16)
    o_ref[0] = x + _dot(att_sc[...], wo_ref[...].astype(BF16))


def _cross_attn(x, gain, w_q, q_gain, mem_k, mem_v, w_o, layer, tq=512):
    b, l, d = x.shape
    mt, mw = mem_k.shape[2], mem_k.shape[3]
    tq = min(tq, l)
    return pl.pallas_call(
        _cross_kernel,
        grid=(b, l // tq),
        in_specs=[pl.BlockSpec((1, tq, d), lambda i, j: (i, j, 0)),
                  pl.BlockSpec((1, d), lambda i, j: (0, 0)),
                  pl.BlockSpec((None, d, mw), lambda i, j: (layer, 0, 0)),
                  pl.BlockSpec((1, HEAD_DIM), lambda i, j: (0, 0)),
                  pl.BlockSpec((None, None, mt, mw), lambda i, j: (layer, i, 0, 0)),
                  pl.BlockSpec((None, None, mt, mw), lambda i, j: (layer, i, 0, 0)),
                  pl.BlockSpec((None, mw, d), lambda i, j: (layer, 0, 0))],
        out_specs=pl.BlockSpec((1, tq, d), lambda i, j: (i, j, 0)),
        out_shape=jax.ShapeDtypeStruct((b, l, d), F32),
        scratch_shapes=[pltpu.VMEM((tq, mw), BF16)],
        compiler_params=_cparams(("parallel", "parallel")),
        name="cross_attn",
    )(x, gain.reshape(1, d), w_q, q_gain.reshape(1, HEAD_DIM), mem_k, mem_v, w_o)


def _lane_vec(vals, lane0):
    v = jnp.zeros((1, LANES), F32)
    return v.at[0, lane0:lane0 + vals.shape[0]].set(vals.astype(F32))


def _head_expand():
    rows = jnp.arange(LANES)[None, :, None]
    cols = jnp.arange(SSD_INNER // SSD_GROUPS)[None, None, :]
    g = jnp.arange(SSD_GROUPS)[:, None, None]
    return (rows == DT_LANE0 + g * (SSD_HEADS // SSD_GROUPS) + cols // SSD_HD).astype(BF16)


def _block_diag(m, reps):
    nb, rows, c = m.shape
    t = jnp.tile(m, (1, 1, reps))
    rb = jnp.arange(rows)[:, None] // (rows // reps)
    cb = jnp.arange(reps * c)[None, :] // c
    return jnp.where((rb == cb)[None], t, 0.0).astype(BF16)


def _pad_rows(x, rows):
    return jnp.pad(x, ((0, 0), (0, rows - x.shape[1]), (0, 0)))


def _trunk(x, W, mem_k, mem_v, conv0, ssm0, s5_re0, s5_im0, fox_cache):
    b, l, d = x.shape
    t = b * l
    depth = W["norm_mix"].shape[0]
    lp = max(l, SSD_CHUNK)
    e_mat = _head_expand()
    fl, hs, bufs, srs, sis = [], [], [], [], []
    kv_stacks = None
    x2 = x.reshape(t, d)
    for i in range(depth):
        j = i // 2
        if i % 2 == 0:
            n_even = W["w_tail"].shape[0]
            q_bf, kf_all, k_bf, vf_all, v_bf = _inproj_qkv(x2, W["norm_mix"][i], W["w_in_even_t"], j, n_even,
                                                           W["fox_q_norm"][j], W["fox_k_norm"][j], kv_stacks)
            kv_stacks = (kf_all, vf_all)
            zx, tail = _inproj_zx(x2, W["norm_mix"][i], W["w_zx"], W["w_tail"], j)
            zx3 = zx.reshape(b, l, -1)
            tail3 = tail.reshape(b, l, LANES)
            tail_p = _pad_rows(tail3, lp) if lp != l else tail3
            lf, negc, c_tm = _forget(tail_p, _lane_vec(W["fox_b_forget"][j], 0))
            lf = lf[:, :l]
            if fox_cache is None:
                o_fox = _fox_prompt(q_bf.reshape(b, l, FOX_W), k_bf.reshape(b, l, FOX_W),
                                    v_bf.reshape(b, l, FOX_W), negc).reshape(t, FOX_W)
            else:
                k_pool, v_pool, loc, tot, page_table = fox_cache
                rows = FOX_HEADS * l
                q_ht = jnp.transpose(q_bf.reshape(b, l, FOX_HEADS, HEAD_DIM), (0, 2, 1, 3)).reshape(b, rows, HEAD_DIM)
                o_ht = _fox_decode(q_ht, _pad_rows(kf_all[j].reshape(b, rows, HEAD_DIM), LANES),
                                   _pad_rows(vf_all[j].reshape(b, rows, HEAD_DIM), LANES),
                                   _pad_rows(c_tm[:, :l].reshape(b, rows, 1), LANES).reshape(b, 1, LANES),
                                   k_pool, v_pool, loc, tot, page_table, j, l)
                o_fox = jnp.transpose(o_ht.reshape(b, FOX_HEADS, l, HEAD_DIM), (0, 2, 1, 3)).reshape(t, FOX_W)
            xc, new_buf = _conv(zx3, SSD_INNER, conv0[j], W["ssd_conv_w"][j], W["ssd_conv_b"][j])
            if lp != l:
                xc_p, z_p = _pad_rows(xc, lp), _pad_rows(zx3[:, :, :SSD_INNER], lp)
            else:
                xc_p, z_p = xc, zx3
            yg, h_last = _ssd(xc_p, tail_p, z_p, _lane_vec(W["ssd_dt_bias"][j], DT_LANE0),
                              _lane_vec(W["ssd_A_log"][j], DT_LANE0),
                              jnp.repeat(W["ssd_D"][j].astype(F32), SSD_HD).reshape(1, SSD_INNER),
                              e_mat, ssm0[j].reshape(b, SSD_INNER, SSD_STATE), l)
            yg = yg[:, :l].reshape(t, SSD_INNER)
            x2 = _outproj_even(o_fox, yg, W["ssd_norm"][j], W["w_out_even"], j, x2)
            fl.append(lf)
            hs.append(h_last.reshape(b, SSD_HEADS, SSD_HD, SSD_STATE))
            bufs.append(new_buf)
        else:
            (u,) = _dense(x2, W["w_in_odd"], layer=j, n_cols=d, tn=512, pro="norm", gain=W["norm_mix"][i],
                          name="inproj_odd", tm_max=TM_WIDE, x_single=True)
            p = W["s5_packed"][j]
            ngb = p["bbd_re"].shape[0]
            g_bf, s_re, s_im = _s5(u.reshape(b, l, d), p,
                                   s5_re0[j].reshape(b, ngb, 1, -1), s5_im0[j].reshape(b, ngb, 1, -1))
            (x2,) = _dense(g_bf.reshape(t, d), W["s5_w_glu"], layer=j, n_cols=d, tn=512, col0=0, col0_2=d,
                           epi="glu_res", res=x2, name="s5_glu_out")
            srs.append(s_re.reshape(b, -1, S5_STATE))
            sis.append(s_im.reshape(b, -1, S5_STATE))
        x3 = _cross_attn(x2.reshape(b, l, d), W["norm_cross"][i], W["w_mq"], W["mem_q_norm"][i],
                         mem_k, mem_v, W["w_mo"], i)
        x2 = x3.reshape(t, d)
        hid = W["w_ffn_down"].shape[1]
        (hmid,) = _dense(x2, W["w_ffn_up"], layer=i, n_cols=hid, tn=512, col0=0, col0_2=hid, pro="norm",
                         gain=W["norm_ffn"][i], epi="swiglu", out_dtypes=(BF16,), name="ffn_up")
        (x2,) = _dense(hmid, W["w_ffn_down"], layer=i, n_cols=d, tn=256, epi="residual", res=x2, name="ffn_down")
    fk, fv = (a.reshape(a.shape[0], b, l, FOX_HEADS, HEAD_DIM) for a in kv_stacks)
    return (x2.reshape(b, l, d), fk, fv, jnp.stack(fl), jnp.stack(hs), jnp.stack(bufs),
            jnp.stack(srs), jnp.stack(sis))


def _pack_s5(a_re, a_im, b_re, b_im, c_re, c_im, d_skip, log_dt):
    g, n, k = b_re.shape
    ngb = g // S5_GB
    ps_re, ps_im, pc_re, pc_im, bb_re, bb_im = _s5_prep(
        a_re.astype(F32), a_im.astype(F32), log_dt.astype(F32),
        jnp.transpose(b_re, (2, 0, 1)).astype(F32), jnp.transpose(b_im, (2, 0, 1)).astype(F32))

    def bmat(bb):
        return _block_diag(jnp.transpose(bb, (1, 0, 2)).reshape(ngb, S5_GB * k, n), S5_GB)

    def cmat(cm):
        return _block_diag(jnp.transpose(cm.astype(F32), (0, 2, 1)).reshape(ngb, S5_GB * n, k), S5_GB)

    def rows(pw):
        return jnp.transpose(pw.reshape(pw.shape[0], ngb, S5_GB * n), (1, 0, 2))

    return dict(bbd_re=bmat(bb_re), bbd_im=bmat(bb_im), cbd_re=cmat(c_re), cbd_im=cmat(c_im),
                ps_re=rows(ps_re), ps_im=rows(ps_im), pc_re=rows(pc_re), pc_im=rows(pc_im),
                d_vec=d_skip.astype(F32).reshape(ngb, 1, S5_GB * k))


def kernel(x_prompt, x_sample, mem_prompt, cache_fox_k, cache_fox_v, cache_fox_logf, cache_mem_k, cache_mem_v,
           state_ssd, state_conv, state_s5_re, state_s5_im, page_table,
           norm_mix, norm_cross, norm_mem, norm_ffn,
           w_in_even, fox_b_forget, fox_q_norm, fox_k_norm, ssd_conv_w, ssd_conv_b, ssd_dt_bias, ssd_A_log,
           ssd_D, ssd_norm, w_out_even,
           w_in_odd, s5_A_re, s5_A_im, s5_B_re, s5_B_im, s5_C_re, s5_C_im, s5_D, s5_log_dt, s5_w_glu,
           w_mq, w_mkv, mem_q_norm, mem_k_norm, w_mo, w_ffn_up, w_ffn_down):
    depth = norm_mix.shape[0]
    n_even, n_odd = w_in_even.shape[0], w_in_odd.shape[0]
    b, l, d = x_prompt.shape
    z0 = 3 * FOX_W + FOX_HEADS
    dt0 = z0 + SSD_INNER + CONV_DIM
    w_t = jnp.swapaxes(w_in_even, 1, 2)
    W = {
        "norm_mix": norm_mix, "norm_cross": norm_cross, "norm_ffn": norm_ffn,
        "w_in_even_t": w_t,
        "w_zx": w_t[:, z0:dt0],
        "w_tail": jnp.concatenate([w_t[:, 3 * FOX_W:z0], w_t[:, dt0:dt0 + SSD_HEADS],
                                   jnp.zeros((n_even, LANES - FOX_HEADS - SSD_HEADS, d), w_t.dtype)], axis=1),
        "fox_b_forget": fox_b_forget, "fox_q_norm": fox_q_norm, "fox_k_norm": fox_k_norm,
        "ssd_conv_w": ssd_conv_w, "ssd_conv_b": ssd_conv_b, "ssd_dt_bias": ssd_dt_bias, "ssd_A_log": ssd_A_log,
        "ssd_D": ssd_D, "ssd_norm": ssd_norm, "w_out_even": w_out_even,
        "w_in_odd": w_in_odd, "s5_w_glu": s5_w_glu,
        "s5_packed": [_pack_s5(s5_A_re[j], s5_A_im[j], s5_B_re[j], s5_B_im[j], s5_C_re[j], s5_C_im[j],
                               s5_D[j], s5_log_dt[j]) for j in range(n_odd)],
        "w_mq": w_mq, "mem_q_norm": mem_q_norm, "w_mo": w_mo, "w_ffn_up": w_ffn_up, "w_ffn_down": w_ffn_down,
    }
    mt = mem_prompt.shape[1]
    mw = w_mkv.shape[2] // 2
    mem2 = mem_prompt.reshape(b * mt, d)
    mk, mv = [], []
    for i in range(depth):
        (k_i,) = _dense(mem2, w_mkv, layer=i, n_cols=mw, tn=mw, col0=0, pro="norm", gain=norm_mem[i],
                        epi="headnorm", head_gain=mem_k_norm[i], name="mem_k_proj")
        (v_i,) = _dense(mem2, w_mkv, layer=i, n_cols=mw, tn=mw, col0=mw, pro="norm", gain=norm_mem[i],
                        name="mem_v_proj")
        mk.append(k_i.reshape(b, mt, mw))
        mv.append(v_i.reshape(b, mt, mw))
    mem_k_p = jnp.stack(mk)
    mem_v_p = jnp.stack(mv)
    n_grp = s5_A_re.shape[1]
    (y_prompt, fox_k_p, fox_v_p, fox_logf_p, ssd_p, conv_p, s5_re_p, s5_im_p) = _trunk(
        x_prompt, W, mem_k_p, mem_v_p,
        jnp.zeros((n_even, b, SSD_CONV - 1, CONV_DIM), F32),
        jnp.zeros((n_even, b, SSD_HEADS, SSD_HD, SSD_STATE), F32),
        jnp.zeros((n_odd, b, n_grp, S5_STATE), F32),
        jnp.zeros((n_odd, b, n_grp, S5_STATE), F32),
        None)
    db = x_sample.shape[0]
    n_pool, page = cache_fox_k.shape[1], cache_fox_k.shape[2]
    assert page == LANES
    width = page * FOX_HEADS
    lf_t = jnp.transpose(cache_fox_logf.astype(F32), (0, 2, 1, 3)).reshape(n_even, page, n_pool * FOX_HEADS)
    incl = _pool_cumsum(lf_t).reshape(n_even, page, n_pool, FOX_HEADS)
    loc = jnp.transpose(incl, (0, 2, 1, 3)).reshape(n_even, n_pool, 1, width)
    tot = jnp.tile(incl[:, page - 1], (1, 1, page)).reshape(n_even, n_pool, 1, width)
    fox_cache = (cache_fox_k.reshape(n_even, n_pool, width, HEAD_DIM),
                 cache_fox_v.reshape(n_even, n_pool, width, HEAD_DIM), loc, tot, page_table)
    (y_sample, fox_k_s, fox_v_s, fox_logf_s, ssd_s, conv_s, s5_re_s, s5_im_s) = _trunk(
        x_sample, W, cache_mem_k.reshape(depth, db, mt, mw), cache_mem_v.reshape(depth, db, mt, mw),
        state_conv, state_ssd, state_s5_re, state_s5_im, fox_cache)
    hd = mw // MEM_HEADS
    return (y_prompt, y_sample,
            fox_k_p, fox_v_p, fox_logf_p,
            mem_k_p.reshape(depth, b, mt, MEM_HEADS, hd), mem_v_p.reshape(depth, b, mt, MEM_HEADS, hd),
            ssd_p, conv_p, s5_re_p, s5_im_p,
            fox_k_s, fox_v_s, fox_logf_s, ssd_s, conv_s, s5_re_s, s5_im_s)
```

```python
import functools
import math

import jax
import jax.numpy as jnp
from jax import lax
from jax.experimental import pallas as pl
from jax.experimental.pallas import tpu as pltpu

F32 = jnp.float32
BF16 = jnp.bfloat16
EPS = 1e-6
LANES = 128
SUBLANES = 8
VMEM_LIMIT_BYTES = 56 * 1024 * 1024

FOX_HEADS = 8
HEAD_DIM = 128
FOX_W = FOX_HEADS * HEAD_DIM
SSD_HEADS = 32
SSD_HD = 64
SSD_GROUPS = 4
SSD_STATE = 128
SSD_CHUNK = 128
SSD_INNER = SSD_HEADS * SSD_HD
SSD_CONV = 4
CONV_DIM = SSD_INNER + 2 * SSD_GROUPS * SSD_STATE
DT_LANE0 = FOX_HEADS
S5_GROUP = 16
S5_STATE = 64
S5_GB = 16
MEM_HEADS = 4
HIGHEST = lax.Precision.HIGHEST


def _cparams(sem):
    return pltpu.CompilerParams(dimension_semantics=sem, vmem_limit_bytes=VMEM_LIMIT_BYTES)


def _gelu_tanh(x):
    return 0.5 * x * (1.0 + jnp.tanh(math.sqrt(2.0 / math.pi) * (x + 0.044715 * x * x * x)))


def _softplus(x):
    return jnp.maximum(x, 0.0) + jnp.log1p(jnp.exp(-jnp.abs(x)))


def _silu(x):
    return x * jax.nn.sigmoid(x)


def _rms(x, gain):
    return x * lax.rsqrt(jnp.mean(x * x, axis=-1, keepdims=True) + EPS) * gain


def _dot(a, b):
    return jnp.dot(a, b, preferred_element_type=F32)


def _dot_nt(a, b):
    return lax.dot_general(a, b, (((1,), (1,)), ((), ())), preferred_element_type=F32)


def _dot_exact(a, b):
    return jnp.dot(a, b, preferred_element_type=F32, precision=HIGHEST)


def _dense_kernel(*refs, pro, epi, n_out):
    it = iter(refs)
    x_ref = next(it)
    gain_ref = next(it) if pro == "norm" else None
    w_ref = next(it)
    w2_ref = next(it) if epi in ("swiglu", "glu_res") else None
    res_ref = next(it) if epi in ("residual", "glu_res") else None
    hg_ref = next(it) if epi == "headnorm" else None
    out_refs = [next(it) for _ in range(n_out)]
    xs_ref = next(it) if pro != "none" else None

    if pro != "none":
        @pl.when(pl.program_id(1) == 0)
        def _():
            xf = x_ref[...].astype(F32)
            if pro == "norm":
                xf = _rms(xf, gain_ref[...])
            elif pro == "gelu":
                xf = _gelu_tanh(xf)
            xs_ref[...] = xf.astype(BF16)
        lhs = xs_ref[...]
    else:
        lhs = x_ref[...]

    acc = _dot(lhs, w_ref[...].astype(BF16))
    if epi == "swiglu":
        acc = _silu(acc) * _dot(lhs, w2_ref[...].astype(BF16))
    elif epi == "glu_res":
        acc = res_ref[...] + acc * jax.nn.sigmoid(_dot(lhs, w2_ref[...].astype(BF16)))
    elif epi == "residual":
        acc = res_ref[...] + acc

    if epi == "headnorm":
        for c in range(acc.shape[1] // HEAD_DIM):
            sl = slice(c * HEAD_DIM, (c + 1) * HEAD_DIM)
            blk = _rms(acc[:, sl], hg_ref[...])
            for o in out_refs:
                o[:, sl] = blk.astype(o.dtype)
    else:
        for o in out_refs:
            o[...] = acc.astype(o.dtype)


def _wspec(w, layer, k, tn, blk0):
    if w.ndim == 2:
        return pl.BlockSpec((k, tn), lambda i, j: (0, blk0 + j))
    return pl.BlockSpec((None, k, tn), lambda i, j: (layer, 0, blk0 + j))


def _dense(x, w, *, n_cols, tn, name, layer=0, col0=0, col0_2=None, pro="none", gain=None, epi="plain",
           res=None, head_gain=None, out_dtypes=(F32,), tm_max=1024, x_single=False):
    m, k = x.shape
    tm = min(tm_max, m)
    assert m % tm == 0 and n_cols % tn == 0 and col0 % tn == 0
    b0 = col0 // tn
    xmode = dict(pipeline_mode=pl.Buffered(1)) if x_single else {}
    in_specs = [pl.BlockSpec((tm, k), lambda i, j: (i, 0), **xmode)]
    args = [x]
    if pro == "norm":
        in_specs.append(pl.BlockSpec((1, k), lambda i, j: (0, 0)))
        args.append(gain.reshape(1, k).astype(F32))
    in_specs.append(_wspec(w, layer, k, tn, b0))
    args.append(w)
    if epi in ("swiglu", "glu_res"):
        assert col0_2 % tn == 0
        in_specs.append(_wspec(w, layer, k, tn, col0_2 // tn))
        args.append(w)
    if epi in ("residual", "glu_res"):
        in_specs.append(pl.BlockSpec((tm, tn), lambda i, j: (i, j)))
        args.append(res)
    if epi == "headnorm":
        in_specs.append(pl.BlockSpec((1, HEAD_DIM), lambda i, j: (0, 0)))
        args.append(head_gain.reshape(1, HEAD_DIM).astype(F32))
    out_shape = [jax.ShapeDtypeStruct((m, n_cols), dt) for dt in out_dtypes]
    out_specs = [pl.BlockSpec((tm, tn), lambda i, j: (i, j)) for _ in out_dtypes]
    scratch = [pltpu.VMEM((tm, k), BF16)] if pro != "none" else []
    outs = pl.pallas_call(
        functools.partial(_dense_kernel, pro=pro, epi=epi, n_out=len(out_dtypes)),
        grid=(m // tm, n_cols // tn),
        in_specs=in_specs, out_specs=out_specs, out_shape=out_shape,
        scratch_shapes=scratch,
        compiler_params=_cparams(("parallel", "arbitrary")),
        name=name,
    )(*args)
    return outs


def _ffn_kernel(x_ref, g_ref, wg_ref, wu_ref, wd_ref, o_ref, xs_ref):
    @pl.when(pl.program_id(1) == 0)
    def _():
        x = x_ref[...]
        xs_ref[...] = _rms(x, g_ref[...]).astype(BF16)
        o_ref[...] = x
    xs = xs_ref[...]
    hid = _silu(_dot(xs, wg_ref[...].astype(BF16))) * _dot(xs, wu_ref[...].astype(BF16))
    o_ref[...] += _dot(hid.astype(BF16), wd_ref[...].astype(BF16))


def _ffn(x, gain, w_up, w_down, layer, th=256, tm_max=1024):
    m, d = x.shape
    hid = w_down.shape[1]
    tm = min(tm_max, m)
    nth = hid // th
    assert m % tm == 0 and hid % th == 0
    return pl.pallas_call(
        _ffn_kernel,
        grid=(m // tm, nth),
        in_specs=[pl.BlockSpec((tm, d), lambda i, j: (i, 0)),
                  pl.BlockSpec((1, d), lambda i, j: (0, 0)),
                  pl.BlockSpec((None, d, th), lambda i, j: (layer, 0, j)),
                  pl.BlockSpec((None, d, th), lambda i, j: (layer, 0, nth + j)),
                  pl.BlockSpec((None, th, d), lambda i, j: (layer, j, 0))],
        out_specs=pl.BlockSpec((tm, d), lambda i, j: (i, 0)),
        out_shape=jax.ShapeDtypeStruct((m, d), F32),
        scratch_shapes=[pltpu.VMEM((tm, d), BF16)],
        compiler_params=_cparams(("parallel", "arbitrary")),
        name="ffn",
    )(x, gain.reshape(1, d), w_up, w_up, w_down)


TM_WIDE = 2048


def _inproj_qkv_kernel(x_ref, g_ref, w_ref, qg_ref, kg_ref, *rest, nq):
    q_ref, kf_ref, kb_ref, vf_ref, vb_ref, xs_ref = rest[-6:]
    j = pl.program_id(1)

    @pl.when(j == 0)
    def _():
        xs_ref[...] = _rms(x_ref[...], g_ref[...]).astype(BF16)

    def headnorm(acc, gain, outs):
        for c in range(acc.shape[1] // HEAD_DIM):
            sl = slice(c * HEAD_DIM, (c + 1) * HEAD_DIM)
            blk = _rms(acc[:, sl], gain)
            for o in outs:
                o[:, sl] = blk.astype(o.dtype)

    @pl.when(j < nq)
    def _():
        headnorm(_dot_nt(xs_ref[...], w_ref[...].astype(BF16)), qg_ref[...], (q_ref,))

    @pl.when((j >= nq) & (j < 2 * nq))
    def _():
        headnorm(_dot_nt(xs_ref[...], w_ref[...].astype(BF16)), kg_ref[...], (kf_ref, kb_ref))

    @pl.when(j >= 2 * nq)
    def _():
        acc = _dot_nt(xs_ref[...], w_ref[...].astype(BF16))
        vf_ref[...] = acc
        vb_ref[...] = acc.astype(BF16)


def _inproj_qkv(x, gain, w_t, layer, n_layers, q_gain, k_gain, stacks=None, tn=256):
    m, k = x.shape
    tm = min(TM_WIDE, m)
    nq = FOX_W // tn
    tile = lambda lo: pl.BlockSpec((tm, tn), lambda i, j: (i, jnp.clip(j - lo, 0, nq - 1)))
    stile = lambda lo: pl.BlockSpec((None, tm, tn), lambda i, j: (layer, i, jnp.clip(j - lo, 0, nq - 1)))
    sd = jax.ShapeDtypeStruct
    in_specs = [pl.BlockSpec((tm, k), lambda i, j: (i, 0), pipeline_mode=pl.Buffered(1)),
                pl.BlockSpec((1, k), lambda i, j: (0, 0)),
                pl.BlockSpec((None, tn, k), lambda i, j: (layer, j, 0)),
                pl.BlockSpec((1, HEAD_DIM), lambda i, j: (0, 0)),
                pl.BlockSpec((1, HEAD_DIM), lambda i, j: (0, 0))]
    args = [x, gain.reshape(1, k), w_t, q_gain.reshape(1, HEAD_DIM), k_gain.reshape(1, HEAD_DIM)]
    aliases = {}
    if stacks is not None:
        in_specs += [pl.BlockSpec(memory_space=pl.ANY), pl.BlockSpec(memory_space=pl.ANY)]
        aliases = {len(args): 1, len(args) + 1: 3}
        args += list(stacks)
    return pl.pallas_call(
        functools.partial(_inproj_qkv_kernel, nq=nq),
        grid=(m // tm, 3 * nq),
        in_specs=in_specs,
        out_specs=[tile(0), stile(nq), tile(nq), stile(2 * nq), tile(2 * nq)],
        out_shape=[sd((m, FOX_W), BF16), sd((n_layers, m, FOX_W), F32), sd((m, FOX_W), BF16),
                   sd((n_layers, m, FOX_W), F32), sd((m, FOX_W), BF16)],
        scratch_shapes=[pltpu.VMEM((tm, k), BF16)],
        input_output_aliases=aliases,
        compiler_params=_cparams(("parallel", "arbitrary")),
        name="inproj_qkv",
    )(*args)


def _inproj_zx_kernel(x_ref, g_ref, wzx_ref, wt_ref, zx_ref, tail_ref, xs_ref, *, nzx):
    j = pl.program_id(1)

    @pl.when(j == 0)
    def _():
        xs_ref[...] = _rms(x_ref[...], g_ref[...]).astype(BF16)

    @pl.when(j < nzx)
    def _():
        zx_ref[...] = _dot_nt(xs_ref[...], wzx_ref[0].astype(BF16))

    @pl.when(j == nzx)
    def _():
        tail_ref[...] = _dot_nt(xs_ref[...], wt_ref[...].astype(BF16))


def _inproj_zx(x, gain, w_t, row0, n_zx_cols, w_tail, layer, tn=512):
    m, k = x.shape
    tm = min(TM_WIDE, m)
    nzx = n_zx_cols // tn
    w_zx = w_t
    return pl.pallas_call(
        functools.partial(_inproj_zx_kernel, nzx=nzx),
        grid=(m // tm, nzx + 1),
        in_specs=[pl.BlockSpec((tm, k), lambda i, j: (i, 0), pipeline_mode=pl.Buffered(1)),
                  pl.BlockSpec((1, k), lambda i, j: (0, 0)),
                  pl.BlockSpec((pl.Element(1), pl.Element(tn), pl.Element(k)),
                               lambda i, j: (layer, pl.multiple_of(row0 + jnp.minimum(j, nzx - 1) * tn, SUBLANES), 0)),
                  pl.BlockSpec((None, LANES, k), lambda i, j: (layer, 0, 0), pipeline_mode=pl.Buffered(1))],
        out_specs=[pl.BlockSpec((tm, tn), lambda i, j: (i, jnp.minimum(j, nzx - 1))),
                   pl.BlockSpec((tm, LANES), lambda i, j: (i, 0))],
        out_shape=[jax.ShapeDtypeStruct((m, nzx * tn), F32), jax.ShapeDtypeStruct((m, LANES), F32)],
        scratch_shapes=[pltpu.VMEM((tm, k), BF16)],
        compiler_params=_cparams(("parallel", "arbitrary")),
        name="inproj_zx",
    )(x, gain.reshape(1, k), w_zx, w_tail)


def _outproj_kernel(o_ref, y_ref, g_ref, w_ref, res_ref, out_ref, yn_ref):
    k1 = o_ref.shape[1]

    @pl.when(pl.program_id(1) == 0)
    def _():
        yn_ref[...] = _rms(y_ref[...], g_ref[...]).astype(BF16)
    acc = _dot(o_ref[...], w_ref[:k1, :].astype(BF16)) + _dot(yn_ref[...], w_ref[k1:, :].astype(BF16))
    out_ref[...] = res_ref[...] + acc


def _outproj_even(o_fox, yg, gain, w_out, layer, res, tn=256, tm_max=TM_WIDE):
    m, d = res.shape
    tm = min(tm_max, m)
    k1, k2 = o_fox.shape[1], yg.shape[1]
    return pl.pallas_call(
        _outproj_kernel,
        grid=(m // tm, d // tn),
        in_specs=[
            pl.BlockSpec((tm, k1), lambda i, j: (i, 0)),
            pl.BlockSpec((tm, k2), lambda i, j: (i, 0), pipeline_mode=pl.Buffered(1)),
            pl.BlockSpec((1, k2), lambda i, j: (0, 0)),
            pl.BlockSpec((None, k1 + k2, tn), lambda i, j: (layer, 0, j)),
            pl.BlockSpec((tm, tn), lambda i, j: (i, j)),
        ],
        out_specs=pl.BlockSpec((tm, tn), lambda i, j: (i, j)),
        out_shape=jax.ShapeDtypeStruct((m, d), F32),
        scratch_shapes=[pltpu.VMEM((tm, k2), BF16)],
        compiler_params=_cparams(("parallel", "arbitrary")),
        name="outproj_even",
    )(o_fox, yg, gain.reshape(1, k2).astype(F32), w_out, res)


def _lane_cumsum(x):
    lane = lax.broadcasted_iota(jnp.int32, x.shape, 1)
    k = 1
    while k < LANES:
        x = x + jnp.where(lane >= k, pltpu.roll(x, k, axis=1), 0.0)
        k *= 2
    return x


def _forget_kernel(raw_ref, b_ref, lf_ref, negc_ref, ctm_ref):
    lp = raw_ref.shape[1]
    lf = -_softplus(-(raw_ref[0] + b_ref[...]))
    lf_ref[0] = lf[:, :FOX_HEADS]
    row_i = lax.broadcasted_iota(jnp.int32, (LANES, LANES), 0)
    col_i = lax.broadcasted_iota(jnp.int32, (LANES, LANES), 1)
    tri = (col_i <= row_i).astype(F32)
    ctm_ref[0] = _dot_exact(tri, lf[:LANES, :])[:, :FOX_HEADS]
    carry = jnp.zeros((FOX_HEADS, 1), F32)
    for c in range(lp // LANES):
        blk = lf[c * LANES:(c + 1) * LANES, :].T[:FOX_HEADS, :]
        cs = _lane_cumsum(blk) + carry
        negc_ref[0, :, c * LANES:(c + 1) * LANES] = -cs
        carry = cs[:, LANES - 1:LANES]


def _forget(raw, b_vec):
    b, lp, _ = raw.shape
    return pl.pallas_call(
        _forget_kernel,
        grid=(b,),
        in_specs=[pl.BlockSpec((1, lp, LANES), lambda i: (i, 0, 0)),
                  pl.BlockSpec((1, LANES), lambda i: (0, 0))],
        out_specs=[pl.BlockSpec((1, lp, FOX_HEADS), lambda i: (i, 0, 0)),
                   pl.BlockSpec((1, FOX_HEADS, lp), lambda i: (i, 0, 0)),
                   pl.BlockSpec((1, LANES, FOX_HEADS), lambda i: (i, 0, 0))],
        out_shape=[jax.ShapeDtypeStruct((b, lp, FOX_HEADS), F32),
                   jax.ShapeDtypeStruct((b, FOX_HEADS, lp), F32),
                   jax.ShapeDtypeStruct((b, LANES, FOX_HEADS), F32)],
        compiler_params=_cparams(("parallel",)),
        name="forget_gates",
    )(raw, b_vec)


def _fox_prompt_kernel(q_ref, k_ref, v_ref, nb_ref, o_ref, m_sc, acc_sc, mask_sc, *, tq, tk):
    qi = pl.program_id(1)
    kj = pl.program_id(2)

    @pl.when(kj == 0)
    def _():
        m_sc[...] = jnp.full(m_sc.shape, -jnp.inf, F32)
        acc_sc[...] = jnp.zeros(acc_sc.shape, F32)

    @pl.when(kj * tk <= qi * tq + (tq - 1))
    def _():
        row = qi * tq + lax.broadcasted_iota(jnp.int32, (tq, tk), 0)
        col = kj * tk + lax.broadcasted_iota(jnp.int32, (tq, tk), 1)
        mask_sc[...] = jnp.where(col <= row, 0.0, -jnp.inf)
        ones = jnp.ones((tk, HEAD_DIM), BF16)
        for h in range(FOX_HEADS):
            sl = slice(h * HEAD_DIM, (h + 1) * HEAD_DIM)
            s = _dot_nt(q_ref[0, :, sl], k_ref[0, :, sl]) * (HEAD_DIM ** -0.5)
            s = s + nb_ref[0, h:h + 1, :] + mask_sc[...]
            m_old = m_sc[h]
            m_new = jnp.maximum(m_old, jnp.max(s, axis=-1, keepdims=True))
            p = jnp.exp(s - m_new).astype(BF16)
            pv = _dot(p, jnp.concatenate([v_ref[0, :, sl], ones], axis=1))
            acc_sc[h] = jnp.exp(m_old - m_new) * acc_sc[h] + pv
            m_sc[h] = m_new

    @pl.when(kj == pl.num_programs(2) - 1)
    def _():
        for h in range(FOX_HEADS):
            acc = acc_sc[h]
            o_ref[0, :, h * HEAD_DIM:(h + 1) * HEAD_DIM] = (acc[:, :HEAD_DIM] / acc[:, HEAD_DIM:]).astype(o_ref.dtype)


def _fox_prompt(q, k, v, negc, tq=512, tk=512):
    b, l, _ = q.shape
    tq, tk = min(tq, l), min(tk, l)
    nq, nk = l // tq, l // tk

    def last_needed(qi, kj):
        return jnp.minimum(kj, (qi * tq + tq - 1) // tk)

    return pl.pallas_call(
        functools.partial(_fox_prompt_kernel, tq=tq, tk=tk),
        grid=(b, nq, nk),
        in_specs=[pl.BlockSpec((1, tq, FOX_W), lambda bi, qi, kj: (bi, qi, 0)),
                  pl.BlockSpec((1, tk, FOX_W), lambda bi, qi, kj: (bi, last_needed(qi, kj), 0)),
                  pl.BlockSpec((1, tk, FOX_W), lambda bi, qi, kj: (bi, last_needed(qi, kj), 0)),
                  pl.BlockSpec((1, FOX_HEADS, tk), lambda bi, qi, kj: (bi, 0, last_needed(qi, kj)))],
        out_specs=pl.BlockSpec((1, tq, FOX_W), lambda bi, qi, kj: (bi, qi, 0)),
        out_shape=jax.ShapeDtypeStruct((b, l, FOX_W), BF16),
        scratch_shapes=[pltpu.VMEM((FOX_HEADS, tq, 1), F32),
                        pltpu.VMEM((FOX_HEADS, tq, 2 * HEAD_DIM), F32),
                        pltpu.VMEM((tq, tk), F32)],
        compiler_params=_cparams(("parallel", "parallel", "arbitrary")),
        name="fox_prompt",
    )(q, k, v, negc)


def _pool_cumsum_kernel(x_ref, o_ref):
    n = x_ref.shape[0]
    row_i = lax.broadcasted_iota(jnp.int32, (n, n), 0)
    col_i = lax.broadcasted_iota(jnp.int32, (n, n), 1)
    o_ref[...] = _dot_exact((col_i <= row_i).astype(F32), x_ref[...])


def _pool_cumsum(x, tc=2048):
    nl, page, cols = x.shape
    tc = tc if cols % tc == 0 else cols
    return pl.pallas_call(
        _pool_cumsum_kernel,
        grid=(nl, cols // tc),
        in_specs=[pl.BlockSpec((None, page, tc), lambda i, j: (i, 0, j))],
        out_specs=pl.BlockSpec((None, page, tc), lambda i, j: (i, 0, j)),
        out_shape=jax.ShapeDtypeStruct((nl, page, cols), F32),
        compiler_params=_cparams(("parallel", "parallel")),
        name="pool_logf_cumsum",
    )(x)


def _fox_decode_kernel(pt_ref, q_ref, *refs, gp, t_new):
    kv_refs = refs[:4 * gp]
    kn_ref, vn_ref, cn_ref, o_ref, m_sc, l_sc, acc_sc, carry_sc = refs[4 * gp:]
    p = pl.program_id(1)
    n_past = pl.num_programs(1) - 1
    rows = q_ref.shape[1]
    scale = HEAD_DIM ** -0.5

    @pl.when(p == 0)
    def _():
        m_sc[...] = jnp.full(m_sc.shape, -jnp.inf, F32)
        l_sc[...] = jnp.zeros(l_sc.shape, F32)
        acc_sc[...] = jnp.zeros(acc_sc.shape, F32)
        carry_sc[...] = jnp.zeros(carry_sc.shape, F32)

    def update(s, v, state):
        m_old, l_old, acc = state
        m_new = jnp.maximum(m_old, jnp.max(s, axis=-1, keepdims=True))
        alpha = jnp.exp(m_old - m_new)
        pr = jnp.exp(s - m_new)
        l_new = alpha * l_old + jnp.sum(pr, axis=-1, keepdims=True)
        acc = alpha * acc + _dot(pr.astype(BF16), v.astype(BF16))
        return m_new, l_new, acc

    @pl.when(p < n_past)
    def _():
        q = q_ref[0]
        width = kv_refs[0].shape[0]
        row = lax.broadcasted_iota(jnp.int32, (rows, width), 0)
        col = lax.broadcasted_iota(jnp.int32, (rows, width), 1)
        own_head = (col % FOX_HEADS) == (row // t_new)
        carry = carry_sc[...]
        scores = []
        for g in range(gp):
            k_ref, _, loc_ref, tot_ref = kv_refs[4 * g:4 * g + 4]
            s = _dot_nt(q, k_ref[...].astype(BF16)) * scale - (carry + loc_ref[...])
            scores.append(jnp.where(own_head, s, -jnp.inf))
            carry = carry + tot_ref[...]
        carry_sc[...] = carry
        m_old = m_sc[...]
        m_new = m_old
        for s in scores:
            m_new = jnp.maximum(m_new, jnp.max(s, axis=-1, keepdims=True))
        alpha = jnp.exp(m_old - m_new)
        l_new = alpha * l_sc[...]
        acc = alpha * acc_sc[...]
        for g, s in enumerate(scores):
            pr = jnp.exp(s - m_new)
            l_new = l_new + jnp.sum(pr, axis=-1, keepdims=True)
            acc = acc + _dot(pr.astype(BF16), kv_refs[4 * g + 1][...].astype(BF16))
        m_sc[...], l_sc[...], acc_sc[...] = m_new, l_new, acc

    @pl.when(p == n_past)
    def _():
        width = kn_ref.shape[1]
        row = lax.broadcasted_iota(jnp.int32, (rows, width), 0)
        col = lax.broadcasted_iota(jnp.int32, (rows, width), 1)
        keep = ((col % FOX_HEADS) == (row // t_new)) & ((col // FOX_HEADS) <= (row % t_new))
        s = _dot_nt(q_ref[0], kn_ref[0].astype(BF16)) * scale - (carry_sc[:, :width] + cn_ref[0])
        m, l, acc = update(jnp.where(keep, s, -jnp.inf), vn_ref[0], (m_sc[...], l_sc[...], acc_sc[...]))
        o_ref[0] = (acc / l).astype(o_ref.dtype)


def _fox_decode(q, k_new, v_new, c_new, k_pool, v_pool, loc, tot, page_table, layer, t_new, gp=16):
    b, rows, _ = q.shape
    n_pages = page_table.shape[1]
    gp = max(g for g in range(1, gp + 1) if n_pages % g == 0)
    n_steps = n_pages // gp
    width = k_pool.shape[2]

    def page_map(g):
        return lambda bi, p, pt: (layer, pt[bi, jnp.minimum(p, n_steps - 1) * gp + g], 0, 0)

    in_specs = [pl.BlockSpec((1, rows, HEAD_DIM), lambda bi, p, pt: (bi, 0, 0))]
    args = [q]
    for g in range(gp):
        in_specs += [pl.BlockSpec((None, None, width, HEAD_DIM), page_map(g)),
                     pl.BlockSpec((None, None, width, HEAD_DIM), page_map(g)),
                     pl.BlockSpec((None, None, 1, width), page_map(g)),
                     pl.BlockSpec((None, None, 1, width), page_map(g))]
        args += [k_pool, v_pool, loc, tot]
    new_spec = pl.BlockSpec((1, k_new.shape[1], HEAD_DIM), lambda bi, p, pt: (bi, 0, 0))
    in_specs += [new_spec, new_spec, pl.BlockSpec((1, 1, c_new.shape[2]), lambda bi, p, pt: (bi, 0, 0))]
    args += [k_new, v_new, c_new]
    grid_spec = pltpu.PrefetchScalarGridSpec(
        num_scalar_prefetch=1,
        grid=(b, n_steps + 1),
        in_specs=in_specs,
        out_specs=pl.BlockSpec((1, rows, HEAD_DIM), lambda bi, p, pt: (bi, 0, 0)),
        scratch_shapes=[pltpu.VMEM((rows, 1), F32), pltpu.VMEM((rows, 1), F32),
                        pltpu.VMEM((rows, HEAD_DIM), F32), pltpu.VMEM((1, width), F32)],
    )
    return pl.pallas_call(
        functools.partial(_fox_decode_kernel, gp=gp, t_new=t_new), grid_spec=grid_spec,
        out_shape=jax.ShapeDtypeStruct((b, rows, HEAD_DIM), BF16),
        compiler_params=_cparams(("parallel", "arbitrary")),
        name="fox_decode",
    )(page_table, *args)


def _conv_kernel(x_ref, buf_ref, w_ref, b_ref, y_ref, nb_ref, full_sc):
    l = x_ref.shape[1]
    pad = SUBLANES
    full_sc[pad - (SSD_CONV - 1):pad, :] = buf_ref[0]
    full_sc[pad:pad + l, :] = x_ref[0]
    acc = b_ref[...] + full_sc[pad - 3:pad - 3 + l, :] * w_ref[0:1, :]
    for kk in range(1, SSD_CONV):
        acc = acc + full_sc[pad - 3 + kk:pad - 3 + kk + l, :] * w_ref[kk:kk + 1, :]
    y_ref[0] = _silu(acc)
    nb_ref[0] = full_sc[pad + l - (SSD_CONV - 1):pad + l, :]


def _conv(zx, col0, buf, w, bias, tc=512):
    b, l, _ = zx.shape
    c = w.shape[1]
    cb0 = col0 // tc
    return pl.pallas_call(
        _conv_kernel,
        grid=(b, c // tc),
        in_specs=[pl.BlockSpec((1, l, tc), lambda i, j: (i, 0, cb0 + j)),
                  pl.BlockSpec((1, SSD_CONV - 1, tc), lambda i, j: (i, 0, j)),
                  pl.BlockSpec((SSD_CONV, tc), lambda i, j: (0, j)),
                  pl.BlockSpec((1, tc), lambda i, j: (0, j))],
        out_specs=[pl.BlockSpec((1, l, tc), lambda i, j: (i, 0, j)),
                   pl.BlockSpec((1, SSD_CONV - 1, tc), lambda i, j: (i, 0, j))],
        out_shape=[jax.ShapeDtypeStruct((b, l, c), F32),
                   jax.ShapeDtypeStruct((b, SSD_CONV - 1, c), F32)],
        scratch_shapes=[pltpu.VMEM((l + SUBLANES, tc), F32)],
        compiler_params=_cparams(("parallel", "parallel")),
        name="ssd_conv",
    )(zx, buf, w, bias.reshape(1, c))


def _ssd_kernel(xc_ref, tail_ref, z_ref, dtb_ref, alog_ref, dexp_ref, e_ref, h0_ref,
                y_ref, hout_ref, h_sc, *, valid_len):
    c = pl.program_id(1)
    q = SSD_CHUNK
    gw = SSD_INNER // SSD_GROUPS
    hpg = SSD_HEADS // SSD_GROUPS

    @pl.when(c == 0)
    def _():
        h_sc[...] = h0_ref[0]

    lane = lax.broadcasted_iota(jnp.int32, (q, LANES), 1)
    row = lax.broadcasted_iota(jnp.int32, (q, LANES), 0)
    col_i = lax.broadcasted_iota(jnp.int32, (q, q), 1)
    row_i = lax.broadcasted_iota(jnp.int32, (q, q), 0)
    causal = col_i <= row_i
    tri = causal.astype(F32)

    dt = _softplus(tail_ref[0] + dtb_ref[...])
    live = (lane >= DT_LANE0) & (lane < DT_LANE0 + SSD_HEADS) & (c * q + row < valid_len)
    dt = jnp.where(live, dt, 0.0)
    a = dt * (-jnp.exp(alog_ref[...]))
    a_cum = _dot_exact(tri, a)
    a_cum_t = a_cum.T
    dt_t = dt.T
    a_last = a_cum[q - 1:q, :]
    fac = jnp.concatenate([dt * jnp.exp(a_last - a_cum), jnp.exp(a_cum)], axis=0)
    fac_hi = fac.astype(BF16)
    fac_lo = (fac - fac_hi.astype(F32)).astype(BF16)
    fac2 = jnp.concatenate([fac_hi, fac_lo], axis=0)

    for g in range(SSD_GROUPS):
        ex = _dot(fac2, e_ref[g])
        ex = ex[:2 * q] + ex[2 * q:]
        w1_e, ea_e = ex[:q], ex[q:]
        xs = xc_ref[0, :, g * gw:(g + 1) * gw]
        bb = xc_ref[0, :, SSD_INNER + g * SSD_STATE:SSD_INNER + (g + 1) * SSD_STATE].astype(BF16)
        cc = xc_ref[0, :, SSD_INNER + (SSD_GROUPS + g) * SSD_STATE:
                    SSD_INNER + (SSD_GROUPS + g + 1) * SSD_STATE].astype(BF16)
        cb = _dot_nt(cc, bb)
        hg = h_sc[g * gw:(g + 1) * gw, :]
        y = _dot_nt(cc, hg.astype(BF16)) * ea_e
        st = _dot((xs * w1_e).T.astype(BF16), bb)
        for hp in range(hpg // 2):
            pair = xs[:, hp * LANES:(hp + 1) * LANES]
            lane_p = lax.broadcasted_iota(jnp.int32, pair.shape, 1)
            yp = None
            for sub in range(2):
                h = 2 * hp + sub
                ln = DT_LANE0 + g * hpg + h
                seg = a_cum[:, ln:ln + 1] - a_cum_t[ln:ln + 1, :]
                dec = jnp.exp(jnp.where(causal, seg, -jnp.inf))
                mat = (cb * dec * dt_t[ln:ln + 1, :]).astype(BF16)
                in_head = (lane_p >= sub * SSD_HD) & (lane_p < (sub + 1) * SSD_HD)
                rhs = jnp.where(in_head, pair, 0.0).astype(BF16)
                part = _dot(mat, rhs)
                yp = part if yp is None else yp + part
                r0 = g * gw + h * SSD_HD
                h_sc[r0:r0 + SSD_HD, :] = (hg[h * SSD_HD:(h + 1) * SSD_HD, :] * jnp.exp(a_last[:, ln:ln + 1])
                                           + st[h * SSD_HD:(h + 1) * SSD_HD, :])
            cs = slice(g * gw + hp * LANES, g * gw + (hp + 1) * LANES)
            yt = yp + y[:, hp * LANES:(hp + 1) * LANES] + dexp_ref[:, cs] * xs[:, hp * LANES:(hp + 1) * LANES]
            y_ref[0, :, cs] = yt * _silu(z_ref[0, :, cs])

    @pl.when(c == pl.num_programs(1) - 1)
    def _():
        hout_ref[0] = h_sc[...]


def _ssd(xc, tail, zx, dtb_vec, alog_vec, d_exp, e_mat, h0, valid_len):
    b, lp, _ = xc.shape
    nc = lp // SSD_CHUNK
    return pl.pallas_call(
        functools.partial(_ssd_kernel, valid_len=valid_len),
        grid=(b, nc),
        in_specs=[pl.BlockSpec((1, SSD_CHUNK, CONV_DIM), lambda i, c: (i, c, 0)),
                  pl.BlockSpec((1, SSD_CHUNK, LANES), lambda i, c: (i, c, 0)),
                  pl.BlockSpec((1, SSD_CHUNK, SSD_INNER), lambda i, c: (i, c, 0)),
                  pl.BlockSpec((1, LANES), lambda i, c: (0, 0)),
                  pl.BlockSpec((1, LANES), lambda i, c: (0, 0)),
                  pl.BlockSpec((1, SSD_INNER), lambda i, c: (0, 0)),
                  pl.BlockSpec((SSD_GROUPS, LANES, SSD_INNER // SSD_GROUPS), lambda i, c: (0, 0, 0)),
                  pl.BlockSpec((1, SSD_INNER, SSD_STATE), lambda i, c: (i, 0, 0))],
        out_specs=[pl.BlockSpec((1, SSD_CHUNK, SSD_INNER), lambda i, c: (i, c, 0)),
                   pl.BlockSpec((1, SSD_INNER, SSD_STATE), lambda i, c: (i, 0, 0))],
        out_shape=[jax.ShapeDtypeStruct((b, lp, SSD_INNER), F32),
                   jax.ShapeDtypeStruct((b, SSD_INNER, SSD_STATE), F32)],
        scratch_shapes=[pltpu.VMEM((SSD_INNER, SSD_STATE), F32)],
        compiler_params=_cparams(("parallel", "arbitrary")),
        name="ssd_chunked",
    )(xc, tail, zx, dtb_vec, alog_vec, d_exp, e_mat, h0)


S5_TILE = 256
S5_STEPS = S5_TILE // SUBLANES


def _cmul(ar, ai, br, bi):
    return ar * br - ai * bi, ar * bi + ai * br


def _s5_prep_kernel(are_ref, aim_ref, ldt_ref, bre_ref, bim_ref, psr_ref, psi_ref, pcr_ref, pci_ref,
                    bbr_ref, bbi_ref):
    lam_re = jnp.minimum(are_ref[...], -1e-4)
    lam_im = aim_ref[...]
    dt = jnp.exp(ldt_ref[...])
    mag = jnp.exp(lam_re * dt)
    ang = lam_im * dt
    lb_re, lb_im = mag * jnp.cos(ang), mag * jnp.sin(ang)
    nr, ni = lb_re - 1.0, lb_im
    den = lam_re * lam_re + lam_im * lam_im
    coef_re = (nr * lam_re + ni * lam_im) / den
    coef_im = (ni * lam_re - nr * lam_im) / den
    for k in range(S5_GROUP):
        br, bi = bre_ref[k], bim_ref[k]
        bbr_ref[k] = coef_re * br - coef_im * bi
        bbi_ref[k] = coef_re * bi + coef_im * br
    pr, pi = lb_re, lb_im
    for r in range(S5_STEPS):
        psr_ref[r], psi_ref[r] = pr, pi
        if r + 1 < S5_STEPS:
            pr, pi = _cmul(pr, pi, lb_re, lb_im)
    qr, qi = pr, pi
    for c in range(SUBLANES):
        pcr_ref[c], pci_ref[c] = qr, qi
        if c + 1 < SUBLANES:
            qr, qi = _cmul(qr, qi, pr, pi)


def _s5_prep(a_re, a_im, log_dt, b_re_t, b_im_t):
    g, n = a_re.shape
    sd = jax.ShapeDtypeStruct
    return pl.pallas_call(
        _s5_prep_kernel,
        out_shape=[sd((S5_STEPS, g, n), F32), sd((S5_STEPS, g, n), F32),
                   sd((SUBLANES, g, n), F32), sd((SUBLANES, g, n), F32),
                   sd((S5_GROUP, g, n), F32), sd((S5_GROUP, g, n), F32)],
        name="s5_discretise",
    )(a_re, a_im, log_dt.reshape(g, 1), b_re_t, b_im_t)


def _s5_kernel(u_ref, perm_ref, permt_ref, bre_ref, bim_ref, cre_ref, cim_ref, psr_ref, psi_ref,
               pcr_ref, pci_ref, d_ref, s0r_ref, s0i_ref, g_ref, sr_ref, si_ref, xr_sc, xi_sc, *, tt, nh):
    l = u_ref.shape[1]
    w = xr_sc.shape[2]
    cw = u_ref.shape[2] // nh
    ts = tt // SUBLANES
    rows = lax.broadcasted_iota(jnp.int32, (SUBLANES, w), 0)
    bc = lambda v: jnp.broadcast_to(v, (SUBLANES, w))

    def blk(i):
        return slice(i * SUBLANES, (i + 1) * SUBLANES)

    def tile(t, carry):
        t0 = pl.multiple_of(t * tt, tt)
        us = [u_ref[0, pl.ds(t0, tt), h * cw:(h + 1) * cw] for h in range(nh)]
        for h in range(nh):
            up = us[h].astype(BF16)
            if ts > 1:
                up = _dot(perm_ref[...], up).astype(BF16)
            xr_sc[h] = _dot(up, bre_ref[h])
            xi_sc[h] = _dot(up, bim_ref[h])
        ends = []
        for h in range(nh):
            lam_r, lam_i = bc(psr_ref[h, 0:1, :]), bc(psi_ref[h, 0:1, :])

            def pass1(i, st, h=h, lam_r=lam_r, lam_i=lam_i):
                sr, si = _cmul(lam_r, lam_i, st[0], st[1])
                sr, si = sr + xr_sc[h, blk(i), :], si + xi_sc[h, blk(i), :]
                xr_sc[h, blk(i), :] = sr
                xi_sc[h, blk(i), :] = si
                return sr, si

            st = (jnp.zeros((SUBLANES, w), F32), jnp.zeros((SUBLANES, w), F32))
            for i in range(ts):
                st = pass1(i, st)
            ends.append(st)
        new_carry = []
        entries = []
        for h in range(nh):
            tr, ti = ends[h]
            cr, ci = carry[2 * h], carry[2 * h + 1]
            pcr, pci = pcr_ref[h], pci_ref[h]
            for kk in (1, 2, 4):
                keep = rows >= kk
                qr = jnp.where(keep, bc(pcr[kk - 1:kk, :]), 0.0)
                qi = jnp.where(keep, bc(pci[kk - 1:kk, :]), 0.0)
                dr, di = _cmul(qr, qi, pltpu.roll(tr, kk, axis=0), pltpu.roll(ti, kk, axis=0))
                tr, ti = tr + dr, ti + di
            dr, di = _cmul(pcr, pci, bc(cr), bc(ci))
            tr, ti = tr + dr, ti + di
            entries.append((jnp.where(rows == 0, bc(cr), pltpu.roll(tr, 1, axis=0)),
                            jnp.where(rows == 0, bc(ci), pltpu.roll(ti, 1, axis=0))))
            new_carry += [tr[SUBLANES - 1:SUBLANES, :], ti[SUBLANES - 1:SUBLANES, :]]
        for h in range(nh):
            er, ei = entries[h]

            def pass2(i, _, h=h, er=er, ei=ei):
                pr, pi = bc(psr_ref[h, i:i + 1, :]), bc(psi_ref[h, i:i + 1, :])
                dr, di = _cmul(pr, pi, er, ei)
                xr_sc[h, blk(i), :] = xr_sc[h, blk(i), :] + dr
                xi_sc[h, blk(i), :] = xi_sc[h, blk(i), :] + di
                return 0

            for i in range(ts):
                pass2(i, 0)
        for h in range(nh):
            y = _dot(xr_sc[h].astype(BF16), cre_ref[h]) - _dot(xi_sc[h].astype(BF16), cim_ref[h])
            if ts > 1:
                hi = y.astype(BF16)
                mid = (y - hi.astype(F32)).astype(BF16)
                pt = permt_ref[...]
                y = _dot(pt, hi) + _dot(pt, mid)
            y = y + d_ref[h] * us[h]
            g_ref[0, pl.ds(t0, tt), h * cw:(h + 1) * cw] = _gelu_tanh(y).astype(g_ref.dtype)
        return tuple(new_carry)

    init = []
    for h in range(nh):
        init += [s0r_ref[0, h], s0i_ref[0, h]]
    fin = lax.fori_loop(0, l // tt, tile, tuple(init))
    for h in range(nh):
        sr_ref[0, h] = fin[2 * h]
        si_ref[0, h] = fin[2 * h + 1]


def _s5(u, p, s0_re, s0_im, nh=2):
    b, l, d = u.shape
    ngb = p["bbd_re"].shape[0]
    cw = d // ngb
    sw = p["bbd_re"].shape[2]
    tt = min(S5_TILE, l)
    ts = tt // SUBLANES
    assert l % tt == 0 and ts in (1, S5_STEPS) and ngb % nh == 0
    steps_re, steps_im = p["ps_re"][:, :ts], p["ps_im"][:, :ts]
    chunk_re, chunk_im = (p["pc_re"], p["pc_im"]) if ts == S5_STEPS else (p["ps_re"][:, :SUBLANES],
                                                                          p["ps_im"][:, :SUBLANES])
    r = jnp.arange(tt)
    perm = (r[None, :] == ((r % SUBLANES) * ts + r // SUBLANES)[:, None]).astype(BF16)
    state_spec = pl.BlockSpec((1, nh, 1, sw), lambda j, i: (i, j, 0, 0))
    whole = lambda a: pl.BlockSpec(a.shape, lambda j, i: (0,) * a.ndim)
    per_gb = lambda a: pl.BlockSpec((nh,) + a.shape[1:], lambda j, i: (j, 0, 0))
    args = [perm, perm.T, p["bbd_re"], p["bbd_im"], p["cbd_re"], p["cbd_im"], steps_re, steps_im,
            chunk_re, chunk_im, p["d_vec"]]
    return pl.pallas_call(
        functools.partial(_s5_kernel, tt=tt, nh=nh),
        grid=(ngb // nh, b),
        in_specs=[pl.BlockSpec((1, l, nh * cw), lambda j, i: (i, 0, j)), whole(perm), whole(perm)]
                 + [per_gb(a) for a in args[2:]] + [state_spec, state_spec],
        out_specs=[pl.BlockSpec((1, l, nh * cw), lambda j, i: (i, 0, j)), state_spec, state_spec],
        out_shape=[jax.ShapeDtypeStruct((b, l, d), BF16),
                   jax.ShapeDtypeStruct((b, ngb, 1, sw), F32),
                   jax.ShapeDtypeStruct((b, ngb, 1, sw), F32)],
        scratch_shapes=[pltpu.VMEM((nh, tt, sw), F32), pltpu.VMEM((nh, tt, sw), F32)],
        compiler_params=_cparams(("parallel", "parallel")),
        name="s5_scan",
    )(u, *args, s0_re, s0_im)


def _cross_kernel(x_ref, g_ref, wq_ref, qg_ref, k_ref, v_ref, wo_ref, o_ref, att_sc):
    x = x_ref[0]
    xn = _rms(x, g_ref[...]).astype(BF16)
    q = _dot(xn, wq_ref[...].astype(BF16))
    for h in range(MEM_HEADS):
        sl = slice(h * HEAD_DIM, (h + 1) * HEAD_DIM)
        qh = _rms(q[:, sl], qg_ref[...]).astype(BF16)
        s = _dot_nt(qh, k_ref[:, sl].astype(BF16)) * (HEAD_DIM ** -0.5)
        p = jnp.exp(s - jnp.max(s, axis=-1, keepdims=True))
        p = p / jnp.sum(p, axis=-1, keepdims=True)
        att_sc[:, sl] = _dot(p.astype(BF16), v_ref[:, sl].astype(BF16)).astype(BF16)
    o_ref[0] = x + _dot(att_sc[...], wo_ref[...].astype(BF16))


def _cross_attn(x, gain, w_q, q_gain, mem_k, mem_v, w_o, layer, tq=512):
    b, l, d = x.shape
    mt, mw = mem_k.shape[2], mem_k.shape[3]
    tq = min(tq, l)
    return pl.pallas_call(
        _cross_kernel,
        grid=(b, l // tq),
        in_specs=[pl.BlockSpec((1, tq, d), lambda i, j: (i, j, 0)),
                  pl.BlockSpec((1, d), lambda i, j: (0, 0)),
                  pl.BlockSpec((None, d, mw), lambda i, j: (layer, 0, 0)),
                  pl.BlockSpec((1, HEAD_DIM), lambda i, j: (0, 0)),
                  pl.BlockSpec((None, None, mt, mw), lambda i, j: (layer, i, 0, 0)),
                  pl.BlockSpec((None, None, mt, mw), lambda i, j: (layer, i, 0, 0)),
                  pl.BlockSpec((None, mw, d), lambda i, j: (layer, 0, 0))],
        out_specs=pl.BlockSpec((1, tq, d), lambda i, j: (i, j, 0)),
        out_shape=jax.ShapeDtypeStruct((b, l, d), F32),
        scratch_shapes=[pltpu.VMEM((tq, mw), BF16)],
        compiler_params=_cparams(("parallel", "parallel")),
        name="cross_attn",
    )(x, gain.reshape(1, d), w_q, q_gain.reshape(1, HEAD_DIM), mem_k, mem_v, w_o)


def _lane_vec(vals, lane0):
    v = jnp.zeros((1, LANES), F32)
    return v.at[0, lane0:lane0 + vals.shape[0]].set(vals.astype(F32))


def _head_expand():
    rows = jnp.arange(LANES)[None, :, None]
    cols = jnp.arange(SSD_INNER // SSD_GROUPS)[None, None, :]
    g = jnp.arange(SSD_GROUPS)[:, None, None]
    return (rows == DT_LANE0 + g * (SSD_HEADS // SSD_GROUPS) + cols // SSD_HD).astype(BF16)


def _block_diag(m, reps):
    nb, rows, c = m.shape
    t = jnp.tile(m, (1, 1, reps))
    rb = jnp.arange(rows)[:, None] // (rows // reps)
    cb = jnp.arange(reps * c)[None, :] // c
    return jnp.where((rb == cb)[None], t, 0.0).astype(BF16)


def _pad_rows(x, rows):
    return jnp.pad(x, ((0, 0), (0, rows - x.shape[1]), (0, 0)))


def _trunk(x, W, mem_k, mem_v, conv0, ssm0, s5_re0, s5_im0, fox_cache):
    b, l, d = x.shape
    t = b * l
    depth = W["norm_mix"].shape[0]
    lp = max(l, SSD_CHUNK)
    e_mat = _head_expand()
    fl, hs, bufs, srs, sis = [], [], [], [], []
    kv_stacks = None
    x2 = x.reshape(t, d)
    for i in range(depth):
        j = i // 2
        if i % 2 == 0:
            n_even = W["w_tail"].shape[0]
            q_bf, kf_all, k_bf, vf_all, v_bf = _inproj_qkv(x2, W["norm_mix"][i], W["w_in_even_t"], j, n_even,
                                                           W["fox_q_norm"][j], W["fox_k_norm"][j], kv_stacks)
            kv_stacks = (kf_all, vf_all)
            zx, tail = _inproj_zx(x2, W["norm_mix"][i], W["w_in_even_t"], 3 * FOX_W + FOX_HEADS,
                                  SSD_INNER + CONV_DIM, W["w_tail"], j)
            zx3 = zx.reshape(b, l, -1)
            tail3 = tail.reshape(b, l, LANES)
            tail_p = _pad_rows(tail3, lp) if lp != l else tail3
            lf, negc, c_tm = _forget(tail_p, _lane_vec(W["fox_b_forget"][j], 0))
            lf = lf[:, :l]
            if fox_cache is None:
                o_fox = _fox_prompt(q_bf.reshape(b, l, FOX_W), k_bf.reshape(b, l, FOX_W),
                                    v_bf.reshape(b, l, FOX_W), negc).reshape(t, FOX_W)
            else:
                k_pool, v_pool, loc, tot, page_table = fox_cache
                rows = FOX_HEADS * l
                q_ht = jnp.transpose(q_bf.reshape(b, l, FOX_HEADS, HEAD_DIM), (0, 2, 1, 3)).reshape(b, rows, HEAD_DIM)
                o_ht = _fox_decode(q_ht, _pad_rows(kf_all[j].reshape(b, rows, HEAD_DIM), LANES),
                                   _pad_rows(vf_all[j].reshape(b, rows, HEAD_DIM), LANES),
                                   _pad_rows(c_tm[:, :l].reshape(b, rows, 1), LANES).reshape(b, 1, LANES),
                                   k_pool, v_pool, loc, tot, page_table, j, l)
                o_fox = jnp.transpose(o_ht.reshape(b, FOX_HEADS, l, HEAD_DIM), (0, 2, 1, 3)).reshape(t, FOX_W)
            xc, new_buf = _conv(zx3, SSD_INNER, conv0[j], W["ssd_conv_w"][j], W["ssd_conv_b"][j])
            if lp != l:
                xc_p, z_p = _pad_rows(xc, lp), _pad_rows(zx3[:, :, :SSD_INNER], lp)
            else:
                xc_p, z_p = xc, zx3
            yg, h_last = _ssd(xc_p, tail_p, z_p, _lane_vec(W["ssd_dt_bias"][j], DT_LANE0),
                              _lane_vec(W["ssd_A_log"][j], DT_LANE0),
                              jnp.repeat(W["ssd_D"][j].astype(F32), SSD_HD).reshape(1, SSD_INNER),
                              e_mat, ssm0[j].reshape(b, SSD_INNER, SSD_STATE), l)
            yg = yg[:, :l].reshape(t, SSD_INNER)
            x2 = _outproj_even(o_fox, yg, W["ssd_norm"][j], W["w_out_even"], j, x2)
            fl.append(lf)
            hs.append(h_last.reshape(b, SSD_HEADS, SSD_HD, SSD_STATE))
            bufs.append(new_buf)
        else:
            (u,) = _dense(x2, W["w_in_odd"], layer=j, n_cols=d, tn=512, pro="norm", gain=W["norm_mix"][i],
                          name="inproj_odd", tm_max=TM_WIDE, x_single=True)
            p = W["s5_packed"][j]
            ngb = p["bbd_re"].shape[0]
            g_bf, s_re, s_im = _s5(u.reshape(b, l, d), p,
                                   s5_re0[j].reshape(b, ngb, 1, -1), s5_im0[j].reshape(b, ngb, 1, -1))
            (x2,) = _dense(g_bf.reshape(t, d), W["s5_w_glu"], layer=j, n_cols=d, tn=512, col0=0, col0_2=d,
                           epi="glu_res", res=x2, name="s5_glu_out")
            srs.append(s_re.reshape(b, -1, S5_STATE))
            sis.append(s_im.reshape(b, -1, S5_STATE))
        x3 = _cross_attn(x2.reshape(b, l, d), W["norm_cross"][i], W["w_mq"], W["mem_q_norm"][i],
                         mem_k, mem_v, W["w_mo"], i)
        x2 = x3.reshape(t, d)
        x2 = _ffn(x2, W["norm_ffn"][i], W["w_ffn_up"], W["w_ffn_down"], i)
    fk, fv = (a.reshape(a.shape[0], b, l, FOX_HEADS, HEAD_DIM) for a in kv_stacks)
    return (x2.reshape(b, l, d), fk, fv, jnp.stack(fl), jnp.stack(hs), jnp.stack(bufs),
            jnp.stack(srs), jnp.stack(sis))


def _pack_s5(a_re, a_im, b_re, b_im, c_re, c_im, d_skip, log_dt):
    g, n, k = b_re.shape
    ngb = g // S5_GB
    ps_re, ps_im, pc_re, pc_im, bb_re, bb_im = _s5_prep(
        a_re.astype(F32), a_im.astype(F32), log_dt.astype(F32),
        jnp.transpose(b_re, (2, 0, 1)).astype(F32), jnp.transpose(b_im, (2, 0, 1)).astype(F32))

    def bmat(bb):
        return _block_diag(jnp.transpose(bb, (1, 0, 2)).reshape(ngb, S5_GB * k, n), S5_GB)

    def cmat(cm):
        return _block_diag(jnp.transpose(cm.astype(F32), (0, 2, 1)).reshape(ngb, S5_GB * n, k), S5_GB)

    def rows(pw):
        return jnp.transpose(pw.reshape(pw.shape[0], ngb, S5_GB * n), (1, 0, 2))

    return dict(bbd_re=bmat(bb_re), bbd_im=bmat(bb_im), cbd_re=cmat(c_re), cbd_im=cmat(c_im),
                ps_re=rows(ps_re), ps_im=rows(ps_im), pc_re=rows(pc_re), pc_im=rows(pc_im),
                d_vec=d_skip.astype(F32).reshape(ngb, 1, S5_GB * k))


def kernel(x_prompt, x_sample, mem_prompt, cache_fox_k, cache_fox_v, cache_fox_logf, cache_mem_k, cache_mem_v,
           state_ssd, state_conv, state_s5_re, state_s5_im, page_table,
           norm_mix, norm_cross, norm_mem, norm_ffn,
           w_in_even, fox_b_forget, fox_q_norm, fox_k_norm, ssd_conv_w, ssd_conv_b, ssd_dt_bias, ssd_A_log,
           ssd_D, ssd_norm, w_out_even,
           w_in_odd, s5_A_re, s5_A_im, s5_B_re, s5_B_im, s5_C_re, s5_C_im, s5_D, s5_log_dt, s5_w_glu,
           w_mq, w_mkv, mem_q_norm, mem_k_norm, w_mo, w_ffn_up, w_ffn_down):
    depth = norm_mix.shape[0]
    n_even, n_odd = w_in_even.shape[0], w_in_odd.shape[0]
    b, l, d = x_prompt.shape
    z0 = 3 * FOX_W + FOX_HEADS
    dt0 = z0 + SSD_INNER + CONV_DIM
    w_t = jnp.swapaxes(w_in_even, 1, 2)
    W = {
        "norm_mix": norm_mix, "norm_cross": norm_cross, "norm_ffn": norm_ffn,
        "w_in_even_t": w_t,
        "w_tail": jnp.concatenate([w_t[:, 3 * FOX_W:z0], w_t[:, dt0:dt0 + SSD_HEADS],
                                   jnp.zeros((n_even, LANES - FOX_HEADS - SSD_HEADS, d), w_t.dtype)], axis=1),
        "fox_b_forget": fox_b_forget, "fox_q_norm": fox_q_norm, "fox_k_norm": fox_k_norm,
        "ssd_conv_w": ssd_conv_w, "ssd_conv_b": ssd_conv_b, "ssd_dt_bias": ssd_dt_bias, "ssd_A_log": ssd_A_log,
        "ssd_D": ssd_D, "ssd_norm": ssd_norm, "w_out_even": w_out_even,
        "w_in_odd": w_in_odd, "s5_w_glu": s5_w_glu,
        "s5_packed": [_pack_s5(s5_A_re[j], s5_A_im[j], s5_B_re[j], s5_B_im[j], s5_C_re[j], s5_C_im[j],
                               s5_D[j], s5_log_dt[j]) for j in range(n_odd)],
        "w_mq": w_mq, "mem_q_norm": mem_q_norm, "w_mo": w_mo, "w_ffn_up": w_ffn_up, "w_ffn_down": w_ffn_down,
    }
    mt = mem_prompt.shape[1]
    mw = w_mkv.shape[2] // 2
    mem2 = mem_prompt.reshape(b * mt, d)
    mk, mv = [], []
    for i in range(depth):
        (k_i,) = _dense(mem2, w_mkv, layer=i, n_cols=mw, tn=mw, col0=0, pro="norm", gain=norm_mem[i],
                        epi="headnorm", head_gain=mem_k_norm[i], name="mem_k_proj")
        (v_i,) = _dense(mem2, w_mkv, layer=i, n_cols=mw, tn=mw, col0=mw, pro="norm", gain=norm_mem[i],
                        name="mem_v_proj")
        mk.append(k_i.reshape(b, mt, mw))
        mv.append(v_i.reshape(b, mt, mw))
    mem_k_p = jnp.stack(mk)
    mem_v_p = jnp.stack(mv)
    n_grp = s5_A_re.shape[1]
    (y_prompt, fox_k_p, fox_v_p, fox_logf_p, ssd_p, conv_p, s5_re_p, s5_im_p) = _trunk(
        x_prompt, W, mem_k_p, mem_v_p,
        jnp.zeros((n_even, b, SSD_CONV - 1, CONV_DIM), F32),
        jnp.zeros((n_even, b, SSD_HEADS, SSD_HD, SSD_STATE), F32),
        jnp.zeros((n_odd, b, n_grp, S5_STATE), F32),
        jnp.zeros((n_odd, b, n_grp, S5_STATE), F32),
        None)
    db = x_sample.shape[0]
    n_pool, page = cache_fox_k.shape[1], cache_fox_k.shape[2]
    assert page == LANES
    width = page * FOX_HEADS
    lf_t = jnp.transpose(cache_fox_logf.astype(F32), (0, 2, 1, 3)).reshape(n_even, page, n_pool * FOX_HEADS)
    incl = _pool_cumsum(lf_t).reshape(n_even, page, n_pool, FOX_HEADS)
    loc = jnp.transpose(incl, (0, 2, 1, 3)).reshape(n_even, n_pool, 1, width)
    tot = jnp.tile(incl[:, page - 1], (1, 1, page)).reshape(n_even, n_pool, 1, width)
    fox_cache = (cache_fox_k.reshape(n_even, n_pool, width, HEAD_DIM),
                 cache_fox_v.reshape(n_even, n_pool, width, HEAD_DIM), loc, tot, page_table)
    (y_sample, fox_k_s, fox_v_s, fox_logf_s, ssd_s, conv_s, s5_re_s, s5_im_s) = _trunk(
        x_sample, W, cache_mem_k.reshape(depth, db, mt, mw), cache_mem_v.reshape(depth, db, mt, mw),
        state_conv, state_ssd, state_s5_re, state_s5_im, fox_cache)
    hd = mw // MEM_HEADS
    return (y_prompt, y_sample,
            fox_k_p, fox_v_p, fox_logf_p,
            mem_k_p.reshape(depth, b, mt, MEM_HEADS, hd), mem_v_p.reshape(depth, b, mt, MEM_HEADS, hd),
            ssd_p, conv_p, s5_re_p, s5_im_p,
            fox_k_s, fox_v_s, fox_logf_s, ssd_s, conv_s, s5_re_s, s5_im_s)
```

```python
import functools
import math

import jax
import jax.numpy as jnp
from jax import lax
from jax.experimental import pallas as pl
from jax.experimental.pallas import tpu as pltpu

F32 = jnp.float32
BF16 = jnp.bfloat16
EPS = 1e-6
LANES = 128
SUBLANES = 8
VMEM_LIMIT_BYTES = 56 * 1024 * 1024

FOX_HEADS = 8
HEAD_DIM = 128
FOX_W = FOX_HEADS * HEAD_DIM
SSD_HEADS = 32
SSD_HD = 64
SSD_GROUPS = 4
SSD_STATE = 128
SSD_CHUNK = 128
SSD_INNER = SSD_HEADS * SSD_HD
SSD_CONV = 4
CONV_DIM = SSD_INNER + 2 * SSD_GROUPS * SSD_STATE
DT_LANE0 = FOX_HEADS
S5_GROUP = 16
S5_STATE = 64
S5_GB = 16
MEM_HEADS = 4
HIGHEST = lax.Precision.HIGHEST


def _cparams(sem):
    return pltpu.CompilerParams(dimension_semantics=sem, vmem_limit_bytes=VMEM_LIMIT_BYTES)


def _gelu_tanh(x):
    return 0.5 * x * (1.0 + jnp.tanh(math.sqrt(2.0 / math.pi) * (x + 0.044715 * x * x * x)))


def _softplus(x):
    return jnp.maximum(x, 0.0) + jnp.log1p(jnp.exp(-jnp.abs(x)))


def _silu(x):
    return x * jax.nn.sigmoid(x)


def _rms(x, gain):
    return x * lax.rsqrt(jnp.mean(x * x, axis=-1, keepdims=True) + EPS) * gain


def _dot(a, b):
    return jnp.dot(a, b, preferred_element_type=F32)


def _dot_nt(a, b):
    return lax.dot_general(a, b, (((1,), (1,)), ((), ())), preferred_element_type=F32)


def _dot_exact(a, b):
    return jnp.dot(a, b, preferred_element_type=F32, precision=HIGHEST)


def _dense_kernel(*refs, pro, epi, n_out):
    it = iter(refs)
    x_ref = next(it)
    gain_ref = next(it) if pro == "norm" else None
    w_ref = next(it)
    w2_ref = next(it) if epi in ("swiglu", "glu_res") else None
    res_ref = next(it) if epi in ("residual", "glu_res") else None
    hg_ref = next(it) if epi == "headnorm" else None
    out_refs = [next(it) for _ in range(n_out)]
    xs_ref = next(it) if pro != "none" else None

    if pro != "none":
        @pl.when(pl.program_id(1) == 0)
        def _():
            xf = x_ref[...].astype(F32)
            if pro == "norm":
                xf = _rms(xf, gain_ref[...])
            elif pro == "gelu":
                xf = _gelu_tanh(xf)
            xs_ref[...] = xf.astype(BF16)
        lhs = xs_ref[...]
    else:
        lhs = x_ref[...]

    acc = _dot(lhs, w_ref[...].astype(BF16))
    if epi == "swiglu":
        acc = _silu(acc) * _dot(lhs, w2_ref[...].astype(BF16))
    elif epi == "glu_res":
        acc = res_ref[...] + acc * jax.nn.sigmoid(_dot(lhs, w2_ref[...].astype(BF16)))
    elif epi == "residual":
        acc = res_ref[...] + acc

    if epi == "headnorm":
        for c in range(acc.shape[1] // HEAD_DIM):
            sl = slice(c * HEAD_DIM, (c + 1) * HEAD_DIM)
            blk = _rms(acc[:, sl], hg_ref[...])
            for o in out_refs:
                o[:, sl] = blk.astype(o.dtype)
    else:
        for o in out_refs:
            o[...] = acc.astype(o.dtype)


def _wspec(w, layer, k, tn, blk0):
    if w.ndim == 2:
        return pl.BlockSpec((k, tn), lambda i, j: (0, blk0 + j))
    return pl.BlockSpec((None, k, tn), lambda i, j: (layer, 0, blk0 + j))


def _dense(x, w, *, n_cols, tn, name, layer=0, col0=0, col0_2=None, pro="none", gain=None, epi="plain",
           res=None, head_gain=None, out_dtypes=(F32,), tm_max=1024, x_single=False):
    m, k = x.shape
    tm = min(tm_max, m)
    assert m % tm == 0 and n_cols % tn == 0 and col0 % tn == 0
    b0 = col0 // tn
    xmode = dict(pipeline_mode=pl.Buffered(1)) if x_single else {}
    in_specs = [pl.BlockSpec((tm, k), lambda i, j: (i, 0), **xmode)]
    args = [x]
    if pro == "norm":
        in_specs.append(pl.BlockSpec((1, k), lambda i, j: (0, 0)))
        args.append(gain.reshape(1, k).astype(F32))
    in_specs.append(_wspec(w, layer, k, tn, b0))
    args.append(w)
    if epi in ("swiglu", "glu_res"):
        assert col0_2 % tn == 0
        in_specs.append(_wspec(w, layer, k, tn, col0_2 // tn))
        args.append(w)
    if epi in ("residual", "glu_res"):
        in_specs.append(pl.BlockSpec((tm, tn), lambda i, j: (i, j)))
        args.append(res)
    if epi == "headnorm":
        in_specs.append(pl.BlockSpec((1, HEAD_DIM), lambda i, j: (0, 0)))
        args.append(head_gain.reshape(1, HEAD_DIM).astype(F32))
    out_shape = [jax.ShapeDtypeStruct((m, n_cols), dt) for dt in out_dtypes]
    out_specs = [pl.BlockSpec((tm, tn), lambda i, j: (i, j)) for _ in out_dtypes]
    scratch = [pltpu.VMEM((tm, k), BF16)] if pro != "none" else []
    outs = pl.pallas_call(
        functools.partial(_dense_kernel, pro=pro, epi=epi, n_out=len(out_dtypes)),
        grid=(m // tm, n_cols // tn),
        in_specs=in_specs, out_specs=out_specs, out_shape=out_shape,
        scratch_shapes=scratch,
        compiler_params=_cparams(("parallel", "arbitrary")),
        name=name,
    )(*args)
    return outs


def _ffn_kernel(x_ref, g_ref, wg_ref, wu_ref, wd_ref, o_ref, xs_ref):
    @pl.when(pl.program_id(1) == 0)
    def _():
        x = x_ref[...]
        xs_ref[...] = _rms(x, g_ref[...]).astype(BF16)
        o_ref[...] = x
    xs = xs_ref[...]
    hid = _silu(_dot(xs, wg_ref[...].astype(BF16))) * _dot(xs, wu_ref[...].astype(BF16))
    o_ref[...] += _dot(hid.astype(BF16), wd_ref[...].astype(BF16))


def _ffn(x, gain, w_up, w_down, layer, th=256, tm_max=1024):
    m, d = x.shape
    hid = w_down.shape[1]
    tm = min(tm_max, m)
    nth = hid // th
    assert m % tm == 0 and hid % th == 0
    return pl.pallas_call(
        _ffn_kernel,
        grid=(m // tm, nth),
        in_specs=[pl.BlockSpec((tm, d), lambda i, j: (i, 0)),
                  pl.BlockSpec((1, d), lambda i, j: (0, 0)),
                  pl.BlockSpec((None, d, th), lambda i, j: (layer, 0, j)),
                  pl.BlockSpec((None, d, th), lambda i, j: (layer, 0, nth + j)),
                  pl.BlockSpec((None, th, d), lambda i, j: (layer, j, 0))],
        out_specs=pl.BlockSpec((tm, d), lambda i, j: (i, 0)),
        out_shape=jax.ShapeDtypeStruct((m, d), F32),
        scratch_shapes=[pltpu.VMEM((tm, d), BF16)],
        compiler_params=_cparams(("parallel", "arbitrary")),
        name="ffn",
    )(x, gain.reshape(1, d), w_up, w_up, w_down)


TM_WIDE = 2048


def _inproj_qkv_kernel(x_ref, g_ref, w_ref, qg_ref, kg_ref, *rest, nq):
    q_ref, kf_ref, kb_ref, vf_ref, vb_ref, xs_ref = rest[-6:]
    j = pl.program_id(1)

    @pl.when(j == 0)
    def _():
        xs_ref[...] = _rms(x_ref[...], g_ref[...]).astype(BF16)

    def headnorm(acc, gain, outs):
        for c in range(acc.shape[1] // HEAD_DIM):
            sl = slice(c * HEAD_DIM, (c + 1) * HEAD_DIM)
            blk = _rms(acc[:, sl], gain)
            for o in outs:
                o[:, sl] = blk.astype(o.dtype)

    @pl.when(j < nq)
    def _():
        headnorm(_dot_nt(xs_ref[...], w_ref[...].astype(BF16)), qg_ref[...] * (HEAD_DIM ** -0.5), (q_ref,))

    @pl.when((j >= nq) & (j < 2 * nq))
    def _():
        headnorm(_dot_nt(xs_ref[...], w_ref[...].astype(BF16)), kg_ref[...], (kf_ref, kb_ref))

    @pl.when(j >= 2 * nq)
    def _():
        acc = _dot_nt(xs_ref[...], w_ref[...].astype(BF16))
        vf_ref[...] = acc
        vb_ref[...] = acc.astype(BF16)


def _inproj_qkv(x, gain, w_t, layer, n_layers, q_gain, k_gain, stacks, tn=256):
    m, k = x.shape
    tm = min(TM_WIDE, m)
    nq = FOX_W // tn
    tile = lambda lo: pl.BlockSpec((tm, tn), lambda i, j: (i, jnp.clip(j - lo, 0, nq - 1)))
    stile = lambda lo: pl.BlockSpec((None, tm, tn), lambda i, j: (layer, i, jnp.clip(j - lo, 0, nq - 1)))
    sd = jax.ShapeDtypeStruct
    in_specs = [pl.BlockSpec((tm, k), lambda i, j: (i, 0), pipeline_mode=pl.Buffered(1)),
                pl.BlockSpec((1, k), lambda i, j: (0, 0)),
                pl.BlockSpec((None, tn, k), lambda i, j: (layer, j, 0)),
                pl.BlockSpec((1, HEAD_DIM), lambda i, j: (0, 0)),
                pl.BlockSpec((1, HEAD_DIM), lambda i, j: (0, 0))]
    args = [x, gain.reshape(1, k), w_t, q_gain.reshape(1, HEAD_DIM), k_gain.reshape(1, HEAD_DIM)]
    in_specs += [pl.BlockSpec(memory_space=pl.ANY), pl.BlockSpec(memory_space=pl.ANY)]
    aliases = {len(args): 1, len(args) + 1: 3}
    args += list(stacks)
    return pl.pallas_call(
        functools.partial(_inproj_qkv_kernel, nq=nq),
        grid=(m // tm, 3 * nq),
        in_specs=in_specs,
        out_specs=[tile(0), stile(nq), tile(nq), stile(2 * nq), tile(2 * nq)],
        out_shape=[sd((m, FOX_W), BF16), sd((n_layers, m, FOX_W), F32), sd((m, FOX_W), BF16),
                   sd((n_layers, m, FOX_W), F32), sd((m, FOX_W), BF16)],
        scratch_shapes=[pltpu.VMEM((tm, k), BF16)],
        input_output_aliases=aliases,
        compiler_params=_cparams(("parallel", "arbitrary")),
        name="inproj_qkv",
    )(*args)


def _inproj_zx_kernel(x_ref, g_ref, wzx_ref, wt_ref, zx_ref, tail_ref, xs_ref, *, nzx):
    j = pl.program_id(1)

    @pl.when(j == 0)
    def _():
        xs_ref[...] = _rms(x_ref[...], g_ref[...]).astype(BF16)

    @pl.when(j < nzx)
    def _():
        zx_ref[...] = _dot_nt(xs_ref[...], wzx_ref[0].astype(BF16))

    @pl.when(j == nzx)
    def _():
        tail_ref[...] = _dot_nt(xs_ref[...], wt_ref[...].astype(BF16))


def _inproj_zx(x, gain, w_t, row0, n_zx_cols, w_tail, layer, tn=512):
    m, k = x.shape
    tm = min(TM_WIDE, m)
    nzx = n_zx_cols // tn
    w_zx = w_t
    return pl.pallas_call(
        functools.partial(_inproj_zx_kernel, nzx=nzx),
        grid=(m // tm, nzx + 1),
        in_specs=[pl.BlockSpec((tm, k), lambda i, j: (i, 0), pipeline_mode=pl.Buffered(1)),
                  pl.BlockSpec((1, k), lambda i, j: (0, 0)),
                  pl.BlockSpec((pl.Element(1), pl.Element(tn), pl.Element(k)),
                               lambda i, j: (layer, pl.multiple_of(row0 + jnp.minimum(j, nzx - 1) * tn, SUBLANES), 0)),
                  pl.BlockSpec((None, LANES, k), lambda i, j: (layer, 0, 0), pipeline_mode=pl.Buffered(1))],
        out_specs=[pl.BlockSpec((tm, tn), lambda i, j: (i, jnp.minimum(j, nzx - 1))),
                   pl.BlockSpec((tm, LANES), lambda i, j: (i, 0))],
        out_shape=[jax.ShapeDtypeStruct((m, nzx * tn), F32), jax.ShapeDtypeStruct((m, LANES), F32)],
        scratch_shapes=[pltpu.VMEM((tm, k), BF16)],
        compiler_params=_cparams(("parallel", "arbitrary")),
        name="inproj_zx",
    )(x, gain.reshape(1, k), w_zx, w_tail)


def _outproj_kernel(o_ref, y_ref, g_ref, w_ref, res_ref, out_ref, yn_ref):
    k1 = o_ref.shape[1]

    @pl.when(pl.program_id(1) == 0)
    def _():
        yn_ref[...] = _rms(y_ref[...], g_ref[...]).astype(BF16)
    acc = _dot(o_ref[...], w_ref[:k1, :].astype(BF16)) + _dot(yn_ref[...], w_ref[k1:, :].astype(BF16))
    out_ref[...] = res_ref[...] + acc


def _outproj_even(o_fox, yg, gain, w_out, layer, res, tn=256, tm_max=TM_WIDE):
    m, d = res.shape
    tm = min(tm_max, m)
    k1, k2 = o_fox.shape[1], yg.shape[1]
    return pl.pallas_call(
        _outproj_kernel,
        grid=(m // tm, d // tn),
        in_specs=[
            pl.BlockSpec((tm, k1), lambda i, j: (i, 0)),
            pl.BlockSpec((tm, k2), lambda i, j: (i, 0), pipeline_mode=pl.Buffered(1)),
            pl.BlockSpec((1, k2), lambda i, j: (0, 0)),
            pl.BlockSpec((None, k1 + k2, tn), lambda i, j: (layer, 0, j)),
            pl.BlockSpec((tm, tn), lambda i, j: (i, j)),
        ],
        out_specs=pl.BlockSpec((tm, tn), lambda i, j: (i, j)),
        out_shape=jax.ShapeDtypeStruct((m, d), F32),
        scratch_shapes=[pltpu.VMEM((tm, k2), BF16)],
        compiler_params=_cparams(("parallel", "arbitrary")),
        name="outproj_even",
    )(o_fox, yg, gain.reshape(1, k2).astype(F32), w_out, res)


def _lane_cumsum(x):
    lane = lax.broadcasted_iota(jnp.int32, x.shape, 1)
    k = 1
    while k < LANES:
        x = x + jnp.where(lane >= k, pltpu.roll(x, k, axis=1), 0.0)
        k *= 2
    return x


def _forget_kernel(raw_ref, b_ref, lf_ref, negc_ref, ctm_ref):
    lp = raw_ref.shape[1]
    lf = -_softplus(-(raw_ref[0] + b_ref[...]))
    lf_ref[0] = lf[:, :FOX_HEADS]
    row_i = lax.broadcasted_iota(jnp.int32, (LANES, LANES), 0)
    col_i = lax.broadcasted_iota(jnp.int32, (LANES, LANES), 1)
    tri = (col_i <= row_i).astype(F32)
    ctm_ref[0] = _dot_exact(tri, lf[:LANES, :])[:, :FOX_HEADS]
    carry = jnp.zeros((FOX_HEADS, 1), F32)
    for c in range(lp // LANES):
        blk = lf[c * LANES:(c + 1) * LANES, :].T[:FOX_HEADS, :]
        cs = _lane_cumsum(blk) + carry
        negc_ref[0, :, c * LANES:(c + 1) * LANES] = -cs
        carry = cs[:, LANES - 1:LANES]


def _forget(raw, b_vec):
    b, lp, _ = raw.shape
    return pl.pallas_call(
        _forget_kernel,
        grid=(b,),
        in_specs=[pl.BlockSpec((1, lp, LANES), lambda i: (i, 0, 0)),
                  pl.BlockSpec((1, LANES), lambda i: (0, 0))],
        out_specs=[pl.BlockSpec((1, lp, FOX_HEADS), lambda i: (i, 0, 0)),
                   pl.BlockSpec((1, FOX_HEADS, lp), lambda i: (i, 0, 0)),
                   pl.BlockSpec((1, LANES, FOX_HEADS), lambda i: (i, 0, 0))],
        out_shape=[jax.ShapeDtypeStruct((b, lp, FOX_HEADS), F32),
                   jax.ShapeDtypeStruct((b, FOX_HEADS, lp), F32),
                   jax.ShapeDtypeStruct((b, LANES, FOX_HEADS), F32)],
        compiler_params=_cparams(("parallel",)),
        name="forget_gates",
    )(raw, b_vec)


def _fox_prompt_kernel(q_ref, k_ref, v_ref, nb_ref, o_ref, m_sc, acc_sc, mask_sc, *, tq, tk):
    qi = pl.program_id(1)
    kj = pl.program_id(2)

    @pl.when(kj == 0)
    def _():
        m_sc[...] = jnp.full(m_sc.shape, -jnp.inf, F32)
        acc_sc[...] = jnp.zeros(acc_sc.shape, F32)

    active = kj * tk <= qi * tq + (tq - 1)
    crosses_diagonal = kj * tk + (tk - 1) > qi * tq

    def step(masked):
        if masked:
            row = qi * tq + lax.broadcasted_iota(jnp.int32, (tq, tk), 0)
            col = kj * tk + lax.broadcasted_iota(jnp.int32, (tq, tk), 1)
            mask_sc[...] = jnp.where(col <= row, 0.0, -jnp.inf)
        ones = jnp.ones((tk, HEAD_DIM), BF16)
        for h in range(FOX_HEADS):
            sl = slice(h * HEAD_DIM, (h + 1) * HEAD_DIM)
            s = _dot_nt(q_ref[0, :, sl], k_ref[0, :, sl]) + nb_ref[0, h:h + 1, :]
            if masked:
                s = s + mask_sc[...]
            m_old = m_sc[h]
            m_new = jnp.maximum(m_old, jnp.max(s, axis=-1, keepdims=True))
            p = jnp.exp(s - m_new).astype(BF16)
            pv = _dot(p, jnp.concatenate([v_ref[0, :, sl], ones], axis=1))
            acc_sc[h] = jnp.exp(m_old - m_new) * acc_sc[h] + pv
            m_sc[h] = m_new

    pl.when(active & crosses_diagonal)(functools.partial(step, True))
    pl.when(active & jnp.logical_not(crosses_diagonal))(functools.partial(step, False))

    @pl.when(kj == pl.num_programs(2) - 1)
    def _():
        for h in range(FOX_HEADS):
            acc = acc_sc[h]
            o_ref[0, :, h * HEAD_DIM:(h + 1) * HEAD_DIM] = (acc[:, :HEAD_DIM] / acc[:, HEAD_DIM:]).astype(o_ref.dtype)


def _fox_prompt(q, k, v, negc, tq=512, tk=512):
    b, l, _ = q.shape
    tq, tk = min(tq, l), min(tk, l)
    nq, nk = l // tq, l // tk

    def last_needed(qi, kj):
        return jnp.minimum(kj, (qi * tq + tq - 1) // tk)

    return pl.pallas_call(
        functools.partial(_fox_prompt_kernel, tq=tq, tk=tk),
        grid=(b, nq, nk),
        in_specs=[pl.BlockSpec((1, tq, FOX_W), lambda bi, qi, kj: (bi, qi, 0)),
                  pl.BlockSpec((1, tk, FOX_W), lambda bi, qi, kj: (bi, last_needed(qi, kj), 0)),
                  pl.BlockSpec((1, tk, FOX_W), lambda bi, qi, kj: (bi, last_needed(qi, kj), 0)),
                  pl.BlockSpec((1, FOX_HEADS, tk), lambda bi, qi, kj: (bi, 0, last_needed(qi, kj)))],
        out_specs=pl.BlockSpec((1, tq, FOX_W), lambda bi, qi, kj: (bi, qi, 0)),
        out_shape=jax.ShapeDtypeStruct((b, l, FOX_W), BF16),
        scratch_shapes=[pltpu.VMEM((FOX_HEADS, tq, 1), F32),
                        pltpu.VMEM((FOX_HEADS, tq, 2 * HEAD_DIM), F32),
                        pltpu.VMEM((tq, tk), F32)],
        compiler_params=_cparams(("parallel", "parallel", "arbitrary")),
        name="fox_prompt",
    )(q, k, v, negc)


def _pool_cumsum_kernel(x_ref, o_ref):
    n = x_ref.shape[0]
    row_i = lax.broadcasted_iota(jnp.int32, (n, n), 0)
    col_i = lax.broadcasted_iota(jnp.int32, (n, n), 1)
    o_ref[...] = _dot_exact((col_i <= row_i).astype(F32), x_ref[...])


def _pool_cumsum(x, tc=2048):
    nl, page, cols = x.shape
    tc = tc if cols % tc == 0 else cols
    return pl.pallas_call(
        _pool_cumsum_kernel,
        grid=(nl, cols // tc),
        in_specs=[pl.BlockSpec((None, page, tc), lambda i, j: (i, 0, j))],
        out_specs=pl.BlockSpec((None, page, tc), lambda i, j: (i, 0, j)),
        out_shape=jax.ShapeDtypeStruct((nl, page, cols), F32),
        compiler_params=_cparams(("parallel", "parallel")),
        name="pool_logf_cumsum",
    )(x)


def _fox_decode_kernel(pt_ref, q_ref, *refs, gp, t_new):
    kv_refs = refs[:4 * gp]
    kn_ref, vn_ref, cn_ref, o_ref, m_sc, l_sc, acc_sc, carry_sc = refs[4 * gp:]
    p = pl.program_id(1)
    n_past = pl.num_programs(1) - 1
    rows = q_ref.shape[1]

    @pl.when(p == 0)
    def _():
        m_sc[...] = jnp.full(m_sc.shape, -jnp.inf, F32)
        l_sc[...] = jnp.zeros(l_sc.shape, F32)
        acc_sc[...] = jnp.zeros(acc_sc.shape, F32)
        carry_sc[...] = jnp.zeros(carry_sc.shape, F32)

    def update(s, v, state):
        m_old, l_old, acc = state
        m_new = jnp.maximum(m_old, jnp.max(s, axis=-1, keepdims=True))
        alpha = jnp.exp(m_old - m_new)
        pr = jnp.exp(s - m_new)
        l_new = alpha * l_old + jnp.sum(pr, axis=-1, keepdims=True)
        acc = alpha * acc + _dot(pr.astype(BF16), v.astype(BF16))
        return m_new, l_new, acc

    @pl.when(p < n_past)
    def _():
        q = q_ref[0]
        width = kv_refs[0].shape[0]
        row = lax.broadcasted_iota(jnp.int32, (rows, width), 0)
        col = lax.broadcasted_iota(jnp.int32, (rows, width), 1)
        own_head = (col % FOX_HEADS) == (row // t_new)
        carry = carry_sc[...]
        scores = []
        for g in range(gp):
            k_ref, _, loc_ref, tot_ref = kv_refs[4 * g:4 * g + 4]
            s = _dot_nt(q, k_ref[...].astype(BF16)) - (carry + loc_ref[...])
            scores.append(jnp.where(own_head, s, -jnp.inf))
            carry = carry + tot_ref[...]
        carry_sc[...] = carry
        m_old = m_sc[...]
        m_new = m_old
        for s in scores:
            m_new = jnp.maximum(m_new, jnp.max(s, axis=-1, keepdims=True))
        alpha = jnp.exp(m_old - m_new)
        l_new = alpha * l_sc[...]
        acc = alpha * acc_sc[...]
        for g, s in enumerate(scores):
            pr = jnp.exp(s - m_new)
            l_new = l_new + jnp.sum(pr, axis=-1, keepdims=True)
            acc = acc + _dot(pr.astype(BF16), kv_refs[4 * g + 1][...].astype(BF16))
        m_sc[...], l_sc[...], acc_sc[...] = m_new, l_new, acc

    @pl.when(p == n_past)
    def _():
        width = kn_ref.shape[1]
        row = lax.broadcasted_iota(jnp.int32, (rows, width), 0)
        col = lax.broadcasted_iota(jnp.int32, (rows, width), 1)
        keep = ((col % FOX_HEADS) == (row // t_new)) & ((col // FOX_HEADS) <= (row % t_new))
        s = _dot_nt(q_ref[0], kn_ref[0].astype(BF16)) - (carry_sc[:, :width] + cn_ref[0])
        m, l, acc = update(jnp.where(keep, s, -jnp.inf), vn_ref[0], (m_sc[...], l_sc[...], acc_sc[...]))
        o_ref[0] = (acc / l).astype(o_ref.dtype)


def _fox_decode(q, k_new, v_new, c_new, k_pool, v_pool, loc, tot, page_table, layer, t_new, gp=16):
    b, rows, _ = q.shape
    n_pages = page_table.shape[1]
    gp = max(g for g in range(1, gp + 1) if n_pages % g == 0)
    n_steps = n_pages // gp
    width = k_pool.shape[2]

    def page_map(g):
        return lambda bi, p, pt: (layer, pt[bi, jnp.minimum(p, n_steps - 1) * gp + g], 0, 0)

    in_specs = [pl.BlockSpec((1, rows, HEAD_DIM), lambda bi, p, pt: (bi, 0, 0))]
    args = [q]
    for g in range(gp):
        in_specs += [pl.BlockSpec((None, None, width, HEAD_DIM), page_map(g)),
                     pl.BlockSpec((None, None, width, HEAD_DIM), page_map(g)),
                     pl.BlockSpec((None, None, 1, width), page_map(g)),
                     pl.BlockSpec((None, None, 1, width), page_map(g))]
        args += [k_pool, v_pool, loc, tot]
    new_spec = pl.BlockSpec((1, k_new.shape[1], HEAD_DIM), lambda bi, p, pt: (bi, 0, 0))
    in_specs += [new_spec, new_spec, pl.BlockSpec((1, 1, c_new.shape[2]), lambda bi, p, pt: (bi, 0, 0))]
    args += [k_new, v_new, c_new]
    grid_spec = pltpu.PrefetchScalarGridSpec(
        num_scalar_prefetch=1,
        grid=(b, n_steps + 1),
        in_specs=in_specs,
        out_specs=pl.BlockSpec((1, rows, HEAD_DIM), lambda bi, p, pt: (bi, 0, 0)),
        scratch_shapes=[pltpu.VMEM((rows, 1), F32), pltpu.VMEM((rows, 1), F32),
                        pltpu.VMEM((rows, HEAD_DIM), F32), pltpu.VMEM((1, width), F32)],
    )
    return pl.pallas_call(
        functools.partial(_fox_decode_kernel, gp=gp, t_new=t_new), grid_spec=grid_spec,
        out_shape=jax.ShapeDtypeStruct((b, rows, HEAD_DIM), BF16),
        compiler_params=_cparams(("parallel", "arbitrary")),
        name="fox_decode",
    )(page_table, *args)


def _conv_kernel(x_ref, buf_ref, w_ref, b_ref, y_ref, nb_ref, full_sc):
    l = x_ref.shape[1]
    pad = SUBLANES
    full_sc[pad - (SSD_CONV - 1):pad, :] = buf_ref[0]
    full_sc[pad:pad + l, :] = x_ref[0]
    acc = b_ref[...] + full_sc[pad - 3:pad - 3 + l, :] * w_ref[0:1, :]
    for kk in range(1, SSD_CONV):
        acc = acc + full_sc[pad - 3 + kk:pad - 3 + kk + l, :] * w_ref[kk:kk + 1, :]
    y_ref[0] = _silu(acc)
    nb_ref[0] = full_sc[pad + l - (SSD_CONV - 1):pad + l, :]


def _conv(zx, col0, buf, w, bias, tc=512):
    b, l, _ = zx.shape
    c = w.shape[1]
    cb0 = col0 // tc
    return pl.pallas_call(
        _conv_kernel,
        grid=(b, c // tc),
        in_specs=[pl.BlockSpec((1, l, tc), lambda i, j: (i, 0, cb0 + j)),
                  pl.BlockSpec((1, SSD_CONV - 1, tc), lambda i, j: (i, 0, j)),
                  pl.BlockSpec((SSD_CONV, tc), lambda i, j: (0, j)),
                  pl.BlockSpec((1, tc), lambda i, j: (0, j))],
        out_specs=[pl.BlockSpec((1, l, tc), lambda i, j: (i, 0, j)),
                   pl.BlockSpec((1, SSD_CONV - 1, tc), lambda i, j: (i, 0, j))],
        out_shape=[jax.ShapeDtypeStruct((b, l, c), F32),
                   jax.ShapeDtypeStruct((b, SSD_CONV - 1, c), F32)],
        scratch_shapes=[pltpu.VMEM((l + SUBLANES, tc), F32)],
        compiler_params=_cparams(("parallel", "parallel")),
        name="ssd_conv",
    )(zx, buf, w, bias.reshape(1, c))


def _ssd_kernel(xc_ref, tail_ref, z_ref, dtb_ref, alog_ref, dexp_ref, e_ref, h0_ref,
                y_ref, hout_ref, h_sc, *, valid_len):
    c = pl.program_id(1)
    q = SSD_CHUNK
    gw = SSD_INNER // SSD_GROUPS
    hpg = SSD_HEADS // SSD_GROUPS

    @pl.when(c == 0)
    def _():
        h_sc[...] = h0_ref[0]

    lane = lax.broadcasted_iota(jnp.int32, (q, LANES), 1)
    row = lax.broadcasted_iota(jnp.int32, (q, LANES), 0)
    col_i = lax.broadcasted_iota(jnp.int32, (q, q), 1)
    row_i = lax.broadcasted_iota(jnp.int32, (q, q), 0)
    causal = col_i <= row_i
    tri = causal.astype(F32)

    dt = _softplus(tail_ref[0] + dtb_ref[...])
    live = (lane >= DT_LANE0) & (lane < DT_LANE0 + SSD_HEADS) & (c * q + row < valid_len)
    dt = jnp.where(live, dt, 0.0)
    a = dt * (-jnp.exp(alog_ref[...]))
    a_cum = _dot_exact(tri, a)
    a_cum_t = a_cum.T
    dt_t = dt.T
    a_last = a_cum[q - 1:q, :]
    fac = jnp.concatenate([dt * jnp.exp(a_last - a_cum), jnp.exp(a_cum)], axis=0)
    fac_hi = fac.astype(BF16)
    fac_lo = (fac - fac_hi.astype(F32)).astype(BF16)
    fac2 = jnp.concatenate([fac_hi, fac_lo], axis=0)

    for g in range(SSD_GROUPS):
        ex = _dot(fac2, e_ref[g])
        ex = ex[:2 * q] + ex[2 * q:]
        w1_e, ea_e = ex[:q], ex[q:]
        xs = xc_ref[0, :, g * gw:(g + 1) * gw]
        bb = xc_ref[0, :, SSD_INNER + g * SSD_STATE:SSD_INNER + (g + 1) * SSD_STATE].astype(BF16)
        cc = xc_ref[0, :, SSD_INNER + (SSD_GROUPS + g) * SSD_STATE:
                    SSD_INNER + (SSD_GROUPS + g + 1) * SSD_STATE].astype(BF16)
        cb = _dot_nt(cc, bb)
        hg = h_sc[g * gw:(g + 1) * gw, :]
        y = _dot_nt(cc, hg.astype(BF16)) * ea_e
        st = _dot((xs * w1_e).T.astype(BF16), bb)
        for hp in range(hpg // 2):
            pair = xs[:, hp * LANES:(hp + 1) * LANES]
            lane_p = lax.broadcasted_iota(jnp.int32, pair.shape, 1)
            yp = None
            for sub in range(2):
                h = 2 * hp + sub
                ln = DT_LANE0 + g * hpg + h
                seg = a_cum[:, ln:ln + 1] - a_cum_t[ln:ln + 1, :]
                dec = jnp.exp(jnp.where(causal, seg, -jnp.inf))
                mat = (cb * dec * dt_t[ln:ln + 1, :]).astype(BF16)
                in_head = (lane_p >= sub * SSD_HD) & (lane_p < (sub + 1) * SSD_HD)
                rhs = jnp.where(in_head, pair, 0.0).astype(BF16)
                part = _dot(mat, rhs)
                yp = part if yp is None else yp + part
                r0 = g * gw + h * SSD_HD
                h_sc[r0:r0 + SSD_HD, :] = (hg[h * SSD_HD:(h + 1) * SSD_HD, :] * jnp.exp(a_last[:, ln:ln + 1])
                                           + st[h * SSD_HD:(h + 1) * SSD_HD, :])
            cs = slice(g * gw + hp * LANES, g * gw + (hp + 1) * LANES)
            yt = yp + y[:, hp * LANES:(hp + 1) * LANES] + dexp_ref[:, cs] * xs[:, hp * LANES:(hp + 1) * LANES]
            y_ref[0, :, cs] = yt * _silu(z_ref[0, :, cs])

    @pl.when(c == pl.num_programs(1) - 1)
    def _():
        hout_ref[0] = h_sc[...]


def _ssd(xc, tail, zx, dtb_vec, alog_vec, d_exp, e_mat, h0, valid_len):
    b, lp, _ = xc.shape
    nc = lp // SSD_CHUNK
    return pl.pallas_call(
        functools.partial(_ssd_kernel, valid_len=valid_len),
        grid=(b, nc),
        in_specs=[pl.BlockSpec((1, SSD_CHUNK, CONV_DIM), lambda i, c: (i, c, 0)),
                  pl.BlockSpec((1, SSD_CHUNK, LANES), lambda i, c: (i, c, 0)),
                  pl.BlockSpec((1, SSD_CHUNK, SSD_INNER), lambda i, c: (i, c, 0)),
                  pl.BlockSpec((1, LANES), lambda i, c: (0, 0)),
                  pl.BlockSpec((1, LANES), lambda i, c: (0, 0)),
                  pl.BlockSpec((1, SSD_INNER), lambda i, c: (0, 0)),
                  pl.BlockSpec((SSD_GROUPS, LANES, SSD_INNER // SSD_GROUPS), lambda i, c: (0, 0, 0)),
                  pl.BlockSpec((1, SSD_INNER, SSD_STATE), lambda i, c: (i, 0, 0))],
        out_specs=[pl.BlockSpec((1, SSD_CHUNK, SSD_INNER), lambda i, c: (i, c, 0)),
                   pl.BlockSpec((1, SSD_INNER, SSD_STATE), lambda i, c: (i, 0, 0))],
        out_shape=[jax.ShapeDtypeStruct((b, lp, SSD_INNER), F32),
                   jax.ShapeDtypeStruct((b, SSD_INNER, SSD_STATE), F32)],
        scratch_shapes=[pltpu.VMEM((SSD_INNER, SSD_STATE), F32)],
        compiler_params=_cparams(("parallel", "arbitrary")),
        name="ssd_chunked",
    )(xc, tail, zx, dtb_vec, alog_vec, d_exp, e_mat, h0)


S5_TILE = 256
S5_STEPS = S5_TILE // SUBLANES


def _cmul(ar, ai, br, bi):
    return ar * br - ai * bi, ar * bi + ai * br


def _s5_prep_kernel(are_ref, aim_ref, ldt_ref, bre_ref, bim_ref, psr_ref, psi_ref, pcr_ref, pci_ref,
                    bbr_ref, bbi_ref):
    lam_re = jnp.minimum(are_ref[...], -1e-4)
    lam_im = aim_ref[...]
    dt = jnp.exp(ldt_ref[...])
    mag = jnp.exp(lam_re * dt)
    ang = lam_im * dt
    lb_re, lb_im = mag * jnp.cos(ang), mag * jnp.sin(ang)
    nr, ni = lb_re - 1.0, lb_im
    den = lam_re * lam_re + lam_im * lam_im
    coef_re = (nr * lam_re + ni * lam_im) / den
    coef_im = (ni * lam_re - nr * lam_im) / den
    for k in range(S5_GROUP):
        br, bi = bre_ref[k], bim_ref[k]
        bbr_ref[k] = coef_re * br - coef_im * bi
        bbi_ref[k] = coef_re * bi + coef_im * br
    pr, pi = lb_re, lb_im
    for r in range(S5_STEPS):
        psr_ref[r], psi_ref[r] = pr, pi
        if r + 1 < S5_STEPS:
            pr, pi = _cmul(pr, pi, lb_re, lb_im)
    qr, qi = pr, pi
    for c in range(SUBLANES):
        pcr_ref[c], pci_ref[c] = qr, qi
        if c + 1 < SUBLANES:
            qr, qi = _cmul(qr, qi, pr, pi)


def _s5_prep(a_re, a_im, log_dt, b_re_t, b_im_t):
    g, n = a_re.shape
    sd = jax.ShapeDtypeStruct
    return pl.pallas_call(
        _s5_prep_kernel,
        out_shape=[sd((S5_STEPS, g, n), F32), sd((S5_STEPS, g, n), F32),
                   sd((SUBLANES, g, n), F32), sd((SUBLANES, g, n), F32),
                   sd((S5_GROUP, g, n), F32), sd((S5_GROUP, g, n), F32)],
        name="s5_discretise",
    )(a_re, a_im, log_dt.reshape(g, 1), b_re_t, b_im_t)


def _s5_kernel(u_ref, perm_ref, permt_ref, bre_ref, bim_ref, cre_ref, cim_ref, psr_ref, psi_ref,
               pcr_ref, pci_ref, d_ref, s0r_ref, s0i_ref, g_ref, sr_ref, si_ref, xr_sc, xi_sc, *, tt, nh):
    l = u_ref.shape[1]
    w = xr_sc.shape[2]
    cw = u_ref.shape[2] // nh
    ts = tt // SUBLANES
    rows = lax.broadcasted_iota(jnp.int32, (SUBLANES, w), 0)
    bc = lambda v: jnp.broadcast_to(v, (SUBLANES, w))

    def blk(i):
        return slice(i * SUBLANES, (i + 1) * SUBLANES)

    def tile(t, carry):
        t0 = pl.multiple_of(t * tt, tt)
        us = [u_ref[0, pl.ds(t0, tt), h * cw:(h + 1) * cw] for h in range(nh)]
        for h in range(nh):
            up = us[h].astype(BF16)
            if ts > 1:
                up = _dot(perm_ref[...], up).astype(BF16)
            xr_sc[h] = _dot(up, bre_ref[h])
            xi_sc[h] = _dot(up, bim_ref[h])
        ends = []
        for h in range(nh):
            lam_r, lam_i = bc(psr_ref[h, 0:1, :]), bc(psi_ref[h, 0:1, :])

            def pass1(i, st, h=h, lam_r=lam_r, lam_i=lam_i):
                sr, si = _cmul(lam_r, lam_i, st[0], st[1])
                sr, si = sr + xr_sc[h, blk(i), :], si + xi_sc[h, blk(i), :]
                xr_sc[h, blk(i), :] = sr
                xi_sc[h, blk(i), :] = si
                return sr, si

            st = (jnp.zeros((SUBLANES, w), F32), jnp.zeros((SUBLANES, w), F32))
            for i in range(ts):
                st = pass1(i, st)
            ends.append(st)
        new_carry = []
        entries = []
        for h in range(nh):
            tr, ti = ends[h]
            cr, ci = carry[2 * h], carry[2 * h + 1]
            pcr, pci = pcr_ref[h], pci_ref[h]
            for kk in (1, 2, 4):
                keep = rows >= kk
                qr = jnp.where(keep, bc(pcr[kk - 1:kk, :]), 0.0)
                qi = jnp.where(keep, bc(pci[kk - 1:kk, :]), 0.0)
                dr, di = _cmul(qr, qi, pltpu.roll(tr, kk, axis=0), pltpu.roll(ti, kk, axis=0))
                tr, ti = tr + dr, ti + di
            dr, di = _cmul(pcr, pci, bc(cr), bc(ci))
            tr, ti = tr + dr, ti + di
            entries.append((jnp.where(rows == 0, bc(cr), pltpu.roll(tr, 1, axis=0)),
                            jnp.where(rows == 0, bc(ci), pltpu.roll(ti, 1, axis=0))))
            new_carry += [tr[SUBLANES - 1:SUBLANES, :], ti[SUBLANES - 1:SUBLANES, :]]
        for h in range(nh):
            er, ei = entries[h]

            def pass2(i, _, h=h, er=er, ei=ei):
                pr, pi = bc(psr_ref[h, i:i + 1, :]), bc(psi_ref[h, i:i + 1, :])
                dr, di = _cmul(pr, pi, er, ei)
                xr_sc[h, blk(i), :] = xr_sc[h, blk(i), :] + dr
                xi_sc[h, blk(i), :] = xi_sc[h, blk(i), :] + di
                return 0

            for i in range(ts):
                pass2(i, 0)
        for h in range(nh):
            y = _dot(xr_sc[h].astype(BF16), cre_ref[h]) - _dot(xi_sc[h].astype(BF16), cim_ref[h])
            if ts > 1:
                hi = y.astype(BF16)
                mid = (y - hi.astype(F32)).astype(BF16)
                pt = permt_ref[...]
                y = _dot(pt, hi) + _dot(pt, mid)
            y = y + d_ref[h] * us[h]
            g_ref[0, pl.ds(t0, tt), h * cw:(h + 1) * cw] = _gelu_tanh(y).astype(g_ref.dtype)
        return tuple(new_carry)

    init = []
    for h in range(nh):
        init += [s0r_ref[0, h], s0i_ref[0, h]]
    fin = lax.fori_loop(0, l // tt, tile, tuple(init))
    for h in range(nh):
        sr_ref[0, h] = fin[2 * h]
        si_ref[0, h] = fin[2 * h + 1]


def _s5(u, p, s0_re, s0_im, nh=2):
    b, l, d = u.shape
    ngb = p["bbd_re"].shape[0]
    cw = d // ngb
    sw = p["bbd_re"].shape[2]
    tt = min(S5_TILE, l)
    ts = tt // SUBLANES
    assert l % tt == 0 and ts in (1, S5_STEPS) and ngb % nh == 0
    steps_re, steps_im = p["ps_re"][:, :ts], p["ps_im"][:, :ts]
    chunk_re, chunk_im = (p["pc_re"], p["pc_im"]) if ts == S5_STEPS else (p["ps_re"][:, :SUBLANES],
                                                                          p["ps_im"][:, :SUBLANES])
    r = jnp.arange(tt)
    perm = (r[None, :] == ((r % SUBLANES) * ts + r // SUBLANES)[:, None]).astype(BF16)
    state_spec = pl.BlockSpec((1, nh, 1, sw), lambda j, i: (i, j, 0, 0))
    whole = lambda a: pl.BlockSpec(a.shape, lambda j, i: (0,) * a.ndim)
    per_gb = lambda a: pl.BlockSpec((nh,) + a.shape[1:], lambda j, i: (j, 0, 0))
    args = [perm, perm.T, p["bbd_re"], p["bbd_im"], p["cbd_re"], p["cbd_im"], steps_re, steps_im,
            chunk_re, chunk_im, p["d_vec"]]
    return pl.pallas_call(
        functools.partial(_s5_kernel, tt=tt, nh=nh),
        grid=(ngb // nh, b),
        in_specs=[pl.BlockSpec((1, l, nh * cw), lambda j, i: (i, 0, j)), whole(perm), whole(perm)]
                 + [per_gb(a) for a in args[2:]] + [state_spec, state_spec],
        out_specs=[pl.BlockSpec((1, l, nh * cw), lambda j, i: (i, 0, j)), state_spec, state_spec],
        out_shape=[jax.ShapeDtypeStruct((b, l, d), BF16),
                   jax.ShapeDtypeStruct((b, ngb, 1, sw), F32),
                   jax.ShapeDtypeStruct((b, ngb, 1, sw), F32)],
        scratch_shapes=[pltpu.VMEM((nh, tt, sw), F32), pltpu.VMEM((nh, tt, sw), F32)],
        compiler_params=_cparams(("parallel", "parallel")),
        name="s5_scan",
    )(u, *args, s0_re, s0_im)


def _cross_kernel(x_ref, g_ref, wq_ref, qg_ref, k_ref, v_ref, wo_ref, o_ref, att_sc):
    x = x_ref[0]
    xn = _rms(x, g_ref[...]).astype(BF16)
    q = _dot(xn, wq_ref[...].astype(BF16))
    for h in range(MEM_HEADS):
        sl = slice(h * HEAD_DIM, (h + 1) * HEAD_DIM)
        qh = _rms(q[:, sl], qg_ref[...]).astype(BF16)
        s = _dot_nt(qh, k_ref[:, sl].astype(BF16)) * (HEAD_DIM ** -0.5)
        p = jnp.exp(s - jnp.max(s, axis=-1, keepdims=True))
        p = p / jnp.sum(p, axis=-1, keepdims=True)
        att_sc[:, sl] = _dot(p.astype(BF16), v_ref[:, sl].astype(BF16)).astype(BF16)
    o_ref[0] = x + _dot(att_sc[...], wo_ref[...].astype(BF16))


def _cross_attn(x, gain, w_q, q_gain, mem_k, mem_v, w_o, layer, tq=512):
    b, l, d = x.shape
    mt, mw = mem_k.shape[2], mem_k.shape[3]
    tq = min(tq, l)
    return pl.pallas_call(
        _cross_kernel,
        grid=(b, l // tq),
        in_specs=[pl.BlockSpec((1, tq, d), lambda i, j: (i, j, 0)),
                  pl.BlockSpec((1, d), lambda i, j: (0, 0)),
                  pl.BlockSpec((None, d, mw), lambda i, j: (layer, 0, 0)),
                  pl.BlockSpec((1, HEAD_DIM), lambda i, j: (0, 0)),
                  pl.BlockSpec((None, None, mt, mw), lambda i, j: (layer, i, 0, 0)),
                  pl.BlockSpec((None, None, mt, mw), lambda i, j: (layer, i, 0, 0)),
                  pl.BlockSpec((None, mw, d), lambda i, j: (layer, 0, 0))],
        out_specs=pl.BlockSpec((1, tq, d), lambda i, j: (i, j, 0)),
        out_shape=jax.ShapeDtypeStruct((b, l, d), F32),
        scratch_shapes=[pltpu.VMEM((tq, mw), BF16)],
        compiler_params=_cparams(("parallel", "parallel")),
        name="cross_attn",
    )(x, gain.reshape(1, d), w_q, q_gain.reshape(1, HEAD_DIM), mem_k, mem_v, w_o)


def _cross_sample_kernel(x_ref, g_ref, wq_ref, qg_ref, k_ref, v_ref, wo_ref, o_ref, att_sc, *, nb, t):
    x = x_ref[...]
    xn = _rms(x, g_ref[...]).astype(BF16)
    q = _dot(xn, wq_ref[...].astype(BF16))
    qn = [_rms(q[:, h * HEAD_DIM:(h + 1) * HEAD_DIM], qg_ref[...]) for h in range(MEM_HEADS)]
    rows, width = MEM_HEADS * t, k_ref.shape[1]
    row = lax.broadcasted_iota(jnp.int32, (rows, width), 0)
    col = lax.broadcasted_iota(jnp.int32, (rows, width), 1)
    own_head = (col % MEM_HEADS) == (row // t)
    for bi in range(nb):
        q_ht = jnp.concatenate([qh[bi * t:(bi + 1) * t] for qh in qn], axis=0).astype(BF16)
        s = _dot_nt(q_ht, k_ref[bi].astype(BF16)) * (HEAD_DIM ** -0.5)
        s = jnp.where(own_head, s, -jnp.inf)
        p = jnp.exp(s - jnp.max(s, axis=-1, keepdims=True))
        p = p / jnp.sum(p, axis=-1, keepdims=True)
        o = _dot(p.astype(BF16), v_ref[bi].astype(BF16))
        for h in range(MEM_HEADS):
            att_sc[bi * t:(bi + 1) * t, h * HEAD_DIM:(h + 1) * HEAD_DIM] = o[h * t:(h + 1) * t]
    o_ref[...] = x + _dot(att_sc[...].astype(BF16), wo_ref[...].astype(BF16))


def _cross_attn_sample(x, gain, w_q, q_gain, mem_k, mem_v, w_o, layer):
    b, t, d = x.shape
    mrows = mem_k.shape[2]
    mw = w_q.shape[2]
    assert t % SUBLANES == 0
    whole = lambda *shape: pl.BlockSpec(shape, lambda i: (0,) * len(shape))
    out = pl.pallas_call(
        functools.partial(_cross_sample_kernel, nb=b, t=t),
        grid=(1,),
        in_specs=[whole(b * t, d), whole(1, d),
                  pl.BlockSpec((None, d, mw), lambda i: (layer, 0, 0)),
                  whole(1, HEAD_DIM),
                  pl.BlockSpec((None, b, mrows, HEAD_DIM), lambda i: (layer, 0, 0, 0)),
                  pl.BlockSpec((None, b, mrows, HEAD_DIM), lambda i: (layer, 0, 0, 0)),
                  pl.BlockSpec((None, mw, d), lambda i: (layer, 0, 0))],
        out_specs=whole(b * t, d),
        out_shape=jax.ShapeDtypeStruct((b * t, d), F32),
        scratch_shapes=[pltpu.VMEM((b * t, mw), F32)],
        compiler_params=_cparams(("arbitrary",)),
        name="cross_attn_sample",
    )(x.reshape(b * t, d), gain.reshape(1, d), w_q, q_gain.reshape(1, HEAD_DIM), mem_k, mem_v, w_o)
    return out.reshape(b, t, d)


def _mem_kv_kernel(x_ref, g_ref, w_ref, hg_ref, k_ref, v_ref, xs_ref):
    j = pl.program_id(1)

    @pl.when(j == 0)
    def _():
        xs_ref[...] = _rms(x_ref[...], g_ref[...]).astype(BF16)

    acc = _dot(xs_ref[...], w_ref[...].astype(BF16))

    @pl.when(j == 0)
    def _():
        for c in range(acc.shape[1] // HEAD_DIM):
            sl = slice(c * HEAD_DIM, (c + 1) * HEAD_DIM)
            k_ref[:, sl] = _rms(acc[:, sl], hg_ref[...])

    @pl.when(j == 1)
    def _():
        v_ref[...] = acc


def _mem_kv(mem, gains, w_mkv, k_gains):
    m, d = mem.shape
    depth = w_mkv.shape[0]
    mw = w_mkv.shape[2] // 2
    sd = jax.ShapeDtypeStruct
    return pl.pallas_call(
        _mem_kv_kernel,
        grid=(depth, 2),
        in_specs=[pl.BlockSpec((m, d), lambda i, j: (0, 0)),
                  pl.BlockSpec((None, 1, d), lambda i, j: (i, 0, 0)),
                  pl.BlockSpec((None, d, mw), lambda i, j: (i, 0, j)),
                  pl.BlockSpec((None, 1, HEAD_DIM), lambda i, j: (i, 0, 0))],
        out_specs=[pl.BlockSpec((None, m, mw), lambda i, j: (i, 0, 0)),
                   pl.BlockSpec((None, m, mw), lambda i, j: (i, 0, 0))],
        out_shape=[sd((depth, m, mw), F32), sd((depth, m, mw), F32)],
        scratch_shapes=[pltpu.VMEM((m, d), BF16)],
        compiler_params=_cparams(("arbitrary", "arbitrary")),
        name="mem_kv_proj",
    )(mem, gains.reshape(depth, 1, d), w_mkv, k_gains.reshape(depth, 1, HEAD_DIM))


def _lane_vec(vals, lane0):
    v = jnp.zeros((1, LANES), F32)
    return v.at[0, lane0:lane0 + vals.shape[0]].set(vals.astype(F32))


def _head_expand():
    rows = jnp.arange(LANES)[None, :, None]
    cols = jnp.arange(SSD_INNER // SSD_GROUPS)[None, None, :]
    g = jnp.arange(SSD_GROUPS)[:, None, None]
    return (rows == DT_LANE0 + g * (SSD_HEADS // SSD_GROUPS) + cols // SSD_HD).astype(BF16)


def _block_diag(m, reps):
    nb, rows, c = m.shape
    t = jnp.tile(m, (1, 1, reps))
    rb = jnp.arange(rows)[:, None] // (rows // reps)
    cb = jnp.arange(reps * c)[None, :] // c
    return jnp.where((rb == cb)[None], t, 0.0).astype(BF16)


def _pad_rows(x, rows):
    return jnp.pad(x, ((0, 0), (0, rows - x.shape[1]), (0, 0)))


def _trunk(x, W, mem_k, mem_v, conv0, ssm0, s5_re0, s5_im0, fox_cache):
    b, l, d = x.shape
    t = b * l
    depth = W["norm_mix"].shape[0]
    lp = max(l, SSD_CHUNK)
    e_mat = _head_expand()
    fl, hs, bufs, srs, sis = [], [], [], [], []
    n_even = W["w_tail"].shape[0]
    kv_stacks = (jnp.zeros((n_even, t, FOX_W), F32), jnp.zeros((n_even, t, FOX_W), F32))
    x2 = x.reshape(t, d)
    for i in range(depth):
        j = i // 2
        if i % 2 == 0:
            q_bf, kf_all, k_bf, vf_all, v_bf = _inproj_qkv(x2, W["norm_mix"][i], W["w_in_even_t"], j, n_even,
                                                           W["fox_q_norm"][j], W["fox_k_norm"][j], kv_stacks)
            kv_stacks = (kf_all, vf_all)
            zx, tail = _inproj_zx(x2, W["norm_mix"][i], W["w_in_even_t"], 3 * FOX_W + FOX_HEADS,
                                  SSD_INNER + CONV_DIM, W["w_tail"], j)
            zx3 = zx.reshape(b, l, -1)
            tail3 = tail.reshape(b, l, LANES)
            tail_p = _pad_rows(tail3, lp) if lp != l else tail3
            lf, negc, c_tm = _forget(tail_p, _lane_vec(W["fox_b_forget"][j], 0))
            lf = lf[:, :l]
            if fox_cache is None:
                o_fox = _fox_prompt(q_bf.reshape(b, l, FOX_W), k_bf.reshape(b, l, FOX_W),
                                    v_bf.reshape(b, l, FOX_W), negc).reshape(t, FOX_W)
            else:
                k_pool, v_pool, loc, tot, page_table = fox_cache
                rows = FOX_HEADS * l
                q_ht = jnp.transpose(q_bf.reshape(b, l, FOX_HEADS, HEAD_DIM), (0, 2, 1, 3)).reshape(b, rows, HEAD_DIM)
                o_ht = _fox_decode(q_ht, _pad_rows(kf_all[j].reshape(b, rows, HEAD_DIM), LANES),
                                   _pad_rows(vf_all[j].reshape(b, rows, HEAD_DIM), LANES),
                                   _pad_rows(c_tm[:, :l].reshape(b, rows, 1), LANES).reshape(b, 1, LANES),
                                   k_pool, v_pool, loc, tot, page_table, j, l)
                o_fox = jnp.transpose(o_ht.reshape(b, FOX_HEADS, l, HEAD_DIM), (0, 2, 1, 3)).reshape(t, FOX_W)
            xc, new_buf = _conv(zx3, SSD_INNER, conv0[j], W["ssd_conv_w"][j], W["ssd_conv_b"][j])
            if lp != l:
                xc_p, z_p = _pad_rows(xc, lp), _pad_rows(zx3[:, :, :SSD_INNER], lp)
            else:
                xc_p, z_p = xc, zx3
            yg, h_last = _ssd(xc_p, tail_p, z_p, _lane_vec(W["ssd_dt_bias"][j], DT_LANE0),
                              _lane_vec(W["ssd_A_log"][j], DT_LANE0),
                              jnp.repeat(W["ssd_D"][j].astype(F32), SSD_HD).reshape(1, SSD_INNER),
                              e_mat, ssm0[j].reshape(b, SSD_INNER, SSD_STATE), l)
            yg = yg[:, :l].reshape(t, SSD_INNER)
            x2 = _outproj_even(o_fox, yg, W["ssd_norm"][j], W["w_out_even"], j, x2)
            fl.append(lf)
            hs.append(h_last.reshape(b, SSD_HEADS, SSD_HD, SSD_STATE))
            bufs.append(new_buf)
        else:
            (u,) = _dense(x2, W["w_in_odd"], layer=j, n_cols=d, tn=512, pro="norm", gain=W["norm_mix"][i],
                          name="inproj_odd", tm_max=TM_WIDE, x_single=True)
            p = W["s5_packed"][j]
            ngb = p["bbd_re"].shape[0]
            g_bf, s_re, s_im = _s5(u.reshape(b, l, d), p,
                                   s5_re0[j].reshape(b, ngb, 1, -1), s5_im0[j].reshape(b, ngb, 1, -1))
            (x2,) = _dense(g_bf.reshape(t, d), W["s5_w_glu"], layer=j, n_cols=d, tn=512, col0=0, col0_2=d,
                           epi="glu_res", res=x2, name="s5_glu_out")
            srs.append(s_re.reshape(b, -1, S5_STATE))
            sis.append(s_im.reshape(b, -1, S5_STATE))
        cross = _cross_attn if fox_cache is None else _cross_attn_sample
        x3 = cross(x2.reshape(b, l, d), W["norm_cross"][i], W["w_mq"], W["mem_q_norm"][i],
                   mem_k, mem_v, W["w_mo"], i)
        x2 = x3.reshape(t, d)
        x2 = _ffn(x2, W["norm_ffn"][i], W["w_ffn_up"], W["w_ffn_down"], i)
    fk, fv = (a.reshape(a.shape[0], b, l, FOX_HEADS, HEAD_DIM) for a in kv_stacks)
    return (x2.reshape(b, l, d), fk, fv, jnp.stack(fl), jnp.stack(hs), jnp.stack(bufs),
            jnp.stack(srs), jnp.stack(sis))


def _pack_s5(a_re, a_im, b_re, b_im, c_re, c_im, d_skip, log_dt):
    g, n, k = b_re.shape
    ngb = g // S5_GB
    ps_re, ps_im, pc_re, pc_im, bb_re, bb_im = _s5_prep(
        a_re.astype(F32), a_im.astype(F32), log_dt.astype(F32),
        jnp.transpose(b_re, (2, 0, 1)).astype(F32), jnp.transpose(b_im, (2, 0, 1)).astype(F32))

    def bmat(bb):
        return _block_diag(jnp.transpose(bb, (1, 0, 2)).reshape(ngb, S5_GB * k, n), S5_GB)

    def cmat(cm):
        return _block_diag(jnp.transpose(cm.astype(F32), (0, 2, 1)).reshape(ngb, S5_GB * n, k), S5_GB)

    def rows(pw):
        return jnp.transpose(pw.reshape(pw.shape[0], ngb, S5_GB * n), (1, 0, 2))

    return dict(bbd_re=bmat(bb_re), bbd_im=bmat(bb_im), cbd_re=cmat(c_re), cbd_im=cmat(c_im),
                ps_re=rows(ps_re), ps_im=rows(ps_im), pc_re=rows(pc_re), pc_im=rows(pc_im),
                d_vec=d_skip.astype(F32).reshape(ngb, 1, S5_GB * k))


def kernel(x_prompt, x_sample, mem_prompt, cache_fox_k, cache_fox_v, cache_fox_logf, cache_mem_k, cache_mem_v,
           state_ssd, state_conv, state_s5_re, state_s5_im, page_table,
           norm_mix, norm_cross, norm_mem, norm_ffn,
           w_in_even, fox_b_forget, fox_q_norm, fox_k_norm, ssd_conv_w, ssd_conv_b, ssd_dt_bias, ssd_A_log,
           ssd_D, ssd_norm, w_out_even,
           w_in_odd, s5_A_re, s5_A_im, s5_B_re, s5_B_im, s5_C_re, s5_C_im, s5_D, s5_log_dt, s5_w_glu,
           w_mq, w_mkv, mem_q_norm, mem_k_norm, w_mo, w_ffn_up, w_ffn_down):
    depth = norm_mix.shape[0]
    n_even, n_odd = w_in_even.shape[0], w_in_odd.shape[0]
    b, l, d = x_prompt.shape
    z0 = 3 * FOX_W + FOX_HEADS
    dt0 = z0 + SSD_INNER + CONV_DIM
    w_t = jnp.swapaxes(w_in_even, 1, 2)
    W = {
        "norm_mix": norm_mix, "norm_cross": norm_cross, "norm_ffn": norm_ffn,
        "w_in_even_t": w_t,
        "w_tail": jnp.concatenate([w_t[:, 3 * FOX_W:z0], w_t[:, dt0:dt0 + SSD_HEADS],
                                   jnp.zeros((n_even, LANES - FOX_HEADS - SSD_HEADS, d), w_t.dtype)], axis=1),
        "fox_b_forget": fox_b_forget, "fox_q_norm": fox_q_norm, "fox_k_norm": fox_k_norm,
        "ssd_conv_w": ssd_conv_w, "ssd_conv_b": ssd_conv_b, "ssd_dt_bias": ssd_dt_bias, "ssd_A_log": ssd_A_log,
        "ssd_D": ssd_D, "ssd_norm": ssd_norm, "w_out_even": w_out_even,
        "w_in_odd": w_in_odd, "s5_w_glu": s5_w_glu,
        "s5_packed": [_pack_s5(s5_A_re[j], s5_A_im[j], s5_B_re[j], s5_B_im[j], s5_C_re[j], s5_C_im[j],
                               s5_D[j], s5_log_dt[j]) for j in range(n_odd)],
        "w_mq": w_mq, "mem_q_norm": mem_q_norm, "w_mo": w_mo, "w_ffn_up": w_ffn_up, "w_ffn_down": w_ffn_down,
    }
    mt = mem_prompt.shape[1]
    mw = w_mkv.shape[2] // 2
    mem_k_p, mem_v_p = (a.reshape(depth, b, mt, mw) for a in
                        _mem_kv(mem_prompt.reshape(b * mt, d), norm_mem, w_mkv, mem_k_norm))
    n_grp = s5_A_re.shape[1]
    (y_prompt, fox_k_p, fox_v_p, fox_logf_p, ssd_p, conv_p, s5_re_p, s5_im_p) = _trunk(
        x_prompt, W, mem_k_p, mem_v_p,
        jnp.zeros((n_even, b, SSD_CONV - 1, CONV_DIM), F32),
        jnp.zeros((n_even, b, SSD_HEADS, SSD_HD, SSD_STATE), F32),
        jnp.zeros((n_odd, b, n_grp, S5_STATE), F32),
        jnp.zeros((n_odd, b, n_grp, S5_STATE), F32),
        None)
    db = x_sample.shape[0]
    n_pool, page = cache_fox_k.shape[1], cache_fox_k.shape[2]
    assert page == LANES
    width = page * FOX_HEADS
    lf_t = jnp.transpose(cache_fox_logf.astype(F32), (0, 2, 1, 3)).reshape(n_even, page, n_pool * FOX_HEADS)
    incl = _pool_cumsum(lf_t).reshape(n_even, page, n_pool, FOX_HEADS)
    loc = jnp.transpose(incl, (0, 2, 1, 3)).reshape(n_even, n_pool, 1, width)
    tot = jnp.tile(incl[:, page - 1], (1, 1, page)).reshape(n_even, n_pool, 1, width)
    fox_cache = (cache_fox_k.reshape(n_even, n_pool, width, HEAD_DIM),
                 cache_fox_v.reshape(n_even, n_pool, width, HEAD_DIM), loc, tot, page_table)
    (y_sample, fox_k_s, fox_v_s, fox_logf_s, ssd_s, conv_s, s5_re_s, s5_im_s) = _trunk(
        x_sample, W, cache_mem_k.reshape(depth, db, mt * MEM_HEADS, HEAD_DIM),
        cache_mem_v.reshape(depth, db, mt * MEM_HEADS, HEAD_DIM),
        state_conv, state_ssd, state_s5_re, state_s5_im, fox_cache)
    hd = mw // MEM_HEADS
    return (y_prompt, y_sample,
            fox_k_p, fox_v_p, fox_logf_p,
            mem_k_p.reshape(depth, b, mt, MEM_HEADS, hd), mem_v_p.reshape(depth, b, mt, MEM_HEADS, hd),
            ssd_p, conv_p, s5_re_p, s5_im_p,
            fox_k_s, fox_v_s, fox_logf_s, ssd_s, conv_s, s5_re_s, s5_im_s)
```

```python
import functools
import math

import jax
import jax.numpy as jnp
from jax import lax
from jax.experimental import pallas as pl
from jax.experimental.pallas import tpu as pltpu

F32 = jnp.float32
BF16 = jnp.bfloat16
EPS = 1e-6
LANES = 128
SUBLANES = 8
VMEM_LIMIT_BYTES = 56 * 1024 * 1024

FOX_HEADS = 8
HEAD_DIM = 128
FOX_W = FOX_HEADS * HEAD_DIM
SSD_HEADS = 32
SSD_HD = 64
SSD_GROUPS = 4
SSD_STATE = 128
SSD_CHUNK = 128
SSD_INNER = SSD_HEADS * SSD_HD
SSD_CONV = 4
CONV_DIM = SSD_INNER + 2 * SSD_GROUPS * SSD_STATE
DT_LANE0 = FOX_HEADS
S5_GROUP = 16
S5_STATE = 64
S5_GB = 16
MEM_HEADS = 4
HIGHEST = lax.Precision.HIGHEST


def _cparams(sem):
    return pltpu.CompilerParams(dimension_semantics=sem, vmem_limit_bytes=VMEM_LIMIT_BYTES)


def _gelu_tanh(x):
    return 0.5 * x * (1.0 + jnp.tanh(math.sqrt(2.0 / math.pi) * (x + 0.044715 * x * x * x)))


def _softplus(x):
    return jnp.maximum(x, 0.0) + jnp.log1p(jnp.exp(-jnp.abs(x)))


def _silu(x):
    return x * jax.nn.sigmoid(x)


def _rms(x, gain):
    return x * lax.rsqrt(jnp.mean(x * x, axis=-1, keepdims=True) + EPS) * gain


def _dot(a, b):
    return jnp.dot(a, b, preferred_element_type=F32)


def _dot_nt(a, b):
    return lax.dot_general(a, b, (((1,), (1,)), ((), ())), preferred_element_type=F32)


def _dot_exact(a, b):
    return jnp.dot(a, b, preferred_element_type=F32, precision=HIGHEST)


def _dense_kernel(*refs, pro, epi, n_out):
    it = iter(refs)
    x_ref = next(it)
    gain_ref = next(it) if pro == "norm" else None
    w_ref = next(it)
    w2_ref = next(it) if epi in ("swiglu", "glu_res") else None
    res_ref = next(it) if epi in ("residual", "glu_res") else None
    hg_ref = next(it) if epi == "headnorm" else None
    out_refs = [next(it) for _ in range(n_out)]
    xs_ref = next(it) if pro != "none" else None

    if pro != "none":
        @pl.when(pl.program_id(1) == 0)
        def _():
            xf = x_ref[...].astype(F32)
            if pro == "norm":
                xf = _rms(xf, gain_ref[...])
            elif pro == "gelu":
                xf = _gelu_tanh(xf)
            xs_ref[...] = xf.astype(BF16)
        lhs = xs_ref[...]
    else:
        lhs = x_ref[...]

    acc = _dot(lhs, w_ref[...].astype(BF16))
    if epi == "swiglu":
        acc = _silu(acc) * _dot(lhs, w2_ref[...].astype(BF16))
    elif epi == "glu_res":
        acc = res_ref[...] + acc * jax.nn.sigmoid(_dot(lhs, w2_ref[...].astype(BF16)))
    elif epi == "residual":
        acc = res_ref[...] + acc

    if epi == "headnorm":
        for c in range(acc.shape[1] // HEAD_DIM):
            sl = slice(c * HEAD_DIM, (c + 1) * HEAD_DIM)
            blk = _rms(acc[:, sl], hg_ref[...])
            for o in out_refs:
                o[:, sl] = blk.astype(o.dtype)
    else:
        for o in out_refs:
            o[...] = acc.astype(o.dtype)


def _wspec(w, layer, k, tn, blk0):
    if w.ndim == 2:
        return pl.BlockSpec((k, tn), lambda i, j: (0, blk0 + j))
    return pl.BlockSpec((None, k, tn), lambda i, j: (layer, 0, blk0 + j))


def _dense(x, w, *, n_cols, tn, name, layer=0, col0=0, col0_2=None, pro="none", gain=None, epi="plain",
           res=None, head_gain=None, out_dtypes=(F32,), tm_max=1024, x_single=False):
    m, k = x.shape
    tm = min(tm_max, m)
    assert m % tm == 0 and n_cols % tn == 0 and col0 % tn == 0
    b0 = col0 // tn
    xmode = dict(pipeline_mode=pl.Buffered(1)) if x_single else {}
    in_specs = [pl.BlockSpec((tm, k), lambda i, j: (i, 0), **xmode)]
    args = [x]
    if pro == "norm":
        in_specs.append(pl.BlockSpec((1, k), lambda i, j: (0, 0)))
        args.append(gain.reshape(1, k).astype(F32))
    in_specs.append(_wspec(w, layer, k, tn, b0))
    args.append(w)
    if epi in ("swiglu", "glu_res"):
        assert col0_2 % tn == 0
        in_specs.append(_wspec(w, layer, k, tn, col0_2 // tn))
        args.append(w)
    if epi in ("residual", "glu_res"):
        in_specs.append(pl.BlockSpec((tm, tn), lambda i, j: (i, j)))
        args.append(res)
    if epi == "headnorm":
        in_specs.append(pl.BlockSpec((1, HEAD_DIM), lambda i, j: (0, 0)))
        args.append(head_gain.reshape(1, HEAD_DIM).astype(F32))
    out_shape = [jax.ShapeDtypeStruct((m, n_cols), dt) for dt in out_dtypes]
    out_specs = [pl.BlockSpec((tm, tn), lambda i, j: (i, j)) for _ in out_dtypes]
    scratch = [pltpu.VMEM((tm, k), BF16)] if pro != "none" else []
    outs = pl.pallas_call(
        functools.partial(_dense_kernel, pro=pro, epi=epi, n_out=len(out_dtypes)),
        grid=(m // tm, n_cols // tn),
        in_specs=in_specs, out_specs=out_specs, out_shape=out_shape,
        scratch_shapes=scratch,
        compiler_params=_cparams(("parallel", "arbitrary")),
        name=name,
    )(*args)
    return outs


def _ffn_kernel(x_ref, g_ref, wg_ref, wu_ref, wd_ref, o_ref, xs_ref):
    @pl.when(pl.program_id(1) == 0)
    def _():
        x = x_ref[...]
        xs_ref[...] = _rms(x, g_ref[...]).astype(BF16)
        o_ref[...] = x
    xs = xs_ref[...]
    hid = _silu(_dot(xs, wg_ref[...].astype(BF16))) * _dot(xs, wu_ref[...].astype(BF16))
    o_ref[...] += _dot(hid.astype(BF16), wd_ref[...].astype(BF16))


def _ffn(x, gain, w_up, w_down, layer, th=256, tm_max=1024):
    m, d = x.shape
    hid = w_down.shape[1]
    tm = min(tm_max, m)
    nth = hid // th
    assert m % tm == 0 and hid % th == 0
    return pl.pallas_call(
        _ffn_kernel,
        grid=(m // tm, nth),
        in_specs=[pl.BlockSpec((tm, d), lambda i, j: (i, 0)),
                  pl.BlockSpec((1, d), lambda i, j: (0, 0)),
                  pl.BlockSpec((None, d, th), lambda i, j: (layer, 0, j)),
                  pl.BlockSpec((None, d, th), lambda i, j: (layer, 0, nth + j)),
                  pl.BlockSpec((None, th, d), lambda i, j: (layer, j, 0))],
        out_specs=pl.BlockSpec((tm, d), lambda i, j: (i, 0)),
        out_shape=jax.ShapeDtypeStruct((m, d), F32),
        scratch_shapes=[pltpu.VMEM((tm, d), BF16)],
        compiler_params=_cparams(("parallel", "arbitrary")),
        name="ffn",
    )(x, gain.reshape(1, d), w_up, w_up, w_down)


TM_WIDE = 2048


def _inproj_qkv_kernel(x_ref, g_ref, w_ref, qg_ref, kg_ref, *rest, nq):
    q_ref, kf_ref, kb_ref, vf_ref, vb_ref, xs_ref = rest[-6:]
    j = pl.program_id(1)

    @pl.when(j == 0)
    def _():
        xs_ref[...] = _rms(x_ref[...], g_ref[...]).astype(BF16)

    def headnorm(acc, gain, outs):
        for c in range(acc.shape[1] // HEAD_DIM):
            sl = slice(c * HEAD_DIM, (c + 1) * HEAD_DIM)
            blk = _rms(acc[:, sl], gain)
            for o in outs:
                o[:, sl] = blk.astype(o.dtype)

    @pl.when(j < nq)
    def _():
        headnorm(_dot_nt(xs_ref[...], w_ref[...].astype(BF16)), qg_ref[...] * (HEAD_DIM ** -0.5), (q_ref,))

    @pl.when((j >= nq) & (j < 2 * nq))
    def _():
        headnorm(_dot_nt(xs_ref[...], w_ref[...].astype(BF16)), kg_ref[...], (kf_ref, kb_ref))

    @pl.when(j >= 2 * nq)
    def _():
        acc = _dot_nt(xs_ref[...], w_ref[...].astype(BF16))
        vf_ref[...] = acc
        vb_ref[...] = acc.astype(BF16)


def _inproj_qkv(x, gain, w_t, layer, n_layers, q_gain, k_gain, stacks, tn=256):
    m, k = x.shape
    tm = min(TM_WIDE, m)
    nq = FOX_W // tn
    tile = lambda lo: pl.BlockSpec((tm, tn), lambda i, j: (i, jnp.clip(j - lo, 0, nq - 1)))
    stile = lambda lo: pl.BlockSpec((None, tm, tn), lambda i, j: (layer, i, jnp.clip(j - lo, 0, nq - 1)))
    sd = jax.ShapeDtypeStruct
    in_specs = [pl.BlockSpec((tm, k), lambda i, j: (i, 0), pipeline_mode=pl.Buffered(1)),
                pl.BlockSpec((1, k), lambda i, j: (0, 0)),
                pl.BlockSpec((None, tn, k), lambda i, j: (layer, j, 0)),
                pl.BlockSpec((1, HEAD_DIM), lambda i, j: (0, 0)),
                pl.BlockSpec((1, HEAD_DIM), lambda i, j: (0, 0))]
    args = [x, gain.reshape(1, k), w_t, q_gain.reshape(1, HEAD_DIM), k_gain.reshape(1, HEAD_DIM)]
    in_specs += [pl.BlockSpec(memory_space=pl.ANY), pl.BlockSpec(memory_space=pl.ANY)]
    aliases = {len(args): 1, len(args) + 1: 3}
    args += list(stacks)
    return pl.pallas_call(
        functools.partial(_inproj_qkv_kernel, nq=nq),
        grid=(m // tm, 3 * nq),
        in_specs=in_specs,
        out_specs=[tile(0), stile(nq), tile(nq), stile(2 * nq), tile(2 * nq)],
        out_shape=[sd((m, FOX_W), BF16), sd((n_layers, m, FOX_W), F32), sd((m, FOX_W), BF16),
                   sd((n_layers, m, FOX_W), F32), sd((m, FOX_W), BF16)],
        scratch_shapes=[pltpu.VMEM((tm, k), BF16)],
        input_output_aliases=aliases,
        compiler_params=_cparams(("parallel", "arbitrary")),
        name="inproj_qkv",
    )(*args)


def _inproj_zx_kernel(x_ref, g_ref, wzx_ref, wt_ref, zx_ref, tail_ref, xs_ref, *, nzx):
    j = pl.program_id(1)

    @pl.when(j == 0)
    def _():
        xs_ref[...] = _rms(x_ref[...], g_ref[...]).astype(BF16)

    @pl.when(j < nzx)
    def _():
        zx_ref[...] = _dot_nt(xs_ref[...], wzx_ref[0].astype(BF16))

    @pl.when(j == nzx)
    def _():
        tail_ref[...] = _dot_nt(xs_ref[...], wt_ref[...].astype(BF16))


def _inproj_zx(x, gain, w_t, row0, n_zx_cols, w_tail, layer, tn=512):
    m, k = x.shape
    tm = min(TM_WIDE, m)
    nzx = n_zx_cols // tn
    w_zx = w_t
    return pl.pallas_call(
        functools.partial(_inproj_zx_kernel, nzx=nzx),
        grid=(m // tm, nzx + 1),
        in_specs=[pl.BlockSpec((tm, k), lambda i, j: (i, 0), pipeline_mode=pl.Buffered(1)),
                  pl.BlockSpec((1, k), lambda i, j: (0, 0)),
                  pl.BlockSpec((pl.Element(1), pl.Element(tn), pl.Element(k)),
                               lambda i, j: (layer, pl.multiple_of(row0 + jnp.minimum(j, nzx - 1) * tn, SUBLANES), 0)),
                  pl.BlockSpec((None, LANES, k), lambda i, j: (layer, 0, 0), pipeline_mode=pl.Buffered(1))],
        out_specs=[pl.BlockSpec((tm, tn), lambda i, j: (i, jnp.minimum(j, nzx - 1))),
                   pl.BlockSpec((tm, LANES), lambda i, j: (i, 0))],
        out_shape=[jax.ShapeDtypeStruct((m, nzx * tn), F32), jax.ShapeDtypeStruct((m, LANES), F32)],
        scratch_shapes=[pltpu.VMEM((tm, k), BF16)],
        compiler_params=_cparams(("parallel", "arbitrary")),
        name="inproj_zx",
    )(x, gain.reshape(1, k), w_zx, w_tail)


def _outproj_kernel(o_ref, y_ref, g_ref, w_ref, res_ref, out_ref, yn_ref):
    k1 = o_ref.shape[1]

    @pl.when(pl.program_id(1) == 0)
    def _():
        yn_ref[...] = _rms(y_ref[...], g_ref[...]).astype(BF16)
    acc = _dot(o_ref[...], w_ref[:k1, :].astype(BF16)) + _dot(yn_ref[...], w_ref[k1:, :].astype(BF16))
    out_ref[...] = res_ref[...] + acc


def _outproj_even(o_fox, yg, gain, w_out, layer, res, tn=256, tm_max=TM_WIDE):
    m, d = res.shape
    tm = min(tm_max, m)
    k1, k2 = o_fox.shape[1], yg.shape[1]
    return pl.pallas_call(
        _outproj_kernel,
        grid=(m // tm, d // tn),
        in_specs=[
            pl.BlockSpec((tm, k1), lambda i, j: (i, 0)),
            pl.BlockSpec((tm, k2), lambda i, j: (i, 0), pipeline_mode=pl.Buffered(1)),
            pl.BlockSpec((1, k2), lambda i, j: (0, 0)),
            pl.BlockSpec((None, k1 + k2, tn), lambda i, j: (layer, 0, j)),
            pl.BlockSpec((tm, tn), lambda i, j: (i, j)),
        ],
        out_specs=pl.BlockSpec((tm, tn), lambda i, j: (i, j)),
        out_shape=jax.ShapeDtypeStruct((m, d), F32),
        scratch_shapes=[pltpu.VMEM((tm, k2), BF16)],
        compiler_params=_cparams(("parallel", "arbitrary")),
        name="outproj_even",
    )(o_fox, yg, gain.reshape(1, k2).astype(F32), w_out, res)


def _lane_cumsum(x):
    lane = lax.broadcasted_iota(jnp.int32, x.shape, 1)
    k = 1
    while k < LANES:
        x = x + jnp.where(lane >= k, pltpu.roll(x, k, axis=1), 0.0)
        k *= 2
    return x


def _forget_kernel(raw_ref, b_ref, lf_ref, negc_ref, ctm_ref):
    lp = raw_ref.shape[1]
    lf = -_softplus(-(raw_ref[0] + b_ref[...]))
    lf_ref[0] = lf[:, :FOX_HEADS]
    row_i = lax.broadcasted_iota(jnp.int32, (LANES, LANES), 0)
    col_i = lax.broadcasted_iota(jnp.int32, (LANES, LANES), 1)
    tri = (col_i <= row_i).astype(F32)
    ctm_ref[0] = _dot_exact(tri, lf[:LANES, :])[:, :FOX_HEADS]
    carry = jnp.zeros((FOX_HEADS, 1), F32)
    for c in range(lp // LANES):
        blk = lf[c * LANES:(c + 1) * LANES, :].T[:FOX_HEADS, :]
        cs = _lane_cumsum(blk) + carry
        negc_ref[0, :, c * LANES:(c + 1) * LANES] = -cs
        carry = cs[:, LANES - 1:LANES]


def _forget(raw, b_vec):
    b, lp, _ = raw.shape
    return pl.pallas_call(
        _forget_kernel,
        grid=(b,),
        in_specs=[pl.BlockSpec((1, lp, LANES), lambda i: (i, 0, 0)),
                  pl.BlockSpec((1, LANES), lambda i: (0, 0))],
        out_specs=[pl.BlockSpec((1, lp, FOX_HEADS), lambda i: (i, 0, 0)),
                   pl.BlockSpec((1, FOX_HEADS, lp), lambda i: (i, 0, 0)),
                   pl.BlockSpec((1, LANES, FOX_HEADS), lambda i: (i, 0, 0))],
        out_shape=[jax.ShapeDtypeStruct((b, lp, FOX_HEADS), F32),
                   jax.ShapeDtypeStruct((b, FOX_HEADS, lp), F32),
                   jax.ShapeDtypeStruct((b, LANES, FOX_HEADS), F32)],
        compiler_params=_cparams(("parallel",)),
        name="forget_gates",
    )(raw, b_vec)


ROW_CHUNK = 64


def _fox_prompt_kernel(q_ref, k_ref, v_ref, nb_ref, o_ref, m_sc, acc_sc, s_sc, p_sc, a_sc, *, tq, tk):
    qi = pl.program_id(1)
    kj = pl.program_id(2)

    @pl.when(kj == 0)
    def _():
        m_sc[...] = jnp.full(m_sc.shape, -jnp.inf, F32)
        acc_sc[...] = jnp.zeros(acc_sc.shape, F32)

    active = kj * tk <= qi * tq + (tq - 1)
    crosses_diagonal = kj * tk + (tk - 1) > qi * tq

    def step(masked):
        rc = min(ROW_CHUNK, tq)
        ones = jnp.ones((tk, HEAD_DIM), BF16)
        if masked:
            row0 = qi * tq + lax.broadcasted_iota(jnp.int32, (rc, tk), 0)
            col = kj * tk + lax.broadcasted_iota(jnp.int32, (rc, tk), 1)
        for h in range(FOX_HEADS):
            sl = slice(h * HEAD_DIM, (h + 1) * HEAD_DIM)
            slot = h % 2
            s_sc[slot] = _dot_nt(q_ref[0, :, sl], k_ref[0, :, sl])
            nb = nb_ref[0, h:h + 1, :]
            for c in range(tq // rc):
                r = slice(c * rc, (c + 1) * rc)
                s = s_sc[slot, r, :] + nb
                if masked:
                    s = jnp.where(col <= row0 + c * rc, s, -jnp.inf)
                m_old = m_sc[h, r, :]
                m_new = jnp.maximum(m_old, jnp.max(s, axis=-1, keepdims=True))
                p_sc[slot, r, :] = jnp.exp(s - m_new).astype(BF16)
                a_sc[slot, r, :] = jnp.exp(m_old - m_new)
                m_sc[h, r, :] = m_new
            pv = _dot(p_sc[slot], jnp.concatenate([v_ref[0, :, sl], ones], axis=1))
            acc_sc[h] = a_sc[slot] * acc_sc[h] + pv

    pl.when(active & crosses_diagonal)(functools.partial(step, True))
    pl.when(active & jnp.logical_not(crosses_diagonal))(functools.partial(step, False))

    @pl.when(kj == pl.num_programs(2) - 1)
    def _():
        for h in range(FOX_HEADS):
            acc = acc_sc[h]
            o_ref[0, :, h * HEAD_DIM:(h + 1) * HEAD_DIM] = (acc[:, :HEAD_DIM] / acc[:, HEAD_DIM:]).astype(o_ref.dtype)


def _fox_prompt(q, k, v, negc, tq=512, tk=512):
    b, l, _ = q.shape
    tq, tk = min(tq, l), min(tk, l)
    nq, nk = l // tq, l // tk

    def last_needed(qi, kj):
        return jnp.minimum(kj, (qi * tq + tq - 1) // tk)

    return pl.pallas_call(
        functools.partial(_fox_prompt_kernel, tq=tq, tk=tk),
        grid=(b, nq, nk),
        in_specs=[pl.BlockSpec((1, tq, FOX_W), lambda bi, qi, kj: (bi, qi, 0)),
                  pl.BlockSpec((1, tk, FOX_W), lambda bi, qi, kj: (bi, last_needed(qi, kj), 0)),
                  pl.BlockSpec((1, tk, FOX_W), lambda bi, qi, kj: (bi, last_needed(qi, kj), 0)),
                  pl.BlockSpec((1, FOX_HEADS, tk), lambda bi, qi, kj: (bi, 0, last_needed(qi, kj)))],
        out_specs=pl.BlockSpec((1, tq, FOX_W), lambda bi, qi, kj: (bi, qi, 0)),
        out_shape=jax.ShapeDtypeStruct((b, l, FOX_W), BF16),
        scratch_shapes=[pltpu.VMEM((FOX_HEADS, tq, 1), F32),
                        pltpu.VMEM((FOX_HEADS, tq, 2 * HEAD_DIM), F32),
                        pltpu.VMEM((2, tq, tk), F32), pltpu.VMEM((2, tq, tk), BF16),
                        pltpu.VMEM((2, tq, 1), F32)],
        compiler_params=_cparams(("parallel", "parallel", "arbitrary")),
        name="fox_prompt",
    )(q, k, v, negc)


def _pool_cumsum_kernel(x_ref, o_ref):
    n = x_ref.shape[0]
    row_i = lax.broadcasted_iota(jnp.int32, (n, n), 0)
    col_i = lax.broadcasted_iota(jnp.int32, (n, n), 1)
    o_ref[...] = _dot_exact((col_i <= row_i).astype(F32), x_ref[...])


def _pool_cumsum(x, tc=2048):
    nl, page, cols = x.shape
    tc = tc if cols % tc == 0 else cols
    return pl.pallas_call(
        _pool_cumsum_kernel,
        grid=(nl, cols // tc),
        in_specs=[pl.BlockSpec((None, page, tc), lambda i, j: (i, 0, j))],
        out_specs=pl.BlockSpec((None, page, tc), lambda i, j: (i, 0, j)),
        out_shape=jax.ShapeDtypeStruct((nl, page, cols), F32),
        compiler_params=_cparams(("parallel", "parallel")),
        name="pool_logf_cumsum",
    )(x)


def _fox_decode_kernel(pt_ref, q_ref, *refs, gp, t_new):
    kv_refs = refs[:4 * gp]
    kn_ref, vn_ref, cn_ref, o_ref, m_sc, l_sc, acc_sc, carry_sc = refs[4 * gp:]
    p = pl.program_id(1)
    n_past = pl.num_programs(1) - 1
    rows = q_ref.shape[1]

    @pl.when(p == 0)
    def _():
        m_sc[...] = jnp.full(m_sc.shape, -jnp.inf, F32)
        l_sc[...] = jnp.zeros(l_sc.shape, F32)
        acc_sc[...] = jnp.zeros(acc_sc.shape, F32)
        carry_sc[...] = jnp.zeros(carry_sc.shape, F32)

    def update(s, v, state):
        m_old, l_old, acc = state
        m_new = jnp.maximum(m_old, jnp.max(s, axis=-1, keepdims=True))
        alpha = jnp.exp(m_old - m_new)
        pr = jnp.exp(s - m_new)
        l_new = alpha * l_old + jnp.sum(pr, axis=-1, keepdims=True)
        acc = alpha * acc + _dot(pr.astype(BF16), v.astype(BF16))
        return m_new, l_new, acc

    @pl.when(p < n_past)
    def _():
        q = q_ref[0]
        width = kv_refs[0].shape[0]
        row = lax.broadcasted_iota(jnp.int32, (rows, width), 0)
        col = lax.broadcasted_iota(jnp.int32, (rows, width), 1)
        own_head = (col % FOX_HEADS) == (row // t_new)
        carry = carry_sc[...]
        scores = []
        for g in range(gp):
            k_ref, _, loc_ref, tot_ref = kv_refs[4 * g:4 * g + 4]
            s = _dot_nt(q, k_ref[...].astype(BF16)) - (carry + loc_ref[...])
            scores.append(jnp.where(own_head, s, -jnp.inf))
            carry = carry + tot_ref[...]
        carry_sc[...] = carry
        m_old = m_sc[...]
        m_new = m_old
        for s in scores:
            m_new = jnp.maximum(m_new, jnp.max(s, axis=-1, keepdims=True))
        alpha = jnp.exp(m_old - m_new)
        l_new = alpha * l_sc[...]
        acc = alpha * acc_sc[...]
        for g, s in enumerate(scores):
            pr = jnp.exp(s - m_new)
            l_new = l_new + jnp.sum(pr, axis=-1, keepdims=True)
            acc = acc + _dot(pr.astype(BF16), kv_refs[4 * g + 1][...].astype(BF16))
        m_sc[...], l_sc[...], acc_sc[...] = m_new, l_new, acc

    @pl.when(p == n_past)
    def _():
        width = kn_ref.shape[1]
        row = lax.broadcasted_iota(jnp.int32, (rows, width), 0)
        col = lax.broadcasted_iota(jnp.int32, (rows, width), 1)
        keep = ((col % FOX_HEADS) == (row // t_new)) & ((col // FOX_HEADS) <= (row % t_new))
        s = _dot_nt(q_ref[0], kn_ref[0].astype(BF16)) - (carry_sc[:, :width] + cn_ref[0])
        m, l, acc = update(jnp.where(keep, s, -jnp.inf), vn_ref[0], (m_sc[...], l_sc[...], acc_sc[...]))
        o_ref[0] = (acc / l).astype(o_ref.dtype)


def _fox_decode(q, k_new, v_new, c_new, k_pool, v_pool, loc, tot, page_table, layer, t_new, gp=16):
    b, rows, _ = q.shape
    n_pages = page_table.shape[1]
    gp = max(g for g in range(1, gp + 1) if n_pages % g == 0)
    n_steps = n_pages // gp
    width = k_pool.shape[2]

    def page_map(g):
        return lambda bi, p, pt: (layer, pt[bi, jnp.minimum(p, n_steps - 1) * gp + g], 0, 0)

    in_specs = [pl.BlockSpec((1, rows, HEAD_DIM), lambda bi, p, pt: (bi, 0, 0))]
    args = [q]
    for g in range(gp):
        in_specs += [pl.BlockSpec((None, None, width, HEAD_DIM), page_map(g)),
                     pl.BlockSpec((None, None, width, HEAD_DIM), page_map(g)),
                     pl.BlockSpec((None, None, 1, width), page_map(g)),
                     pl.BlockSpec((None, None, 1, width), page_map(g))]
        args += [k_pool, v_pool, loc, tot]
    new_spec = pl.BlockSpec((1, k_new.shape[1], HEAD_DIM), lambda bi, p, pt: (bi, 0, 0))
    in_specs += [new_spec, new_spec, pl.BlockSpec((1, 1, c_new.shape[2]), lambda bi, p, pt: (bi, 0, 0))]
    args += [k_new, v_new, c_new]
    grid_spec = pltpu.PrefetchScalarGridSpec(
        num_scalar_prefetch=1,
        grid=(b, n_steps + 1),
        in_specs=in_specs,
        out_specs=pl.BlockSpec((1, rows, HEAD_DIM), lambda bi, p, pt: (bi, 0, 0)),
        scratch_shapes=[pltpu.VMEM((rows, 1), F32), pltpu.VMEM((rows, 1), F32),
                        pltpu.VMEM((rows, HEAD_DIM), F32), pltpu.VMEM((1, width), F32)],
    )
    return pl.pallas_call(
        functools.partial(_fox_decode_kernel, gp=gp, t_new=t_new), grid_spec=grid_spec,
        out_shape=jax.ShapeDtypeStruct((b, rows, HEAD_DIM), BF16),
        compiler_params=_cparams(("parallel", "arbitrary")),
        name="fox_decode",
    )(page_table, *args)


def _conv_kernel(x_ref, buf_ref, w_ref, b_ref, y_ref, nb_ref, full_sc):
    l = x_ref.shape[1]
    pad = SUBLANES
    full_sc[pad - (SSD_CONV - 1):pad, :] = buf_ref[0]
    full_sc[pad:pad + l, :] = x_ref[0]
    acc = b_ref[...] + full_sc[pad - 3:pad - 3 + l, :] * w_ref[0:1, :]
    for kk in range(1, SSD_CONV):
        acc = acc + full_sc[pad - 3 + kk:pad - 3 + kk + l, :] * w_ref[kk:kk + 1, :]
    y_ref[0] = _silu(acc)
    nb_ref[0] = full_sc[pad + l - (SSD_CONV - 1):pad + l, :]


def _conv(zx, col0, buf, w, bias, tc=512):
    b, l, _ = zx.shape
    c = w.shape[1]
    cb0 = col0 // tc
    return pl.pallas_call(
        _conv_kernel,
        grid=(b, c // tc),
        in_specs=[pl.BlockSpec((1, l, tc), lambda i, j: (i, 0, cb0 + j)),
                  pl.BlockSpec((1, SSD_CONV - 1, tc), lambda i, j: (i, 0, j)),
                  pl.BlockSpec((SSD_CONV, tc), lambda i, j: (0, j)),
                  pl.BlockSpec((1, tc), lambda i, j: (0, j))],
        out_specs=[pl.BlockSpec((1, l, tc), lambda i, j: (i, 0, j)),
                   pl.BlockSpec((1, SSD_CONV - 1, tc), lambda i, j: (i, 0, j))],
        out_shape=[jax.ShapeDtypeStruct((b, l, c), F32),
                   jax.ShapeDtypeStruct((b, SSD_CONV - 1, c), F32)],
        scratch_shapes=[pltpu.VMEM((l + SUBLANES, tc), F32)],
        compiler_params=_cparams(("parallel", "parallel")),
        name="ssd_conv",
    )(zx, buf, w, bias.reshape(1, c))


def _ssd_kernel(xc_ref, tail_ref, z_ref, dtb_ref, alog_ref, dexp_ref, e_ref, h0_ref,
                y_ref, hout_ref, h_sc, *, valid_len):
    c = pl.program_id(1)
    q = SSD_CHUNK
    gw = SSD_INNER // SSD_GROUPS
    hpg = SSD_HEADS // SSD_GROUPS

    @pl.when(c == 0)
    def _():
        h_sc[...] = h0_ref[0]

    lane = lax.broadcasted_iota(jnp.int32, (q, LANES), 1)
    row = lax.broadcasted_iota(jnp.int32, (q, LANES), 0)
    col_i = lax.broadcasted_iota(jnp.int32, (q, q), 1)
    row_i = lax.broadcasted_iota(jnp.int32, (q, q), 0)
    causal = col_i <= row_i
    tri = causal.astype(F32)

    dt = _softplus(tail_ref[0] + dtb_ref[...])
    live = (lane >= DT_LANE0) & (lane < DT_LANE0 + SSD_HEADS) & (c * q + row < valid_len)
    dt = jnp.where(live, dt, 0.0)
    a = dt * (-jnp.exp(alog_ref[...]))
    a_cum = _dot_exact(tri, a)
    a_cum_t = a_cum.T
    dt_t = dt.T
    a_last = a_cum[q - 1:q, :]
    fac = jnp.concatenate([dt * jnp.exp(a_last - a_cum), jnp.exp(a_cum)], axis=0)
    fac_hi = fac.astype(BF16)
    fac_lo = (fac - fac_hi.astype(F32)).astype(BF16)
    fac2 = jnp.concatenate([fac_hi, fac_lo], axis=0)

    for g in range(SSD_GROUPS):
        ex = _dot(fac2, e_ref[g])
        ex = ex[:2 * q] + ex[2 * q:]
        w1_e, ea_e = ex[:q], ex[q:]
        xs = xc_ref[0, :, g * gw:(g + 1) * gw]
        bb = xc_ref[0, :, SSD_INNER + g * SSD_STATE:SSD_INNER + (g + 1) * SSD_STATE].astype(BF16)
        cc = xc_ref[0, :, SSD_INNER + (SSD_GROUPS + g) * SSD_STATE:
                    SSD_INNER + (SSD_GROUPS + g + 1) * SSD_STATE].astype(BF16)
        cb = _dot_nt(cc, bb)
        hg = h_sc[g * gw:(g + 1) * gw, :]
        y = _dot_nt(cc, hg.astype(BF16)) * ea_e
        st = _dot((xs * w1_e).T.astype(BF16), bb)
        for hp in range(hpg // 2):
            pair = xs[:, hp * LANES:(hp + 1) * LANES]
            lane_p = lax.broadcasted_iota(jnp.int32, pair.shape, 1)
            yp = None
            for sub in range(2):
                h = 2 * hp + sub
                ln = DT_LANE0 + g * hpg + h
                seg = a_cum[:, ln:ln + 1] - a_cum_t[ln:ln + 1, :]
                dec = jnp.exp(jnp.where(causal, seg, -jnp.inf))
                mat = (cb * dec * dt_t[ln:ln + 1, :]).astype(BF16)
                in_head = (lane_p >= sub * SSD_HD) & (lane_p < (sub + 1) * SSD_HD)
                rhs = jnp.where(in_head, pair, 0.0).astype(BF16)
                part = _dot(mat, rhs)
                yp = part if yp is None else yp + part
                r0 = g * gw + h * SSD_HD
                h_sc[r0:r0 + SSD_HD, :] = (hg[h * SSD_HD:(h + 1) * SSD_HD, :] * jnp.exp(a_last[:, ln:ln + 1])
                                           + st[h * SSD_HD:(h + 1) * SSD_HD, :])
            cs = slice(g * gw + hp * LANES, g * gw + (hp + 1) * LANES)
            yt = yp + y[:, hp * LANES:(hp + 1) * LANES] + dexp_ref[:, cs] * xs[:, hp * LANES:(hp + 1) * LANES]
            y_ref[0, :, cs] = yt * _silu(z_ref[0, :, cs])

    @pl.when(c == pl.num_programs(1) - 1)
    def _():
        hout_ref[0] = h_sc[...]


def _ssd(xc, tail, zx, dtb_vec, alog_vec, d_exp, e_mat, h0, valid_len):
    b, lp, _ = xc.shape
    nc = lp // SSD_CHUNK
    return pl.pallas_call(
        functools.partial(_ssd_kernel, valid_len=valid_len),
        grid=(b, nc),
        in_specs=[pl.BlockSpec((1, SSD_CHUNK, CONV_DIM), lambda i, c: (i, c, 0)),
                  pl.BlockSpec((1, SSD_CHUNK, LANES), lambda i, c: (i, c, 0)),
                  pl.BlockSpec((1, SSD_CHUNK, SSD_INNER), lambda i, c: (i, c, 0)),
                  pl.BlockSpec((1, LANES), lambda i, c: (0, 0)),
                  pl.BlockSpec((1, LANES), lambda i, c: (0, 0)),
                  pl.BlockSpec((1, SSD_INNER), lambda i, c: (0, 0)),
                  pl.BlockSpec((SSD_GROUPS, LANES, SSD_INNER // SSD_GROUPS), lambda i, c: (0, 0, 0)),
                  pl.BlockSpec((1, SSD_INNER, SSD_STATE), lambda i, c: (i, 0, 0))],
        out_specs=[pl.BlockSpec((1, SSD_CHUNK, SSD_INNER), lambda i, c: (i, c, 0)),
                   pl.BlockSpec((1, SSD_INNER, SSD_STATE), lambda i, c: (i, 0, 0))],
        out_shape=[jax.ShapeDtypeStruct((b, lp, SSD_INNER), F32),
                   jax.ShapeDtypeStruct((b, SSD_INNER, SSD_STATE), F32)],
        scratch_shapes=[pltpu.VMEM((SSD_INNER, SSD_STATE), F32)],
        compiler_params=_cparams(("parallel", "arbitrary")),
        name="ssd_chunked",
    )(xc, tail, zx, dtb_vec, alog_vec, d_exp, e_mat, h0)


S5_TILE = 256
S5_STEPS = S5_TILE // SUBLANES


def _cmul(ar, ai, br, bi):
    return ar * br - ai * bi, ar * bi + ai * br


def _s5_prep_kernel(are_ref, aim_ref, ldt_ref, bre_ref, bim_ref, psr_ref, psi_ref, pcr_ref, pci_ref,
                    bbr_ref, bbi_ref):
    lam_re = jnp.minimum(are_ref[...], -1e-4)
    lam_im = aim_ref[...]
    dt = jnp.exp(ldt_ref[...])
    mag = jnp.exp(lam_re * dt)
    ang = lam_im * dt
    lb_re, lb_im = mag * jnp.cos(ang), mag * jnp.sin(ang)
    nr, ni = lb_re - 1.0, lb_im
    den = lam_re * lam_re + lam_im * lam_im
    coef_re = (nr * lam_re + ni * lam_im) / den
    coef_im = (ni * lam_re - nr * lam_im) / den
    for k in range(S5_GROUP):
        br, bi = bre_ref[k], bim_ref[k]
        bbr_ref[k] = coef_re * br - coef_im * bi
        bbi_ref[k] = coef_re * bi + coef_im * br
    pr, pi = lb_re, lb_im
    for r in range(S5_STEPS):
        psr_ref[r], psi_ref[r] = pr, pi
        if r + 1 < S5_STEPS:
            pr, pi = _cmul(pr, pi, lb_re, lb_im)
    qr, qi = pr, pi
    for c in range(SUBLANES):
        pcr_ref[c], pci_ref[c] = qr, qi
        if c + 1 < SUBLANES:
            qr, qi = _cmul(qr, qi, pr, pi)


def _s5_prep(a_re, a_im, log_dt, b_re_t, b_im_t):
    g, n = a_re.shape
    sd = jax.ShapeDtypeStruct
    return pl.pallas_call(
        _s5_prep_kernel,
        out_shape=[sd((S5_STEPS, g, n), F32), sd((S5_STEPS, g, n), F32),
                   sd((SUBLANES, g, n), F32), sd((SUBLANES, g, n), F32),
                   sd((S5_GROUP, g, n), F32), sd((S5_GROUP, g, n), F32)],
        name="s5_discretise",
    )(a_re, a_im, log_dt.reshape(g, 1), b_re_t, b_im_t)


def _s5_kernel(u_ref, perm_ref, permt_ref, bre_ref, bim_ref, cre_ref, cim_ref, psr_ref, psi_ref,
               pcr_ref, pci_ref, d_ref, s0r_ref, s0i_ref, g_ref, sr_ref, si_ref, xr_sc, xi_sc, *, tt, nh):
    l = u_ref.shape[1]
    w = xr_sc.shape[2]
    cw = u_ref.shape[2] // nh
    ts = tt // SUBLANES
    rows = lax.broadcasted_iota(jnp.int32, (SUBLANES, w), 0)
    bc = lambda v: jnp.broadcast_to(v, (SUBLANES, w))

    def blk(i):
        return slice(i * SUBLANES, (i + 1) * SUBLANES)

    def tile(t, carry):
        t0 = pl.multiple_of(t * tt, tt)
        us = [u_ref[0, pl.ds(t0, tt), h * cw:(h + 1) * cw] for h in range(nh)]
        for h in range(nh):
            up = us[h].astype(BF16)
            if ts > 1:
                up = _dot(perm_ref[...], up).astype(BF16)
            xr_sc[h] = _dot(up, bre_ref[h])
            xi_sc[h] = _dot(up, bim_ref[h])
        ends = []
        for h in range(nh):
            lam_r, lam_i = bc(psr_ref[h, 0:1, :]), bc(psi_ref[h, 0:1, :])

            def pass1(i, st, h=h, lam_r=lam_r, lam_i=lam_i):
                sr, si = _cmul(lam_r, lam_i, st[0], st[1])
                sr, si = sr + xr_sc[h, blk(i), :], si + xi_sc[h, blk(i), :]
                xr_sc[h, blk(i), :] = sr
                xi_sc[h, blk(i), :] = si
                return sr, si

            st = (jnp.zeros((SUBLANES, w), F32), jnp.zeros((SUBLANES, w), F32))
            for i in range(ts):
                st = pass1(i, st)
            ends.append(st)
        new_carry = []
        entries = []
        for h in range(nh):
            tr, ti = ends[h]
            cr, ci = carry[2 * h], carry[2 * h + 1]
            pcr, pci = pcr_ref[h], pci_ref[h]
            for kk in (1, 2, 4):
                keep = rows >= kk
                qr = jnp.where(keep, bc(pcr[kk - 1:kk, :]), 0.0)
                qi = jnp.where(keep, bc(pci[kk - 1:kk, :]), 0.0)
                dr, di = _cmul(qr, qi, pltpu.roll(tr, kk, axis=0), pltpu.roll(ti, kk, axis=0))
                tr, ti = tr + dr, ti + di
            dr, di = _cmul(pcr, pci, bc(cr), bc(ci))
            tr, ti = tr + dr, ti + di
            entries.append((jnp.where(rows == 0, bc(cr), pltpu.roll(tr, 1, axis=0)),
                            jnp.where(rows == 0, bc(ci), pltpu.roll(ti, 1, axis=0))))
            new_carry += [tr[SUBLANES - 1:SUBLANES, :], ti[SUBLANES - 1:SUBLANES, :]]
        for h in range(nh):
            er, ei = entries[h]

            def pass2(i, _, h=h, er=er, ei=ei):
                pr, pi = bc(psr_ref[h, i:i + 1, :]), bc(psi_ref[h, i:i + 1, :])
                dr, di = _cmul(pr, pi, er, ei)
                xr_sc[h, blk(i), :] = xr_sc[h, blk(i), :] + dr
                xi_sc[h, blk(i), :] = xi_sc[h, blk(i), :] + di
                return 0

            for i in range(ts):
                pass2(i, 0)
        for h in range(nh):
            y = _dot(xr_sc[h].astype(BF16), cre_ref[h]) - _dot(xi_sc[h].astype(BF16), cim_ref[h])
            if ts > 1:
                hi = y.astype(BF16)
                mid = (y - hi.astype(F32)).astype(BF16)
                pt = permt_ref[...]
                y = _dot(pt, hi) + _dot(pt, mid)
            y = y + d_ref[h] * us[h]
            g_ref[0, pl.ds(t0, tt), h * cw:(h + 1) * cw] = _gelu_tanh(y).astype(g_ref.dtype)
        return tuple(new_carry)

    init = []
    for h in range(nh):
        init += [s0r_ref[0, h], s0i_ref[0, h]]
    fin = lax.fori_loop(0, l // tt, tile, tuple(init))
    for h in range(nh):
        sr_ref[0, h] = fin[2 * h]
        si_ref[0, h] = fin[2 * h + 1]


def _s5(u, p, s0_re, s0_im, nh=2):
    b, l, d = u.shape
    ngb = p["bbd_re"].shape[0]
    cw = d // ngb
    sw = p["bbd_re"].shape[2]
    tt = min(S5_TILE, l)
    ts = tt // SUBLANES
    assert l % tt == 0 and ts in (1, S5_STEPS) and ngb % nh == 0
    steps_re, steps_im = p["ps_re"][:, :ts], p["ps_im"][:, :ts]
    chunk_re, chunk_im = (p["pc_re"], p["pc_im"]) if ts == S5_STEPS else (p["ps_re"][:, :SUBLANES],
                                                                          p["ps_im"][:, :SUBLANES])
    r = jnp.arange(tt)
    perm = (r[None, :] == ((r % SUBLANES) * ts + r // SUBLANES)[:, None]).astype(BF16)
    state_spec = pl.BlockSpec((1, nh, 1, sw), lambda j, i: (i, j, 0, 0))
    whole = lambda a: pl.BlockSpec(a.shape, lambda j, i: (0,) * a.ndim)
    per_gb = lambda a: pl.BlockSpec((nh,) + a.shape[1:], lambda j, i: (j, 0, 0))
    args = [perm, perm.T, p["bbd_re"], p["bbd_im"], p["cbd_re"], p["cbd_im"], steps_re, steps_im,
            chunk_re, chunk_im, p["d_vec"]]
    return pl.pallas_call(
        functools.partial(_s5_kernel, tt=tt, nh=nh),
        grid=(ngb // nh, b),
        in_specs=[pl.BlockSpec((1, l, nh * cw), lambda j, i: (i, 0, j)), whole(perm), whole(perm)]
                 + [per_gb(a) for a in args[2:]] + [state_spec, state_spec],
        out_specs=[pl.BlockSpec((1, l, nh * cw), lambda j, i: (i, 0, j)), state_spec, state_spec],
        out_shape=[jax.ShapeDtypeStruct((b, l, d), BF16),
                   jax.ShapeDtypeStruct((b, ngb, 1, sw), F32),
                   jax.ShapeDtypeStruct((b, ngb, 1, sw), F32)],
        scratch_shapes=[pltpu.VMEM((nh, tt, sw), F32), pltpu.VMEM((nh, tt, sw), F32)],
        compiler_params=_cparams(("parallel", "parallel")),
        name="s5_scan",
    )(u, *args, s0_re, s0_im)


def _cross_kernel(x_ref, g_ref, wq_ref, qg_ref, k_ref, v_ref, wo_ref, o_ref, att_sc, s_sc, p_sc):
    x = x_ref[0]
    tq = x.shape[0]
    rc = min(2 * ROW_CHUNK, tq)
    xn = _rms(x, g_ref[...]).astype(BF16)
    q = _dot(xn, wq_ref[...].astype(BF16))
    for h in range(MEM_HEADS):
        sl = slice(h * HEAD_DIM, (h + 1) * HEAD_DIM)
        slot = h % 2
        qh = _rms(q[:, sl], qg_ref[...]).astype(BF16)
        s_sc[slot] = _dot_nt(qh, k_ref[:, sl].astype(BF16))
        for c in range(tq // rc):
            r = slice(c * rc, (c + 1) * rc)
            s = s_sc[slot, r, :] * (HEAD_DIM ** -0.5)
            p = jnp.exp(s - jnp.max(s, axis=-1, keepdims=True))
            p_sc[slot, r, :] = (p / jnp.sum(p, axis=-1, keepdims=True)).astype(BF16)
        att_sc[:, sl] = _dot(p_sc[slot], v_ref[:, sl].astype(BF16)).astype(BF16)
    o_ref[0] = x + _dot(att_sc[...], wo_ref[...].astype(BF16))


def _cross_attn(x, gain, w_q, q_gain, mem_k, mem_v, w_o, layer, tq=512):
    b, l, d = x.shape
    mt, mw = mem_k.shape[2], mem_k.shape[3]
    tq = min(tq, l)
    return pl.pallas_call(
        _cross_kernel,
        grid=(b, l // tq),
        in_specs=[pl.BlockSpec((1, tq, d), lambda i, j: (i, j, 0)),
                  pl.BlockSpec((1, d), lambda i, j: (0, 0)),
                  pl.BlockSpec((None, d, mw), lambda i, j: (layer, 0, 0)),
                  pl.BlockSpec((1, HEAD_DIM), lambda i, j: (0, 0)),
                  pl.BlockSpec((None, None, mt, mw), lambda i, j: (layer, i, 0, 0)),
                  pl.BlockSpec((None, None, mt, mw), lambda i, j: (layer, i, 0, 0)),
                  pl.BlockSpec((None, mw, d), lambda i, j: (layer, 0, 0))],
        out_specs=pl.BlockSpec((1, tq, d), lambda i, j: (i, j, 0)),
        out_shape=jax.ShapeDtypeStruct((b, l, d), F32),
        scratch_shapes=[pltpu.VMEM((tq, mw), BF16), pltpu.VMEM((2, tq, mt), F32),
                        pltpu.VMEM((2, tq, mt), BF16)],
        compiler_params=_cparams(("parallel", "parallel")),
        name="cross_attn",
    )(x, gain.reshape(1, d), w_q, q_gain.reshape(1, HEAD_DIM), mem_k, mem_v, w_o)


def _cross_sample_kernel(x_ref, g_ref, wq_ref, qg_ref, k_ref, v_ref, wo_ref, o_ref, att_sc, *, nb, t):
    x = x_ref[...]
    xn = _rms(x, g_ref[...]).astype(BF16)
    q = _dot(xn, wq_ref[...].astype(BF16))
    qn = [_rms(q[:, h * HEAD_DIM:(h + 1) * HEAD_DIM], qg_ref[...]) for h in range(MEM_HEADS)]
    rows, width = MEM_HEADS * t, k_ref.shape[1]
    row = lax.broadcasted_iota(jnp.int32, (rows, width), 0)
    col = lax.broadcasted_iota(jnp.int32, (rows, width), 1)
    own_head = (col % MEM_HEADS) == (row // t)
    for bi in range(nb):
        q_ht = jnp.concatenate([qh[bi * t:(bi + 1) * t] for qh in qn], axis=0).astype(BF16)
        s = _dot_nt(q_ht, k_ref[bi].astype(BF16)) * (HEAD_DIM ** -0.5)
        s = jnp.where(own_head, s, -jnp.inf)
        p = jnp.exp(s - jnp.max(s, axis=-1, keepdims=True))
        p = p / jnp.sum(p, axis=-1, keepdims=True)
        o = _dot(p.astype(BF16), v_ref[bi].astype(BF16))
        for h in range(MEM_HEADS):
            att_sc[bi * t:(bi + 1) * t, h * HEAD_DIM:(h + 1) * HEAD_DIM] = o[h * t:(h + 1) * t]
    o_ref[...] = x + _dot(att_sc[...].astype(BF16), wo_ref[...].astype(BF16))


def _cross_attn_sample(x, gain, w_q, q_gain, mem_k, mem_v, w_o, layer):
    b, t, d = x.shape
    mrows = mem_k.shape[2]
    mw = w_q.shape[2]
    assert t % SUBLANES == 0
    whole = lambda *shape: pl.BlockSpec(shape, lambda i: (0,) * len(shape))
    out = pl.pallas_call(
        functools.partial(_cross_sample_kernel, nb=b, t=t),
        grid=(1,),
        in_specs=[whole(b * t, d), whole(1, d),
                  pl.BlockSpec((None, d, mw), lambda i: (layer, 0, 0)),
                  whole(1, HEAD_DIM),
                  pl.BlockSpec((None, b, mrows, HEAD_DIM), lambda i: (layer, 0, 0, 0)),
                  pl.BlockSpec((None, b, mrows, HEAD_DIM), lambda i: (layer, 0, 0, 0)),
                  pl.BlockSpec((None, mw, d), lambda i: (layer, 0, 0))],
        out_specs=whole(b * t, d),
        out_shape=jax.ShapeDtypeStruct((b * t, d), F32),
        scratch_shapes=[pltpu.VMEM((b * t, mw), F32)],
        compiler_params=_cparams(("arbitrary",)),
        name="cross_attn_sample",
    )(x.reshape(b * t, d), gain.reshape(1, d), w_q, q_gain.reshape(1, HEAD_DIM), mem_k, mem_v, w_o)
    return out.reshape(b, t, d)


def _mem_kv_kernel(x_ref, g_ref, w_ref, hg_ref, k_ref, v_ref, xs_ref):
    j = pl.program_id(1)

    @pl.when(j == 0)
    def _():
        xs_ref[...] = _rms(x_ref[...], g_ref[...]).astype(BF16)

    acc = _dot(xs_ref[...], w_ref[...].astype(BF16))

    @pl.when(j == 0)
    def _():
        for c in range(acc.shape[1] // HEAD_DIM):
            sl = slice(c * HEAD_DIM, (c + 1) * HEAD_DIM)
            k_ref[:, sl] = _rms(acc[:, sl], hg_ref[...])

    @pl.when(j == 1)
    def _():
        v_ref[...] = acc


def _mem_kv(mem, gains, w_mkv, k_gains):
    m, d = mem.shape
    depth = w_mkv.shape[0]
    mw = w_mkv.shape[2] // 2
    sd = jax.ShapeDtypeStruct
    return pl.pallas_call(
        _mem_kv_kernel,
        grid=(depth, 2),
        in_specs=[pl.BlockSpec((m, d), lambda i, j: (0, 0)),
                  pl.BlockSpec((None, 1, d), lambda i, j: (i, 0, 0)),
                  pl.BlockSpec((None, d, mw), lambda i, j: (i, 0, j)),
                  pl.BlockSpec((None, 1, HEAD_DIM), lambda i, j: (i, 0, 0))],
        out_specs=[pl.BlockSpec((None, m, mw), lambda i, j: (i, 0, 0)),
                   pl.BlockSpec((None, m, mw), lambda i, j: (i, 0, 0))],
        out_shape=[sd((depth, m, mw), F32), sd((depth, m, mw), F32)],
        scratch_shapes=[pltpu.VMEM((m, d), BF16)],
        compiler_params=_cparams(("arbitrary", "arbitrary")),
        name="mem_kv_proj",
    )(mem, gains.reshape(depth, 1, d), w_mkv, k_gains.reshape(depth, 1, HEAD_DIM))


def _lane_vec(vals, lane0):
    v = jnp.zeros((1, LANES), F32)
    return v.at[0, lane0:lane0 + vals.shape[0]].set(vals.astype(F32))


def _head_expand():
    rows = jnp.arange(LANES)[None, :, None]
    cols = jnp.arange(SSD_INNER // SSD_GROUPS)[None, None, :]
    g = jnp.arange(SSD_GROUPS)[:, None, None]
    return (rows == DT_LANE0 + g * (SSD_HEADS // SSD_GROUPS) + cols // SSD_HD).astype(BF16)


def _block_diag(m, reps):
    nb, rows, c = m.shape
    t = jnp.tile(m, (1, 1, reps))
    rb = jnp.arange(rows)[:, None] // (rows // reps)
    cb = jnp.arange(reps * c)[None, :] // c
    return jnp.where((rb == cb)[None], t, 0.0).astype(BF16)


def _pad_rows(x, rows):
    return jnp.pad(x, ((0, 0), (0, rows - x.shape[1]), (0, 0)))


def _trunk(x, W, mem_k, mem_v, conv0, ssm0, s5_re0, s5_im0, fox_cache):
    b, l, d = x.shape
    t = b * l
    depth = W["norm_mix"].shape[0]
    lp = max(l, SSD_CHUNK)
    e_mat = _head_expand()
    fl, hs, bufs, srs, sis = [], [], [], [], []
    n_even = W["w_tail"].shape[0]
    kv_stacks = (jnp.zeros((n_even, t, FOX_W), F32), jnp.zeros((n_even, t, FOX_W), F32))
    x2 = x.reshape(t, d)
    for i in range(depth):
        j = i // 2
        if i % 2 == 0:
            q_bf, kf_all, k_bf, vf_all, v_bf = _inproj_qkv(x2, W["norm_mix"][i], W["w_in_even_t"], j, n_even,
                                                           W["fox_q_norm"][j], W["fox_k_norm"][j], kv_stacks)
            kv_stacks = (kf_all, vf_all)
            zx, tail = _inproj_zx(x2, W["norm_mix"][i], W["w_in_even_t"], 3 * FOX_W + FOX_HEADS,
                                  SSD_INNER + CONV_DIM, W["w_tail"], j)
            zx3 = zx.reshape(b, l, -1)
            tail3 = tail.reshape(b, l, LANES)
            tail_p = _pad_rows(tail3, lp) if lp != l else tail3
            lf, negc, c_tm = _forget(tail_p, _lane_vec(W["fox_b_forget"][j], 0))
            lf = lf[:, :l]
            if fox_cache is None:
                o_fox = _fox_prompt(q_bf.reshape(b, l, FOX_W), k_bf.reshape(b, l, FOX_W),
                                    v_bf.reshape(b, l, FOX_W), negc).reshape(t, FOX_W)
            else:
                k_pool, v_pool, loc, tot, page_table = fox_cache
                rows = FOX_HEADS * l
                q_ht = jnp.transpose(q_bf.reshape(b, l, FOX_HEADS, HEAD_DIM), (0, 2, 1, 3)).reshape(b, rows, HEAD_DIM)
                o_ht = _fox_decode(q_ht, _pad_rows(kf_all[j].reshape(b, rows, HEAD_DIM), LANES),
                                   _pad_rows(vf_all[j].reshape(b, rows, HEAD_DIM), LANES),
                                   _pad_rows(c_tm[:, :l].reshape(b, rows, 1), LANES).reshape(b, 1, LANES),
                                   k_pool, v_pool, loc, tot, page_table, j, l)
                o_fox = jnp.transpose(o_ht.reshape(b, FOX_HEADS, l, HEAD_DIM), (0, 2, 1, 3)).reshape(t, FOX_W)
            xc, new_buf = _conv(zx3, SSD_INNER, conv0[j], W["ssd_conv_w"][j], W["ssd_conv_b"][j])
            if lp != l:
                xc_p, z_p = _pad_rows(xc, lp), _pad_rows(zx3[:, :, :SSD_INNER], lp)
            else:
                xc_p, z_p = xc, zx3
            yg, h_last = _ssd(xc_p, tail_p, z_p, _lane_vec(W["ssd_dt_bias"][j], DT_LANE0),
                              _lane_vec(W["ssd_A_log"][j], DT_LANE0),
                              jnp.repeat(W["ssd_D"][j].astype(F32), SSD_HD).reshape(1, SSD_INNER),
                              e_mat, ssm0[j].reshape(b, SSD_INNER, SSD_STATE), l)
            yg = yg[:, :l].reshape(t, SSD_INNER)
            x2 = _outproj_even(o_fox, yg, W["ssd_norm"][j], W["w_out_even"], j, x2)
            fl.append(lf)
            hs.append(h_last.reshape(b, SSD_HEADS, SSD_HD, SSD_STATE))
            bufs.append(new_buf)
        else:
            (u,) = _dense(x2, W["w_in_odd"], layer=j, n_cols=d, tn=512, pro="norm", gain=W["norm_mix"][i],
                          name="inproj_odd", tm_max=TM_WIDE, x_single=True)
            p = W["s5_packed"][j]
            ngb = p["bbd_re"].shape[0]
            g_bf, s_re, s_im = _s5(u.reshape(b, l, d), p,
                                   s5_re0[j].reshape(b, ngb, 1, -1), s5_im0[j].reshape(b, ngb, 1, -1))
            (x2,) = _dense(g_bf.reshape(t, d), W["s5_w_glu"], layer=j, n_cols=d, tn=512, col0=0, col0_2=d,
                           epi="glu_res", res=x2, name="s5_glu_out")
            srs.append(s_re.reshape(b, -1, S5_STATE))
            sis.append(s_im.reshape(b, -1, S5_STATE))
        cross = _cross_attn if fox_cache is None else _cross_attn_sample
        x3 = cross(x2.reshape(b, l, d), W["norm_cross"][i], W["w_mq"], W["mem_q_norm"][i],
                   mem_k, mem_v, W["w_mo"], i)
        x2 = x3.reshape(t, d)
        x2 = _ffn(x2, W["norm_ffn"][i], W["w_ffn_up"], W["w_ffn_down"], i)
    fk, fv = (a.reshape(a.shape[0], b, l, FOX_HEADS, HEAD_DIM) for a in kv_stacks)
    return (x2.reshape(b, l, d), fk, fv, jnp.stack(fl), jnp.stack(hs), jnp.stack(bufs),
            jnp.stack(srs), jnp.stack(sis))


def _pack_s5(a_re, a_im, b_re, b_im, c_re, c_im, d_skip, log_dt):
    g, n, k = b_re.shape
    ngb = g // S5_GB
    ps_re, ps_im, pc_re, pc_im, bb_re, bb_im = _s5_prep(
        a_re.astype(F32), a_im.astype(F32), log_dt.astype(F32),
        jnp.transpose(b_re, (2, 0, 1)).astype(F32), jnp.transpose(b_im, (2, 0, 1)).astype(F32))

    def bmat(bb):
        return _block_diag(jnp.transpose(bb, (1, 0, 2)).reshape(ngb, S5_GB * k, n), S5_GB)

    def cmat(cm):
        return _block_diag(jnp.transpose(cm.astype(F32), (0, 2, 1)).reshape(ngb, S5_GB * n, k), S5_GB)

    def rows(pw):
        return jnp.transpose(pw.reshape(pw.shape[0], ngb, S5_GB * n), (1, 0, 2))

    return dict(bbd_re=bmat(bb_re), bbd_im=bmat(bb_im), cbd_re=cmat(c_re), cbd_im=cmat(c_im),
                ps_re=rows(ps_re), ps_im=rows(ps_im), pc_re=rows(pc_re), pc_im=rows(pc_im),
                d_vec=d_skip.astype(F32).reshape(ngb, 1, S5_GB * k))


def kernel(x_prompt, x_sample, mem_prompt, cache_fox_k, cache_fox_v, cache_fox_logf, cache_mem_k, cache_mem_v,
           state_ssd, state_conv, state_s5_re, state_s5_im, page_table,
           norm_mix, norm_cross, norm_mem, norm_ffn,
           w_in_even, fox_b_forget, fox_q_norm, fox_k_norm, ssd_conv_w, ssd_conv_b, ssd_dt_bias, ssd_A_log,
           ssd_D, ssd_norm, w_out_even,
           w_in_odd, s5_A_re, s5_A_im, s5_B_re, s5_B_im, s5_C_re, s5_C_im, s5_D, s5_log_dt, s5_w_glu,
           w_mq, w_mkv, mem_q_norm, mem_k_norm, w_mo, w_ffn_up, w_ffn_down):
    depth = norm_mix.shape[0]
    n_even, n_odd = w_in_even.shape[0], w_in_odd.shape[0]
    b, l, d = x_prompt.shape
    z0 = 3 * FOX_W + FOX_HEADS
    dt0 = z0 + SSD_INNER + CONV_DIM
    w_t = jnp.swapaxes(w_in_even, 1, 2)
    W = {
        "norm_mix": norm_mix, "norm_cross": norm_cross, "norm_ffn": norm_ffn,
        "w_in_even_t": w_t,
        "w_tail": jnp.concatenate([w_t[:, 3 * FOX_W:z0], w_t[:, dt0:dt0 + SSD_HEADS],
                                   jnp.zeros((n_even, LANES - FOX_HEADS - SSD_HEADS, d), w_t.dtype)], axis=1),
        "fox_b_forget": fox_b_forget, "fox_q_norm": fox_q_norm, "fox_k_norm": fox_k_norm,
        "ssd_conv_w": ssd_conv_w, "ssd_conv_b": ssd_conv_b, "ssd_dt_bias": ssd_dt_bias, "ssd_A_log": ssd_A_log,
        "ssd_D": ssd_D, "ssd_norm": ssd_norm, "w_out_even": w_out_even,
        "w_in_odd": w_in_odd, "s5_w_glu": s5_w_glu,
        "s5_packed": [_pack_s5(s5_A_re[j], s5_A_im[j], s5_B_re[j], s5_B_im[j], s5_C_re[j], s5_C_im[j],
                               s5_D[j], s5_log_dt[j]) for j in range(n_odd)],
        "w_mq": w_mq, "mem_q_norm": mem_q_norm, "w_mo": w_mo, "w_ffn_up": w_ffn_up, "w_ffn_down": w_ffn_down,
    }
    mt = mem_prompt.shape[1]
    mw = w_mkv.shape[2] // 2
    mem_k_p, mem_v_p = (a.reshape(depth, b, mt, mw) for a in
                        _mem_kv(mem_prompt.reshape(b * mt, d), norm_mem, w_mkv, mem_k_norm))
    n_grp = s5_A_re.shape[1]
    (y_prompt, fox_k_p, fox_v_p, fox_logf_p, ssd_p, conv_p, s5_re_p, s5_im_p) = _trunk(
        x_prompt, W, mem_k_p, mem_v_p,
        jnp.zeros((n_even, b, SSD_CONV - 1, CONV_DIM), F32),
        jnp.zeros((n_even, b, SSD_HEADS, SSD_HD, SSD_STATE), F32),
        jnp.zeros((n_odd, b, n_grp, S5_STATE), F32),
        jnp.zeros((n_odd, b, n_grp, S5_STATE), F32),
        None)
    db = x_sample.shape[0]
    n_pool, page = cache_fox_k.shape[1], cache_fox_k.shape[2]
    assert page == LANES
    width = page * FOX_HEADS
    lf_t = jnp.transpose(cache_fox_logf.astype(F32), (0, 2, 1, 3)).reshape(n_even, page, n_pool * FOX_HEADS)
    incl = _pool_cumsum(lf_t).reshape(n_even, page, n_pool, FOX_HEADS)
    loc = jnp.transpose(incl, (0, 2, 1, 3)).reshape(n_even, n_pool, 1, width)
    tot = jnp.tile(incl[:, page - 1], (1, 1, page)).reshape(n_even, n_pool, 1, width)
    fox_cache = (cache_fox_k.reshape(n_even, n_pool, width, HEAD_DIM),
                 cache_fox_v.reshape(n_even, n_pool, width, HEAD_DIM), loc, tot, page_table)
    (y_sample, fox_k_s, fox_v_s, fox_logf_s, ssd_s, conv_s, s5_re_s, s5_im_s) = _trunk(
        x_sample, W, cache_mem_k.reshape(depth, db, mt * MEM_HEADS, HEAD_DIM),
        cache_mem_v.reshape(depth, db, mt * MEM_HEADS, HEAD_DIM),
        state_conv, state_ssd, state_s5_re, state_s5_im, fox_cache)
    hd = mw // MEM_HEADS
    return (y_prompt, y_sample,
            fox_k_p, fox_v_p, fox_logf_p,
            mem_k_p.reshape(depth, b, mt, MEM_HEADS, hd), mem_v_p.reshape(depth, b, mt, MEM_HEADS, hd),
            ssd_p, conv_p, s5_re_p, s5_im_p,
            fox_k_s, fox_v_s, fox_logf_s, ssd_s, conv_s, s5_re_s, s5_im_s)
```

```python
import functools
import math

import jax
import jax.numpy as jnp
from jax import lax
from jax.experimental import pallas as pl
from jax.experimental.pallas import tpu as pltpu

F32 = jnp.float32
BF16 = jnp.bfloat16
EPS = 1e-6
LANES = 128
SUBLANES = 8
VMEM_LIMIT_BYTES = 56 * 1024 * 1024

FOX_HEADS = 8
HEAD_DIM = 128
FOX_W = FOX_HEADS * HEAD_DIM
SSD_HEADS = 32
SSD_HD = 64
SSD_GROUPS = 4
SSD_STATE = 128
SSD_CHUNK = 128
SSD_INNER = SSD_HEADS * SSD_HD
SSD_CONV = 4
CONV_DIM = SSD_INNER + 2 * SSD_GROUPS * SSD_STATE
DT_LANE0 = FOX_HEADS
S5_GROUP = 16
S5_STATE = 64
S5_GB = 16
MEM_HEADS = 4
HIGHEST = lax.Precision.HIGHEST


def _cparams(sem):
    return pltpu.CompilerParams(dimension_semantics=sem, vmem_limit_bytes=VMEM_LIMIT_BYTES)


def _gelu_tanh(x):
    return 0.5 * x * (1.0 + jnp.tanh(math.sqrt(2.0 / math.pi) * (x + 0.044715 * x * x * x)))


def _softplus(x):
    return jnp.maximum(x, 0.0) + jnp.log1p(jnp.exp(-jnp.abs(x)))


def _silu(x):
    return x * jax.nn.sigmoid(x)


def _rms(x, gain):
    return x * lax.rsqrt(jnp.mean(x * x, axis=-1, keepdims=True) + EPS) * gain


def _dot(a, b):
    return jnp.dot(a, b, preferred_element_type=F32)


def _dot_nt(a, b):
    return lax.dot_general(a, b, (((1,), (1,)), ((), ())), preferred_element_type=F32)


def _dot_exact(a, b):
    return jnp.dot(a, b, preferred_element_type=F32, precision=HIGHEST)


def _dense_kernel(*refs, pro, epi, n_out):
    it = iter(refs)
    x_ref = next(it)
    gain_ref = next(it) if pro == "norm" else None
    w_ref = next(it)
    w2_ref = next(it) if epi in ("swiglu", "glu_res") else None
    res_ref = next(it) if epi in ("residual", "glu_res") else None
    hg_ref = next(it) if epi == "headnorm" else None
    out_refs = [next(it) for _ in range(n_out)]
    xs_ref = next(it) if pro != "none" else None

    if pro != "none":
        @pl.when(pl.program_id(1) == 0)
        def _():
            xf = x_ref[...].astype(F32)
            if pro == "norm":
                xf = _rms(xf, gain_ref[...])
            elif pro == "gelu":
                xf = _gelu_tanh(xf)
            xs_ref[...] = xf.astype(BF16)
        lhs = xs_ref[...]
    else:
        lhs = x_ref[...]

    acc = _dot(lhs, w_ref[...].astype(BF16))
    if epi == "swiglu":
        acc = _silu(acc) * _dot(lhs, w2_ref[...].astype(BF16))
    elif epi == "glu_res":
        acc = res_ref[...] + acc * jax.nn.sigmoid(_dot(lhs, w2_ref[...].astype(BF16)))
    elif epi == "residual":
        acc = res_ref[...] + acc

    if epi == "headnorm":
        for c in range(acc.shape[1] // HEAD_DIM):
            sl = slice(c * HEAD_DIM, (c + 1) * HEAD_DIM)
            blk = _rms(acc[:, sl], hg_ref[...])
            for o in out_refs:
                o[:, sl] = blk.astype(o.dtype)
    else:
        for o in out_refs:
            o[...] = acc.astype(o.dtype)


def _wspec(w, layer, k, tn, blk0):
    if w.ndim == 2:
        return pl.BlockSpec((k, tn), lambda i, j: (0, blk0 + j))
    return pl.BlockSpec((None, k, tn), lambda i, j: (layer, 0, blk0 + j))


def _dense(x, w, *, n_cols, tn, name, layer=0, col0=0, col0_2=None, pro="none", gain=None, epi="plain",
           res=None, head_gain=None, out_dtypes=(F32,), tm_max=1024, x_single=False):
    m, k = x.shape
    tm = min(tm_max, m)
    assert m % tm == 0 and n_cols % tn == 0 and col0 % tn == 0
    b0 = col0 // tn
    xmode = dict(pipeline_mode=pl.Buffered(1)) if x_single else {}
    in_specs = [pl.BlockSpec((tm, k), lambda i, j: (i, 0), **xmode)]
    args = [x]
    if pro == "norm":
        in_specs.append(pl.BlockSpec((1, k), lambda i, j: (0, 0)))
        args.append(gain.reshape(1, k).astype(F32))
    in_specs.append(_wspec(w, layer, k, tn, b0))
    args.append(w)
    if epi in ("swiglu", "glu_res"):
        assert col0_2 % tn == 0
        in_specs.append(_wspec(w, layer, k, tn, col0_2 // tn))
        args.append(w)
    if epi in ("residual", "glu_res"):
        in_specs.append(pl.BlockSpec((tm, tn), lambda i, j: (i, j)))
        args.append(res)
    if epi == "headnorm":
        in_specs.append(pl.BlockSpec((1, HEAD_DIM), lambda i, j: (0, 0)))
        args.append(head_gain.reshape(1, HEAD_DIM).astype(F32))
    out_shape = [jax.ShapeDtypeStruct((m, n_cols), dt) for dt in out_dtypes]
    out_specs = [pl.BlockSpec((tm, tn), lambda i, j: (i, j)) for _ in out_dtypes]
    scratch = [pltpu.VMEM((tm, k), BF16)] if pro != "none" else []
    outs = pl.pallas_call(
        functools.partial(_dense_kernel, pro=pro, epi=epi, n_out=len(out_dtypes)),
        grid=(m // tm, n_cols // tn),
        in_specs=in_specs, out_specs=out_specs, out_shape=out_shape,
        scratch_shapes=scratch,
        compiler_params=_cparams(("parallel", "arbitrary")),
        name=name,
    )(*args)
    return outs


def _ffn_kernel(x_ref, g_ref, gn_ref, wg_ref, wu_ref, wd_ref, o_ref, xn_ref, xs_ref):
    @pl.when(pl.program_id(1) == 0)
    def _():
        x = x_ref[...]
        xs_ref[...] = _rms(x, g_ref[...]).astype(BF16)
        o_ref[...] = x
    xs = xs_ref[...]
    hid = _silu(_dot(xs, wg_ref[...].astype(BF16))) * _dot(xs, wu_ref[...].astype(BF16))
    o_ref[...] += _dot(hid.astype(BF16), wd_ref[...].astype(BF16))

    @pl.when(pl.program_id(1) == pl.num_programs(1) - 1)
    def _():
        rows = o_ref.shape[0]
        step = min(256, rows)
        for r in range(0, rows, step):
            xn_ref[r:r + step, :] = _rms(o_ref[r:r + step, :], gn_ref[...]).astype(BF16)


def _ffn(x, gain, next_gain, w_up, w_down, layer, th=256, tm_max=1024):
    m, d = x.shape
    hid = w_down.shape[1]
    tm = min(tm_max, m)
    nth = hid // th
    assert m % tm == 0 and hid % th == 0
    return pl.pallas_call(
        _ffn_kernel,
        grid=(m // tm, nth),
        in_specs=[pl.BlockSpec((tm, d), lambda i, j: (i, 0), pipeline_mode=pl.Buffered(1)),
                  pl.BlockSpec((1, d), lambda i, j: (0, 0)),
                  pl.BlockSpec((1, d), lambda i, j: (0, 0)),
                  pl.BlockSpec((None, d, th), lambda i, j: (layer, 0, j)),
                  pl.BlockSpec((None, d, th), lambda i, j: (layer, 0, nth + j)),
                  pl.BlockSpec((None, th, d), lambda i, j: (layer, j, 0))],
        out_specs=[pl.BlockSpec((tm, d), lambda i, j: (i, 0)), pl.BlockSpec((tm, d), lambda i, j: (i, 0))],
        out_shape=[jax.ShapeDtypeStruct((m, d), F32), jax.ShapeDtypeStruct((m, d), BF16)],
        scratch_shapes=[pltpu.VMEM((tm, d), BF16)],
        compiler_params=_cparams(("parallel", "arbitrary")),
        name="ffn",
    )(x, gain.reshape(1, d), next_gain.reshape(1, d), w_up, w_up, w_down)


def _norm_kernel(x_ref, g_ref, o_ref):
    o_ref[...] = _rms(x_ref[...], g_ref[...]).astype(o_ref.dtype)


def _norm_cast(x, gain, tm_max=1024):
    m, d = x.shape
    tm = min(tm_max, m)
    return pl.pallas_call(
        _norm_kernel,
        grid=(m // tm,),
        in_specs=[pl.BlockSpec((tm, d), lambda i: (i, 0)), pl.BlockSpec((1, d), lambda i: (0, 0))],
        out_specs=pl.BlockSpec((tm, d), lambda i: (i, 0)),
        out_shape=jax.ShapeDtypeStruct((m, d), BF16),
        compiler_params=_cparams(("parallel",)),
        name="rmsnorm",
    )(x, gain.reshape(1, d))


TM_WIDE = 2048


def _inproj_qkv_kernel(xs_ref, w_ref, qg_ref, kg_ref, *rest, nq):
    q_ref, kf_ref, kb_ref, vf_ref, vb_ref = rest[-5:]
    j = pl.program_id(1)

    def headnorm(acc, gain, outs):
        for c in range(acc.shape[1] // HEAD_DIM):
            sl = slice(c * HEAD_DIM, (c + 1) * HEAD_DIM)
            blk = _rms(acc[:, sl], gain)
            for o in outs:
                o[:, sl] = blk.astype(o.dtype)

    @pl.when(j < nq)
    def _():
        headnorm(_dot_nt(xs_ref[...], w_ref[...].astype(BF16)), qg_ref[...] * (HEAD_DIM ** -0.5), (q_ref,))

    @pl.when((j >= nq) & (j < 2 * nq))
    def _():
        headnorm(_dot_nt(xs_ref[...], w_ref[...].astype(BF16)), kg_ref[...], (kf_ref, kb_ref))

    @pl.when(j >= 2 * nq)
    def _():
        acc = _dot_nt(xs_ref[...], w_ref[...].astype(BF16))
        vf_ref[...] = acc
        vb_ref[...] = acc.astype(BF16)


def _inproj_qkv(xn, w_t, layer, n_layers, q_gain, k_gain, stacks, tn=256):
    m, k = xn.shape
    tm = min(TM_WIDE, m)
    nq = FOX_W // tn
    tile = lambda lo: pl.BlockSpec((tm, tn), lambda i, j: (i, jnp.clip(j - lo, 0, nq - 1)))
    stile = lambda lo: pl.BlockSpec((None, tm, tn), lambda i, j: (layer, i, jnp.clip(j - lo, 0, nq - 1)))
    sd = jax.ShapeDtypeStruct
    in_specs = [pl.BlockSpec((tm, k), lambda i, j: (i, 0)),
                pl.BlockSpec((None, tn, k), lambda i, j: (layer, j, 0)),
                pl.BlockSpec((1, HEAD_DIM), lambda i, j: (0, 0)),
                pl.BlockSpec((1, HEAD_DIM), lambda i, j: (0, 0))]
    args = [xn, w_t, q_gain.reshape(1, HEAD_DIM), k_gain.reshape(1, HEAD_DIM)]
    in_specs += [pl.BlockSpec(memory_space=pl.ANY), pl.BlockSpec(memory_space=pl.ANY)]
    aliases = {len(args): 1, len(args) + 1: 3}
    args += list(stacks)
    return pl.pallas_call(
        functools.partial(_inproj_qkv_kernel, nq=nq),
        grid=(m // tm, 3 * nq),
        in_specs=in_specs,
        out_specs=[tile(0), stile(nq), tile(nq), stile(2 * nq), tile(2 * nq)],
        out_shape=[sd((m, FOX_W), BF16), sd((n_layers, m, FOX_W), F32), sd((m, FOX_W), BF16),
                   sd((n_layers, m, FOX_W), F32), sd((m, FOX_W), BF16)],
        input_output_aliases=aliases,
        compiler_params=_cparams(("parallel", "arbitrary")),
        name="inproj_qkv",
    )(*args)


def _inproj_zx_kernel(xs_ref, wzx_ref, wt_ref, zx_ref, tail_ref, *, nzx):
    j = pl.program_id(1)

    @pl.when(j < nzx)
    def _():
        zx_ref[...] = _dot_nt(xs_ref[...], wzx_ref[0].astype(BF16))

    @pl.when(j == nzx)
    def _():
        tail_ref[...] = _dot_nt(xs_ref[...], wt_ref[...].astype(BF16))


def _inproj_zx(xn, w_t, row0, n_zx_cols, w_tail, layer, tn=512):
    m, k = xn.shape
    tm = min(TM_WIDE, m)
    nzx = n_zx_cols // tn
    return pl.pallas_call(
        functools.partial(_inproj_zx_kernel, nzx=nzx),
        grid=(m // tm, nzx + 1),
        in_specs=[pl.BlockSpec((tm, k), lambda i, j: (i, 0)),
                  pl.BlockSpec((pl.Element(1), pl.Element(tn), pl.Element(k)),
                               lambda i, j: (layer, pl.multiple_of(row0 + jnp.minimum(j, nzx - 1) * tn, SUBLANES), 0)),
                  pl.BlockSpec((None, LANES, k), lambda i, j: (layer, 0, 0), pipeline_mode=pl.Buffered(1))],
        out_specs=[pl.BlockSpec((tm, tn), lambda i, j: (i, jnp.minimum(j, nzx - 1))),
                   pl.BlockSpec((tm, LANES), lambda i, j: (i, 0))],
        out_shape=[jax.ShapeDtypeStruct((m, nzx * tn), F32), jax.ShapeDtypeStruct((m, LANES), F32)],
        compiler_params=_cparams(("parallel", "arbitrary")),
        name="inproj_zx",
    )(xn, w_t, w_tail)


def _outproj_kernel(o_ref, y_ref, g_ref, w_ref, res_ref, out_ref, yn_ref):
    k1 = o_ref.shape[1]

    @pl.when(pl.program_id(1) == 0)
    def _():
        yn_ref[...] = _rms(y_ref[...], g_ref[...]).astype(BF16)
    acc = _dot(o_ref[...], w_ref[:k1, :].astype(BF16)) + _dot(yn_ref[...], w_ref[k1:, :].astype(BF16))
    out_ref[...] = res_ref[...] + acc


def _outproj_even(o_fox, yg, gain, w_out, layer, res, tn=256, tm_max=TM_WIDE):
    m, d = res.shape
    tm = min(tm_max, m)
    k1, k2 = o_fox.shape[1], yg.shape[1]
    return pl.pallas_call(
        _outproj_kernel,
        grid=(m // tm, d // tn),
        in_specs=[
            pl.BlockSpec((tm, k1), lambda i, j: (i, 0)),
            pl.BlockSpec((tm, k2), lambda i, j: (i, 0), pipeline_mode=pl.Buffered(1)),
            pl.BlockSpec((1, k2), lambda i, j: (0, 0)),
            pl.BlockSpec((None, k1 + k2, tn), lambda i, j: (layer, 0, j)),
            pl.BlockSpec((tm, tn), lambda i, j: (i, j)),
        ],
        out_specs=pl.BlockSpec((tm, tn), lambda i, j: (i, j)),
        out_shape=jax.ShapeDtypeStruct((m, d), F32),
        scratch_shapes=[pltpu.VMEM((tm, k2), BF16)],
        compiler_params=_cparams(("parallel", "arbitrary")),
        name="outproj_even",
    )(o_fox, yg, gain.reshape(1, k2).astype(F32), w_out, res)


def _lane_cumsum(x):
    lane = lax.broadcasted_iota(jnp.int32, x.shape, 1)
    k = 1
    while k < LANES:
        x = x + jnp.where(lane >= k, pltpu.roll(x, k, axis=1), 0.0)
        k *= 2
    return x


def _forget_kernel(raw_ref, b_ref, lf_ref, negc_ref, ctm_ref):
    lp = raw_ref.shape[1]
    lf = -_softplus(-(raw_ref[0] + b_ref[...]))
    lf_ref[0] = lf[:, :FOX_HEADS]
    row_i = lax.broadcasted_iota(jnp.int32, (LANES, LANES), 0)
    col_i = lax.broadcasted_iota(jnp.int32, (LANES, LANES), 1)
    tri = (col_i <= row_i).astype(F32)
    ctm_ref[0] = _dot_exact(tri, lf[:LANES, :])[:, :FOX_HEADS]
    carry = jnp.zeros((FOX_HEADS, 1), F32)
    for c in range(lp // LANES):
        blk = lf[c * LANES:(c + 1) * LANES, :].T[:FOX_HEADS, :]
        cs = _lane_cumsum(blk) + carry
        negc_ref[0, :, c * LANES:(c + 1) * LANES] = -cs
        carry = cs[:, LANES - 1:LANES]


def _forget(raw, b_vec):
    b, lp, _ = raw.shape
    return pl.pallas_call(
        _forget_kernel,
        grid=(b,),
        in_specs=[pl.BlockSpec((1, lp, LANES), lambda i: (i, 0, 0)),
                  pl.BlockSpec((1, LANES), lambda i: (0, 0))],
        out_specs=[pl.BlockSpec((1, lp, FOX_HEADS), lambda i: (i, 0, 0)),
                   pl.BlockSpec((1, FOX_HEADS, lp), lambda i: (i, 0, 0)),
                   pl.BlockSpec((1, LANES, FOX_HEADS), lambda i: (i, 0, 0))],
        out_shape=[jax.ShapeDtypeStruct((b, lp, FOX_HEADS), F32),
                   jax.ShapeDtypeStruct((b, FOX_HEADS, lp), F32),
                   jax.ShapeDtypeStruct((b, LANES, FOX_HEADS), F32)],
        compiler_params=_cparams(("parallel",)),
        name="forget_gates",
    )(raw, b_vec)


ROW_CHUNK = 64


def _fox_prompt_kernel(q_ref, k_ref, v_ref, nb_ref, o_ref, m_sc, acc_sc, s_sc, p_sc, a_sc, *, tq, tk):
    qi = pl.program_id(1)
    kj = pl.program_id(2)

    @pl.when(kj == 0)
    def _():
        m_sc[...] = jnp.full(m_sc.shape, -jnp.inf, F32)
        acc_sc[...] = jnp.zeros(acc_sc.shape, F32)

    active = kj * tk <= qi * tq + (tq - 1)
    crosses_diagonal = kj * tk + (tk - 1) > qi * tq

    def step(masked):
        rc = min(ROW_CHUNK, tq)
        ones = jnp.ones((tk, HEAD_DIM), BF16)
        if masked:
            row0 = qi * tq + lax.broadcasted_iota(jnp.int32, (rc, tk), 0)
            col = kj * tk + lax.broadcasted_iota(jnp.int32, (rc, tk), 1)
        for h in range(FOX_HEADS):
            sl = slice(h * HEAD_DIM, (h + 1) * HEAD_DIM)
            slot = h % 2
            s_sc[slot] = _dot_nt(q_ref[0, :, sl], k_ref[0, :, sl])
            nb = nb_ref[0, h:h + 1, :]
            for c in range(tq // rc):
                r = slice(c * rc, (c + 1) * rc)
                s = s_sc[slot, r, :] + nb
                if masked:
                    s = jnp.where(col <= row0 + c * rc, s, -jnp.inf)
                m_old = m_sc[h, r, :]
                m_new = jnp.maximum(m_old, jnp.max(s, axis=-1, keepdims=True))
                p_sc[slot, r, :] = jnp.exp(s - m_new).astype(BF16)
                a_sc[slot, r, :] = jnp.exp(m_old - m_new)
                m_sc[h, r, :] = m_new
            pv = _dot(p_sc[slot], jnp.concatenate([v_ref[0, :, sl], ones], axis=1))
            acc_sc[h] = a_sc[slot] * acc_sc[h] + pv

    pl.when(active & crosses_diagonal)(functools.partial(step, True))
    pl.when(active & jnp.logical_not(crosses_diagonal))(functools.partial(step, False))

    @pl.when(kj == pl.num_programs(2) - 1)
    def _():
        for h in range(FOX_HEADS):
            acc = acc_sc[h]
            o_ref[0, :, h * HEAD_DIM:(h + 1) * HEAD_DIM] = (acc[:, :HEAD_DIM] / acc[:, HEAD_DIM:]).astype(o_ref.dtype)


def _fox_prompt(q, k, v, negc, tq=512, tk=512):
    b, l, _ = q.shape
    tq, tk = min(tq, l), min(tk, l)
    nq, nk = l // tq, l // tk

    def last_needed(qi, kj):
        return jnp.minimum(kj, (qi * tq + tq - 1) // tk)

    return pl.pallas_call(
        functools.partial(_fox_prompt_kernel, tq=tq, tk=tk),
        grid=(b, nq, nk),
        in_specs=[pl.BlockSpec((1, tq, FOX_W), lambda bi, qi, kj: (bi, qi, 0)),
                  pl.BlockSpec((1, tk, FOX_W), lambda bi, qi, kj: (bi, last_needed(qi, kj), 0)),
                  pl.BlockSpec((1, tk, FOX_W), lambda bi, qi, kj: (bi, last_needed(qi, kj), 0)),
                  pl.BlockSpec((1, FOX_HEADS, tk), lambda bi, qi, kj: (bi, 0, last_needed(qi, kj)))],
        out_specs=pl.BlockSpec((1, tq, FOX_W), lambda bi, qi, kj: (bi, qi, 0)),
        out_shape=jax.ShapeDtypeStruct((b, l, FOX_W), BF16),
        scratch_shapes=[pltpu.VMEM((FOX_HEADS, tq, 1), F32),
                        pltpu.VMEM((FOX_HEADS, tq, 2 * HEAD_DIM), F32),
                        pltpu.VMEM((2, tq, tk), F32), pltpu.VMEM((2, tq, tk), BF16),
                        pltpu.VMEM((2, tq, 1), F32)],
        compiler_params=_cparams(("parallel", "parallel", "arbitrary")),
        name="fox_prompt",
    )(q, k, v, negc)


def _pool_cumsum_kernel(x_ref, o_ref):
    n = x_ref.shape[0]
    row_i = lax.broadcasted_iota(jnp.int32, (n, n), 0)
    col_i = lax.broadcasted_iota(jnp.int32, (n, n), 1)
    o_ref[...] = _dot_exact((col_i <= row_i).astype(F32), x_ref[...])


def _pool_cumsum(x, tc=2048):
    nl, page, cols = x.shape
    tc = tc if cols % tc == 0 else cols
    return pl.pallas_call(
        _pool_cumsum_kernel,
        grid=(nl, cols // tc),
        in_specs=[pl.BlockSpec((None, page, tc), lambda i, j: (i, 0, j))],
        out_specs=pl.BlockSpec((None, page, tc), lambda i, j: (i, 0, j)),
        out_shape=jax.ShapeDtypeStruct((nl, page, cols), F32),
        compiler_params=_cparams(("parallel", "parallel")),
        name="pool_logf_cumsum",
    )(x)


def _fox_decode_kernel(pt_ref, q_ref, *refs, gp, t_new):
    kv_refs = refs[:4 * gp]
    kn_ref, vn_ref, cn_ref, o_ref, m_sc, l_sc, acc_sc, carry_sc = refs[4 * gp:]
    p = pl.program_id(1)
    n_past = pl.num_programs(1) - 1
    rows = q_ref.shape[1]

    @pl.when(p == 0)
    def _():
        m_sc[...] = jnp.full(m_sc.shape, -jnp.inf, F32)
        l_sc[...] = jnp.zeros(l_sc.shape, F32)
        acc_sc[...] = jnp.zeros(acc_sc.shape, F32)
        carry_sc[...] = jnp.zeros(carry_sc.shape, F32)

    def update(s, v, state):
        m_old, l_old, acc = state
        m_new = jnp.maximum(m_old, jnp.max(s, axis=-1, keepdims=True))
        alpha = jnp.exp(m_old - m_new)
        pr = jnp.exp(s - m_new)
        l_new = alpha * l_old + jnp.sum(pr, axis=-1, keepdims=True)
        acc = alpha * acc + _dot(pr.astype(BF16), v.astype(BF16))
        return m_new, l_new, acc

    @pl.when(p < n_past)
    def _():
        q = q_ref[0]
        width = kv_refs[0].shape[0]
        row = lax.broadcasted_iota(jnp.int32, (rows, width), 0)
        col = lax.broadcasted_iota(jnp.int32, (rows, width), 1)
        own_head = (col % FOX_HEADS) == (row // t_new)
        carry = carry_sc[...]
        scores = []
        for g in range(gp):
            k_ref, _, loc_ref, tot_ref = kv_refs[4 * g:4 * g + 4]
            s = _dot_nt(q, k_ref[...].astype(BF16)) - (carry + loc_ref[...])
            scores.append(jnp.where(own_head, s, -jnp.inf))
            carry = carry + tot_ref[...]
        carry_sc[...] = carry
        m_old = m_sc[...]
        m_new = m_old
        for s in scores:
            m_new = jnp.maximum(m_new, jnp.max(s, axis=-1, keepdims=True))
        alpha = jnp.exp(m_old - m_new)
        l_new = alpha * l_sc[...]
        acc = alpha * acc_sc[...]
        for g, s in enumerate(scores):
            pr = jnp.exp(s - m_new)
            l_new = l_new + jnp.sum(pr, axis=-1, keepdims=True)
            acc = acc + _dot(pr.astype(BF16), kv_refs[4 * g + 1][...].astype(BF16))
        m_sc[...], l_sc[...], acc_sc[...] = m_new, l_new, acc

    @pl.when(p == n_past)
    def _():
        width = kn_ref.shape[1]
        row = lax.broadcasted_iota(jnp.int32, (rows, width), 0)
        col = lax.broadcasted_iota(jnp.int32, (rows, width), 1)
        keep = ((col % FOX_HEADS) == (row // t_new)) & ((col // FOX_HEADS) <= (row % t_new))
        s = _dot_nt(q_ref[0], kn_ref[0].astype(BF16)) - (carry_sc[:, :width] + cn_ref[0])
        m, l, acc = update(jnp.where(keep, s, -jnp.inf), vn_ref[0], (m_sc[...], l_sc[...], acc_sc[...]))
        o_ref[0] = (acc / l).astype(o_ref.dtype)


def _fox_decode(q, k_new, v_new, c_new, k_pool, v_pool, loc, tot, page_table, layer, t_new, gp=16):
    b, rows, _ = q.shape
    n_pages = page_table.shape[1]
    gp = max(g for g in range(1, gp + 1) if n_pages % g == 0)
    n_steps = n_pages // gp
    width = k_pool.shape[2]

    def page_map(g):
        return lambda bi, p, pt: (layer, pt[bi, jnp.minimum(p, n_steps - 1) * gp + g], 0, 0)

    in_specs = [pl.BlockSpec((1, rows, HEAD_DIM), lambda bi, p, pt: (bi, 0, 0))]
    args = [q]
    for g in range(gp):
        in_specs += [pl.BlockSpec((None, None, width, HEAD_DIM), page_map(g)),
                     pl.BlockSpec((None, None, width, HEAD_DIM), page_map(g)),
                     pl.BlockSpec((None, None, 1, width), page_map(g)),
                     pl.BlockSpec((None, None, 1, width), page_map(g))]
        args += [k_pool, v_pool, loc, tot]
    new_spec = pl.BlockSpec((1, k_new.shape[1], HEAD_DIM), lambda bi, p, pt: (bi, 0, 0))
    in_specs += [new_spec, new_spec, pl.BlockSpec((1, 1, c_new.shape[2]), lambda bi, p, pt: (bi, 0, 0))]
    args += [k_new, v_new, c_new]
    grid_spec = pltpu.PrefetchScalarGridSpec(
        num_scalar_prefetch=1,
        grid=(b, n_steps + 1),
        in_specs=in_specs,
        out_specs=pl.BlockSpec((1, rows, HEAD_DIM), lambda bi, p, pt: (bi, 0, 0)),
        scratch_shapes=[pltpu.VMEM((rows, 1), F32), pltpu.VMEM((rows, 1), F32),
                        pltpu.VMEM((rows, HEAD_DIM), F32), pltpu.VMEM((1, width), F32)],
    )
    return pl.pallas_call(
        functools.partial(_fox_decode_kernel, gp=gp, t_new=t_new), grid_spec=grid_spec,
        out_shape=jax.ShapeDtypeStruct((b, rows, HEAD_DIM), BF16),
        compiler_params=_cparams(("parallel", "arbitrary")),
        name="fox_decode",
    )(page_table, *args)


def _conv_kernel(x_ref, buf_ref, w_ref, b_ref, y_ref, nb_ref, full_sc):
    l = x_ref.shape[1]
    pad = SUBLANES
    full_sc[pad - (SSD_CONV - 1):pad, :] = buf_ref[0]
    full_sc[pad:pad + l, :] = x_ref[0]
    acc = b_ref[...] + full_sc[pad - 3:pad - 3 + l, :] * w_ref[0:1, :]
    for kk in range(1, SSD_CONV):
        acc = acc + full_sc[pad - 3 + kk:pad - 3 + kk + l, :] * w_ref[kk:kk + 1, :]
    y_ref[0] = _silu(acc)
    nb_ref[0] = full_sc[pad + l - (SSD_CONV - 1):pad + l, :]


def _conv(zx, col0, buf, w, bias, tc=512):
    b, l, _ = zx.shape
    c = w.shape[1]
    cb0 = col0 // tc
    return pl.pallas_call(
        _conv_kernel,
        grid=(b, c // tc),
        in_specs=[pl.BlockSpec((1, l, tc), lambda i, j: (i, 0, cb0 + j)),
                  pl.BlockSpec((1, SSD_CONV - 1, tc), lambda i, j: (i, 0, j)),
                  pl.BlockSpec((SSD_CONV, tc), lambda i, j: (0, j)),
                  pl.BlockSpec((1, tc), lambda i, j: (0, j))],
        out_specs=[pl.BlockSpec((1, l, tc), lambda i, j: (i, 0, j)),
                   pl.BlockSpec((1, SSD_CONV - 1, tc), lambda i, j: (i, 0, j))],
        out_shape=[jax.ShapeDtypeStruct((b, l, c), F32),
                   jax.ShapeDtypeStruct((b, SSD_CONV - 1, c), F32)],
        scratch_shapes=[pltpu.VMEM((l + SUBLANES, tc), F32)],
        compiler_params=_cparams(("parallel", "parallel")),
        name="ssd_conv",
    )(zx, buf, w, bias.reshape(1, c))


def _ssd_kernel(xc_ref, tail_ref, z_ref, dtb_ref, alog_ref, dexp_ref, e_ref, h0_ref,
                y_ref, hout_ref, h_sc, *, valid_len):
    c = pl.program_id(1)
    q = SSD_CHUNK
    gw = SSD_INNER // SSD_GROUPS
    hpg = SSD_HEADS // SSD_GROUPS

    @pl.when(c == 0)
    def _():
        h_sc[...] = h0_ref[0]

    lane = lax.broadcasted_iota(jnp.int32, (q, LANES), 1)
    row = lax.broadcasted_iota(jnp.int32, (q, LANES), 0)
    col_i = lax.broadcasted_iota(jnp.int32, (q, q), 1)
    row_i = lax.broadcasted_iota(jnp.int32, (q, q), 0)
    causal = col_i <= row_i
    tri = causal.astype(F32)

    dt = _softplus(tail_ref[0] + dtb_ref[...])
    live = (lane >= DT_LANE0) & (lane < DT_LANE0 + SSD_HEADS) & (c * q + row < valid_len)
    dt = jnp.where(live, dt, 0.0)
    a = dt * (-jnp.exp(alog_ref[...]))
    a_cum = _dot_exact(tri, a)
    a_cum_t = a_cum.T
    dt_t = dt.T
    a_last = a_cum[q - 1:q, :]
    fac = jnp.concatenate([dt * jnp.exp(a_last - a_cum), jnp.exp(a_cum)], axis=0)
    fac_hi = fac.astype(BF16)
    fac_lo = (fac - fac_hi.astype(F32)).astype(BF16)
    fac2 = jnp.concatenate([fac_hi, fac_lo], axis=0)

    for g in range(SSD_GROUPS):
        ex = _dot(fac2, e_ref[g])
        ex = ex[:2 * q] + ex[2 * q:]
        w1_e, ea_e = ex[:q], ex[q:]
        xs = xc_ref[0, :, g * gw:(g + 1) * gw]
        bb = xc_ref[0, :, SSD_INNER + g * SSD_STATE:SSD_INNER + (g + 1) * SSD_STATE].astype(BF16)
        cc = xc_ref[0, :, SSD_INNER + (SSD_GROUPS + g) * SSD_STATE:
                    SSD_INNER + (SSD_GROUPS + g + 1) * SSD_STATE].astype(BF16)
        cb = _dot_nt(cc, bb)
        hg = h_sc[g * gw:(g + 1) * gw, :]
        y = _dot_nt(cc, hg.astype(BF16)) * ea_e
        st = _dot((xs * w1_e).T.astype(BF16), bb)
        for hp in range(hpg // 2):
            pair = xs[:, hp * LANES:(hp + 1) * LANES]
            lane_p = lax.broadcasted_iota(jnp.int32, pair.shape, 1)
            yp = None
            for sub in range(2):
                h = 2 * hp + sub
                ln = DT_LANE0 + g * hpg + h
                seg = a_cum[:, ln:ln + 1] - a_cum_t[ln:ln + 1, :]
                dec = jnp.exp(jnp.where(causal, seg, -jnp.inf))
                mat = (cb * dec * dt_t[ln:ln + 1, :]).astype(BF16)
                in_head = (lane_p >= sub * SSD_HD) & (lane_p < (sub + 1) * SSD_HD)
                rhs = jnp.where(in_head, pair, 0.0).astype(BF16)
                part = _dot(mat, rhs)
                yp = part if yp is None else yp + part
                r0 = g * gw + h * SSD_HD
                h_sc[r0:r0 + SSD_HD, :] = (hg[h * SSD_HD:(h + 1) * SSD_HD, :] * jnp.exp(a_last[:, ln:ln + 1])
                                           + st[h * SSD_HD:(h + 1) * SSD_HD, :])
            cs = slice(g * gw + hp * LANES, g * gw + (hp + 1) * LANES)
            yt = yp + y[:, hp * LANES:(hp + 1) * LANES] + dexp_ref[:, cs] * xs[:, hp * LANES:(hp + 1) * LANES]
            y_ref[0, :, cs] = yt * _silu(z_ref[0, :, cs])

    @pl.when(c == pl.num_programs(1) - 1)
    def _():
        hout_ref[0] = h_sc[...]


def _ssd(xc, tail, zx, dtb_vec, alog_vec, d_exp, e_mat, h0, valid_len):
    b, lp, _ = xc.shape
    nc = lp // SSD_CHUNK
    return pl.pallas_call(
        functools.partial(_ssd_kernel, valid_len=valid_len),
        grid=(b, nc),
        in_specs=[pl.BlockSpec((1, SSD_CHUNK, CONV_DIM), lambda i, c: (i, c, 0)),
                  pl.BlockSpec((1, SSD_CHUNK, LANES), lambda i, c: (i, c, 0)),
                  pl.BlockSpec((1, SSD_CHUNK, SSD_INNER), lambda i, c: (i, c, 0)),
                  pl.BlockSpec((1, LANES), lambda i, c: (0, 0)),
                  pl.BlockSpec((1, LANES), lambda i, c: (0, 0)),
                  pl.BlockSpec((1, SSD_INNER), lambda i, c: (0, 0)),
                  pl.BlockSpec((SSD_GROUPS, LANES, SSD_INNER // SSD_GROUPS), lambda i, c: (0, 0, 0)),
                  pl.BlockSpec((1, SSD_INNER, SSD_STATE), lambda i, c: (i, 0, 0))],
        out_specs=[pl.BlockSpec((1, SSD_CHUNK, SSD_INNER), lambda i, c: (i, c, 0)),
                   pl.BlockSpec((1, SSD_INNER, SSD_STATE), lambda i, c: (i, 0, 0))],
        out_shape=[jax.ShapeDtypeStruct((b, lp, SSD_INNER), F32),
                   jax.ShapeDtypeStruct((b, SSD_INNER, SSD_STATE), F32)],
        scratch_shapes=[pltpu.VMEM((SSD_INNER, SSD_STATE), F32)],
        compiler_params=_cparams(("parallel", "arbitrary")),
        name="ssd_chunked",
    )(xc, tail, zx, dtb_vec, alog_vec, d_exp, e_mat, h0)


S5_TILE = 256
S5_STEPS = S5_TILE // SUBLANES


def _cmul(ar, ai, br, bi):
    return ar * br - ai * bi, ar * bi + ai * br


def _s5_prep_kernel(are_ref, aim_ref, ldt_ref, bre_ref, bim_ref, psr_ref, psi_ref, pcr_ref, pci_ref,
                    bbr_ref, bbi_ref):
    lam_re = jnp.minimum(are_ref[...], -1e-4)
    lam_im = aim_ref[...]
    dt = jnp.exp(ldt_ref[...])
    mag = jnp.exp(lam_re * dt)
    ang = lam_im * dt
    lb_re, lb_im = mag * jnp.cos(ang), mag * jnp.sin(ang)
    nr, ni = lb_re - 1.0, lb_im
    den = lam_re * lam_re + lam_im * lam_im
    coef_re = (nr * lam_re + ni * lam_im) / den
    coef_im = (ni * lam_re - nr * lam_im) / den
    for k in range(S5_GROUP):
        br, bi = bre_ref[k], bim_ref[k]
        bbr_ref[k] = coef_re * br - coef_im * bi
        bbi_ref[k] = coef_re * bi + coef_im * br
    pr, pi = lb_re, lb_im
    for r in range(S5_STEPS):
        psr_ref[r], psi_ref[r] = pr, pi
        if r + 1 < S5_STEPS:
            pr, pi = _cmul(pr, pi, lb_re, lb_im)
    qr, qi = pr, pi
    for c in range(SUBLANES):
        pcr_ref[c], pci_ref[c] = qr, qi
        if c + 1 < SUBLANES:
            qr, qi = _cmul(qr, qi, pr, pi)


def _s5_prep(a_re, a_im, log_dt, b_re_t, b_im_t):
    g, n = a_re.shape
    sd = jax.ShapeDtypeStruct
    return pl.pallas_call(
        _s5_prep_kernel,
        out_shape=[sd((S5_STEPS, g, n), F32), sd((S5_STEPS, g, n), F32),
                   sd((SUBLANES, g, n), F32), sd((SUBLANES, g, n), F32),
                   sd((S5_GROUP, g, n), F32), sd((S5_GROUP, g, n), F32)],
        name="s5_discretise",
    )(a_re, a_im, log_dt.reshape(g, 1), b_re_t, b_im_t)


def _s5_kernel(u_ref, perm_ref, permt_ref, bre_ref, bim_ref, cre_ref, cim_ref, psr_ref, psi_ref,
               pcr_ref, pci_ref, d_ref, s0r_ref, s0i_ref, g_ref, sr_ref, si_ref, xr_sc, xi_sc, *, tt, nh):
    l = u_ref.shape[1]
    w = xr_sc.shape[2]
    cw = u_ref.shape[2] // nh
    ts = tt // SUBLANES
    rows = lax.broadcasted_iota(jnp.int32, (SUBLANES, w), 0)
    bc = lambda v: jnp.broadcast_to(v, (SUBLANES, w))

    def blk(i):
        return slice(i * SUBLANES, (i + 1) * SUBLANES)

    def tile(t, carry):
        t0 = pl.multiple_of(t * tt, tt)
        us = [u_ref[0, pl.ds(t0, tt), h * cw:(h + 1) * cw] for h in range(nh)]
        for h in range(nh):
            up = us[h].astype(BF16)
            if ts > 1:
                up = _dot(perm_ref[...], up).astype(BF16)
            xr_sc[h] = _dot(up, bre_ref[h])
            xi_sc[h] = _dot(up, bim_ref[h])
        ends = []
        for h in range(nh):
            lam_r, lam_i = bc(psr_ref[h, 0:1, :]), bc(psi_ref[h, 0:1, :])

            def pass1(i, st, h=h, lam_r=lam_r, lam_i=lam_i):
                sr, si = _cmul(lam_r, lam_i, st[0], st[1])
                sr, si = sr + xr_sc[h, blk(i), :], si + xi_sc[h, blk(i), :]
                xr_sc[h, blk(i), :] = sr
                xi_sc[h, blk(i), :] = si
                return sr, si

            st = (jnp.zeros((SUBLANES, w), F32), jnp.zeros((SUBLANES, w), F32))
            for i in range(ts):
                st = pass1(i, st)
            ends.append(st)
        new_carry = []
        entries = []
        for h in range(nh):
            tr, ti = ends[h]
            cr, ci = carry[2 * h], carry[2 * h + 1]
            pcr, pci = pcr_ref[h], pci_ref[h]
            for kk in (1, 2, 4):
                keep = rows >= kk
                qr = jnp.where(keep, bc(pcr[kk - 1:kk, :]), 0.0)
                qi = jnp.where(keep, bc(pci[kk - 1:kk, :]), 0.0)
                dr, di = _cmul(qr, qi, pltpu.roll(tr, kk, axis=0), pltpu.roll(ti, kk, axis=0))
                tr, ti = tr + dr, ti + di
            dr, di = _cmul(pcr, pci, bc(cr), bc(ci))
            tr, ti = tr + dr, ti + di
            entries.append((jnp.where(rows == 0, bc(cr), pltpu.roll(tr, 1, axis=0)),
                            jnp.where(rows == 0, bc(ci), pltpu.roll(ti, 1, axis=0))))
            new_carry += [tr[SUBLANES - 1:SUBLANES, :], ti[SUBLANES - 1:SUBLANES, :]]
        for h in range(nh):
            er, ei = entries[h]

            def pass2(i, _, h=h, er=er, ei=ei):
                pr, pi = bc(psr_ref[h, i:i + 1, :]), bc(psi_ref[h, i:i + 1, :])
                dr, di = _cmul(pr, pi, er, ei)
                xr_sc[h, blk(i), :] = xr_sc[h, blk(i), :] + dr
                xi_sc[h, blk(i), :] = xi_sc[h, blk(i), :] + di
                return 0

            for i in range(ts):
                pass2(i, 0)
        for h in range(nh):
            y = _dot(xr_sc[h].astype(BF16), cre_ref[h]) - _dot(xi_sc[h].astype(BF16), cim_ref[h])
            if ts > 1:
                hi = y.astype(BF16)
                mid = (y - hi.astype(F32)).astype(BF16)
                pt = permt_ref[...]
                y = _dot(pt, hi) + _dot(pt, mid)
            y = y + d_ref[h] * us[h]
            g_ref[0, pl.ds(t0, tt), h * cw:(h + 1) * cw] = _gelu_tanh(y).astype(g_ref.dtype)
        return tuple(new_carry)

    init = []
    for h in range(nh):
        init += [s0r_ref[0, h], s0i_ref[0, h]]
    fin = lax.fori_loop(0, l // tt, tile, tuple(init))
    for h in range(nh):
        sr_ref[0, h] = fin[2 * h]
        si_ref[0, h] = fin[2 * h + 1]


def _s5(u, p, s0_re, s0_im, nh=4):
    b, l, d = u.shape
    ngb = p["bbd_re"].shape[0]
    cw = d // ngb
    sw = p["bbd_re"].shape[2]
    tt = min(S5_TILE, l)
    ts = tt // SUBLANES
    assert l % tt == 0 and ts in (1, S5_STEPS) and ngb % nh == 0
    steps_re, steps_im = p["ps_re"][:, :ts], p["ps_im"][:, :ts]
    chunk_re, chunk_im = (p["pc_re"], p["pc_im"]) if ts == S5_STEPS else (p["ps_re"][:, :SUBLANES],
                                                                          p["ps_im"][:, :SUBLANES])
    r = jnp.arange(tt)
    perm = (r[None, :] == ((r % SUBLANES) * ts + r // SUBLANES)[:, None]).astype(BF16)
    state_spec = pl.BlockSpec((1, nh, 1, sw), lambda j, i: (i, j, 0, 0))
    whole = lambda a: pl.BlockSpec(a.shape, lambda j, i: (0,) * a.ndim)
    per_gb = lambda a: pl.BlockSpec((nh,) + a.shape[1:], lambda j, i: (j, 0, 0))
    args = [perm, perm.T, p["bbd_re"], p["bbd_im"], p["cbd_re"], p["cbd_im"], steps_re, steps_im,
            chunk_re, chunk_im, p["d_vec"]]
    return pl.pallas_call(
        functools.partial(_s5_kernel, tt=tt, nh=nh),
        grid=(ngb // nh, b),
        in_specs=[pl.BlockSpec((1, l, nh * cw), lambda j, i: (i, 0, j)), whole(perm), whole(perm)]
                 + [per_gb(a) for a in args[2:]] + [state_spec, state_spec],
        out_specs=[pl.BlockSpec((1, l, nh * cw), lambda j, i: (i, 0, j)), state_spec, state_spec],
        out_shape=[jax.ShapeDtypeStruct((b, l, d), BF16),
                   jax.ShapeDtypeStruct((b, ngb, 1, sw), F32),
                   jax.ShapeDtypeStruct((b, ngb, 1, sw), F32)],
        scratch_shapes=[pltpu.VMEM((nh, tt, sw), F32), pltpu.VMEM((nh, tt, sw), F32)],
        compiler_params=_cparams(("parallel", "parallel")),
        name="s5_scan",
    )(u, *args, s0_re, s0_im)


def _cross_kernel(x_ref, g_ref, wq_ref, qg_ref, k_ref, v_ref, wo_ref, o_ref, att_sc, s_sc, p_sc):
    x = x_ref[0]
    tq = x.shape[0]
    rc = min(2 * ROW_CHUNK, tq)
    xn = _rms(x, g_ref[...]).astype(BF16)
    q = _dot(xn, wq_ref[...].astype(BF16))
    for h in range(MEM_HEADS):
        sl = slice(h * HEAD_DIM, (h + 1) * HEAD_DIM)
        slot = h % 2
        qh = _rms(q[:, sl], qg_ref[...]).astype(BF16)
        s_sc[slot] = _dot_nt(qh, k_ref[:, sl].astype(BF16))
        for c in range(tq // rc):
            r = slice(c * rc, (c + 1) * rc)
            s = s_sc[slot, r, :] * (HEAD_DIM ** -0.5)
            p = jnp.exp(s - jnp.max(s, axis=-1, keepdims=True))
            p_sc[slot, r, :] = (p / jnp.sum(p, axis=-1, keepdims=True)).astype(BF16)
        att_sc[:, sl] = _dot(p_sc[slot], v_ref[:, sl].astype(BF16)).astype(BF16)
    o_ref[0] = x + _dot(att_sc[...], wo_ref[...].astype(BF16))


def _cross_attn(x, gain, w_q, q_gain, mem_k, mem_v, w_o, layer, tq=512):
    b, l, d = x.shape
    mt, mw = mem_k.shape[2], mem_k.shape[3]
    tq = min(tq, l)
    return pl.pallas_call(
        _cross_kernel,
        grid=(b, l // tq),
        in_specs=[pl.BlockSpec((1, tq, d), lambda i, j: (i, j, 0)),
                  pl.BlockSpec((1, d), lambda i, j: (0, 0)),
                  pl.BlockSpec((None, d, mw), lambda i, j: (layer, 0, 0)),
                  pl.BlockSpec((1, HEAD_DIM), lambda i, j: (0, 0)),
                  pl.BlockSpec((None, None, mt, mw), lambda i, j: (layer, i, 0, 0)),
                  pl.BlockSpec((None, None, mt, mw), lambda i, j: (layer, i, 0, 0)),
                  pl.BlockSpec((None, mw, d), lambda i, j: (layer, 0, 0))],
        out_specs=pl.BlockSpec((1, tq, d), lambda i, j: (i, j, 0)),
        out_shape=jax.ShapeDtypeStruct((b, l, d), F32),
        scratch_shapes=[pltpu.VMEM((tq, mw), BF16), pltpu.VMEM((2, tq, mt), F32),
                        pltpu.VMEM((2, tq, mt), BF16)],
        compiler_params=_cparams(("parallel", "parallel")),
        name="cross_attn",
    )(x, gain.reshape(1, d), w_q, q_gain.reshape(1, HEAD_DIM), mem_k, mem_v, w_o)


def _cross_sample_kernel(x_ref, g_ref, wq_ref, qg_ref, k_ref, v_ref, wo_ref, o_ref, att_sc, *, nb, t):
    x = x_ref[...]
    xn = _rms(x, g_ref[...]).astype(BF16)
    q = _dot(xn, wq_ref[...].astype(BF16))
    qn = [_rms(q[:, h * HEAD_DIM:(h + 1) * HEAD_DIM], qg_ref[...]) for h in range(MEM_HEADS)]
    rows, width = MEM_HEADS * t, k_ref.shape[1]
    row = lax.broadcasted_iota(jnp.int32, (rows, width), 0)
    col = lax.broadcasted_iota(jnp.int32, (rows, width), 1)
    own_head = (col % MEM_HEADS) == (row // t)
    for bi in range(nb):
        q_ht = jnp.concatenate([qh[bi * t:(bi + 1) * t] for qh in qn], axis=0).astype(BF16)
        s = _dot_nt(q_ht, k_ref[bi].astype(BF16)) * (HEAD_DIM ** -0.5)
        s = jnp.where(own_head, s, -jnp.inf)
        p = jnp.exp(s - jnp.max(s, axis=-1, keepdims=True))
        p = p / jnp.sum(p, axis=-1, keepdims=True)
        o = _dot(p.astype(BF16), v_ref[bi].astype(BF16))
        for h in range(MEM_HEADS):
            att_sc[bi * t:(bi + 1) * t, h * HEAD_DIM:(h + 1) * HEAD_DIM] = o[h * t:(h + 1) * t]
    o_ref[...] = x + _dot(att_sc[...].astype(BF16), wo_ref[...].astype(BF16))


def _cross_attn_sample(x, gain, w_q, q_gain, mem_k, mem_v, w_o, layer):
    b, t, d = x.shape
    mrows = mem_k.shape[2]
    mw = w_q.shape[2]
    assert t % SUBLANES == 0
    whole = lambda *shape: pl.BlockSpec(shape, lambda i: (0,) * len(shape))
    out = pl.pallas_call(
        functools.partial(_cross_sample_kernel, nb=b, t=t),
        grid=(1,),
        in_specs=[whole(b * t, d), whole(1, d),
                  pl.BlockSpec((None, d, mw), lambda i: (layer, 0, 0)),
                  whole(1, HEAD_DIM),
                  pl.BlockSpec((None, b, mrows, HEAD_DIM), lambda i: (layer, 0, 0, 0)),
                  pl.BlockSpec((None, b, mrows, HEAD_DIM), lambda i: (layer, 0, 0, 0)),
                  pl.BlockSpec((None, mw, d), lambda i: (layer, 0, 0))],
        out_specs=whole(b * t, d),
        out_shape=jax.ShapeDtypeStruct((b * t, d), F32),
        scratch_shapes=[pltpu.VMEM((b * t, mw), F32)],
        compiler_params=_cparams(("arbitrary",)),
        name="cross_attn_sample",
    )(x.reshape(b * t, d), gain.reshape(1, d), w_q, q_gain.reshape(1, HEAD_DIM), mem_k, mem_v, w_o)
    return out.reshape(b, t, d)


def _mem_kv_kernel(x_ref, g_ref, w_ref, hg_ref, k_ref, v_ref, xs_ref):
    j = pl.program_id(1)

    @pl.when(j == 0)
    def _():
        xs_ref[...] = _rms(x_ref[...], g_ref[...]).astype(BF16)

    acc = _dot(xs_ref[...], w_ref[...].astype(BF16))

    @pl.when(j == 0)
    def _():
        for c in range(acc.shape[1] // HEAD_DIM):
            sl = slice(c * HEAD_DIM, (c + 1) * HEAD_DIM)
            k_ref[:, sl] = _rms(acc[:, sl], hg_ref[...])

    @pl.when(j == 1)
    def _():
        v_ref[...] = acc


def _mem_kv(mem, gains, w_mkv, k_gains):
    m, d = mem.shape
    depth = w_mkv.shape[0]
    mw = w_mkv.shape[2] // 2
    sd = jax.ShapeDtypeStruct
    return pl.pallas_call(
        _mem_kv_kernel,
        grid=(depth, 2),
        in_specs=[pl.BlockSpec((m, d), lambda i, j: (0, 0)),
                  pl.BlockSpec((None, 1, d), lambda i, j: (i, 0, 0)),
                  pl.BlockSpec((None, d, mw), lambda i, j: (i, 0, j)),
                  pl.BlockSpec((None, 1, HEAD_DIM), lambda i, j: (i, 0, 0))],
        out_specs=[pl.BlockSpec((None, m, mw), lambda i, j: (i, 0, 0)),
                   pl.BlockSpec((None, m, mw), lambda i, j: (i, 0, 0))],
        out_shape=[sd((depth, m, mw), F32), sd((depth, m, mw), F32)],
        scratch_shapes=[pltpu.VMEM((m, d), BF16)],
        compiler_params=_cparams(("arbitrary", "arbitrary")),
        name="mem_kv_proj",
    )(mem, gains.reshape(depth, 1, d), w_mkv, k_gains.reshape(depth, 1, HEAD_DIM))


def _lane_vec(vals, lane0):
    v = jnp.zeros((1, LANES), F32)
    return v.at[0, lane0:lane0 + vals.shape[0]].set(vals.astype(F32))


def _head_expand():
    rows = jnp.arange(LANES)[None, :, None]
    cols = jnp.arange(SSD_INNER // SSD_GROUPS)[None, None, :]
    g = jnp.arange(SSD_GROUPS)[:, None, None]
    return (rows == DT_LANE0 + g * (SSD_HEADS // SSD_GROUPS) + cols // SSD_HD).astype(BF16)


def _block_diag(m, reps):
    nb, rows, c = m.shape
    t = jnp.tile(m, (1, 1, reps))
    rb = jnp.arange(rows)[:, None] // (rows // reps)
    cb = jnp.arange(reps * c)[None, :] // c
    return jnp.where((rb == cb)[None], t, 0.0).astype(BF16)


def _pad_rows(x, rows):
    return jnp.pad(x, ((0, 0), (0, rows - x.shape[1]), (0, 0)))


def _trunk(x, W, mem_k, mem_v, conv0, ssm0, s5_re0, s5_im0, fox_cache):
    b, l, d = x.shape
    t = b * l
    depth = W["norm_mix"].shape[0]
    lp = max(l, SSD_CHUNK)
    e_mat = _head_expand()
    fl, hs, bufs, srs, sis = [], [], [], [], []
    n_even = W["w_tail"].shape[0]
    kv_stacks = (jnp.zeros((n_even, t, FOX_W), F32), jnp.zeros((n_even, t, FOX_W), F32))
    x2 = x.reshape(t, d)
    xn = _norm_cast(x2, W["norm_mix"][0])
    for i in range(depth):
        j = i // 2
        if i % 2 == 0:
            q_bf, kf_all, k_bf, vf_all, v_bf = _inproj_qkv(xn, W["w_in_even_t"], j, n_even,
                                                           W["fox_q_norm"][j], W["fox_k_norm"][j], kv_stacks)
            kv_stacks = (kf_all, vf_all)
            zx, tail = _inproj_zx(xn, W["w_in_even_t"], 3 * FOX_W + FOX_HEADS,
                                  SSD_INNER + CONV_DIM, W["w_tail"], j)
            zx3 = zx.reshape(b, l, -1)
            tail3 = tail.reshape(b, l, LANES)
            tail_p = _pad_rows(tail3, lp) if lp != l else tail3
            lf, negc, c_tm = _forget(tail_p, _lane_vec(W["fox_b_forget"][j], 0))
            lf = lf[:, :l]
            if fox_cache is None:
                o_fox = _fox_prompt(q_bf.reshape(b, l, FOX_W), k_bf.reshape(b, l, FOX_W),
                                    v_bf.reshape(b, l, FOX_W), negc).reshape(t, FOX_W)
            else:
                k_pool, v_pool, loc, tot, page_table = fox_cache
                rows = FOX_HEADS * l
                q_ht = jnp.transpose(q_bf.reshape(b, l, FOX_HEADS, HEAD_DIM), (0, 2, 1, 3)).reshape(b, rows, HEAD_DIM)
                o_ht = _fox_decode(q_ht, _pad_rows(kf_all[j].reshape(b, rows, HEAD_DIM), LANES),
                                   _pad_rows(vf_all[j].reshape(b, rows, HEAD_DIM), LANES),
                                   _pad_rows(c_tm[:, :l].reshape(b, rows, 1), LANES).reshape(b, 1, LANES),
                                   k_pool, v_pool, loc, tot, page_table, j, l)
                o_fox = jnp.transpose(o_ht.reshape(b, FOX_HEADS, l, HEAD_DIM), (0, 2, 1, 3)).reshape(t, FOX_W)
            xc, new_buf = _conv(zx3, SSD_INNER, conv0[j], W["ssd_conv_w"][j], W["ssd_conv_b"][j])
            if lp != l:
                xc_p, z_p = _pad_rows(xc, lp), _pad_rows(zx3[:, :, :SSD_INNER], lp)
            else:
                xc_p, z_p = xc, zx3
            yg, h_last = _ssd(xc_p, tail_p, z_p, _lane_vec(W["ssd_dt_bias"][j], DT_LANE0),
                              _lane_vec(W["ssd_A_log"][j], DT_LANE0),
                              jnp.repeat(W["ssd_D"][j].astype(F32), SSD_HD).reshape(1, SSD_INNER),
                              e_mat, ssm0[j].reshape(b, SSD_INNER, SSD_STATE), l)
            yg = yg[:, :l].reshape(t, SSD_INNER)
            x2 = _outproj_even(o_fox, yg, W["ssd_norm"][j], W["w_out_even"], j, x2)
            fl.append(lf)
            hs.append(h_last.reshape(b, SSD_HEADS, SSD_HD, SSD_STATE))
            bufs.append(new_buf)
        else:
            (u,) = _dense(xn, W["w_in_odd"], layer=j, n_cols=d, tn=512, name="inproj_odd", tm_max=TM_WIDE)
            p = W["s5_packed"][j]
            ngb = p["bbd_re"].shape[0]
            g_bf, s_re, s_im = _s5(u.reshape(b, l, d), p,
                                   s5_re0[j].reshape(b, ngb, 1, -1), s5_im0[j].reshape(b, ngb, 1, -1))
            (x2,) = _dense(g_bf.reshape(t, d), W["s5_w_glu"], layer=j, n_cols=d, tn=512, col0=0, col0_2=d,
                           epi="glu_res", res=x2, name="s5_glu_out")
            srs.append(s_re.reshape(b, -1, S5_STATE))
            sis.append(s_im.reshape(b, -1, S5_STATE))
        cross = _cross_attn if fox_cache is None else _cross_attn_sample
        x3 = cross(x2.reshape(b, l, d), W["norm_cross"][i], W["w_mq"], W["mem_q_norm"][i],
                   mem_k, mem_v, W["w_mo"], i)
        next_gain = W["norm_mix"][min(i + 1, depth - 1)]
        x2, xn = _ffn(x3.reshape(t, d), W["norm_ffn"][i], next_gain, W["w_ffn_up"], W["w_ffn_down"], i)
    fk, fv = (a.reshape(a.shape[0], b, l, FOX_HEADS, HEAD_DIM) for a in kv_stacks)
    return (x2.reshape(b, l, d), fk, fv, jnp.stack(fl), jnp.stack(hs), jnp.stack(bufs),
            jnp.stack(srs), jnp.stack(sis))


def _pack_s5(a_re, a_im, b_re, b_im, c_re, c_im, d_skip, log_dt):
    g, n, k = b_re.shape
    ngb = g // S5_GB
    ps_re, ps_im, pc_re, pc_im, bb_re, bb_im = _s5_prep(
        a_re.astype(F32), a_im.astype(F32), log_dt.astype(F32),
        jnp.transpose(b_re, (2, 0, 1)).astype(F32), jnp.transpose(b_im, (2, 0, 1)).astype(F32))

    def bmat(bb):
        return _block_diag(jnp.transpose(bb, (1, 0, 2)).reshape(ngb, S5_GB * k, n), S5_GB)

    def cmat(cm):
        return _block_diag(jnp.transpose(cm.astype(F32), (0, 2, 1)).reshape(ngb, S5_GB * n, k), S5_GB)

    def rows(pw):
        return jnp.transpose(pw.reshape(pw.shape[0], ngb, S5_GB * n), (1, 0, 2))

    return dict(bbd_re=bmat(bb_re), bbd_im=bmat(bb_im), cbd_re=cmat(c_re), cbd_im=cmat(c_im),
                ps_re=rows(ps_re), ps_im=rows(ps_im), pc_re=rows(pc_re), pc_im=rows(pc_im),
                d_vec=d_skip.astype(F32).reshape(ngb, 1, S5_GB * k))


def kernel(x_prompt, x_sample, mem_prompt, cache_fox_k, cache_fox_v, cache_fox_logf, cache_mem_k, cache_mem_v,
           state_ssd, state_conv, state_s5_re, state_s5_im, page_table,
           norm_mix, norm_cross, norm_mem, norm_ffn,
           w_in_even, fox_b_forget, fox_q_norm, fox_k_norm, ssd_conv_w, ssd_conv_b, ssd_dt_bias, ssd_A_log,
           ssd_D, ssd_norm, w_out_even,
           w_in_odd, s5_A_re, s5_A_im, s5_B_re, s5_B_im, s5_C_re, s5_C_im, s5_D, s5_log_dt, s5_w_glu,
           w_mq, w_mkv, mem_q_norm, mem_k_norm, w_mo, w_ffn_up, w_ffn_down):
    depth = norm_mix.shape[0]
    n_even, n_odd = w_in_even.shape[0], w_in_odd.shape[0]
    b, l, d = x_prompt.shape
    z0 = 3 * FOX_W + FOX_HEADS
    dt0 = z0 + SSD_INNER + CONV_DIM
    w_t = jnp.swapaxes(w_in_even, 1, 2)
    W = {
        "norm_mix": norm_mix, "norm_cross": norm_cross, "norm_ffn": norm_ffn,
        "w_in_even_t": w_t,
        "w_tail": jnp.concatenate([w_t[:, 3 * FOX_W:z0], w_t[:, dt0:dt0 + SSD_HEADS],
                                   jnp.zeros((n_even, LANES - FOX_HEADS - SSD_HEADS, d), w_t.dtype)], axis=1),
        "fox_b_forget": fox_b_forget, "fox_q_norm": fox_q_norm, "fox_k_norm": fox_k_norm,
        "ssd_conv_w": ssd_conv_w, "ssd_conv_b": ssd_conv_b, "ssd_dt_bias": ssd_dt_bias, "ssd_A_log": ssd_A_log,
        "ssd_D": ssd_D, "ssd_norm": ssd_norm, "w_out_even": w_out_even,
        "w_in_odd": w_in_odd, "s5_w_glu": s5_w_glu,
        "s5_packed": [_pack_s5(s5_A_re[j], s5_A_im[j], s5_B_re[j], s5_B_im[j], s5_C_re[j], s5_C_im[j],
                               s5_D[j], s5_log_dt[j]) for j in range(n_odd)],
        "w_mq": w_mq, "mem_q_norm": mem_q_norm, "w_mo": w_mo, "w_ffn_up": w_ffn_up, "w_ffn_down": w_ffn_down,
    }
    mt = mem_prompt.shape[1]
    mw = w_mkv.shape[2] // 2
    mem_k_p, mem_v_p = (a.reshape(depth, b, mt, mw) for a in
                        _mem_kv(mem_prompt.reshape(b * mt, d), norm_mem, w_mkv, mem_k_norm))
    n_grp = s5_A_re.shape[1]
    prompt_trunk = _trunk(
        x_prompt, W, mem_k_p, mem_v_p,
        jnp.zeros((n_even, b, SSD_CONV - 1, CONV_DIM), F32),
        jnp.zeros((n_even, b, SSD_HEADS, SSD_HD, SSD_STATE), F32),
        jnp.zeros((n_odd, b, n_grp, S5_STATE), F32),
        jnp.zeros((n_odd, b, n_grp, S5_STATE), F32),
        None)
    db = x_sample.shape[0]
    n_pool, page = cache_fox_k.shape[1], cache_fox_k.shape[2]
    assert page == LANES
    width = page * FOX_HEADS
    lf_t = jnp.transpose(cache_fox_logf.astype(F32), (0, 2, 1, 3)).reshape(n_even, page, n_pool * FOX_HEADS)
    incl = _pool_cumsum(lf_t).reshape(n_even, page, n_pool, FOX_HEADS)
    loc = jnp.transpose(incl, (0, 2, 1, 3)).reshape(n_even, n_pool, 1, width)
    tot = jnp.tile(incl[:, page - 1], (1, 1, page)).reshape(n_even, n_pool, 1, width)
    fox_cache = (cache_fox_k.reshape(n_even, n_pool, width, HEAD_DIM),
                 cache_fox_v.reshape(n_even, n_pool, width, HEAD_DIM), loc, tot, page_table)
    sample_trunk = _trunk(
        x_sample, W, cache_mem_k.reshape(depth, db, mt * MEM_HEADS, HEAD_DIM),
        cache_mem_v.reshape(depth, db, mt * MEM_HEADS, HEAD_DIM),
        state_conv, state_ssd, state_s5_re, state_s5_im, fox_cache)
    (y_prompt, fox_k_p, fox_v_p, fox_logf_p, ssd_p, conv_p, s5_re_p, s5_im_p) = prompt_trunk
    (y_sample, fox_k_s, fox_v_s, fox_logf_s, ssd_s, conv_s, s5_re_s, s5_im_s) = sample_trunk
    hd = mw // MEM_HEADS
    return (y_prompt, y_sample,
            fox_k_p, fox_v_p, fox_logf_p,
            mem_k_p.reshape(depth, b, mt, MEM_HEADS, hd), mem_v_p.reshape(depth, b, mt, MEM_HEADS, hd),
            ssd_p, conv_p, s5_re_p, s5_im_p,
            fox_k_s, fox_v_s, fox_logf_s, ssd_s, conv_s, s5_re_s, s5_im_s)
```

```python
import functools
import math

import jax
import jax.numpy as jnp
from jax import lax
from jax.experimental import pallas as pl
from jax.experimental.pallas import tpu as pltpu

F32 = jnp.float32
BF16 = jnp.bfloat16
EPS = 1e-6
LANES = 128
SUBLANES = 8
VMEM_LIMIT_BYTES = 56 * 1024 * 1024

FOX_HEADS = 8
HEAD_DIM = 128
FOX_W = FOX_HEADS * HEAD_DIM
SSD_HEADS = 32
SSD_HD = 64
SSD_GROUPS = 4
SSD_STATE = 128
SSD_CHUNK = 128
SSD_INNER = SSD_HEADS * SSD_HD
SSD_CONV = 4
CONV_DIM = SSD_INNER + 2 * SSD_GROUPS * SSD_STATE
DT_LANE0 = FOX_HEADS
S5_GROUP = 16
S5_STATE = 64
S5_GB = 16
MEM_HEADS = 4
HIGHEST = lax.Precision.HIGHEST


def _cparams(sem):
    return pltpu.CompilerParams(dimension_semantics=sem, vmem_limit_bytes=VMEM_LIMIT_BYTES)


def _gelu_tanh(x):
    return 0.5 * x * (1.0 + jnp.tanh(math.sqrt(2.0 / math.pi) * (x + 0.044715 * x * x * x)))


def _softplus(x):
    return jnp.maximum(x, 0.0) + jnp.log1p(jnp.exp(-jnp.abs(x)))


def _silu(x):
    return x * jax.nn.sigmoid(x)


def _rms(x, gain):
    return x * lax.rsqrt(jnp.mean(x * x, axis=-1, keepdims=True) + EPS) * gain


def _dot(a, b):
    return jnp.dot(a, b, preferred_element_type=F32)


def _dot_nt(a, b):
    return lax.dot_general(a, b, (((1,), (1,)), ((), ())), preferred_element_type=F32)


def _dot_exact(a, b):
    return jnp.dot(a, b, preferred_element_type=F32, precision=HIGHEST)


def _dense_kernel(*refs, pro, epi, n_out):
    it = iter(refs)
    x_ref = next(it)
    gain_ref = next(it) if pro == "norm" else None
    w_ref = next(it)
    w2_ref = next(it) if epi in ("swiglu", "glu_res") else None
    res_ref = next(it) if epi in ("residual", "glu_res") else None
    hg_ref = next(it) if epi == "headnorm" else None
    out_refs = [next(it) for _ in range(n_out)]
    xs_ref = next(it) if pro != "none" else None

    if pro != "none":
        @pl.when(pl.program_id(1) == 0)
        def _():
            xf = x_ref[...].astype(F32)
            if pro == "norm":
                xf = _rms(xf, gain_ref[...])
            elif pro == "gelu":
                xf = _gelu_tanh(xf)
            xs_ref[...] = xf.astype(BF16)
        lhs = xs_ref[...]
    else:
        lhs = x_ref[...]

    acc = _dot(lhs, w_ref[...].astype(BF16))
    if epi == "swiglu":
        acc = _silu(acc) * _dot(lhs, w2_ref[...].astype(BF16))
    elif epi == "glu_res":
        acc = res_ref[...] + acc * jax.nn.sigmoid(_dot(lhs, w2_ref[...].astype(BF16)))
    elif epi == "residual":
        acc = res_ref[...] + acc

    if epi == "headnorm":
        for c in range(acc.shape[1] // HEAD_DIM):
            sl = slice(c * HEAD_DIM, (c + 1) * HEAD_DIM)
            blk = _rms(acc[:, sl], hg_ref[...])
            for o in out_refs:
                o[:, sl] = blk.astype(o.dtype)
    else:
        for o in out_refs:
            o[...] = acc.astype(o.dtype)


def _wspec(w, layer, k, tn, blk0):
    if w.ndim == 2:
        return pl.BlockSpec((k, tn), lambda i, j: (0, blk0 + j))
    return pl.BlockSpec((None, k, tn), lambda i, j: (layer, 0, blk0 + j))


def _dense(x, w, *, n_cols, tn, name, layer=0, col0=0, col0_2=None, pro="none", gain=None, epi="plain",
           res=None, head_gain=None, out_dtypes=(F32,), tm_max=1024, x_single=False):
    m, k = x.shape
    tm = min(tm_max, m)
    assert m % tm == 0 and n_cols % tn == 0 and col0 % tn == 0
    b0 = col0 // tn
    xmode = dict(pipeline_mode=pl.Buffered(1)) if x_single else {}
    in_specs = [pl.BlockSpec((tm, k), lambda i, j: (i, 0), **xmode)]
    args = [x]
    if pro == "norm":
        in_specs.append(pl.BlockSpec((1, k), lambda i, j: (0, 0)))
        args.append(gain.reshape(1, k).astype(F32))
    in_specs.append(_wspec(w, layer, k, tn, b0))
    args.append(w)
    if epi in ("swiglu", "glu_res"):
        assert col0_2 % tn == 0
        in_specs.append(_wspec(w, layer, k, tn, col0_2 // tn))
        args.append(w)
    if epi in ("residual", "glu_res"):
        in_specs.append(pl.BlockSpec((tm, tn), lambda i, j: (i, j)))
        args.append(res)
    if epi == "headnorm":
        in_specs.append(pl.BlockSpec((1, HEAD_DIM), lambda i, j: (0, 0)))
        args.append(head_gain.reshape(1, HEAD_DIM).astype(F32))
    out_shape = [jax.ShapeDtypeStruct((m, n_cols), dt) for dt in out_dtypes]
    out_specs = [pl.BlockSpec((tm, tn), lambda i, j: (i, j)) for _ in out_dtypes]
    scratch = [pltpu.VMEM((tm, k), BF16)] if pro != "none" else []
    outs = pl.pallas_call(
        functools.partial(_dense_kernel, pro=pro, epi=epi, n_out=len(out_dtypes)),
        grid=(m // tm, n_cols // tn),
        in_specs=in_specs, out_specs=out_specs, out_shape=out_shape,
        scratch_shapes=scratch,
        compiler_params=_cparams(("parallel", "arbitrary")),
        name=name,
    )(*args)
    return outs


def _ffn_kernel(x_ref, g_ref, wg_ref, wu_ref, wd_ref, o_ref, xs_ref):
    @pl.when(pl.program_id(1) == 0)
    def _():
        x = x_ref[...]
        xs_ref[...] = _rms(x, g_ref[...]).astype(BF16)
        o_ref[...] = x
    xs = xs_ref[...]
    hid = _silu(_dot(xs, wg_ref[...].astype(BF16))) * _dot(xs, wu_ref[...].astype(BF16))
    o_ref[...] += _dot(hid.astype(BF16), wd_ref[...].astype(BF16))


def _ffn(x, gain, w_up, w_down, layer, th=256, tm_max=1024):
    m, d = x.shape
    hid = w_down.shape[1]
    tm = min(tm_max, m)
    nth = hid // th
    assert m % tm == 0 and hid % th == 0
    return pl.pallas_call(
        _ffn_kernel,
        grid=(m // tm, nth),
        in_specs=[pl.BlockSpec((tm, d), lambda i, j: (i, 0)),
                  pl.BlockSpec((1, d), lambda i, j: (0, 0)),
                  pl.BlockSpec((None, d, th), lambda i, j: (layer, 0, j)),
                  pl.BlockSpec((None, d, th), lambda i, j: (layer, 0, nth + j)),
                  pl.BlockSpec((None, th, d), lambda i, j: (layer, j, 0))],
        out_specs=pl.BlockSpec((tm, d), lambda i, j: (i, 0)),
        out_shape=jax.ShapeDtypeStruct((m, d), F32),
        scratch_shapes=[pltpu.VMEM((tm, d), BF16)],
        compiler_params=_cparams(("parallel", "arbitrary")),
        name="ffn",
    )(x, gain.reshape(1, d), w_up, w_up, w_down)


def _norm_kernel(x_ref, g_ref, o_ref):
    o_ref[...] = _rms(x_ref[...], g_ref[...]).astype(o_ref.dtype)


def _norm_cast(x, gain, tm_max=1024):
    m, d = x.shape
    tm = min(tm_max, m)
    return pl.pallas_call(
        _norm_kernel,
        grid=(m // tm,),
        in_specs=[pl.BlockSpec((tm, d), lambda i: (i, 0)), pl.BlockSpec((1, d), lambda i: (0, 0))],
        out_specs=pl.BlockSpec((tm, d), lambda i: (i, 0)),
        out_shape=jax.ShapeDtypeStruct((m, d), BF16),
        compiler_params=_cparams(("parallel",)),
        name="rmsnorm",
    )(x, gain.reshape(1, d))


TM_WIDE = 2048


def _inproj_qkv_kernel(xs_ref, w_ref, qg_ref, kg_ref, *rest, nq):
    q_ref, kf_ref, kb_ref, vf_ref, vb_ref = rest[-5:]
    j = pl.program_id(1)

    def headnorm(acc, gain, outs):
        for c in range(acc.shape[1] // HEAD_DIM):
            sl = slice(c * HEAD_DIM, (c + 1) * HEAD_DIM)
            blk = _rms(acc[:, sl], gain)
            for o in outs:
                o[:, sl] = blk.astype(o.dtype)

    @pl.when(j < nq)
    def _():
        headnorm(_dot_nt(xs_ref[...], w_ref[...].astype(BF16)), qg_ref[...] * (HEAD_DIM ** -0.5), (q_ref,))

    @pl.when((j >= nq) & (j < 2 * nq))
    def _():
        headnorm(_dot_nt(xs_ref[...], w_ref[...].astype(BF16)), kg_ref[...], (kf_ref, kb_ref))

    @pl.when(j >= 2 * nq)
    def _():
        acc = _dot_nt(xs_ref[...], w_ref[...].astype(BF16))
        vf_ref[...] = acc
        vb_ref[...] = acc.astype(BF16)


def _inproj_qkv(xn, w_t, layer, n_layers, q_gain, k_gain, stacks, tn=256):
    m, k = xn.shape
    tm = min(TM_WIDE, m)
    nq = FOX_W // tn
    tile = lambda lo: pl.BlockSpec((tm, tn), lambda i, j: (i, jnp.clip(j - lo, 0, nq - 1)))
    stile = lambda lo: pl.BlockSpec((None, tm, tn), lambda i, j: (layer, i, jnp.clip(j - lo, 0, nq - 1)))
    sd = jax.ShapeDtypeStruct
    in_specs = [pl.BlockSpec((tm, k), lambda i, j: (i, 0)),
                pl.BlockSpec((None, tn, k), lambda i, j: (layer, j, 0)),
                pl.BlockSpec((1, HEAD_DIM), lambda i, j: (0, 0)),
                pl.BlockSpec((1, HEAD_DIM), lambda i, j: (0, 0))]
    args = [xn, w_t, q_gain.reshape(1, HEAD_DIM), k_gain.reshape(1, HEAD_DIM)]
    in_specs += [pl.BlockSpec(memory_space=pl.ANY), pl.BlockSpec(memory_space=pl.ANY)]
    aliases = {len(args): 1, len(args) + 1: 3}
    args += list(stacks)
    return pl.pallas_call(
        functools.partial(_inproj_qkv_kernel, nq=nq),
        grid=(m // tm, 3 * nq),
        in_specs=in_specs,
        out_specs=[tile(0), stile(nq), tile(nq), stile(2 * nq), tile(2 * nq)],
        out_shape=[sd((m, FOX_W), BF16), sd((n_layers, m, FOX_W), F32), sd((m, FOX_W), BF16),
                   sd((n_layers, m, FOX_W), F32), sd((m, FOX_W), BF16)],
        input_output_aliases=aliases,
        compiler_params=_cparams(("parallel", "arbitrary")),
        name="inproj_qkv",
    )(*args)


def _inproj_zx_kernel(xs_ref, wzx_ref, wt_ref, zx_ref, tail_ref, *, nzx):
    j = pl.program_id(1)

    @pl.when(j < nzx)
    def _():
        zx_ref[...] = _dot_nt(xs_ref[...], wzx_ref[0].astype(BF16))

    @pl.when(j == nzx)
    def _():
        tail_ref[...] = _dot_nt(xs_ref[...], wt_ref[...].astype(BF16))


def _inproj_zx(xn, w_t, row0, n_zx_cols, w_tail, layer, tn=512):
    m, k = xn.shape
    tm = min(TM_WIDE, m)
    nzx = n_zx_cols // tn
    return pl.pallas_call(
        functools.partial(_inproj_zx_kernel, nzx=nzx),
        grid=(m // tm, nzx + 1),
        in_specs=[pl.BlockSpec((tm, k), lambda i, j: (i, 0)),
                  pl.BlockSpec((pl.Element(1), pl.Element(tn), pl.Element(k)),
                               lambda i, j: (layer, pl.multiple_of(row0 + jnp.minimum(j, nzx - 1) * tn, SUBLANES), 0)),
                  pl.BlockSpec((None, LANES, k), lambda i, j: (layer, 0, 0), pipeline_mode=pl.Buffered(1))],
        out_specs=[pl.BlockSpec((tm, tn), lambda i, j: (i, jnp.minimum(j, nzx - 1))),
                   pl.BlockSpec((tm, LANES), lambda i, j: (i, 0))],
        out_shape=[jax.ShapeDtypeStruct((m, nzx * tn), F32), jax.ShapeDtypeStruct((m, LANES), F32)],
        compiler_params=_cparams(("parallel", "arbitrary")),
        name="inproj_zx",
    )(xn, w_t, w_tail)


def _outproj_kernel(o_ref, y_ref, g_ref, w_ref, res_ref, out_ref, yn_ref):
    k1 = o_ref.shape[1]

    @pl.when(pl.program_id(1) == 0)
    def _():
        yn_ref[...] = _rms(y_ref[...], g_ref[...]).astype(BF16)
    acc = _dot(o_ref[...], w_ref[:k1, :].astype(BF16)) + _dot(yn_ref[...], w_ref[k1:, :].astype(BF16))
    out_ref[...] = res_ref[...] + acc


def _outproj_even(o_fox, yg, gain, w_out, layer, res, tn=256, tm_max=TM_WIDE):
    m, d = res.shape
    tm = min(tm_max, m)
    k1, k2 = o_fox.shape[1], yg.shape[1]
    return pl.pallas_call(
        _outproj_kernel,
        grid=(m // tm, d // tn),
        in_specs=[
            pl.BlockSpec((tm, k1), lambda i, j: (i, 0)),
            pl.BlockSpec((tm, k2), lambda i, j: (i, 0), pipeline_mode=pl.Buffered(1)),
            pl.BlockSpec((1, k2), lambda i, j: (0, 0)),
            pl.BlockSpec((None, k1 + k2, tn), lambda i, j: (layer, 0, j)),
            pl.BlockSpec((tm, tn), lambda i, j: (i, j)),
        ],
        out_specs=pl.BlockSpec((tm, tn), lambda i, j: (i, j)),
        out_shape=jax.ShapeDtypeStruct((m, d), F32),
        scratch_shapes=[pltpu.VMEM((tm, k2), BF16)],
        compiler_params=_cparams(("parallel", "arbitrary")),
        name="outproj_even",
    )(o_fox, yg, gain.reshape(1, k2).astype(F32), w_out, res)


def _lane_cumsum(x):
    lane = lax.broadcasted_iota(jnp.int32, x.shape, 1)
    k = 1
    while k < LANES:
        x = x + jnp.where(lane >= k, pltpu.roll(x, k, axis=1), 0.0)
        k *= 2
    return x


def _forget_kernel(raw_ref, b_ref, lf_ref, negc_ref, ctm_ref):
    lp = raw_ref.shape[1]
    lf = -_softplus(-(raw_ref[0] + b_ref[...]))
    lf_ref[0] = lf[:, :FOX_HEADS]
    row_i = lax.broadcasted_iota(jnp.int32, (LANES, LANES), 0)
    col_i = lax.broadcasted_iota(jnp.int32, (LANES, LANES), 1)
    tri = (col_i <= row_i).astype(F32)
    ctm_ref[0] = _dot_exact(tri, lf[:LANES, :])[:, :FOX_HEADS]
    carry = jnp.zeros((FOX_HEADS, 1), F32)
    for c in range(lp // LANES):
        blk = lf[c * LANES:(c + 1) * LANES, :].T[:FOX_HEADS, :]
        cs = _lane_cumsum(blk) + carry
        negc_ref[0, :, c * LANES:(c + 1) * LANES] = -cs
        carry = cs[:, LANES - 1:LANES]


def _forget(raw, b_vec):
    b, lp, _ = raw.shape
    return pl.pallas_call(
        _forget_kernel,
        grid=(b,),
        in_specs=[pl.BlockSpec((1, lp, LANES), lambda i: (i, 0, 0)),
                  pl.BlockSpec((1, LANES), lambda i: (0, 0))],
        out_specs=[pl.BlockSpec((1, lp, FOX_HEADS), lambda i: (i, 0, 0)),
                   pl.BlockSpec((1, FOX_HEADS, lp), lambda i: (i, 0, 0)),
                   pl.BlockSpec((1, LANES, FOX_HEADS), lambda i: (i, 0, 0))],
        out_shape=[jax.ShapeDtypeStruct((b, lp, FOX_HEADS), F32),
                   jax.ShapeDtypeStruct((b, FOX_HEADS, lp), F32),
                   jax.ShapeDtypeStruct((b, LANES, FOX_HEADS), F32)],
        compiler_params=_cparams(("parallel",)),
        name="forget_gates",
    )(raw, b_vec)


ROW_CHUNK = 64


def _fox_prompt_kernel(q_ref, k_ref, v_ref, nb_ref, o_ref, m_sc, acc_sc, s_sc, p_sc, a_sc, *, tq, tk):
    qi = pl.program_id(1)
    kj = pl.program_id(2)

    @pl.when(kj == 0)
    def _():
        m_sc[...] = jnp.full(m_sc.shape, -jnp.inf, F32)
        acc_sc[...] = jnp.zeros(acc_sc.shape, F32)

    active = kj * tk <= qi * tq + (tq - 1)
    crosses_diagonal = kj * tk + (tk - 1) > qi * tq

    def step(masked):
        rc = min(ROW_CHUNK, tq)
        ones = jnp.ones((tk, HEAD_DIM), BF16)
        if masked:
            row0 = qi * tq + lax.broadcasted_iota(jnp.int32, (rc, tk), 0)
            col = kj * tk + lax.broadcasted_iota(jnp.int32, (rc, tk), 1)
        for h in range(FOX_HEADS):
            sl = slice(h * HEAD_DIM, (h + 1) * HEAD_DIM)
            slot = h % 2
            s_sc[slot] = _dot_nt(q_ref[0, :, sl], k_ref[0, :, sl])
            nb = nb_ref[0, h:h + 1, :]
            for c in range(tq // rc):
                r = slice(c * rc, (c + 1) * rc)
                s = s_sc[slot, r, :] + nb
                if masked:
                    s = jnp.where(col <= row0 + c * rc, s, -jnp.inf)
                m_old = m_sc[h, r, :]
                m_new = jnp.maximum(m_old, jnp.max(s, axis=-1, keepdims=True))
                p_sc[slot, r, :] = jnp.exp(s - m_new).astype(BF16)
                a_sc[slot, r, :] = jnp.exp(m_old - m_new)
                m_sc[h, r, :] = m_new
            pv = _dot(p_sc[slot], jnp.concatenate([v_ref[0, :, sl], ones], axis=1))
            acc_sc[h] = a_sc[slot] * acc_sc[h] + pv

    pl.when(active & crosses_diagonal)(functools.partial(step, True))
    pl.when(active & jnp.logical_not(crosses_diagonal))(functools.partial(step, False))

    @pl.when(kj == pl.num_programs(2) - 1)
    def _():
        for h in range(FOX_HEADS):
            acc = acc_sc[h]
            o_ref[0, :, h * HEAD_DIM:(h + 1) * HEAD_DIM] = (acc[:, :HEAD_DIM] / acc[:, HEAD_DIM:]).astype(o_ref.dtype)


def _fox_prompt(q, k, v, negc, tq=512, tk=512):
    b, l, _ = q.shape
    tq, tk = min(tq, l), min(tk, l)
    nq, nk = l // tq, l // tk

    def last_needed(qi, kj):
        return jnp.minimum(kj, (qi * tq + tq - 1) // tk)

    return pl.pallas_call(
        functools.partial(_fox_prompt_kernel, tq=tq, tk=tk),
        grid=(b, nq, nk),
        in_specs=[pl.BlockSpec((1, tq, FOX_W), lambda bi, qi, kj: (bi, qi, 0)),
                  pl.BlockSpec((1, tk, FOX_W), lambda bi, qi, kj: (bi, last_needed(qi, kj), 0)),
                  pl.BlockSpec((1, tk, FOX_W), lambda bi, qi, kj: (bi, last_needed(qi, kj), 0)),
                  pl.BlockSpec((1, FOX_HEADS, tk), lambda bi, qi, kj: (bi, 0, last_needed(qi, kj)))],
        out_specs=pl.BlockSpec((1, tq, FOX_W), lambda bi, qi, kj: (bi, qi, 0)),
        out_shape=jax.ShapeDtypeStruct((b, l, FOX_W), BF16),
        scratch_shapes=[pltpu.VMEM((FOX_HEADS, tq, 1), F32),
                        pltpu.VMEM((FOX_HEADS, tq, 2 * HEAD_DIM), F32),
                        pltpu.VMEM((2, tq, tk), F32), pltpu.VMEM((2, tq, tk), BF16),
                        pltpu.VMEM((2, tq, 1), F32)],
        compiler_params=_cparams(("parallel", "parallel", "arbitrary")),
        name="fox_prompt",
    )(q, k, v, negc)


def _pool_cumsum_kernel(x_ref, o_ref):
    n = x_ref.shape[0]
    row_i = lax.broadcasted_iota(jnp.int32, (n, n), 0)
    col_i = lax.broadcasted_iota(jnp.int32, (n, n), 1)
    o_ref[...] = _dot_exact((col_i <= row_i).astype(F32), x_ref[...])


def _pool_cumsum(x, tc=2048):
    nl, page, cols = x.shape
    tc = tc if cols % tc == 0 else cols
    return pl.pallas_call(
        _pool_cumsum_kernel,
        grid=(nl, cols // tc),
        in_specs=[pl.BlockSpec((None, page, tc), lambda i, j: (i, 0, j))],
        out_specs=pl.BlockSpec((None, page, tc), lambda i, j: (i, 0, j)),
        out_shape=jax.ShapeDtypeStruct((nl, page, cols), F32),
        compiler_params=_cparams(("parallel", "parallel")),
        name="pool_logf_cumsum",
    )(x)


def _fox_decode_kernel(pt_ref, q_ref, *refs, gp, t_new):
    kv_refs = refs[:4 * gp]
    kn_ref, vn_ref, cn_ref, o_ref, m_sc, l_sc, acc_sc, carry_sc = refs[4 * gp:]
    p = pl.program_id(1)
    n_past = pl.num_programs(1) - 1
    rows = q_ref.shape[1]

    @pl.when(p == 0)
    def _():
        m_sc[...] = jnp.full(m_sc.shape, -jnp.inf, F32)
        l_sc[...] = jnp.zeros(l_sc.shape, F32)
        acc_sc[...] = jnp.zeros(acc_sc.shape, F32)
        carry_sc[...] = jnp.zeros(carry_sc.shape, F32)

    def update(s, v, state):
        m_old, l_old, acc = state
        m_new = jnp.maximum(m_old, jnp.max(s, axis=-1, keepdims=True))
        alpha = jnp.exp(m_old - m_new)
        pr = jnp.exp(s - m_new)
        l_new = alpha * l_old + jnp.sum(pr, axis=-1, keepdims=True)
        acc = alpha * acc + _dot(pr.astype(BF16), v.astype(BF16))
        return m_new, l_new, acc

    @pl.when(p < n_past)
    def _():
        q = q_ref[0]
        width = kv_refs[0].shape[0]
        row = lax.broadcasted_iota(jnp.int32, (rows, width), 0)
        col = lax.broadcasted_iota(jnp.int32, (rows, width), 1)
        own_head = (col % FOX_HEADS) == (row // t_new)
        carry = carry_sc[...]
        scores = []
        for g in range(gp):
            k_ref, _, loc_ref, tot_ref = kv_refs[4 * g:4 * g + 4]
            s = _dot_nt(q, k_ref[...].astype(BF16)) - (carry + loc_ref[...])
            scores.append(jnp.where(own_head, s, -jnp.inf))
            carry = carry + tot_ref[...]
        carry_sc[...] = carry
        m_old = m_sc[...]
        m_new = m_old
        for s in scores:
            m_new = jnp.maximum(m_new, jnp.max(s, axis=-1, keepdims=True))
        alpha = jnp.exp(m_old - m_new)
        l_new = alpha * l_sc[...]
        acc = alpha * acc_sc[...]
        for g, s in enumerate(scores):
            pr = jnp.exp(s - m_new)
            l_new = l_new + jnp.sum(pr, axis=-1, keepdims=True)
            acc = acc + _dot(pr.astype(BF16), kv_refs[4 * g + 1][...].astype(BF16))
        m_sc[...], l_sc[...], acc_sc[...] = m_new, l_new, acc

    @pl.when(p == n_past)
    def _():
        width = kn_ref.shape[1]
        row = lax.broadcasted_iota(jnp.int32, (rows, width), 0)
        col = lax.broadcasted_iota(jnp.int32, (rows, width), 1)
        keep = ((col % FOX_HEADS) == (row // t_new)) & ((col // FOX_HEADS) <= (row % t_new))
        s = _dot_nt(q_ref[0], kn_ref[0].astype(BF16)) - (carry_sc[:, :width] + cn_ref[0])
        m, l, acc = update(jnp.where(keep, s, -jnp.inf), vn_ref[0], (m_sc[...], l_sc[...], acc_sc[...]))
        o_ref[0] = (acc / l).astype(o_ref.dtype)


def _fox_decode(q, k_new, v_new, c_new, k_pool, v_pool, loc, tot, page_table, layer, t_new, gp=16):
    b, rows, _ = q.shape
    n_pages = page_table.shape[1]
    gp = max(g for g in range(1, gp + 1) if n_pages % g == 0)
    n_steps = n_pages // gp
    width = k_pool.shape[2]

    def page_map(g):
        return lambda bi, p, pt: (layer, pt[bi, jnp.minimum(p, n_steps - 1) * gp + g], 0, 0)

    in_specs = [pl.BlockSpec((1, rows, HEAD_DIM), lambda bi, p, pt: (bi, 0, 0))]
    args = [q]
    for g in range(gp):
        in_specs += [pl.BlockSpec((None, None, width, HEAD_DIM), page_map(g)),
                     pl.BlockSpec((None, None, width, HEAD_DIM), page_map(g)),
                     pl.BlockSpec((None, None, 1, width), page_map(g)),
                     pl.BlockSpec((None, None, 1, width), page_map(g))]
        args += [k_pool, v_pool, loc, tot]
    new_spec = pl.BlockSpec((1, k_new.shape[1], HEAD_DIM), lambda bi, p, pt: (bi, 0, 0))
    in_specs += [new_spec, new_spec, pl.BlockSpec((1, 1, c_new.shape[2]), lambda bi, p, pt: (bi, 0, 0))]
    args += [k_new, v_new, c_new]
    grid_spec = pltpu.PrefetchScalarGridSpec(
        num_scalar_prefetch=1,
        grid=(b, n_steps + 1),
        in_specs=in_specs,
        out_specs=pl.BlockSpec((1, rows, HEAD_DIM), lambda bi, p, pt: (bi, 0, 0)),
        scratch_shapes=[pltpu.VMEM((rows, 1), F32), pltpu.VMEM((rows, 1), F32),
                        pltpu.VMEM((rows, HEAD_DIM), F32), pltpu.VMEM((1, width), F32)],
    )
    return pl.pallas_call(
        functools.partial(_fox_decode_kernel, gp=gp, t_new=t_new), grid_spec=grid_spec,
        out_shape=jax.ShapeDtypeStruct((b, rows, HEAD_DIM), BF16),
        compiler_params=_cparams(("parallel", "arbitrary")),
        name="fox_decode",
    )(page_table, *args)


def _conv_kernel(x_ref, buf_ref, w_ref, b_ref, y_ref, nb_ref, full_sc):
    l = x_ref.shape[1]
    pad = SUBLANES
    full_sc[pad - (SSD_CONV - 1):pad, :] = buf_ref[0]
    full_sc[pad:pad + l, :] = x_ref[0]
    acc = b_ref[...] + full_sc[pad - 3:pad - 3 + l, :] * w_ref[0:1, :]
    for kk in range(1, SSD_CONV):
        acc = acc + full_sc[pad - 3 + kk:pad - 3 + kk + l, :] * w_ref[kk:kk + 1, :]
    y_ref[0] = _silu(acc)
    nb_ref[0] = full_sc[pad + l - (SSD_CONV - 1):pad + l, :]


def _conv(zx, col0, buf, w, bias, tc=512):
    b, l, _ = zx.shape
    c = w.shape[1]
    cb0 = col0 // tc
    return pl.pallas_call(
        _conv_kernel,
        grid=(b, c // tc),
        in_specs=[pl.BlockSpec((1, l, tc), lambda i, j: (i, 0, cb0 + j)),
                  pl.BlockSpec((1, SSD_CONV - 1, tc), lambda i, j: (i, 0, j)),
                  pl.BlockSpec((SSD_CONV, tc), lambda i, j: (0, j)),
                  pl.BlockSpec((1, tc), lambda i, j: (0, j))],
        out_specs=[pl.BlockSpec((1, l, tc), lambda i, j: (i, 0, j)),
                   pl.BlockSpec((1, SSD_CONV - 1, tc), lambda i, j: (i, 0, j))],
        out_shape=[jax.ShapeDtypeStruct((b, l, c), F32),
                   jax.ShapeDtypeStruct((b, SSD_CONV - 1, c), F32)],
        scratch_shapes=[pltpu.VMEM((l + SUBLANES, tc), F32)],
        compiler_params=_cparams(("parallel", "parallel")),
        name="ssd_conv",
    )(zx, buf, w, bias.reshape(1, c))


def _ssd_kernel(xc_ref, tail_ref, z_ref, dtb_ref, alog_ref, dexp_ref, e_ref, h0_ref,
                y_ref, hout_ref, h_sc, *, valid_len):
    c = pl.program_id(1)
    q = SSD_CHUNK
    gw = SSD_INNER // SSD_GROUPS
    hpg = SSD_HEADS // SSD_GROUPS

    @pl.when(c == 0)
    def _():
        h_sc[...] = h0_ref[0]

    lane = lax.broadcasted_iota(jnp.int32, (q, LANES), 1)
    row = lax.broadcasted_iota(jnp.int32, (q, LANES), 0)
    col_i = lax.broadcasted_iota(jnp.int32, (q, q), 1)
    row_i = lax.broadcasted_iota(jnp.int32, (q, q), 0)
    causal = col_i <= row_i
    tri = causal.astype(F32)

    dt = _softplus(tail_ref[0] + dtb_ref[...])
    live = (lane >= DT_LANE0) & (lane < DT_LANE0 + SSD_HEADS) & (c * q + row < valid_len)
    dt = jnp.where(live, dt, 0.0)
    a = dt * (-jnp.exp(alog_ref[...]))
    a_cum = _dot_exact(tri, a)
    a_cum_t = a_cum.T
    dt_t = dt.T
    a_last = a_cum[q - 1:q, :]
    fac = jnp.concatenate([dt * jnp.exp(a_last - a_cum), jnp.exp(a_cum)], axis=0)
    fac_hi = fac.astype(BF16)
    fac_lo = (fac - fac_hi.astype(F32)).astype(BF16)
    fac2 = jnp.concatenate([fac_hi, fac_lo], axis=0)

    for g in range(SSD_GROUPS):
        ex = _dot(fac2, e_ref[g])
        ex = ex[:2 * q] + ex[2 * q:]
        w1_e, ea_e = ex[:q], ex[q:]
        xs = xc_ref[0, :, g * gw:(g + 1) * gw]
        bb = xc_ref[0, :, SSD_INNER + g * SSD_STATE:SSD_INNER + (g + 1) * SSD_STATE].astype(BF16)
        cc = xc_ref[0, :, SSD_INNER + (SSD_GROUPS + g) * SSD_STATE:
                    SSD_INNER + (SSD_GROUPS + g + 1) * SSD_STATE].astype(BF16)
        cb = _dot_nt(cc, bb)
        hg = h_sc[g * gw:(g + 1) * gw, :]
        y = _dot_nt(cc, hg.astype(BF16)) * ea_e
        st = _dot((xs * w1_e).T.astype(BF16), bb)
        for hp in range(hpg // 2):
            pair = xs[:, hp * LANES:(hp + 1) * LANES]
            lane_p = lax.broadcasted_iota(jnp.int32, pair.shape, 1)
            yp = None
            for sub in range(2):
                h = 2 * hp + sub
                ln = DT_LANE0 + g * hpg + h
                seg = a_cum[:, ln:ln + 1] - a_cum_t[ln:ln + 1, :]
                dec = jnp.exp(jnp.where(causal, seg, -jnp.inf))
                mat = (cb * dec * dt_t[ln:ln + 1, :]).astype(BF16)
                in_head = (lane_p >= sub * SSD_HD) & (lane_p < (sub + 1) * SSD_HD)
                rhs = jnp.where(in_head, pair, 0.0).astype(BF16)
                part = _dot(mat, rhs)
                yp = part if yp is None else yp + part
                r0 = g * gw + h * SSD_HD
                h_sc[r0:r0 + SSD_HD, :] = (hg[h * SSD_HD:(h + 1) * SSD_HD, :] * jnp.exp(a_last[:, ln:ln + 1])
                                           + st[h * SSD_HD:(h + 1) * SSD_HD, :])
            cs = slice(g * gw + hp * LANES, g * gw + (hp + 1) * LANES)
            yt = yp + y[:, hp * LANES:(hp + 1) * LANES] + dexp_ref[:, cs] * xs[:, hp * LANES:(hp + 1) * LANES]
            y_ref[0, :, cs] = yt * _silu(z_ref[0, :, cs])

    @pl.when(c == pl.num_programs(1) - 1)
    def _():
        hout_ref[0] = h_sc[...]


def _ssd(xc, tail, zx, dtb_vec, alog_vec, d_exp, e_mat, h0, valid_len):
    b, lp, _ = xc.shape
    nc = lp // SSD_CHUNK
    return pl.pallas_call(
        functools.partial(_ssd_kernel, valid_len=valid_len),
        grid=(b, nc),
        in_specs=[pl.BlockSpec((1, SSD_CHUNK, CONV_DIM), lambda i, c: (i, c, 0)),
                  pl.BlockSpec((1, SSD_CHUNK, LANES), lambda i, c: (i, c, 0)),
                  pl.BlockSpec((1, SSD_CHUNK, SSD_INNER), lambda i, c: (i, c, 0)),
                  pl.BlockSpec((1, LANES), lambda i, c: (0, 0)),
                  pl.BlockSpec((1, LANES), lambda i, c: (0, 0)),
                  pl.BlockSpec((1, SSD_INNER), lambda i, c: (0, 0)),
                  pl.BlockSpec((SSD_GROUPS, LANES, SSD_INNER // SSD_GROUPS), lambda i, c: (0, 0, 0)),
                  pl.BlockSpec((1, SSD_INNER, SSD_STATE), lambda i, c: (i, 0, 0))],
        out_specs=[pl.BlockSpec((1, SSD_CHUNK, SSD_INNER), lambda i, c: (i, c, 0)),
                   pl.BlockSpec((1, SSD_INNER, SSD_STATE), lambda i, c: (i, 0, 0))],
        out_shape=[jax.ShapeDtypeStruct((b, lp, SSD_INNER), F32),
                   jax.ShapeDtypeStruct((b, SSD_INNER, SSD_STATE), F32)],
        scratch_shapes=[pltpu.VMEM((SSD_INNER, SSD_STATE), F32)],
        compiler_params=_cparams(("parallel", "arbitrary")),
        name="ssd_chunked",
    )(xc, tail, zx, dtb_vec, alog_vec, d_exp, e_mat, h0)


S5_TILE = 256
S5_STEPS = S5_TILE // SUBLANES


def _cmul(ar, ai, br, bi):
    return ar * br - ai * bi, ar * bi + ai * br


def _s5_prep_kernel(are_ref, aim_ref, ldt_ref, bre_ref, bim_ref, psr_ref, psi_ref, pcr_ref, pci_ref,
                    bbr_ref, bbi_ref):
    lam_re = jnp.minimum(are_ref[...], -1e-4)
    lam_im = aim_ref[...]
    dt = jnp.exp(ldt_ref[...])
    mag = jnp.exp(lam_re * dt)
    ang = lam_im * dt
    lb_re, lb_im = mag * jnp.cos(ang), mag * jnp.sin(ang)
    nr, ni = lb_re - 1.0, lb_im
    den = lam_re * lam_re + lam_im * lam_im
    coef_re = (nr * lam_re + ni * lam_im) / den
    coef_im = (ni * lam_re - nr * lam_im) / den
    for k in range(S5_GROUP):
        br, bi = bre_ref[k], bim_ref[k]
        bbr_ref[k] = coef_re * br - coef_im * bi
        bbi_ref[k] = coef_re * bi + coef_im * br
    pr, pi = lb_re, lb_im
    for r in range(S5_STEPS):
        psr_ref[r], psi_ref[r] = pr, pi
        if r + 1 < S5_STEPS:
            pr, pi = _cmul(pr, pi, lb_re, lb_im)
    qr, qi = pr, pi
    for c in range(SUBLANES):
        pcr_ref[c], pci_ref[c] = qr, qi
        if c + 1 < SUBLANES:
            qr, qi = _cmul(qr, qi, pr, pi)


def _s5_prep(a_re, a_im, log_dt, b_re_t, b_im_t):
    g, n = a_re.shape
    sd = jax.ShapeDtypeStruct
    return pl.pallas_call(
        _s5_prep_kernel,
        out_shape=[sd((S5_STEPS, g, n), F32), sd((S5_STEPS, g, n), F32),
                   sd((SUBLANES, g, n), F32), sd((SUBLANES, g, n), F32),
                   sd((S5_GROUP, g, n), F32), sd((S5_GROUP, g, n), F32)],
        name="s5_discretise",
    )(a_re, a_im, log_dt.reshape(g, 1), b_re_t, b_im_t)


def _s5_kernel(u_ref, perm_ref, permt_ref, bre_ref, bim_ref, cre_ref, cim_ref, psr_ref, psi_ref,
               pcr_ref, pci_ref, d_ref, s0r_ref, s0i_ref, g_ref, sr_ref, si_ref, xr_sc, xi_sc, *, tt, nh):
    l = u_ref.shape[1]
    w = xr_sc.shape[2]
    cw = u_ref.shape[2] // nh
    ts = tt // SUBLANES
    rows = lax.broadcasted_iota(jnp.int32, (SUBLANES, w), 0)
    bc = lambda v: jnp.broadcast_to(v, (SUBLANES, w))

    def blk(i):
        return slice(i * SUBLANES, (i + 1) * SUBLANES)

    def tile(t, carry):
        t0 = pl.multiple_of(t * tt, tt)
        us = [u_ref[0, pl.ds(t0, tt), h * cw:(h + 1) * cw] for h in range(nh)]
        for h in range(nh):
            up = us[h].astype(BF16)
            if ts > 1:
                up = _dot(perm_ref[...], up).astype(BF16)
            xr_sc[h] = _dot(up, bre_ref[h])
            xi_sc[h] = _dot(up, bim_ref[h])
        ends = []
        for h in range(nh):
            lam_r, lam_i = bc(psr_ref[h, 0:1, :]), bc(psi_ref[h, 0:1, :])

            def pass1(i, st, h=h, lam_r=lam_r, lam_i=lam_i):
                sr, si = _cmul(lam_r, lam_i, st[0], st[1])
                sr, si = sr + xr_sc[h, blk(i), :], si + xi_sc[h, blk(i), :]
                xr_sc[h, blk(i), :] = sr
                xi_sc[h, blk(i), :] = si
                return sr, si

            st = (jnp.zeros((SUBLANES, w), F32), jnp.zeros((SUBLANES, w), F32))
            for i in range(ts):
                st = pass1(i, st)
            ends.append(st)
        new_carry = []
        entries = []
        for h in range(nh):
            tr, ti = ends[h]
            cr, ci = carry[2 * h], carry[2 * h + 1]
            pcr, pci = pcr_ref[h], pci_ref[h]
            for kk in (1, 2, 4):
                keep = rows >= kk
                qr = jnp.where(keep, bc(pcr[kk - 1:kk, :]), 0.0)
                qi = jnp.where(keep, bc(pci[kk - 1:kk, :]), 0.0)
                dr, di = _cmul(qr, qi, pltpu.roll(tr, kk, axis=0), pltpu.roll(ti, kk, axis=0))
                tr, ti = tr + dr, ti + di
            dr, di = _cmul(pcr, pci, bc(cr), bc(ci))
            tr, ti = tr + dr, ti + di
            entries.append((jnp.where(rows == 0, bc(cr), pltpu.roll(tr, 1, axis=0)),
                            jnp.where(rows == 0, bc(ci), pltpu.roll(ti, 1, axis=0))))
            new_carry += [tr[SUBLANES - 1:SUBLANES, :], ti[SUBLANES - 1:SUBLANES, :]]
        for h in range(nh):
            er, ei = entries[h]

            def pass2(i, _, h=h, er=er, ei=ei):
                pr, pi = bc(psr_ref[h, i:i + 1, :]), bc(psi_ref[h, i:i + 1, :])
                dr, di = _cmul(pr, pi, er, ei)
                xr_sc[h, blk(i), :] = xr_sc[h, blk(i), :] + dr
                xi_sc[h, blk(i), :] = xi_sc[h, blk(i), :] + di
                return 0

            for i in range(ts):
                pass2(i, 0)
        for h in range(nh):
            y = _dot(xr_sc[h].astype(BF16), cre_ref[h]) - _dot(xi_sc[h].astype(BF16), cim_ref[h])
            if ts > 1:
                hi = y.astype(BF16)
                mid = (y - hi.astype(F32)).astype(BF16)
                pt = permt_ref[...]
                y = _dot(pt, hi) + _dot(pt, mid)
            y = y + d_ref[h] * us[h]
            g_ref[0, pl.ds(t0, tt), h * cw:(h + 1) * cw] = _gelu_tanh(y).astype(g_ref.dtype)
        return tuple(new_carry)

    init = []
    for h in range(nh):
        init += [s0r_ref[0, h], s0i_ref[0, h]]
    fin = lax.fori_loop(0, l // tt, tile, tuple(init))
    for h in range(nh):
        sr_ref[0, h] = fin[2 * h]
        si_ref[0, h] = fin[2 * h + 1]


def _s5(u, p, s0_re, s0_im, nh=4):
    b, l, d = u.shape
    ngb = p["bbd_re"].shape[0]
    cw = d // ngb
    sw = p["bbd_re"].shape[2]
    tt = min(S5_TILE, l)
    ts = tt // SUBLANES
    assert l % tt == 0 and ts in (1, S5_STEPS) and ngb % nh == 0
    steps_re, steps_im = p["ps_re"][:, :ts], p["ps_im"][:, :ts]
    chunk_re, chunk_im = (p["pc_re"], p["pc_im"]) if ts == S5_STEPS else (p["ps_re"][:, :SUBLANES],
                                                                          p["ps_im"][:, :SUBLANES])
    r = jnp.arange(tt)
    perm = (r[None, :] == ((r % SUBLANES) * ts + r // SUBLANES)[:, None]).astype(BF16)
    state_spec = pl.BlockSpec((1, nh, 1, sw), lambda j, i: (i, j, 0, 0))
    whole = lambda a: pl.BlockSpec(a.shape, lambda j, i: (0,) * a.ndim)
    per_gb = lambda a: pl.BlockSpec((nh,) + a.shape[1:], lambda j, i: (j, 0, 0))
    args = [perm, perm.T, p["bbd_re"], p["bbd_im"], p["cbd_re"], p["cbd_im"], steps_re, steps_im,
            chunk_re, chunk_im, p["d_vec"]]
    return pl.pallas_call(
        functools.partial(_s5_kernel, tt=tt, nh=nh),
        grid=(ngb // nh, b),
        in_specs=[pl.BlockSpec((1, l, nh * cw), lambda j, i: (i, 0, j)), whole(perm), whole(perm)]
                 + [per_gb(a) for a in args[2:]] + [state_spec, state_spec],
        out_specs=[pl.BlockSpec((1, l, nh * cw), lambda j, i: (i, 0, j)), state_spec, state_spec],
        out_shape=[jax.ShapeDtypeStruct((b, l, d), BF16),
                   jax.ShapeDtypeStruct((b, ngb, 1, sw), F32),
                   jax.ShapeDtypeStruct((b, ngb, 1, sw), F32)],
        scratch_shapes=[pltpu.VMEM((nh, tt, sw), F32), pltpu.VMEM((nh, tt, sw), F32)],
        compiler_params=_cparams(("parallel", "parallel")),
        name="s5_scan",
    )(u, *args, s0_re, s0_im)


def _cross_kernel(x_ref, g_ref, wq_ref, qg_ref, k_ref, v_ref, wo_ref, o_ref, att_sc, s_sc, p_sc):
    x = x_ref[0]
    tq = x.shape[0]
    rc = min(2 * ROW_CHUNK, tq)
    xn = _rms(x, g_ref[...]).astype(BF16)
    q = _dot(xn, wq_ref[...].astype(BF16))
    for h in range(MEM_HEADS):
        sl = slice(h * HEAD_DIM, (h + 1) * HEAD_DIM)
        slot = h % 2
        qh = _rms(q[:, sl], qg_ref[...]).astype(BF16)
        s_sc[slot] = _dot_nt(qh, k_ref[:, sl].astype(BF16))
        for c in range(tq // rc):
            r = slice(c * rc, (c + 1) * rc)
            s = s_sc[slot, r, :] * (HEAD_DIM ** -0.5)
            p = jnp.exp(s - jnp.max(s, axis=-1, keepdims=True))
            p_sc[slot, r, :] = (p / jnp.sum(p, axis=-1, keepdims=True)).astype(BF16)
        att_sc[:, sl] = _dot(p_sc[slot], v_ref[:, sl].astype(BF16)).astype(BF16)
    o_ref[0] = x + _dot(att_sc[...], wo_ref[...].astype(BF16))


def _cross_attn(x, gain, w_q, q_gain, mem_k, mem_v, w_o, layer, tq=512):
    b, l, d = x.shape
    mt, mw = mem_k.shape[2], mem_k.shape[3]
    tq = min(tq, l)
    return pl.pallas_call(
        _cross_kernel,
        grid=(b, l // tq),
        in_specs=[pl.BlockSpec((1, tq, d), lambda i, j: (i, j, 0)),
                  pl.BlockSpec((1, d), lambda i, j: (0, 0)),
                  pl.BlockSpec((None, d, mw), lambda i, j: (layer, 0, 0)),
                  pl.BlockSpec((1, HEAD_DIM), lambda i, j: (0, 0)),
                  pl.BlockSpec((None, None, mt, mw), lambda i, j: (layer, i, 0, 0)),
                  pl.BlockSpec((None, None, mt, mw), lambda i, j: (layer, i, 0, 0)),
                  pl.BlockSpec((None, mw, d), lambda i, j: (layer, 0, 0))],
        out_specs=pl.BlockSpec((1, tq, d), lambda i, j: (i, j, 0)),
        out_shape=jax.ShapeDtypeStruct((b, l, d), F32),
        scratch_shapes=[pltpu.VMEM((tq, mw), BF16), pltpu.VMEM((2, tq, mt), F32),
                        pltpu.VMEM((2, tq, mt), BF16)],
        compiler_params=_cparams(("parallel", "parallel")),
        name="cross_attn",
    )(x, gain.reshape(1, d), w_q, q_gain.reshape(1, HEAD_DIM), mem_k, mem_v, w_o)


def _cross_sample_kernel(x_ref, g_ref, wq_ref, qg_ref, k_ref, v_ref, wo_ref, o_ref, att_sc, *, nb, t):
    x = x_ref[...]
    xn = _rms(x, g_ref[...]).astype(BF16)
    q = _dot(xn, wq_ref[...].astype(BF16))
    qn = [_rms(q[:, h * HEAD_DIM:(h + 1) * HEAD_DIM], qg_ref[...]) for h in range(MEM_HEADS)]
    rows, width = MEM_HEADS * t, k_ref.shape[1]
    row = lax.broadcasted_iota(jnp.int32, (rows, width), 0)
    col = lax.broadcasted_iota(jnp.int32, (rows, width), 1)
    own_head = (col % MEM_HEADS) == (row // t)
    for bi in range(nb):
        q_ht = jnp.concatenate([qh[bi * t:(bi + 1) * t] for qh in qn], axis=0).astype(BF16)
        s = _dot_nt(q_ht, k_ref[bi].astype(BF16)) * (HEAD_DIM ** -0.5)
        s = jnp.where(own_head, s, -jnp.inf)
        p = jnp.exp(s - jnp.max(s, axis=-1, keepdims=True))
        p = p / jnp.sum(p, axis=-1, keepdims=True)
        o = _dot(p.astype(BF16), v_ref[bi].astype(BF16))
        for h in range(MEM_HEADS):
            att_sc[bi * t:(bi + 1) * t, h * HEAD_DIM:(h + 1) * HEAD_DIM] = o[h * t:(h + 1) * t]
    o_ref[...] = x + _dot(att_sc[...].astype(BF16), wo_ref[...].astype(BF16))


def _cross_attn_sample(x, gain, w_q, q_gain, mem_k, mem_v, w_o, layer):
    b, t, d = x.shape
    mrows = mem_k.shape[2]
    mw = w_q.shape[2]
    assert t % SUBLANES == 0
    whole = lambda *shape: pl.BlockSpec(shape, lambda i: (0,) * len(shape))
    out = pl.pallas_call(
        functools.partial(_cross_sample_kernel, nb=b, t=t),
        grid=(1,),
        in_specs=[whole(b * t, d), whole(1, d),
                  pl.BlockSpec((None, d, mw), lambda i: (layer, 0, 0)),
                  whole(1, HEAD_DIM),
                  pl.BlockSpec((None, b, mrows, HEAD_DIM), lambda i: (layer, 0, 0, 0)),
                  pl.BlockSpec((None, b, mrows, HEAD_DIM), lambda i: (layer, 0, 0, 0)),
                  pl.BlockSpec((None, mw, d), lambda i: (layer, 0, 0))],
        out_specs=whole(b * t, d),
        out_shape=jax.ShapeDtypeStruct((b * t, d), F32),
        scratch_shapes=[pltpu.VMEM((b * t, mw), F32)],
        compiler_params=_cparams(("arbitrary",)),
        name="cross_attn_sample",
    )(x.reshape(b * t, d), gain.reshape(1, d), w_q, q_gain.reshape(1, HEAD_DIM), mem_k, mem_v, w_o)
    return out.reshape(b, t, d)


def _mem_kv_kernel(x_ref, g_ref, w_ref, hg_ref, k_ref, v_ref, xs_ref):
    j = pl.program_id(1)

    @pl.when(j == 0)
    def _():
        xs_ref[...] = _rms(x_ref[...], g_ref[...]).astype(BF16)

    acc = _dot(xs_ref[...], w_ref[...].astype(BF16))

    @pl.when(j == 0)
    def _():
        for c in range(acc.shape[1] // HEAD_DIM):
            sl = slice(c * HEAD_DIM, (c + 1) * HEAD_DIM)
            k_ref[:, sl] = _rms(acc[:, sl], hg_ref[...])

    @pl.when(j == 1)
    def _():
        v_ref[...] = acc


def _mem_kv(mem, gains, w_mkv, k_gains):
    m, d = mem.shape
    depth = w_mkv.shape[0]
    mw = w_mkv.shape[2] // 2
    sd = jax.ShapeDtypeStruct
    return pl.pallas_call(
        _mem_kv_kernel,
        grid=(depth, 2),
        in_specs=[pl.BlockSpec((m, d), lambda i, j: (0, 0)),
                  pl.BlockSpec((None, 1, d), lambda i, j: (i, 0, 0)),
                  pl.BlockSpec((None, d, mw), lambda i, j: (i, 0, j)),
                  pl.BlockSpec((None, 1, HEAD_DIM), lambda i, j: (i, 0, 0))],
        out_specs=[pl.BlockSpec((None, m, mw), lambda i, j: (i, 0, 0)),
                   pl.BlockSpec((None, m, mw), lambda i, j: (i, 0, 0))],
        out_shape=[sd((depth, m, mw), F32), sd((depth, m, mw), F32)],
        scratch_shapes=[pltpu.VMEM((m, d), BF16)],
        compiler_params=_cparams(("arbitrary", "arbitrary")),
        name="mem_kv_proj",
    )(mem, gains.reshape(depth, 1, d), w_mkv, k_gains.reshape(depth, 1, HEAD_DIM))


def _lane_vec(vals, lane0):
    v = jnp.zeros((1, LANES), F32)
    return v.at[0, lane0:lane0 + vals.shape[0]].set(vals.astype(F32))


def _head_expand():
    rows = jnp.arange(LANES)[None, :, None]
    cols = jnp.arange(SSD_INNER // SSD_GROUPS)[None, None, :]
    g = jnp.arange(SSD_GROUPS)[:, None, None]
    return (rows == DT_LANE0 + g * (SSD_HEADS // SSD_GROUPS) + cols // SSD_HD).astype(BF16)


def _block_diag(m, reps):
    nb, rows, c = m.shape
    t = jnp.tile(m, (1, 1, reps))
    rb = jnp.arange(rows)[:, None] // (rows // reps)
    cb = jnp.arange(reps * c)[None, :] // c
    return jnp.where((rb == cb)[None], t, 0.0).astype(BF16)


def _pad_rows(x, rows):
    return jnp.pad(x, ((0, 0), (0, rows - x.shape[1]), (0, 0)))


def _trunk(x, W, mem_k, mem_v, conv0, ssm0, s5_re0, s5_im0, fox_cache):
    b, l, d = x.shape
    t = b * l
    depth = W["norm_mix"].shape[0]
    lp = max(l, SSD_CHUNK)
    e_mat = _head_expand()
    fl, hs, bufs, srs, sis = [], [], [], [], []
    n_even = W["w_tail"].shape[0]
    kv_stacks = (jnp.zeros((n_even, t, FOX_W), F32), jnp.zeros((n_even, t, FOX_W), F32))
    x2 = x.reshape(t, d)
    for i in range(depth):
        j = i // 2
        xn = _norm_cast(x2, W["norm_mix"][i])
        if i % 2 == 0:
            q_bf, kf_all, k_bf, vf_all, v_bf = _inproj_qkv(xn, W["w_in_even_t"], j, n_even,
                                                           W["fox_q_norm"][j], W["fox_k_norm"][j], kv_stacks)
            kv_stacks = (kf_all, vf_all)
            zx, tail = _inproj_zx(xn, W["w_in_even_t"], 3 * FOX_W + FOX_HEADS,
                                  SSD_INNER + CONV_DIM, W["w_tail"], j)
            zx3 = zx.reshape(b, l, -1)
            tail3 = tail.reshape(b, l, LANES)
            tail_p = _pad_rows(tail3, lp) if lp != l else tail3
            lf, negc, c_tm = _forget(tail_p, _lane_vec(W["fox_b_forget"][j], 0))
            lf = lf[:, :l]
            if fox_cache is None:
                o_fox = _fox_prompt(q_bf.reshape(b, l, FOX_W), k_bf.reshape(b, l, FOX_W),
                                    v_bf.reshape(b, l, FOX_W), negc).reshape(t, FOX_W)
            else:
                k_pool, v_pool, loc, tot, page_table = fox_cache
                rows = FOX_HEADS * l
                q_ht = jnp.transpose(q_bf.reshape(b, l, FOX_HEADS, HEAD_DIM), (0, 2, 1, 3)).reshape(b, rows, HEAD_DIM)
                o_ht = _fox_decode(q_ht, _pad_rows(kf_all[j].reshape(b, rows, HEAD_DIM), LANES),
                                   _pad_rows(vf_all[j].reshape(b, rows, HEAD_DIM), LANES),
                                   _pad_rows(c_tm[:, :l].reshape(b, rows, 1), LANES).reshape(b, 1, LANES),
                                   k_pool, v_pool, loc, tot, page_table, j, l)
                o_fox = jnp.transpose(o_ht.reshape(b, FOX_HEADS, l, HEAD_DIM), (0, 2, 1, 3)).reshape(t, FOX_W)
            xc, new_buf = _conv(zx3, SSD_INNER, conv0[j], W["ssd_conv_w"][j], W["ssd_conv_b"][j])
            if lp != l:
                xc_p, z_p = _pad_rows(xc, lp), _pad_rows(zx3[:, :, :SSD_INNER], lp)
            else:
                xc_p, z_p = xc, zx3
            yg, h_last = _ssd(xc_p, tail_p, z_p, _lane_vec(W["ssd_dt_bias"][j], DT_LANE0),
                              _lane_vec(W["ssd_A_log"][j], DT_LANE0),
                              jnp.repeat(W["ssd_D"][j].astype(F32), SSD_HD).reshape(1, SSD_INNER),
                              e_mat, ssm0[j].reshape(b, SSD_INNER, SSD_STATE), l)
            yg = yg[:, :l].reshape(t, SSD_INNER)
            x2 = _outproj_even(o_fox, yg, W["ssd_norm"][j], W["w_out_even"], j, x2)
            fl.append(lf)
            hs.append(h_last.reshape(b, SSD_HEADS, SSD_HD, SSD_STATE))
            bufs.append(new_buf)
        else:
            (u,) = _dense(xn, W["w_in_odd"], layer=j, n_cols=d, tn=512, name="inproj_odd", tm_max=TM_WIDE)
            p = W["s5_packed"][j]
            ngb = p["bbd_re"].shape[0]
            g_bf, s_re, s_im = _s5(u.reshape(b, l, d), p,
                                   s5_re0[j].reshape(b, ngb, 1, -1), s5_im0[j].reshape(b, ngb, 1, -1))
            (x2,) = _dense(g_bf.reshape(t, d), W["s5_w_glu"], layer=j, n_cols=d, tn=512, col0=0, col0_2=d,
                           epi="glu_res", res=x2, name="s5_glu_out")
            srs.append(s_re.reshape(b, -1, S5_STATE))
            sis.append(s_im.reshape(b, -1, S5_STATE))
        cross = _cross_attn if fox_cache is None else _cross_attn_sample
        x3 = cross(x2.reshape(b, l, d), W["norm_cross"][i], W["w_mq"], W["mem_q_norm"][i],
                   mem_k, mem_v, W["w_mo"], i)
        x2 = _ffn(x3.reshape(t, d), W["norm_ffn"][i], W["w_ffn_up"], W["w_ffn_down"], i)
    fk, fv = (a.reshape(a.shape[0], b, l, FOX_HEADS, HEAD_DIM) for a in kv_stacks)
    return (x2.reshape(b, l, d), fk, fv, jnp.stack(fl), jnp.stack(hs), jnp.stack(bufs),
            jnp.stack(srs), jnp.stack(sis))


def _pack_s5(a_re, a_im, b_re, b_im, c_re, c_im, d_skip, log_dt):
    g, n, k = b_re.shape
    ngb = g // S5_GB
    ps_re, ps_im, pc_re, pc_im, bb_re, bb_im = _s5_prep(
        a_re.astype(F32), a_im.astype(F32), log_dt.astype(F32),
        jnp.transpose(b_re, (2, 0, 1)).astype(F32), jnp.transpose(b_im, (2, 0, 1)).astype(F32))

    def bmat(bb):
        return _block_diag(jnp.transpose(bb, (1, 0, 2)).reshape(ngb, S5_GB * k, n), S5_GB)

    def cmat(cm):
        return _block_diag(jnp.transpose(cm.astype(F32), (0, 2, 1)).reshape(ngb, S5_GB * n, k), S5_GB)

    def rows(pw):
        return jnp.transpose(pw.reshape(pw.shape[0], ngb, S5_GB * n), (1, 0, 2))

    return dict(bbd_re=bmat(bb_re), bbd_im=bmat(bb_im), cbd_re=cmat(c_re), cbd_im=cmat(c_im),
                ps_re=rows(ps_re), ps_im=rows(ps_im), pc_re=rows(pc_re), pc_im=rows(pc_im),
                d_vec=d_skip.astype(F32).reshape(ngb, 1, S5_GB * k))


def kernel(x_prompt, x_sample, mem_prompt, cache_fox_k, cache_fox_v, cache_fox_logf, cache_mem_k, cache_mem_v,
           state_ssd, state_conv, state_s5_re, state_s5_im, page_table,
           norm_mix, norm_cross, norm_mem, norm_ffn,
           w_in_even, fox_b_forget, fox_q_norm, fox_k_norm, ssd_conv_w, ssd_conv_b, ssd_dt_bias, ssd_A_log,
           ssd_D, ssd_norm, w_out_even,
           w_in_odd, s5_A_re, s5_A_im, s5_B_re, s5_B_im, s5_C_re, s5_C_im, s5_D, s5_log_dt, s5_w_glu,
           w_mq, w_mkv, mem_q_norm, mem_k_norm, w_mo, w_ffn_up, w_ffn_down):
    depth = norm_mix.shape[0]
    n_even, n_odd = w_in_even.shape[0], w_in_odd.shape[0]
    b, l, d = x_prompt.shape
    z0 = 3 * FOX_W + FOX_HEADS
    dt0 = z0 + SSD_INNER + CONV_DIM
    w_t = jnp.swapaxes(w_in_even, 1, 2)
    W = {
        "norm_mix": norm_mix, "norm_cross": norm_cross, "norm_ffn": norm_ffn,
        "w_in_even_t": w_t,
        "w_tail": jnp.concatenate([w_t[:, 3 * FOX_W:z0], w_t[:, dt0:dt0 + SSD_HEADS],
                                   jnp.zeros((n_even, LANES - FOX_HEADS - SSD_HEADS, d), w_t.dtype)], axis=1),
        "fox_b_forget": fox_b_forget, "fox_q_norm": fox_q_norm, "fox_k_norm": fox_k_norm,
        "ssd_conv_w": ssd_conv_w, "ssd_conv_b": ssd_conv_b, "ssd_dt_bias": ssd_dt_bias, "ssd_A_log": ssd_A_log,
        "ssd_D": ssd_D, "ssd_norm": ssd_norm, "w_out_even": w_out_even,
        "w_in_odd": w_in_odd, "s5_w_glu": s5_w_glu,
        "s5_packed": [_pack_s5(s5_A_re[j], s5_A_im[j], s5_B_re[j], s5_B_im[j], s5_C_re[j], s5_C_im[j],
                               s5_D[j], s5_log_dt[j]) for j in range(n_odd)],
        "w_mq": w_mq, "mem_q_norm": mem_q_norm, "w_mo": w_mo, "w_ffn_up": w_ffn_up, "w_ffn_down": w_ffn_down,
    }
    mt = mem_prompt.shape[1]
    mw = w_mkv.shape[2] // 2
    mem_k_p, mem_v_p = (a.reshape(depth, b, mt, mw) for a in
                        _mem_kv(mem_prompt.reshape(b * mt, d), norm_mem, w_mkv, mem_k_norm))
    n_grp = s5_A_re.shape[1]
    prompt_trunk = _trunk(
        x_prompt, W, mem_k_p, mem_v_p,
        jnp.zeros((n_even, b, SSD_CONV - 1, CONV_DIM), F32),
        jnp.zeros((n_even, b, SSD_HEADS, SSD_HD, SSD_STATE), F32),
        jnp.zeros((n_odd, b, n_grp, S5_STATE), F32),
        jnp.zeros((n_odd, b, n_grp, S5_STATE), F32),
        None)
    db = x_sample.shape[0]
    n_pool, page = cache_fox_k.shape[1], cache_fox_k.shape[2]
    assert page == LANES
    width = page * FOX_HEADS
    lf_t = jnp.transpose(cache_fox_logf.astype(F32), (0, 2, 1, 3)).reshape(n_even, page, n_pool * FOX_HEADS)
    incl = _pool_cumsum(lf_t).reshape(n_even, page, n_pool, FOX_HEADS)
    loc = jnp.transpose(incl, (0, 2, 1, 3)).reshape(n_even, n_pool, 1, width)
    tot = jnp.tile(incl[:, page - 1], (1, 1, page)).reshape(n_even, n_pool, 1, width)
    fox_cache = (cache_fox_k.reshape(n_even, n_pool, width, HEAD_DIM),
                 cache_fox_v.reshape(n_even, n_pool, width, HEAD_DIM), loc, tot, page_table)
    sample_trunk = _trunk(
        x_sample, W, cache_mem_k.reshape(depth, db, mt * MEM_HEADS, HEAD_DIM),
        cache_mem_v.reshape(depth, db, mt * MEM_HEADS, HEAD_DIM),
        state_conv, state_ssd, state_s5_re, state_s5_im, fox_cache)
    (y_prompt, fox_k_p, fox_v_p, fox_logf_p, ssd_p, conv_p, s5_re_p, s5_im_p) = prompt_trunk
    (y_sample, fox_k_s, fox_v_s, fox_logf_s, ssd_s, conv_s, s5_re_s, s5_im_s) = sample_trunk
    hd = mw // MEM_HEADS
    return (y_prompt, y_sample,
            fox_k_p, fox_v_p, fox_logf_p,
            mem_k_p.reshape(depth, b, mt, MEM_HEADS, hd), mem_v_p.reshape(depth, b, mt, MEM_HEADS, hd),
            ssd_p, conv_p, s5_re_p, s5_im_p,
            fox_k_s, fox_v_s, fox_logf_s, ssd_s, conv_s, s5_re_s, s5_im_s)
```
